```python
import math
import jax
import jax.numpy as jnp
from jax import lax
import numpy as np

D_MODEL = 1024
BATCH = 2
SEQ = 16384
DEPTH = 2

GRID_W = 64
CTX_LEN = 256
N_MIX_GROUPS = 4
D_GROUP = D_MODEL // N_MIX_GROUPS
D_MIX = N_MIX_GROUPS * D_GROUP
A_HEADS = 4
A_QK = 32
A_V = D_GROUP // A_HEADS
Q_BLOCK = 128
ROPE_BASE = 10000.0
B_HEADS = 4
B_DIM = D_GROUP // B_HEADS
WIN_R = 8
WIN_C = 16
POOL_SIZES = (2, 4, 8, 16)
POOL_CH = D_GROUP // len(POOL_SIZES)
HY_CH = D_GROUP
HY_ORDER = 2
HY_SHORT = 3
HY_BANDS = 16
HY_EMB = 1 + 2 * HY_BANDS
HY_HIDDEN = 64
HY_SIN_FREQ = 1.0
HY_MIN_DECAY = math.log(1e-2) / 1.5
HY_MAX_DECAY = math.log(1e-2) / 0.3
D_IN = 2 * A_HEADS * 2 * A_QK + A_HEADS * A_V + 3 * B_HEADS * B_DIM + D_GROUP + (HY_ORDER + 1) * HY_CH
N_EXPERTS = 16
N_EXPERT_GROUPS = 4
TOPK_GROUPS = 1
TOP_K = 2
D_EXPERT = 512
EPS = 1e-6

kernel_name = 'hybrid_diffusion_parallel_heads_moe'

F32 = jnp.float32


def rmsnorm(x, g):
    xf = x.astype(F32)
    return (xf * lax.rsqrt(jnp.mean(xf * xf, axis=-1, keepdims=True) + EPS)).astype(x.dtype) * g


def split_cols(u):
    sizes = (A_HEADS * 2 * A_QK, A_HEADS * 2 * A_QK, A_HEADS * A_V,
             B_HEADS * B_DIM, B_HEADS * B_DIM, B_HEADS * B_DIM,
             D_GROUP, (HY_ORDER + 1) * HY_CH)
    offs, acc = [], 0
    for s in sizes[:-1]:
        acc += s
        offs.append(acc)
    return jnp.split(u, offs, axis=-1)


def axial_rope(x):
    L, dim = x.shape[1], x.shape[-1]
    n_freq = dim // 4
    inv = ROPE_BASE ** (-jnp.arange(n_freq, dtype=F32) / n_freq)
    t = jnp.arange(L)
    row = (t // GRID_W).astype(F32)
    col = (t % GRID_W).astype(F32)
    ang = jnp.concatenate([row[:, None] * inv, col[:, None] * inv], axis=-1)
    shape = (1, L) + (1,) * (x.ndim - 3) + (dim // 2,)
    cos, sin = jnp.cos(ang).reshape(shape), jnp.sin(ang).reshape(shape)
    x1, x2 = x[..., : dim // 2].astype(F32), x[..., dim // 2:].astype(F32)
    return jnp.concatenate([x1 * cos - x2 * sin, x2 * cos + x1 * sin], axis=-1).astype(x.dtype)


def diff_attention(uq_c, uk_c, uv_c, uq_l, uk_l, uv_l, lam_vecs, subln_g, li, need_ctx):
    B, S = uq_l.shape[:2]
    C = uq_c.shape[1]
    lam_init = 0.8 - 0.6 * math.exp(-0.3 * li)
    lv = lam_vecs.astype(F32)
    lam = jnp.exp(jnp.sum(lv[0] * lv[1])) - jnp.exp(jnp.sum(lv[2] * lv[3])) + lam_init
    qc = uq_c.reshape(B, C, A_HEADS, 2, A_QK)
    kc = uk_c.reshape(B, C, A_HEADS, 2, A_QK)
    vc = uv_c.reshape(B, C, A_HEADS, A_V)
    ql = axial_rope(uq_l.reshape(B, S, A_HEADS, 2, A_QK))
    kl = axial_rope(uk_l.reshape(B, S, A_HEADS, 2, A_QK))
    vl = uv_l.reshape(B, S, A_HEADS, A_V)
    k_all = jnp.concatenate([kc, kl], axis=1)
    v_all = jnp.concatenate([vc, vl], axis=1)

    def attend(q, k, v):
        s = jnp.einsum('bqhcd,bkhcd->bhcqk', q, k, preferred_element_type=F32) * (A_QK ** -0.5)
        p = jax.nn.softmax(s, axis=-1)
        a = p[:, :, 0] - lam * p[:, :, 1]
        return jnp.einsum('bhqk,bkhd->bqhd', a.astype(v.dtype), v)

    def post(o):
        o = rmsnorm(o, subln_g) * (1.0 - lam_init)
        return o.reshape(o.shape[0], o.shape[1], A_HEADS * A_V)

    nb = S // Q_BLOCK
    qb = jnp.moveaxis(ql.reshape(B, nb, Q_BLOCK, A_HEADS, 2, A_QK), 1, 0)
    ol = lax.map(lambda qblk: attend(qblk, k_all, v_all), qb)
    ol = jnp.moveaxis(ol, 0, 1).reshape(B, S, A_HEADS, A_V)
    out_c = post(attend(qc, kc, vc)) if need_ctx else None
    return out_c, post(ol)


def neighbourhood_attention(uq_c, uk_c, uv_c, uq_l, uk_l, uv_l, rpb, need_ctx):
    B, S = uq_l.shape[:2]
    C = uq_c.shape[1]
    R = S // GRID_W
    wr = min(WIN_R, R)
    scale = B_DIM ** -0.5
    qc = uq_c.reshape(B, C, B_HEADS, B_DIM)
    kc = uk_c.reshape(B, C, B_HEADS, B_DIM)
    vc = uv_c.reshape(B, C, B_HEADS, B_DIM)
    qg = uq_l.reshape(B, R, GRID_W, B_HEADS, B_DIM)
    kg = uk_l.reshape(B, R, GRID_W, B_HEADS, B_DIM)
    vg = uv_l.reshape(B, R, GRID_W, B_HEADS, B_DIM)
    rows = jnp.arange(R)
    r0 = jnp.clip(rows - wr // 2, 0, R - wr)
    row_idx = r0[:, None] + jnp.arange(wr)[None, :]
    k_rows = kg[:, row_idx]
    v_rows = vg[:, row_idx]
    cols = jnp.arange(GRID_W)
    c0 = jnp.clip(cols - WIN_C // 2, 0, GRID_W - WIN_C)
    in_win = (cols[None, :] >= c0[:, None]) & (cols[None, :] < c0[:, None] + WIN_C)
    dr = row_idx - rows[:, None]
    dc = jnp.clip(cols[None, :] - cols[:, None], -(WIN_C - 1), WIN_C - 1)
    bias = rpb.astype(F32)[:, dr[:, None, :, None] + (WIN_R - 1), dc[None, :, None, :] + (WIN_C - 1)]
    bias = jnp.where(in_win[:, None, :], bias, -jnp.inf)
    bias = jnp.transpose(bias, (1, 0, 2, 3, 4))[None]
    s_win = jnp.einsum('brqhd,brikhd->brhqik', qg, k_rows, preferred_element_type=F32) * scale + bias
    s_win = s_win.reshape(B, R, B_HEADS, GRID_W, wr * GRID_W)
    s_ctx = jnp.einsum('brqhd,bkhd->brhqk', qg, kc, preferred_element_type=F32) * scale
    p = jax.nn.softmax(jnp.concatenate([s_win, s_ctx], axis=-1), axis=-1)
    p_win = p[..., : wr * GRID_W].reshape(B, R, B_HEADS, GRID_W, wr, GRID_W).astype(vg.dtype)
    p_ctx = p[..., wr * GRID_W:].astype(vg.dtype)
    ol = jnp.einsum('brhqik,brikhd->brqhd', p_win, v_rows) + jnp.einsum('brhqk,bkhd->brqhd', p_ctx, vc)
    ol = ol.reshape(B, S, B_HEADS * B_DIM)
    out_c = None
    if need_ctx:
        pc = jax.nn.softmax(jnp.einsum('bqhd,bkhd->bhqk', qc, kc, preferred_element_type=F32) * scale, axis=-1)
        out_c = jnp.einsum('bhqk,bkhd->bqhd', pc.astype(vc.dtype), vc).reshape(B, C, B_HEADS * B_DIM)
    return out_c, ol


def pool_mix(u, w_pool, pool_scale):
    B, L, _ = u.shape
    uf = u.astype(F32)
    cs = jnp.concatenate([jnp.zeros((B, 1, D_GROUP), F32), lax.cumsum(uf, axis=1)], axis=1)
    t = jnp.arange(L)
    outs = []
    for g, w in enumerate(POOL_SIZES):
        lo = jnp.clip(t - w // 2, 0, L)
        hi = jnp.clip(t + w // 2, 0, L)
        sl = slice(g * POOL_CH, (g + 1) * POOL_CH)
        mean = (cs[:, hi, sl] - cs[:, lo, sl]) / (hi - lo).astype(F32)[None, :, None]
        outs.append((mean - uf[..., sl]).astype(u.dtype) @ w_pool[g])
    return jnp.concatenate(outs, axis=-1) * pool_scale


def hyena_filters(L, w1, b1, w2, b2, w3, b3):
    t = jnp.arange(L, dtype=F32)
    t_norm = t / max(L - 1, 1)
    bands = jnp.linspace(1e-4, HY_BANDS - 1, HY_BANDS, dtype=F32)
    ang = (2.0 * math.pi / L) * t[:, None] * bands[None, :]
    z = jnp.concatenate([t_norm[:, None], jnp.cos(ang), jnp.sin(ang)], axis=-1)
    h = jnp.sin(HY_SIN_FREQ * (z @ w1.astype(F32) + b1.astype(F32)))
    h = jnp.sin(HY_SIN_FREQ * (h @ w2.astype(F32) + b2.astype(F32)))
    h = (h @ w3.astype(F32) + b3.astype(F32)).reshape(L, HY_ORDER, 2, HY_CH)
    deltas = jnp.abs(jnp.linspace(HY_MIN_DECAY, HY_MAX_DECAY, HY_CH, dtype=F32))
    h = h * jnp.exp(-t_norm[:, None, None, None] * deltas)
    return h * lax.rsqrt(jnp.sum(h * h, axis=(0, 2), keepdims=True) + EPS)


def bidir_long_conv(u, h_fwd, h_bwd, d_skip):
    L = u.shape[1]
    k = jnp.concatenate([h_fwd, jnp.zeros_like(h_fwd[:1]), h_bwd[:0:-1]], axis=0)
    uf = u.astype(F32)
    y = jnp.fft.irfft(jnp.fft.rfft(uf, n=2 * L, axis=1) * jnp.fft.rfft(k, axis=0)[None], n=2 * L, axis=1)[:, :L]
    return (y + uf * d_skip.astype(F32)).astype(u.dtype)


def hyena_mix(u, w_short, b_short, filt, d_skip):
    up = jnp.pad(u, ((0, 0), (1, 1), (0, 0)))
    u = up[:, :-2] * w_short[0] + up[:, 1:-1] * w_short[1] + up[:, 2:] * w_short[2] + b_short
    x1, x2, v = jnp.split(u, 3, axis=-1)
    z = x1 * bidir_long_conv(v, filt[:, 0, 0], filt[:, 0, 1], d_skip[0])
    return x2 * bidir_long_conv(z, filt[:, 1, 0], filt[:, 1, 1], d_skip[1])


def moe(x, router_w, router_b, w1, w3, w2):
    N = x.shape[0]
    per_group = N_EXPERTS // N_EXPERT_GROUPS
    scores = jax.nn.softmax((x @ router_w).astype(F32), axis=-1)
    sel = scores + router_b.astype(F32)
    grp_score = jnp.sum(lax.top_k(sel.reshape(N, N_EXPERT_GROUPS, per_group), TOP_K)[0], axis=-1)
    _, gidx = lax.top_k(grp_score, TOPK_GROUPS)
    gmask = jnp.sum(jax.nn.one_hot(gidx, N_EXPERT_GROUPS, dtype=F32), axis=1)
    emask = jnp.repeat(gmask, per_group, axis=1) > 0
    _, eidx = lax.top_k(jnp.where(emask, sel, -jnp.inf), TOP_K)
    w = jnp.take_along_axis(scores, eidx, axis=1)
    w = w / jnp.sum(w, axis=-1, keepdims=True)
    gates = jnp.sum(jax.nn.one_hot(eidx, N_EXPERTS, dtype=F32) * w[..., None], axis=1).astype(x.dtype)
    out = jnp.zeros_like(x)
    for e in range(N_EXPERTS):
        h = jax.nn.silu(x @ w1[e]) * (x @ w3[e])
        out = out + gates[:, e:e + 1] * (h @ w2[e])
    return out


def trunk_layer(xc, xl, c, c_ctx, lp, router_w, router_b, li, need_ctx):
    B, S, D = xl.shape
    C = xc.shape[1]
    mod_l = jnp.split(jax.nn.silu(c) @ lp['ada_w'] + lp['ada_b'], 6, axis=-1)
    mod_c = jnp.split(jax.nn.silu(c_ctx) @ lp['ada_w'] + lp['ada_b'], 6, axis=-1)
    sh1_l, sc1_l, g1_l, sh2_l, sc2_l, g2_l = [m[:, None, :] for m in mod_l]
    sh1_c, sc1_c, g1_c, sh2_c, sc2_c, g2_c = mod_c
    hl = rmsnorm(xl, lp['norm1_g']) * (1.0 + sc1_l) + sh1_l
    hc = rmsnorm(xc, lp['norm1_g']) * (1.0 + sc1_c) + sh1_c
    aq_l, ak_l, av_l, bq_l, bk_l, bv_l, pool_l, hy_l = split_cols(hl @ lp['w_in'])
    aq_c, ak_c, av_c, bq_c, bk_c, bv_c, pool_c, hy_c = split_cols(hc @ lp['w_in'])
    a_c, a_l = diff_attention(aq_c, ak_c, av_c, aq_l, ak_l, av_l, lp['a_lambda'], lp['a_subln_g'], li, need_ctx)
    b_c, b_l = neighbourhood_attention(bq_c, bk_c, bv_c, bq_l, bk_l, bv_l, lp['b_rpb'], need_ctx)
    p_l = pool_mix(pool_l, lp['pool_w'], lp['pool_scale'])
    filt_l = hyena_filters(S, lp['hy_f_w1'], lp['hy_f_b1'], lp['hy_f_w2'], lp['hy_f_b2'], lp['hy_f_w3'], lp['hy_f_b3'])
    h_l = hyena_mix(hy_l, lp['hy_short_w'], lp['hy_short_b'], filt_l, lp['hy_skip'])
    xl = xl + g1_l * (jnp.concatenate([a_l, b_l, p_l, h_l], axis=-1) @ lp['w_out'])
    if need_ctx:
        p_c = pool_mix(pool_c, lp['pool_w'], lp['pool_scale'])
        filt_c = hyena_filters(C, lp['hy_f_w1'], lp['hy_f_b1'], lp['hy_f_w2'], lp['hy_f_b2'], lp['hy_f_w3'], lp['hy_f_b3'])
        h_c = hyena_mix(hy_c, lp['hy_short_w'], lp['hy_short_b'], filt_c, lp['hy_skip'])
        xc = xc + g1_c * (jnp.concatenate([a_c, b_c, p_c, h_c], axis=-1) @ lp['w_out'])
    hl2 = rmsnorm(xl, lp['norm2_g']) * (1.0 + sc2_l) + sh2_l
    if need_ctx:
        hc2 = rmsnorm(xc, lp['norm2_g']) * (1.0 + sc2_c) + sh2_c
        tok = jnp.concatenate([hc2, hl2], axis=1).reshape(B * (C + S), D)
        y = moe(tok, router_w, router_b, lp['moe_w1'], lp['moe_w3'], lp['moe_w2']).reshape(B, C + S, D)
        xc = xc + g2_c * y[:, :C]
        xl = xl + g2_l * y[:, C:]
    else:
        y = moe(hl2.reshape(B * S, D), router_w, router_b, lp['moe_w1'], lp['moe_w3'], lp['moe_w2'])
        xl = xl + g2_l * y.reshape(B, S, D)
    return xc, xl


def setup_inputs(seed: int = 0) -> dict:
    key = jax.random.key(seed)
    ks = iter(jax.random.split(key, 40))

    def nrm(shape, s):
        return s * jax.random.normal(next(ks), shape, F32)

    return {
        'x': nrm((BATCH, SEQ, D_MODEL), 1.0),
        'c': nrm((BATCH, D_MODEL), 1.0),
        'ctx': nrm((BATCH, CTX_LEN, D_MODEL), 1.0),
        'c_ctx': nrm((D_MODEL,), 1.0),
        'norm1_g': 1.0 + nrm((DEPTH, D_MODEL), 0.1),
        'norm2_g': 1.0 + nrm((DEPTH, D_MODEL), 0.1),
        'ada_w': nrm((DEPTH, D_MODEL, 6 * D_MODEL), 0.5 * D_MODEL ** -0.5),
        'ada_b': nrm((DEPTH, 6 * D_MODEL), 0.02),
        'w_in': nrm((DEPTH, D_MODEL, D_IN), D_MODEL ** -0.5),
        'w_out': nrm((DEPTH, D_MIX, D_MODEL), D_MIX ** -0.5),
        'a_lambda': nrm((DEPTH, 4, A_QK), 0.1),
        'a_subln_g': 1.0 + nrm((DEPTH, A_V), 0.1),
        'b_rpb': nrm((DEPTH, B_HEADS, 2 * WIN_R - 1, 2 * WIN_C - 1), 0.1),
        'pool_w': nrm((DEPTH, len(POOL_SIZES), POOL_CH, POOL_CH), POOL_CH ** -0.5),
        'pool_scale': 1.0 + nrm((DEPTH, D_GROUP), 0.1),
        'hy_short_w': nrm((DEPTH, HY_SHORT, (HY_ORDER + 1) * HY_CH), 0.5),
        'hy_short_b': nrm((DEPTH, (HY_ORDER + 1) * HY_CH), 0.02),
        'hy_f_w1': nrm((DEPTH, HY_EMB, HY_HIDDEN), HY_EMB ** -0.5),
        'hy_f_b1': nrm((DEPTH, HY_HIDDEN), 0.1),
        'hy_f_w2': nrm((DEPTH, HY_HIDDEN, HY_HIDDEN), HY_HIDDEN ** -0.5),
        'hy_f_b2': nrm((DEPTH, HY_HIDDEN), 0.1),
        'hy_f_w3': nrm((DEPTH, HY_HIDDEN, HY_ORDER * 2 * HY_CH), HY_HIDDEN ** -0.5),
        'hy_f_b3': nrm((DEPTH, HY_ORDER * 2 * HY_CH), 0.1),
        'hy_skip': nrm((DEPTH, HY_ORDER, HY_CH), 0.5),
        'router_w': nrm((D_MODEL, N_EXPERTS), D_MODEL ** -0.5),
        'router_b': nrm((N_EXPERTS,), 0.01),
        'moe_w1': nrm((DEPTH, N_EXPERTS, D_MODEL, D_EXPERT), D_MODEL ** -0.5),
        'moe_w3': nrm((DEPTH, N_EXPERTS, D_MODEL, D_EXPERT), D_MODEL ** -0.5),
        'moe_w2': nrm((DEPTH, N_EXPERTS, D_EXPERT, D_MODEL), D_EXPERT ** -0.5),
        'final_g': 1.0 + nrm((D_MODEL,), 0.1),
    }


def reference(x, c, ctx, c_ctx, norm1_g, norm2_g, ada_w, ada_b, w_in, w_out, a_lambda, a_subln_g,
              b_rpb, pool_w, pool_scale, hy_short_w, hy_short_b, hy_f_w1, hy_f_b1, hy_f_w2, hy_f_b2,
              hy_f_w3, hy_f_b3, hy_skip, router_w, router_b, moe_w1, moe_w3, moe_w2, final_g):
    xl, xc = x, ctx
    for li in range(DEPTH):
        lp = {
            'norm1_g': norm1_g[li], 'norm2_g': norm2_g[li], 'ada_w': ada_w[li], 'ada_b': ada_b[li],
            'w_in': w_in[li], 'w_out': w_out[li], 'a_lambda': a_lambda[li], 'a_subln_g': a_subln_g[li],
            'b_rpb': b_rpb[li], 'pool_w': pool_w[li], 'pool_scale': pool_scale[li],
            'hy_short_w': hy_short_w[li], 'hy_short_b': hy_short_b[li],
            'hy_f_w1': hy_f_w1[li], 'hy_f_b1': hy_f_b1[li], 'hy_f_w2': hy_f_w2[li], 'hy_f_b2': hy_f_b2[li],
            'hy_f_w3': hy_f_w3[li], 'hy_f_b3': hy_f_b3[li], 'hy_skip': hy_skip[li],
            'moe_w1': moe_w1[li], 'moe_w3': moe_w3[li], 'moe_w2': moe_w2[li],
        }
        xc, xl = trunk_layer(xc, xl, c, c_ctx, lp, router_w, router_b, li, need_ctx=(li < DEPTH - 1))
    return rmsnorm(xl, final_g)
```

```python
import functools
import math

import numpy as np
import jax
import jax.numpy as jnp
from jax import lax
from jax.experimental import pallas as pl
from jax.experimental.pallas import tpu as pltpu

F32 = jnp.float32
BF16 = jnp.bfloat16
HI = lax.Precision.HIGHEST

GRID_W = 64
A_HEADS = 4
A_QK = 32
A_V = 64
ROPE_BASE = 10000.0
B_HEADS = 4
B_DIM = 64
WIN_R = 8
WIN_C = 16
POOL_SIZES = (2, 4, 8, 16)
POOL_CH = 64
D_GROUP = 256
HY_CH = 256
HY_BANDS = 16
HY_EMB = 1 + 2 * HY_BANDS
HY_HIDDEN = 64
HY_SIN_FREQ = 1.0
HY_MIN_DECAY = math.log(1e-2) / 1.5
HY_MAX_DECAY = math.log(1e-2) / 0.3
N_EXPERTS = 16
N_EXPERT_GROUPS = 4
D_EXPERT = 512
EPS = 1e-6

LANES = 128
SUBLANES = 8
VMEM_LIMIT = 56 * 1024 * 1024


def _params(sem):
    return pltpu.CompilerParams(dimension_semantics=sem, vmem_limit_bytes=VMEM_LIMIT)


def _dot(a, b, prec=None):
    return jnp.dot(a, b, precision=prec, preferred_element_type=F32)


def _dot_nt(a, b):
    return lax.dot_general(a, b, (((1,), (1,)), ((), ())), preferred_element_type=F32)


def _ada_kernel(c_ref, w_ref, b_ref, o_ref):
    cf = c_ref[...]
    s = cf * jax.nn.sigmoid(cf)
    o_ref[...] = _dot(s, w_ref[...], HI) + b_ref[...]


def _ada(cpad, w, b):
    d = cpad.shape[1]
    n = w.shape[1]
    return pl.pallas_call(
        _ada_kernel,
        grid=(n // d,),
        in_specs=[pl.BlockSpec((SUBLANES, d), lambda j: (0, 0)),
                  pl.BlockSpec((d, d), lambda j: (0, j)),
                  pl.BlockSpec((1, d), lambda j: (0, j))],
        out_specs=pl.BlockSpec((SUBLANES, d), lambda j: (0, j)),
        out_shape=jax.ShapeDtypeStruct((SUBLANES, n), F32),
        compiler_params=_params(("arbitrary",)),
        name="ada_mod",
    )(cpad, w, b)


def _norm_proj_kernel(x_ref, g_ref, sc_ref, sh_ref, w_ref, o_ref):
    x = x_ref[...]
    ms = jnp.mean(x * x, axis=-1, keepdims=True)
    h = (x * lax.rsqrt(ms + EPS)) * g_ref[...] * (1.0 + sc_ref[...]) + sh_ref[...]
    o_ref[...] = _dot(h.astype(BF16), w_ref[...])


def _norm_proj(x, g, mod3, row_of_batch, j_shift, j_scale, w, tm):
    bsz, t, d = x.shape
    n = w.shape[1]
    return pl.pallas_call(
        _norm_proj_kernel,
        grid=(bsz, t // tm),
        in_specs=[pl.BlockSpec((None, tm, d), lambda b, i: (b, i, 0)),
                  pl.BlockSpec((1, d), lambda b, i: (0, 0)),
                  pl.BlockSpec((None, 1, d), lambda b, i: (row_of_batch(b), 0, j_scale)),
                  pl.BlockSpec((None, 1, d), lambda b, i: (row_of_batch(b), 0, j_shift)),
                  pl.BlockSpec((d, n), lambda b, i: (0, 0))],
        out_specs=pl.BlockSpec((None, tm, n), lambda b, i: (b, i, 0)),
        out_shape=jax.ShapeDtypeStruct((bsz, t, n), F32),
        compiler_params=_params(("parallel", "arbitrary")),
        name="norm_in_proj",
    )(x, g, mod3, mod3, w)


def _aprep_kernel(*refs, rope):
    if rope:
        u_ref, cos_ref, sin_ref, q_ref, kt_ref, v_ref = refs
    else:
        u_ref, q_ref, kt_ref, v_ref = refs
    u = u_ref[...]
    q = u[:, 0:256]
    k = u[:, 256:512]
    v = u[:, 512:768]
    if rope:
        cos_t = cos_ref[...]
        sin_t = sin_ref[...]
        lane = lax.broadcasted_iota(jnp.int32, cos_t.shape, 1)
        first = (lane % (2 * 16)) < 16

        def rot(x):
            halves = []
            for j in range(2):
                xh = x[:, j * LANES:(j + 1) * LANES]
                swap = jnp.where(first, pltpu.roll(xh, LANES - 16, axis=1), pltpu.roll(xh, 16, axis=1))
                halves.append(xh * cos_t + swap * sin_t)
            return jnp.concatenate(halves, axis=1)

        q = rot(q)
        k = rot(k)
    q = q * (A_QK ** -0.5)
    kt = k.T
    for hc in range(2 * A_HEADS):
        q_ref[hc] = q[:, hc * A_QK:(hc + 1) * A_QK].astype(BF16)
        kt_ref[hc] = kt[hc * A_QK:(hc + 1) * A_QK, :].astype(BF16)
    for h in range(A_HEADS):
        v_ref[h] = v[:, h * A_V:(h + 1) * A_V].astype(BF16)


def _rope_tables(length):
    n_freq = A_QK // 4
    inv = ROPE_BASE ** (-jnp.arange(n_freq, dtype=F32) / n_freq)
    t = jnp.arange(length)
    row = (t // GRID_W).astype(F32)
    col = (t % GRID_W).astype(F32)
    ang = jnp.concatenate([row[:, None] * inv, col[:, None] * inv], axis=-1)
    cos, sin = jnp.cos(ang), jnp.sin(ang)
    cos_t = jnp.tile(jnp.concatenate([cos, cos], axis=-1), (1, LANES // 32))
    sin_t = jnp.tile(jnp.concatenate([-sin, sin], axis=-1), (1, LANES // 32))
    return cos_t, sin_t


def _attn_prep(u, tm, rope):
    bsz, t, _ = u.shape
    nh = 2 * A_HEADS
    in_specs = [pl.BlockSpec((None, tm, 768), lambda b, i: (b, i, 0))]
    args = [u]
    if rope:
        cos_t, sin_t = _rope_tables(t)
        in_specs += [pl.BlockSpec((tm, LANES), lambda b, i: (i, 0))] * 2
        args += [cos_t, sin_t]
    return pl.pallas_call(
        functools.partial(_aprep_kernel, rope=rope),
        grid=(bsz, t // tm),
        in_specs=in_specs,
        out_specs=[pl.BlockSpec((None, nh, tm, A_QK), lambda b, i: (b, 0, i, 0)),
                   pl.BlockSpec((None, nh, A_QK, tm), lambda b, i: (b, 0, 0, i)),
                   pl.BlockSpec((None, A_HEADS, tm, A_V), lambda b, i: (b, 0, i, 0))],
        out_shape=[jax.ShapeDtypeStruct((bsz, nh, t, A_QK), BF16),
                   jax.ShapeDtypeStruct((bsz, nh, A_QK, t), BF16),
                   jax.ShapeDtypeStruct((bsz, A_HEADS, t, A_V), BF16)],
        compiler_params=_params(("parallel", "arbitrary")),
        name="attn_prep_rope" if rope else "attn_prep_ctx",
    )(*args)


def _dattn_kernel(*refs, lam_init, has_lat):
    if has_lat:
        lam_ref, g_ref, q_ref, kc_ref, vc_ref, k_ref, v_ref, o_ref, m_scr, l_scr, acc_scr = refs
    else:
        lam_ref, g_ref, q_ref, kc_ref, vc_ref, o_ref, m_scr, l_scr, acc_scr = refs
    ki = pl.program_id(2)
    nk = pl.num_programs(2)

    def update(kt_r, v_r):
        for hc in range(2 * A_HEADS):
            s = _dot(q_ref[hc], kt_r[hc])
            m_prev = m_scr[hc]
            m_new = jnp.maximum(m_prev, jnp.max(s, axis=1, keepdims=True))
            alpha = jnp.exp(m_prev - m_new)
            p = jnp.exp(s - m_new[:, :1])
            l_scr[hc] = alpha * l_scr[hc] + jnp.sum(p, axis=1, keepdims=True)
            acc_scr[hc] = alpha[:, :A_V] * acc_scr[hc] + _dot(p.astype(BF16), v_r[hc // 2])
            m_scr[hc] = m_new

    @pl.when(ki == 0)
    def _():
        m_scr[...] = jnp.full(m_scr.shape, -jnp.inf, F32)
        l_scr[...] = jnp.zeros(l_scr.shape, F32)
        acc_scr[...] = jnp.zeros(acc_scr.shape, F32)
        update(kc_ref, vc_ref)

    if has_lat:
        @pl.when(ki > 0)
        def _():
            update(k_ref, v_ref)

    @pl.when(ki == nk - 1)
    def _():
        lv = lam_ref[...]
        lam = (jnp.exp(jnp.sum(lv[0:1] * lv[1:2], axis=1, keepdims=True))
               - jnp.exp(jnp.sum(lv[2:3] * lv[3:4], axis=1, keepdims=True)) + lam_init)
        for h in range(A_HEADS):
            o = (acc_scr[2 * h] / l_scr[2 * h][:, :A_V]
                 - lam * (acc_scr[2 * h + 1] / l_scr[2 * h + 1][:, :A_V]))
            ms = jnp.mean(o * o, axis=-1, keepdims=True)
            o_ref[:, h * A_V:(h + 1) * A_V] = (o * lax.rsqrt(ms + EPS)) * g_ref[...] * (1.0 - lam_init)


def _diff_attn(q, kct, vc, kt, v, lam_vecs, subln_g, lam_init, tq, tk):
    bsz, nh, t, _ = q.shape
    c = kct.shape[-1]
    has_lat = kt is not None
    nk = 1 + (kt.shape[-1] // tk if has_lat else 0)
    in_specs = [pl.BlockSpec((4, A_QK), lambda b, i, k: (0, 0)),
                pl.BlockSpec((1, A_V), lambda b, i, k: (0, 0)),
                pl.BlockSpec((None, nh, tq, A_QK), lambda b, i, k: (b, 0, i, 0)),
                pl.BlockSpec((None, nh, A_QK, c), lambda b, i, k: (b, 0, 0, 0)),
                pl.BlockSpec((None, A_HEADS, c, A_V), lambda b, i, k: (b, 0, 0, 0))]
    args = [lam_vecs, subln_g, q, kct, vc]
    if has_lat:
        in_specs += [pl.BlockSpec((None, nh, A_QK, tk), lambda b, i, k: (b, 0, 0, jnp.maximum(k - 1, 0))),
                     pl.BlockSpec((None, A_HEADS, tk, A_V), lambda b, i, k: (b, 0, jnp.maximum(k - 1, 0), 0))]
        args += [kt, v]
    return pl.pallas_call(
        functools.partial(_dattn_kernel, lam_init=lam_init, has_lat=has_lat),
        grid=(bsz, t // tq, nk),
        in_specs=in_specs,
        out_specs=pl.BlockSpec((None, tq, A_HEADS * A_V), lambda b, i, k: (b, i, 0)),
        out_shape=jax.ShapeDtypeStruct((bsz, t, A_HEADS * A_V), F32),
        scratch_shapes=[pltpu.VMEM((nh, tq, LANES), F32),
                        pltpu.VMEM((nh, tq, LANES), F32),
                        pltpu.VMEM((nh, tq, A_V), F32)],
        compiler_params=_params(("parallel", "parallel", "arbitrary")),
        name="diff_attn" if has_lat else "diff_attn_ctx",
    )(*args)


NB_ROWS = 8


def _nbr_bias(rpb):
    cols = jnp.arange(GRID_W)
    c0 = jnp.clip(cols - WIN_C // 2, 0, GRID_W - WIN_C)
    in_win = (cols[None, :] >= c0[:, None]) & (cols[None, :] < c0[:, None] + WIN_C)
    dc = jnp.clip(cols[None, :] - cols[:, None], -(WIN_C - 1), WIN_C - 1) + (WIN_C - 1)
    a = jnp.arange(WIN_R)[:, None] + jnp.arange(WIN_R)[None, :]
    b = rpb.astype(F32)[:, a[:, :, None, None], dc[None, None, :, :]]
    b = jnp.where(in_win[None, None, None], b, -jnp.inf)
    b = jnp.transpose(b, (1, 0, 3, 2, 4))
    return b.reshape(WIN_R, B_HEADS, GRID_W, WIN_R * GRID_W)


def _nbr_kernel(q_ref, kp_ref, kc_ref, kn_ref, vp_ref, vcur_ref, vn_ref, kctx_ref, vctx_ref, bias_ref,
                o_ref, kwin, vwin, kcx, vcx, *, n_rows):
    rb = pl.program_id(1)
    blk = NB_ROWS * GRID_W
    scale = B_DIM ** -0.5
    for h in range(B_HEADS):
        sl = slice(h * B_DIM, (h + 1) * B_DIM)
        for j, (kr, vr) in enumerate(((kp_ref, vp_ref), (kc_ref, vcur_ref), (kn_ref, vn_ref))):
            kwin[h, j * blk:(j + 1) * blk, :] = kr[:, sl].astype(BF16)
            vwin[h, j * blk:(j + 1) * blk, :] = vr[:, sl].astype(BF16)
        kcx[h] = kctx_ref[:, sl].astype(BF16)
        vcx[h] = vctx_ref[:, sl].astype(BF16)

    def row_body(rr, carry):
        r = rb * NB_ROWS + rr
        r0 = jnp.clip(r - WIN_R // 2, 0, n_rows - WIN_R)
        off = pl.multiple_of((r0 - (rb - 1) * NB_ROWS) * GRID_W, GRID_W)
        a0 = r0 - r + (WIN_R - 1)
        qrow = q_ref[pl.ds(pl.multiple_of(rr * GRID_W, GRID_W), GRID_W), :]
        outs = []
        for h in range(B_HEADS):
            qh = qrow[:, h * B_DIM:(h + 1) * B_DIM].astype(BF16)
            kr = kwin[h, pl.ds(off, WIN_R * GRID_W), :]
            vr = vwin[h, pl.ds(off, WIN_R * GRID_W), :]
            s = _dot_nt(qh, kr) * scale + bias_ref[a0, h]
            sc = _dot_nt(qh, kcx[h]) * scale
            m = jnp.maximum(jnp.max(s, axis=1, keepdims=True), jnp.max(sc, axis=1, keepdims=True))
            p = jnp.exp(s - m)
            pc = jnp.exp(sc - m)
            l = jnp.sum(p, axis=1, keepdims=True) + jnp.sum(pc, axis=1, keepdims=True)
            o = _dot(p.astype(BF16), vr) + _dot(pc.astype(BF16), vcx[h])
            outs.append(o / l)
        o_ref[pl.ds(pl.multiple_of(rr * GRID_W, GRID_W), GRID_W), :] = jnp.concatenate(outs, axis=1)
        return carry

    lax.fori_loop(0, NB_ROWS, row_body, 0)


def _nbr_attn(u, uc, bias8):
    bsz, s, _ = u.shape
    c = uc.shape[1]
    n_rows = s // GRID_W
    nb = n_rows // NB_ROWS
    blk = NB_ROWS * GRID_W
    w = B_HEADS * B_DIM

    def spec(col, shift):
        return pl.BlockSpec((None, blk, w), lambda b, i: (b, jnp.clip(i + shift, 0, nb - 1), col))

    return pl.pallas_call(
        functools.partial(_nbr_kernel, n_rows=n_rows),
        grid=(bsz, nb),
        in_specs=[spec(3, 0), spec(4, -1), spec(4, 0), spec(4, 1), spec(5, -1), spec(5, 0), spec(5, 1),
                  pl.BlockSpec((None, c, w), lambda b, i: (b, 0, 4)),
                  pl.BlockSpec((None, c, w), lambda b, i: (b, 0, 5)),
                  pl.BlockSpec(bias8.shape, lambda b, i: (0, 0, 0, 0))],
        out_specs=pl.BlockSpec((None, blk, w), lambda b, i: (b, i, 0)),
        out_shape=jax.ShapeDtypeStruct((bsz, s, w), F32),
        scratch_shapes=[pltpu.VMEM((B_HEADS, 3 * blk, B_DIM), BF16),
                        pltpu.VMEM((B_HEADS, 3 * blk, B_DIM), BF16),
                        pltpu.VMEM((B_HEADS, c, B_DIM), BF16),
                        pltpu.VMEM((B_HEADS, c, B_DIM), BF16)],
        compiler_params=_params(("parallel", "arbitrary")),
        name="nbr_attn",
    )(u, u, u, u, u, u, u, uc, uc, bias8)


def _nbr_ctx_kernel(q_ref, k_ref, v_ref, o_ref):
    scale = B_DIM ** -0.5
    outs = []
    for h in range(B_HEADS):
        sl = slice(h * B_DIM, (h + 1) * B_DIM)
        s = _dot_nt(q_ref[:, sl].astype(BF16), k_ref[:, sl].astype(BF16)) * scale
        m = jnp.max(s, axis=1, keepdims=True)
        p = jnp.exp(s - m)
        l = jnp.sum(p, axis=1, keepdims=True)
        outs.append(_dot(p.astype(BF16), v_ref[:, sl].astype(BF16)) / l)
    o_ref[...] = jnp.concatenate(outs, axis=1)


def _nbr_ctx_attn(uc):
    bsz, c, _ = uc.shape
    w = B_HEADS * B_DIM
    return pl.pallas_call(
        _nbr_ctx_kernel,
        grid=(bsz,),
        in_specs=[pl.BlockSpec((None, c, w), lambda b: (b, 0, 3)),
                  pl.BlockSpec((None, c, w), lambda b: (b, 0, 4)),
                  pl.BlockSpec((None, c, w), lambda b: (b, 0, 5))],
        out_specs=pl.BlockSpec((None, c, w), lambda b: (b, 0, 0)),
        out_shape=jax.ShapeDtypeStruct((bsz, c, w), F32),
        compiler_params=_params(("arbitrary",)),
        name="nbr_attn_ctx",
    )(uc, uc, uc)


HALO = SUBLANES


def _halo_specs(tm, length, col, width):
    nt = length // tm
    per = tm // HALO
    last = length // HALO - 1
    return [pl.BlockSpec((None, HALO, width), lambda b, i, *_: (b, jnp.maximum(i * per - 1, 0), col)),
            pl.BlockSpec((None, tm, width), lambda b, i, *_: (b, i, col)),
            pl.BlockSpec((None, HALO, width), lambda b, i, *_: (b, jnp.minimum((i + 1) * per, last), col))], nt


def _fill_halo(buf, prev_ref, cur_ref, next_ref, i, nt, tm):
    zero = jnp.zeros(prev_ref.shape, F32)
    buf[0:HALO, :] = jnp.where(i > 0, prev_ref[...], zero)
    buf[HALO:HALO + tm, :] = cur_ref[...]
    buf[HALO + tm:, :] = jnp.where(i < nt - 1, next_ref[...], zero)


def _pool_kernel(prev_ref, cur_ref, next_ref, w_ref, ps_ref, o_ref, buf, *, tm, nt, length):
    i = pl.program_id(1)
    _fill_halo(buf, prev_ref, cur_ref, next_ref, i, nt, tm)

    def sh(j):
        return buf[HALO + j:HALO + j + tm, :]

    u = sh(0)
    sums = []
    acc = None
    lo, hi = 0, 0
    for w in POOL_SIZES:
        for j in list(range(-(w // 2), lo)) + list(range(hi, w // 2)):
            acc = sh(j) if acc is None else acc + sh(j)
        lo, hi = -(w // 2), w // 2
        sums.append(acc)
    lane = lax.broadcasted_iota(jnp.int32, (tm, D_GROUP), 1)
    t = (i * tm + lax.broadcasted_iota(jnp.int32, (tm, D_GROUP), 0))
    wsum = sums[-1]
    half = jnp.full((tm, D_GROUP), POOL_SIZES[-1] // 2, jnp.int32)
    for g in range(len(POOL_SIZES) - 2, -1, -1):
        sel = lane < (g + 1) * POOL_CH
        wsum = jnp.where(sel, sums[g], wsum)
        half = jnp.where(sel, POOL_SIZES[g] // 2, half)
    cnt = (jnp.minimum(t + half, length) - jnp.maximum(t - half, 0)).astype(F32)
    d = wsum / cnt - u
    o_ref[...] = _dot(d.astype(BF16), w_ref[...]) * ps_ref[...]


def _pool_mix(u, wbd, pool_scale, tm):
    bsz, length, _ = u.shape
    specs, nt = _halo_specs(tm, length, 6, D_GROUP)
    return pl.pallas_call(
        functools.partial(_pool_kernel, tm=tm, nt=nt, length=length),
        grid=(bsz, nt),
        in_specs=specs + [pl.BlockSpec((D_GROUP, D_GROUP), lambda b, i: (0, 0)),
                          pl.BlockSpec((1, D_GROUP), lambda b, i: (0, 0))],
        out_specs=pl.BlockSpec((None, tm, D_GROUP), lambda b, i: (b, i, 0)),
        out_shape=jax.ShapeDtypeStruct((bsz, length, D_GROUP), F32),
        scratch_shapes=[pltpu.VMEM((tm + 2 * HALO, D_GROUP), F32)],
        compiler_params=_params(("parallel", "arbitrary")),
        name="pool_mix",
    )(u, u, u, wbd, pool_scale)


def _hy_short_kernel(prev_ref, cur_ref, next_ref, w_ref, b_ref, o_ref, buf, *, tm, nt):
    i = pl.program_id(1)
    _fill_halo(buf, prev_ref, cur_ref, next_ref, i, nt, tm)
    w = w_ref[...]
    y = (buf[HALO - 1:HALO - 1 + tm, :] * w[0:1] + buf[HALO:HALO + tm, :] * w[1:2]
         + buf[HALO + 1:HALO + 1 + tm, :] * w[2:3] + b_ref[...])
    o_ref[...] = y.T


def _hy_short(u, w_short, b_short, tm):
    bsz, length, _ = u.shape
    nt = length // tm
    per = tm // HALO
    last = length // HALO - 1
    c0 = 7
    in_specs = [pl.BlockSpec((None, HALO, HY_CH), lambda b, i, j: (b, jnp.maximum(i * per - 1, 0), c0 + j)),
                pl.BlockSpec((None, tm, HY_CH), lambda b, i, j: (b, i, c0 + j)),
                pl.BlockSpec((None, HALO, HY_CH), lambda b, i, j: (b, jnp.minimum((i + 1) * per, last), c0 + j)),
                pl.BlockSpec((3, HY_CH), lambda b, i, j: (0, j)),
                pl.BlockSpec((1, HY_CH), lambda b, i, j: (0, j))]
    return pl.pallas_call(
        functools.partial(_hy_short_kernel, tm=tm, nt=nt),
        grid=(bsz, nt, 3),
        in_specs=in_specs,
        out_specs=pl.BlockSpec((None, None, HY_CH, tm), lambda b, i, j: (j, b, 0, i)),
        out_shape=jax.ShapeDtypeStruct((3, bsz, HY_CH, length), F32),
        scratch_shapes=[pltpu.VMEM((tm + 2 * HALO, HY_CH), F32)],
        compiler_params=_params(("parallel", "arbitrary", "arbitrary")),
        name="hyena_short_conv",
    )(u, u, u, w_short, b_short)


HY_FEAT = 40


def _hy_filter_kernel(band_ref, w1_ref, b1_ref, w2_ref, b2_ref, w3_ref, b3_ref, dl_ref,
                      k_ref, ssq_ref, nrm_ref, *, tp, length):
    i = pl.program_id(0)
    n_i = pl.num_programs(0)
    m = i * tp + lax.broadcasted_iota(jnp.int32, (1, tp), 1)
    t = jnp.where(m <= length, m, 2 * length - m).astype(F32)
    t_norm = t / max(length - 1, 1)
    ang = ((2.0 * math.pi / length) * t) * band_ref[...]
    row = lax.broadcasted_iota(jnp.int32, (HY_FEAT, tp), 0)
    z = jnp.where(row == 0, t_norm,
                  jnp.where(row <= HY_BANDS, jnp.cos(ang), jnp.where(row < HY_EMB, jnp.sin(ang), 0.0)))
    z = jnp.concatenate([z, jnp.zeros((LANES - HY_FEAT, tp), F32)], axis=0)
    h = jnp.sin(HY_SIN_FREQ * (_dot(w1_ref[...], z, HI) + b1_ref[...]))
    h = jnp.sin(HY_SIN_FREQ * (_dot(w2_ref[...], h, HI) + b2_ref[...]))
    h = _dot(w3_ref[...], h, HI) + b3_ref[...]
    h = h * jnp.exp(-t_norm * dl_ref[...])

    @pl.when(i == 0)
    def _():
        ssq_ref[...] = jnp.zeros(ssq_ref.shape, F32)

    for o in range(2):
        fwd = h[o * 2 * HY_CH:o * 2 * HY_CH + HY_CH]
        bwd = h[o * 2 * HY_CH + HY_CH:(o + 1) * 2 * HY_CH]
        k = jnp.where(m < length, fwd, jnp.where(m == length, 0.0, bwd))
        k_ref[o] = k
        extra = jnp.where(m == 0, bwd * bwd, 0.0)
        ssq_ref[o] += jnp.sum(k * k + extra, axis=1, keepdims=True)

    @pl.when(i == n_i - 1)
    def _():
        nrm_ref[...] = lax.rsqrt(ssq_ref[...] + EPS)


def _hy_filters(length, w1, b1, w2, b2, w3, b3, tp):
    bands = jnp.linspace(1e-4, HY_BANDS - 1, HY_BANDS, dtype=F32)
    band_col = jnp.concatenate([jnp.zeros((1,), F32), bands, bands,
                                jnp.zeros((HY_FEAT - HY_EMB,), F32)])[:, None]
    deltas = jnp.abs(jnp.linspace(HY_MIN_DECAY, HY_MAX_DECAY, HY_CH, dtype=F32))
    dl_col = jnp.tile(deltas, 4)[:, None]
    w1t = jnp.pad(w1.astype(F32).T, ((0, 0), (0, LANES - HY_EMB)))
    full = lambda shape: pl.BlockSpec(shape, lambda i: (0,) * len(shape))
    n = 2 * length
    return pl.pallas_call(
        functools.partial(_hy_filter_kernel, tp=tp, length=length),
        grid=(n // tp,),
        in_specs=[full((HY_FEAT, 1)), full((HY_HIDDEN, LANES)), full((HY_HIDDEN, 1)),
                  full((HY_HIDDEN, HY_HIDDEN)), full((HY_HIDDEN, 1)),
                  full((4 * HY_CH, HY_HIDDEN)), full((4 * HY_CH, 1)), full((4 * HY_CH, 1))],
        out_specs=[pl.BlockSpec((2, HY_CH, tp), lambda i: (0, 0, i)),
                   full((2, HY_CH, 1)), full((2, HY_CH, 1))],
        out_shape=[jax.ShapeDtypeStruct((2, HY_CH, n), F32),
                   jax.ShapeDtypeStruct((2, HY_CH, 1), F32),
                   jax.ShapeDtypeStruct((2, HY_CH, 1), F32)],
        compiler_params=_params(("arbitrary",)),
        name="hyena_filters",
    )(band_col, w1t, b1.astype(F32)[:, None], w2.astype(F32).T, b2.astype(F32)[:, None],
      w3.astype(F32).T, b3.astype(F32)[:, None], dl_col)


def _dft_consts(n1, n2):
    n = n1 * n2
    a1 = 2.0 * np.pi * ((np.arange(n1)[:, None] * np.arange(n1)[None, :]) % n1) / n1
    c1, s1 = np.cos(a1), np.sin(a1)
    a2 = 2.0 * np.pi * ((np.arange(n2)[:, None] * np.arange(n2)[None, :]) % n2) / n2
    c2, s2 = np.cos(a2), np.sin(a2)
    at = 2.0 * np.pi * ((np.arange(n1)[:, None] * np.arange(n2)[None, :]) % n) / n
    f1_full = np.concatenate([c1, -s1], axis=0)
    f1_half = f1_full[:, :n1 // 2]
    g1 = np.concatenate([c1[:n1 // 2], -s1[:n1 // 2]], axis=1)
    w2f = np.block([[c2, -s2], [s2, c2]])
    w2i = np.block([[c2, s2], [-s2, c2]])
    f = lambda x: jnp.asarray(x, F32)
    return dict(f1_full=f(f1_full), f1_half=f(f1_half), g1=f(g1), w2f=f(w2f), w2i=f(w2i),
                tr=f(np.cos(at)), ti=f(-np.sin(at)))


def _fft_fwd(slabs, f1, tr, ti, w2f, stack, n1, n2):
    for c, x in enumerate(slabs):
        a = _dot(f1, x, HI)
        ar, ai = a[:n1], a[n1:]
        stack[c * n1:(c + 1) * n1, 0:n2] = ar * tr - ai * ti
        stack[c * n1:(c + 1) * n1, n2:2 * n2] = ar * ti + ai * tr
    return _dot(stack[...], w2f, HI)


def _hy_spec_kernel(nrm_ref, k_ref, f1_ref, tr_ref, ti_ref, w2f_ref, o_ref, stack, *, cg, n1, n2):
    o = pl.program_id(0)
    g = pl.program_id(1)
    x = _fft_fwd([k_ref[c] for c in range(cg)], f1_ref[...], tr_ref[...], ti_ref[...], w2f_ref[...],
                 stack, n1, n2)
    for c in range(cg):
        sc = nrm_ref[o * HY_CH + g * cg + c] * (1.0 / (n1 * n2))
        xc = x[c * n1:(c + 1) * n1] * sc
        o_ref[c, 0] = xc[:, :n2]
        o_ref[c, 1] = xc[:, n2:]


def _hy_spec(k4, nrm_flat, dc, cg, n1, n2):
    full = lambda shape: pl.BlockSpec(shape, lambda o, g: (0,) * len(shape))
    return pl.pallas_call(
        functools.partial(_hy_spec_kernel, cg=cg, n1=n1, n2=n2),
        grid=(2, HY_CH // cg),
        in_specs=[pl.BlockSpec(memory_space=pltpu.SMEM),
                  pl.BlockSpec((None, cg, n1, n2), lambda o, g: (o, g, 0, 0)),
                  full((2 * n1, n1)), full((n1, n2)), full((n1, n2)), full((2 * n2, 2 * n2))],
        out_specs=pl.BlockSpec((None, cg, 2, n1, n2), lambda o, g: (o, g, 0, 0, 0)),
        out_shape=jax.ShapeDtypeStruct((2, HY_CH, 2, n1, n2), F32),
        scratch_shapes=[pltpu.VMEM((cg * n1, 2 * n2), F32)],
        compiler_params=_params(("parallel", "arbitrary")),
        name="hyena_filter_spectrum",
    )(nrm_flat, k4, dc["f1_full"], dc["tr"], dc["ti"], dc["w2f"])


def _hy_conv_kernel(skip_ref, x_ref, ks_ref, f1_ref, g1_ref, tr_ref, ti_ref, w2f_ref, w2i_ref,
                    o_ref, stack, *, cg, n1, n2):
    g = pl.program_id(1)
    tr = tr_ref[...]
    ti = ti_ref[...]

    def conv(slabs, order):
        x = _fft_fwd(slabs, f1_ref[...], tr, ti, w2f_ref[...], stack, n1, n2)
        for c in range(cg):
            xr, xi = x[c * n1:(c + 1) * n1, :n2], x[c * n1:(c + 1) * n1, n2:]
            kr, ki = ks_ref[order, c, 0], ks_ref[order, c, 1]
            stack[c * n1:(c + 1) * n1, 0:n2] = xr * kr - xi * ki
            stack[c * n1:(c + 1) * n1, n2:2 * n2] = xr * ki + xi * kr
        bm = _dot(stack[...], w2i_ref[...], HI)
        outs = []
        for c in range(cg):
            br, bi = bm[c * n1:(c + 1) * n1, :n2], bm[c * n1:(c + 1) * n1, n2:]
            b2 = jnp.concatenate([br * tr + bi * ti, bi * tr - br * ti], axis=0)
            y = _dot(g1_ref[...], b2, HI)
            outs.append(y + slabs[c] * skip_ref[order * HY_CH + g * cg + c])
        return outs

    v = [x_ref[2, c] for c in range(cg)]
    y0 = conv(v, 0)
    z = [x_ref[0, c] * y0[c] for c in range(cg)]
    y1 = conv(z, 1)
    for c in range(cg):
        o_ref[c] = x_ref[1, c] * y1[c]


def _hy_conv(x4, kspec, skip_flat, dc, cg, n1, n2):
    bsz = x4.shape[1]
    full = lambda shape: pl.BlockSpec(shape, lambda b, g: (0,) * len(shape))
    return pl.pallas_call(
        functools.partial(_hy_conv_kernel, cg=cg, n1=n1, n2=n2),
        grid=(bsz, HY_CH // cg),
        in_specs=[pl.BlockSpec(memory_space=pltpu.SMEM),
                  pl.BlockSpec((3, None, cg, n1 // 2, n2), lambda b, g: (0, b, g, 0, 0)),
                  pl.BlockSpec((2, cg, 2, n1, n2), lambda b, g: (0, g, 0, 0, 0)),
                  full((2 * n1, n1 // 2)), full((n1 // 2, 2 * n1)), full((n1, n2)), full((n1, n2)),
                  full((2 * n2, 2 * n2)), full((2 * n2, 2 * n2))],
        out_specs=pl.BlockSpec((None, cg, n1 // 2, n2), lambda b, g: (b, g, 0, 0)),
        out_shape=jax.ShapeDtypeStruct((bsz, HY_CH, n1 // 2, n2), F32),
        scratch_shapes=[pltpu.VMEM((cg * n1, 2 * n2), F32)],
        compiler_params=_params(("parallel", "arbitrary")),
        name="hyena_long_conv",
    )(skip_flat, x4, kspec, dc["f1_half"], dc["g1"], dc["tr"], dc["ti"], dc["w2f"], dc["w2i"])


def _hy_ctx_kernel(x_ref, k_ref, nrm_ref, skip_ref, fc_ref, fs_ref, o_ref, *, c):
    fc = fc_ref[...]
    fs = fs_ref[...]
    inv_n = 1.0 / (2 * c)

    def conv(x, order):
        kk = k_ref[order]
        kr, ki = _dot(kk, fc, HI), -_dot(kk, fs, HI)
        xr, xi = _dot(x, fc[:c], HI), -_dot(x, fs[:c], HI)
        yr, yi = xr * kr - xi * ki, xr * ki + xi * kr
        y = (_dot(yr, fc[:, :c], HI) - _dot(yi, fs[:, :c], HI)) * inv_n
        return y * nrm_ref[order] + x * skip_ref[order]

    z = x_ref[0] * conv(x_ref[2], 0)
    o_ref[...] = x_ref[1] * conv(z, 1)


def _hy_ctx_conv(xt, kt, nrm, skip_col):
    _, bsz, ch, c = xt.shape
    n = 2 * c
    ang = 2.0 * np.pi * ((np.arange(n)[:, None] * np.arange(n)[None, :]) % n) / n
    fc, fs = jnp.asarray(np.cos(ang), F32), jnp.asarray(np.sin(ang), F32)
    full = lambda shape: pl.BlockSpec(shape, lambda b: (0,) * len(shape))
    return pl.pallas_call(
        functools.partial(_hy_ctx_kernel, c=c),
        grid=(bsz,),
        in_specs=[pl.BlockSpec((3, None, ch, c), lambda b: (0, b, 0, 0)),
                  full((2, ch, n)), full((2, ch, 1)), full((2, ch, 1)), full((n, n)), full((n, n))],
        out_specs=pl.BlockSpec((None, ch, c), lambda b: (b, 0, 0)),
        out_shape=jax.ShapeDtypeStruct((bsz, ch, c), F32),
        compiler_params=_params(("arbitrary",)),
        name="hyena_ctx_conv",
    )(xt, kt, nrm, skip_col, fc, fs)


def _out_proj_kernel(a_ref, b_ref, p_ref, ht_ref, x_ref, g_ref, w_ref, o_ref):
    w = D_GROUP
    acc = _dot(a_ref[...].astype(BF16), w_ref[0:w])
    acc += _dot(b_ref[...].astype(BF16), w_ref[w:2 * w])
    acc += _dot(p_ref[...].astype(BF16), w_ref[2 * w:3 * w])
    acc += _dot(ht_ref[...].T.astype(BF16), w_ref[3 * w:4 * w])
    o_ref[...] = x_ref[...] + g_ref[...] * acc


def _out_proj(a, b, p, ht, x, mod3, row_of_batch, j_gate, w_out, tm):
    bsz, t, d = x.shape
    w = D_GROUP
    tok = pl.BlockSpec((None, tm, w), lambda bb, i: (bb, i, 0))
    return pl.pallas_call(
        _out_proj_kernel,
        grid=(bsz, t // tm),
        in_specs=[tok, tok, tok,
                  pl.BlockSpec((None, w, tm), lambda bb, i: (bb, 0, i)),
                  pl.BlockSpec((None, tm, d), lambda bb, i: (bb, i, 0)),
                  pl.BlockSpec((None, 1, d), lambda bb, i: (row_of_batch(bb), 0, j_gate)),
                  pl.BlockSpec((4 * w, d), lambda bb, i: (0, 0))],
        out_specs=pl.BlockSpec((None, tm, d), lambda bb, i: (bb, i, 0)),
        out_shape=jax.ShapeDtypeStruct((bsz, t, d), F32),
        compiler_params=_params(("parallel", "arbitrary")),
        name="out_proj_residual",
    )(a, b, p, ht, x, mod3, w_out)


def _moe_kernel(x_ref, g_ref, sc_ref, sh_ref, gate_ref, rw_ref, rb_ref, w1_ref, w3_ref, w2_ref, fg_ref,
                o_ref, h_scr, gates_scr, acc_scr, *, final):
    e = pl.program_id(2)
    per_group = N_EXPERTS // N_EXPERT_GROUPS
    tm = x_ref.shape[0]

    @pl.when(e == 0)
    def _():
        x = x_ref[...]
        ms = jnp.mean(x * x, axis=-1, keepdims=True)
        h = (x * lax.rsqrt(ms + EPS)) * g_ref[...] * (1.0 + sc_ref[...]) + sh_ref[...]
        h_scr[...] = h.astype(BF16)
        acc_scr[...] = jnp.zeros(acc_scr.shape, F32)
        lane = lax.broadcasted_iota(jnp.int32, (tm, LANES), 1)
        valid = lane < N_EXPERTS
        neg = -jnp.inf
        logits = jnp.where(valid, _dot(h, rw_ref[...], HI), neg)
        mx = jnp.max(logits, axis=1, keepdims=True)
        ex = jnp.exp(logits - mx)
        scores = ex / jnp.sum(ex, axis=1, keepdims=True)
        sel = scores + rb_ref[...]
        big = jnp.int32(LANES)
        best = None
        for g in range(N_EXPERT_GROUPS):
            in_g = (lane >= g * per_group) & (lane < (g + 1) * per_group)
            v1 = jnp.max(jnp.where(in_g, sel, neg), axis=1, keepdims=True)
            i1 = jnp.min(jnp.where(in_g & (sel == v1), lane, big), axis=1, keepdims=True)
            rest = in_g & (lane != i1)
            v2 = jnp.max(jnp.where(rest, sel, neg), axis=1, keepdims=True)
            i2 = jnp.min(jnp.where(rest & (sel == v2), lane, big), axis=1, keepdims=True)
            gs = v1 + v2
            if best is None:
                best, e1, e2 = gs, i1, i2
            else:
                upd = gs > best
                best = jnp.where(upd, gs, best)
                e1 = jnp.where(upd, i1, e1)
                e2 = jnp.where(upd, i2, e2)
        w1 = jnp.sum(jnp.where(lane == e1, scores, 0.0), axis=1, keepdims=True)
        w2 = jnp.sum(jnp.where(lane == e2, scores, 0.0), axis=1, keepdims=True)
        tot = w1 + w2
        gates_scr[...] = jnp.where(lane == e1, w1 / tot, 0.0) + jnp.where(lane == e2, w2 / tot, 0.0)

    lane = lax.broadcasted_iota(jnp.int32, (tm, LANES), 1)
    ge = jnp.sum(jnp.where(lane == e, gates_scr[...], 0.0), axis=1, keepdims=True)
    hb = h_scr[...]
    a = _dot(hb, w1_ref[...])
    b = _dot(hb, w3_ref[...])
    act = (a * jax.nn.sigmoid(a)) * b
    acc_scr[...] += ge * _dot(act.astype(BF16), w2_ref[...])

    @pl.when(e == N_EXPERTS - 1)
    def _():
        y = x_ref[...] + gate_ref[...] * acc_scr[...]
        if final:
            ms = jnp.mean(y * y, axis=-1, keepdims=True)
            y = (y * lax.rsqrt(ms + EPS)) * fg_ref[...]
        o_ref[...] = y


def _moe(x, g, mod3, row_of_batch, j_shift, j_scale, j_gate, rw, rb, w1, w3, w2, final_g, final, tm):
    bsz, t, d = x.shape
    vec = lambda j: pl.BlockSpec((None, 1, d), lambda b, i, e: (row_of_batch(b), 0, j))
    full = lambda shape: pl.BlockSpec(shape, lambda b, i, e: (0,) * len(shape))
    return pl.pallas_call(
        functools.partial(_moe_kernel, final=final),
        grid=(bsz, t // tm, N_EXPERTS),
        in_specs=[pl.BlockSpec((None, tm, d), lambda b, i, e: (b, i, 0)),
                  full((1, d)), vec(j_scale), vec(j_shift), vec(j_gate),
                  full((d, LANES)), full((1, LANES)),
                  pl.BlockSpec((None, d, D_EXPERT), lambda b, i, e: (e, 0, 0)),
                  pl.BlockSpec((None, d, D_EXPERT), lambda b, i, e: (e, 0, 0)),
                  pl.BlockSpec((None, D_EXPERT, d), lambda b, i, e: (e, 0, 0)),
                  full((1, d))],
        out_specs=pl.BlockSpec((None, tm, d), lambda b, i, e: (b, i, 0)),
        out_shape=jax.ShapeDtypeStruct((bsz, t, d), F32),
        scratch_shapes=[pltpu.VMEM((tm, d), BF16), pltpu.VMEM((tm, LANES), F32), pltpu.VMEM((tm, d), F32)],
        compiler_params=_params(("parallel", "parallel", "arbitrary")),
        name="moe_final" if final else "moe",
    )(x, g, mod3, mod3, mod3, rw, rb, w1, w3, w2, final_g)


def _tile(n, pref):
    t = min(n, pref)
    assert n % t == 0
    return t


def _fft_split(n):
    n2 = LANES
    assert n % n2 == 0
    return n // n2, n2


def _mixers(u, uc, lp, li, need_ctx):
    bsz, s, _ = u.shape
    c = uc.shape[1]
    lam_init = 0.8 - 0.6 * math.exp(-0.3 * li)
    lam_vecs = lp["a_lambda"].astype(F32)
    subln = lp["a_subln_g"].astype(F32)[None, :]

    qc, kct, vc = _attn_prep(uc, _tile(c, 256), rope=False)
    ql, klt, vl = _attn_prep(u, _tile(s, 256), rope=True)
    a_l = _diff_attn(ql, kct, vc, klt, vl, lam_vecs, subln, lam_init, _tile(s, 512), _tile(s, 512))
    bias8 = _nbr_bias(lp["b_rpb"])
    b_l = _nbr_attn(u, uc, bias8)
    wbd = jax.scipy.linalg.block_diag(*[lp["pool_w"][g] for g in range(len(POOL_SIZES))]).astype(BF16)
    pscale = lp["pool_scale"].astype(F32)[None, :]
    p_l = _pool_mix(u, wbd, pscale, _tile(s, 512))
    fargs = (lp["hy_f_w1"], lp["hy_f_b1"], lp["hy_f_w2"], lp["hy_f_b2"], lp["hy_f_w3"], lp["hy_f_b3"])
    skip = lp["hy_skip"].astype(F32)
    w_short = lp["hy_short_w"].astype(F32)
    b_short = lp["hy_short_b"].astype(F32)[None, :]
    n1, n2 = _fft_split(2 * s)
    dc = _dft_consts(n1, n2)
    cg = 8
    kt_l, _, nrm_l = _hy_filters(s, *fargs, tp=_tile(2 * s, 1024))
    kspec = _hy_spec(kt_l.reshape(2, HY_CH, n1, n2), nrm_l.reshape(2 * HY_CH), dc, cg, n1, n2)
    xt = _hy_short(u, w_short, b_short, _tile(s, 512))
    h_l = _hy_conv(xt.reshape(3, bsz, HY_CH, n1 // 2, n2), kspec, skip.reshape(2 * HY_CH), dc, cg, n1, n2)
    h_l = h_l.reshape(bsz, HY_CH, s)
    lat = (a_l, b_l, p_l, h_l)
    if not need_ctx:
        return lat, None
    a_c = _diff_attn(qc, kct, vc, None, None, lam_vecs, subln, lam_init, _tile(c, 256), None)
    b_c = _nbr_ctx_attn(uc)
    p_c = _pool_mix(uc, wbd, pscale, _tile(c, 256))
    kt_c, _, nrm_c = _hy_filters(c, *fargs, tp=_tile(2 * c, 512))
    xtc = _hy_short(uc, w_short, b_short, _tile(c, 256))
    h_c = _hy_ctx_conv(xtc, kt_c, nrm_c, skip[:, :, None])
    return lat, (a_c, b_c, p_c, h_c)


def kernel(x, c, ctx, c_ctx, norm1_g, norm2_g, ada_w, ada_b, w_in, w_out, a_lambda, a_subln_g, b_rpb, pool_w, pool_scale, hy_short_w, hy_short_b, hy_f_w1, hy_f_b1, hy_f_w2, hy_f_b2, hy_f_w3, hy_f_b3, hy_skip, router_w, router_b, moe_w1, moe_w3, moe_w2, final_g):
    depth = norm1_g.shape[0]
    bsz, s, d = x.shape
    cl = ctx.shape[1]
    assert bsz <= SUBLANES - 1
    xl, xc = x, ctx
    cpad = jnp.zeros((SUBLANES, d), F32).at[:bsz].set(c.astype(F32)).at[bsz].set(c_ctx.astype(F32))
    rw = jnp.pad(router_w.astype(F32), ((0, 0), (0, LANES - N_EXPERTS)))
    rb = jnp.pad(router_b.astype(F32), (0, LANES - N_EXPERTS))[None, :]
    lat_row = lambda b: b
    ctx_row = lambda b: bsz
    fg = final_g.astype(F32)[None, :]
    tm = _tile(s, 512)
    tmc = _tile(cl, 256)
    for li in range(depth):
        need_ctx = li < depth - 1
        lp = dict(a_lambda=a_lambda[li], a_subln_g=a_subln_g[li], b_rpb=b_rpb[li], pool_w=pool_w[li],
                  pool_scale=pool_scale[li], hy_short_w=hy_short_w[li], hy_short_b=hy_short_b[li],
                  hy_f_w1=hy_f_w1[li], hy_f_b1=hy_f_b1[li], hy_f_w2=hy_f_w2[li], hy_f_b2=hy_f_b2[li],
                  hy_f_w3=hy_f_w3[li], hy_f_b3=hy_f_b3[li], hy_skip=hy_skip[li])
        mod3 = _ada(cpad, ada_w[li].astype(F32), ada_b[li].astype(F32)[None, :]).reshape(SUBLANES, 1, 6 * d)
        n1g = norm1_g[li].astype(F32)[None, :]
        n2g = norm2_g[li].astype(F32)[None, :]
        w_in_b = w_in[li].astype(BF16)
        w_out_b = w_out[li].astype(BF16)
        u = _norm_proj(xl, n1g, mod3, lat_row, 0, 1, w_in_b, tm)
        uc = _norm_proj(xc, n1g, mod3, ctx_row, 0, 1, w_in_b, tmc)
        lat, cx = _mixers(u, uc, lp, li, need_ctx)
        xl = _out_proj(*lat, xl, mod3, lat_row, 2, w_out_b, tm)
        w1b, w3b, w2b = moe_w1[li].astype(BF16), moe_w3[li].astype(BF16), moe_w2[li].astype(BF16)
        if need_ctx:
            xc = _out_proj(*cx, xc, mod3, ctx_row, 2, w_out_b, tmc)
            xc = _moe(xc, n2g, mod3, ctx_row, 3, 4, 5, rw, rb, w1b, w3b, w2b, fg, False, tmc)
        xl = _moe(xl, n2g, mod3, lat_row, 3, 4, 5, rw, rb, w1b, w3b, w2b, fg, li == depth - 1, tm)
    return xl
```

```python
import functools
import math

import numpy as np
import jax
import jax.numpy as jnp
from jax import lax
from jax.experimental import pallas as pl
from jax.experimental.pallas import tpu as pltpu

F32 = jnp.float32
BF16 = jnp.bfloat16
HI = lax.Precision.HIGHEST

GRID_W = 64
A_HEADS = 4
A_QK = 32
A_V = 64
ROPE_BASE = 10000.0
B_HEADS = 4
B_DIM = 64
WIN_R = 8
WIN_C = 16
POOL_SIZES = (2, 4, 8, 16)
POOL_CH = 64
D_GROUP = 256
HY_CH = 256
HY_BANDS = 16
HY_EMB = 1 + 2 * HY_BANDS
HY_HIDDEN = 64
HY_SIN_FREQ = 1.0
HY_MIN_DECAY = math.log(1e-2) / 1.5
HY_MAX_DECAY = math.log(1e-2) / 0.3
N_EXPERTS = 16
N_EXPERT_GROUPS = 4
D_EXPERT = 512
EPS = 1e-6
LOG2E = 1.4426950408889634

LANES = 128
SUBLANES = 8
VMEM_LIMIT = 56 * 1024 * 1024


def _params(sem):
    return pltpu.CompilerParams(dimension_semantics=sem, vmem_limit_bytes=VMEM_LIMIT)


def _dot(a, b, prec=None):
    return jnp.dot(a, b, precision=prec, preferred_element_type=F32)


def _dot_nt(a, b):
    return lax.dot_general(a, b, (((1,), (1,)), ((), ())), preferred_element_type=F32)


def _ada_kernel(c_ref, w_ref, b_ref, o_ref):
    cf = c_ref[...]
    s = cf * jax.nn.sigmoid(cf)
    o_ref[...] = _dot(s, w_ref[...], HI) + b_ref[...]


def _ada(cpad, w, b):
    d = cpad.shape[1]
    n = w.shape[1]
    return pl.pallas_call(
        _ada_kernel,
        grid=(n // d,),
        in_specs=[pl.BlockSpec((SUBLANES, d), lambda j: (0, 0)),
                  pl.BlockSpec((d, d), lambda j: (0, j)),
                  pl.BlockSpec((1, d), lambda j: (0, j))],
        out_specs=pl.BlockSpec((SUBLANES, d), lambda j: (0, j)),
        out_shape=jax.ShapeDtypeStruct((SUBLANES, n), F32),
        compiler_params=_params(("arbitrary",)),
        name="ada_mod",
    )(cpad, w, b)


def _norm_proj_kernel(x_ref, g_ref, sc_ref, sh_ref, w_ref, o_ref):
    x = x_ref[...]
    ms = jnp.mean(x * x, axis=-1, keepdims=True)
    h = (x * lax.rsqrt(ms + EPS)) * g_ref[...] * (1.0 + sc_ref[...]) + sh_ref[...]
    o_ref[...] = _dot(h.astype(BF16), w_ref[...])


def _norm_proj(x, g, mod3, row_of_batch, j_shift, j_scale, w, tm):
    bsz, t, d = x.shape
    n = w.shape[1]
    return pl.pallas_call(
        _norm_proj_kernel,
        grid=(bsz, t // tm),
        in_specs=[pl.BlockSpec((None, tm, d), lambda b, i: (b, i, 0)),
                  pl.BlockSpec((1, d), lambda b, i: (0, 0)),
                  pl.BlockSpec((None, 1, d), lambda b, i: (row_of_batch(b), 0, j_scale)),
                  pl.BlockSpec((None, 1, d), lambda b, i: (row_of_batch(b), 0, j_shift)),
                  pl.BlockSpec((d, n), lambda b, i: (0, 0))],
        out_specs=pl.BlockSpec((None, tm, n), lambda b, i: (b, i, 0)),
        out_shape=jax.ShapeDtypeStruct((bsz, t, n), F32),
        compiler_params=_params(("parallel", "arbitrary")),
        name="norm_in_proj",
    )(x, g, mod3, mod3, w)


def _aprep_kernel(*refs, rope):
    if rope:
        u_ref, cos_ref, sin_ref, q_ref, kt_ref, v_ref = refs
    else:
        u_ref, q_ref, kt_ref, v_ref = refs
    u = u_ref[...]
    q = u[:, 0:256]
    k = u[:, 256:512]
    v = u[:, 512:768]
    if rope:
        cos_t = cos_ref[...]
        sin_t = sin_ref[...]
        lane = lax.broadcasted_iota(jnp.int32, cos_t.shape, 1)
        first = (lane % (2 * 16)) < 16

        def rot(x):
            halves = []
            for j in range(2):
                xh = x[:, j * LANES:(j + 1) * LANES]
                swap = jnp.where(first, pltpu.roll(xh, LANES - 16, axis=1), pltpu.roll(xh, 16, axis=1))
                halves.append(xh * cos_t + swap * sin_t)
            return jnp.concatenate(halves, axis=1)

        q = rot(q)
        k = rot(k)
    q = q * (A_QK ** -0.5 * LOG2E)
    kt = k.T
    for hc in range(2 * A_HEADS):
        q_ref[hc] = q[:, hc * A_QK:(hc + 1) * A_QK].astype(BF16)
        kt_ref[hc] = kt[hc * A_QK:(hc + 1) * A_QK, :].astype(BF16)
    lane = lax.broadcasted_iota(jnp.int32, (v.shape[0], LANES - A_V), 1)
    ones_col = jnp.where(lane == 0, 1.0, 0.0)
    for h in range(A_HEADS):
        v_ref[h] = jnp.concatenate([v[:, h * A_V:(h + 1) * A_V], ones_col], axis=1).astype(BF16)


def _rope_tables(length):
    n_freq = A_QK // 4
    inv = ROPE_BASE ** (-jnp.arange(n_freq, dtype=F32) / n_freq)
    t = jnp.arange(length)
    row = (t // GRID_W).astype(F32)
    col = (t % GRID_W).astype(F32)
    ang = jnp.concatenate([row[:, None] * inv, col[:, None] * inv], axis=-1)
    cos, sin = jnp.cos(ang), jnp.sin(ang)
    cos_t = jnp.tile(jnp.concatenate([cos, cos], axis=-1), (1, LANES // 32))
    sin_t = jnp.tile(jnp.concatenate([-sin, sin], axis=-1), (1, LANES // 32))
    return cos_t, sin_t


def _attn_prep(u, tm, rope):
    bsz, t, _ = u.shape
    nh = 2 * A_HEADS
    in_specs = [pl.BlockSpec((None, tm, 768), lambda b, i: (b, i, 0))]
    args = [u]
    if rope:
        cos_t, sin_t = _rope_tables(t)
        in_specs += [pl.BlockSpec((tm, LANES), lambda b, i: (i, 0))] * 2
        args += [cos_t, sin_t]
    return pl.pallas_call(
        functools.partial(_aprep_kernel, rope=rope),
        grid=(bsz, t // tm),
        in_specs=in_specs,
        out_specs=[pl.BlockSpec((None, nh, tm, A_QK), lambda b, i: (b, 0, i, 0)),
                   pl.BlockSpec((None, nh, A_QK, tm), lambda b, i: (b, 0, 0, i)),
                   pl.BlockSpec((None, A_HEADS, tm, LANES), lambda b, i: (b, 0, i, 0))],
        out_shape=[jax.ShapeDtypeStruct((bsz, nh, t, A_QK), BF16),
                   jax.ShapeDtypeStruct((bsz, nh, A_QK, t), BF16),
                   jax.ShapeDtypeStruct((bsz, A_HEADS, t, LANES), BF16)],
        compiler_params=_params(("parallel", "arbitrary")),
        name="attn_prep_rope" if rope else "attn_prep_ctx",
    )(*args)


QK_LOOKAHEAD = 3


def _dattn_kernel(*refs, lam_init, has_lat):
    if has_lat:
        lam_ref, g_ref, q_ref, kc_ref, vc_ref, k_ref, v_ref, o_ref, m_scr, acc_scr = refs
    else:
        lam_ref, g_ref, q_ref, kc_ref, vc_ref, o_ref, m_scr, acc_scr = refs
    ki = pl.program_id(2)
    nk = pl.num_programs(2)
    nh = 2 * A_HEADS

    def update(kt_r, v_r):
        scores = [_dot(q_ref[j], kt_r[j]) for j in range(QK_LOOKAHEAD)]
        for hc in range(nh):
            s = scores[hc]
            if hc + QK_LOOKAHEAD < nh:
                scores.append(_dot(q_ref[hc + QK_LOOKAHEAD], kt_r[hc + QK_LOOKAHEAD]))
            m_prev = m_scr[hc]
            m_new = jnp.maximum(m_prev, jnp.max(s, axis=1, keepdims=True))
            alpha = jnp.exp2(m_prev - m_new)
            p = jnp.exp2((s - m_new[:, :1]).astype(BF16))
            acc_scr[hc] = alpha * acc_scr[hc] + _dot(p, v_r[hc // 2])
            m_scr[hc] = m_new

    @pl.when(ki == 0)
    def _():
        m_scr[...] = jnp.full(m_scr.shape, -jnp.inf, F32)
        acc_scr[...] = jnp.zeros(acc_scr.shape, F32)
        update(kc_ref, vc_ref)

    if has_lat:
        @pl.when(ki > 0)
        def _():
            update(k_ref, v_ref)

    @pl.when(ki == nk - 1)
    def _():
        lv = lam_ref[...]
        lam = (jnp.exp(jnp.sum(lv[0:1] * lv[1:2], axis=1, keepdims=True))
               - jnp.exp(jnp.sum(lv[2:3] * lv[3:4], axis=1, keepdims=True)) + lam_init)
        for h in range(A_HEADS):
            a0 = acc_scr[2 * h]
            a1 = acc_scr[2 * h + 1]
            o = a0[:, :A_V] / a0[:, A_V:A_V + 1] - lam * (a1[:, :A_V] / a1[:, A_V:A_V + 1])
            ms = jnp.mean(o * o, axis=-1, keepdims=True)
            o_ref[:, h * A_V:(h + 1) * A_V] = (o * lax.rsqrt(ms + EPS)) * g_ref[...] * (1.0 - lam_init)


def _diff_attn(q, kct, vc, kt, v, lam_vecs, subln_g, lam_init, tq, tk):
    bsz, nh, t, _ = q.shape
    c = kct.shape[-1]
    has_lat = kt is not None
    nk = 1 + (kt.shape[-1] // tk if has_lat else 0)
    in_specs = [pl.BlockSpec((4, A_QK), lambda b, i, k: (0, 0)),
                pl.BlockSpec((1, A_V), lambda b, i, k: (0, 0)),
                pl.BlockSpec((None, nh, tq, A_QK), lambda b, i, k: (b, 0, i, 0)),
                pl.BlockSpec((None, nh, A_QK, c), lambda b, i, k: (b, 0, 0, 0)),
                pl.BlockSpec((None, A_HEADS, c, LANES), lambda b, i, k: (b, 0, 0, 0))]
    args = [lam_vecs, subln_g, q, kct, vc]
    if has_lat:
        in_specs += [pl.BlockSpec((None, nh, A_QK, tk), lambda b, i, k: (b, 0, 0, jnp.maximum(k - 1, 0))),
                     pl.BlockSpec((None, A_HEADS, tk, LANES), lambda b, i, k: (b, 0, jnp.maximum(k - 1, 0), 0))]
        args += [kt, v]
    return pl.pallas_call(
        functools.partial(_dattn_kernel, lam_init=lam_init, has_lat=has_lat),
        grid=(bsz, t // tq, nk),
        in_specs=in_specs,
        out_specs=pl.BlockSpec((None, tq, A_HEADS * A_V), lambda b, i, k: (b, i, 0)),
        out_shape=jax.ShapeDtypeStruct((bsz, t, A_HEADS * A_V), F32),
        scratch_shapes=[pltpu.VMEM((nh, tq, LANES), F32),
                        pltpu.VMEM((nh, tq, LANES), F32)],
        compiler_params=_params(("parallel", "parallel", "arbitrary")),
        name="diff_attn" if has_lat else "diff_attn_ctx",
    )(*args)


NB_ROWS = 8


def _nbr_bias(rpb):
    cols = jnp.arange(GRID_W)
    c0 = jnp.clip(cols - WIN_C // 2, 0, GRID_W - WIN_C)
    in_win = (cols[None, :] >= c0[:, None]) & (cols[None, :] < c0[:, None] + WIN_C)
    dc = jnp.clip(cols[None, :] - cols[:, None], -(WIN_C - 1), WIN_C - 1) + (WIN_C - 1)
    onehot = (dc[None] == jnp.arange(2 * WIN_C - 1)[:, None, None]).astype(F32)
    g = jnp.einsum("hab,bqk->haqk", rpb.astype(F32), onehot, precision=HI)
    g = jnp.where(in_win[None, None], g, -jnp.inf)
    b = jnp.stack([g[:, a0:a0 + WIN_R] for a0 in range(WIN_R)], axis=0)
    b = jnp.transpose(b, (0, 1, 3, 2, 4))
    return b.reshape(WIN_R, B_HEADS, GRID_W, WIN_R * GRID_W)


def _nbr_kernel(q_ref, kp_ref, kc_ref, kn_ref, vp_ref, vcur_ref, vn_ref, kctx_ref, vctx_ref, bias_ref,
                o_ref, kwin, vwin, kcx, vcx, *, n_rows):
    rb = pl.program_id(1)
    blk = NB_ROWS * GRID_W
    scale = B_DIM ** -0.5
    for h in range(B_HEADS):
        sl = slice(h * B_DIM, (h + 1) * B_DIM)
        for j, (kr, vr) in enumerate(((kp_ref, vp_ref), (kc_ref, vcur_ref), (kn_ref, vn_ref))):
            kwin[h, j * blk:(j + 1) * blk, :] = kr[:, sl].astype(BF16)
            vwin[h, j * blk:(j + 1) * blk, :] = vr[:, sl].astype(BF16)
        kcx[h] = kctx_ref[:, sl].astype(BF16)
        vcx[h] = vctx_ref[:, sl].astype(BF16)

    def row_body(rr, carry):
        r = rb * NB_ROWS + rr
        r0 = jnp.clip(r - WIN_R // 2, 0, n_rows - WIN_R)
        off = pl.multiple_of((r0 - (rb - 1) * NB_ROWS) * GRID_W, GRID_W)
        a0 = r0 - r + (WIN_R - 1)
        qrow = q_ref[pl.ds(pl.multiple_of(rr * GRID_W, GRID_W), GRID_W), :]
        outs = []
        for h in range(B_HEADS):
            qh = qrow[:, h * B_DIM:(h + 1) * B_DIM].astype(BF16)
            kr = kwin[h, pl.ds(off, WIN_R * GRID_W), :]
            vr = vwin[h, pl.ds(off, WIN_R * GRID_W), :]
            s = _dot_nt(qh, kr) * scale + bias_ref[a0, h]
            sc = _dot_nt(qh, kcx[h]) * scale
            m = jnp.maximum(jnp.max(s, axis=1, keepdims=True), jnp.max(sc, axis=1, keepdims=True))
            p = jnp.exp(s - m)
            pc = jnp.exp(sc - m)
            l = jnp.sum(p, axis=1, keepdims=True) + jnp.sum(pc, axis=1, keepdims=True)
            o = _dot(p.astype(BF16), vr) + _dot(pc.astype(BF16), vcx[h])
            outs.append(o / l)
        o_ref[pl.ds(pl.multiple_of(rr * GRID_W, GRID_W), GRID_W), :] = jnp.concatenate(outs, axis=1)
        return carry

    lax.fori_loop(0, NB_ROWS, row_body, 0)


def _nbr_attn(u, uc, bias8):
    bsz, s, _ = u.shape
    c = uc.shape[1]
    n_rows = s // GRID_W
    nb = n_rows // NB_ROWS
    blk = NB_ROWS * GRID_W
    w = B_HEADS * B_DIM

    def spec(col, shift):
        return pl.BlockSpec((None, blk, w), lambda b, i: (b, jnp.clip(i + shift, 0, nb - 1), col))

    return pl.pallas_call(
        functools.partial(_nbr_kernel, n_rows=n_rows),
        grid=(bsz, nb),
        in_specs=[spec(3, 0), spec(4, -1), spec(4, 0), spec(4, 1), spec(5, -1), spec(5, 0), spec(5, 1),
                  pl.BlockSpec((None, c, w), lambda b, i: (b, 0, 4)),
                  pl.BlockSpec((None, c, w), lambda b, i: (b, 0, 5)),
                  pl.BlockSpec(bias8.shape, lambda b, i: (0, 0, 0, 0))],
        out_specs=pl.BlockSpec((None, blk, w), lambda b, i: (b, i, 0)),
        out_shape=jax.ShapeDtypeStruct((bsz, s, w), F32),
        scratch_shapes=[pltpu.VMEM((B_HEADS, 3 * blk, B_DIM), BF16),
                        pltpu.VMEM((B_HEADS, 3 * blk, B_DIM), BF16),
                        pltpu.VMEM((B_HEADS, c, B_DIM), BF16),
                        pltpu.VMEM((B_HEADS, c, B_DIM), BF16)],
        compiler_params=_params(("parallel", "arbitrary")),
        name="nbr_attn",
    )(u, u, u, u, u, u, u, uc, uc, bias8)


def _nbr_ctx_kernel(q_ref, k_ref, v_ref, o_ref):
    scale = B_DIM ** -0.5
    outs = []
    for h in range(B_HEADS):
        sl = slice(h * B_DIM, (h + 1) * B_DIM)
        s = _dot_nt(q_ref[:, sl].astype(BF16), k_ref[:, sl].astype(BF16)) * scale
        m = jnp.max(s, axis=1, keepdims=True)
        p = jnp.exp(s - m)
        l = jnp.sum(p, axis=1, keepdims=True)
        outs.append(_dot(p.astype(BF16), v_ref[:, sl].astype(BF16)) / l)
    o_ref[...] = jnp.concatenate(outs, axis=1)


def _nbr_ctx_attn(uc):
    bsz, c, _ = uc.shape
    w = B_HEADS * B_DIM
    return pl.pallas_call(
        _nbr_ctx_kernel,
        grid=(bsz,),
        in_specs=[pl.BlockSpec((None, c, w), lambda b: (b, 0, 3)),
                  pl.BlockSpec((None, c, w), lambda b: (b, 0, 4)),
                  pl.BlockSpec((None, c, w), lambda b: (b, 0, 5))],
        out_specs=pl.BlockSpec((None, c, w), lambda b: (b, 0, 0)),
        out_shape=jax.ShapeDtypeStruct((bsz, c, w), F32),
        compiler_params=_params(("arbitrary",)),
        name="nbr_attn_ctx",
    )(uc, uc, uc)


HALO = SUBLANES


def _halo_specs(tm, length, col, width):
    nt = length // tm
    per = tm // HALO
    last = length // HALO - 1
    return [pl.BlockSpec((None, HALO, width), lambda b, i, *_: (b, jnp.maximum(i * per - 1, 0), col)),
            pl.BlockSpec((None, tm, width), lambda b, i, *_: (b, i, col)),
            pl.BlockSpec((None, HALO, width), lambda b, i, *_: (b, jnp.minimum((i + 1) * per, last), col))], nt


def _fill_halo(buf, prev_ref, cur_ref, next_ref, i, nt, tm):
    zero = jnp.zeros(prev_ref.shape, F32)
    buf[0:HALO, :] = jnp.where(i > 0, prev_ref[...], zero)
    buf[HALO:HALO + tm, :] = cur_ref[...]
    buf[HALO + tm:, :] = jnp.where(i < nt - 1, next_ref[...], zero)


def _pool_kernel(prev_ref, cur_ref, next_ref, w_ref, ps_ref, o_ref, buf, *, tm, nt, length):
    i = pl.program_id(1)
    _fill_halo(buf, prev_ref, cur_ref, next_ref, i, nt, tm)

    def sh(j):
        return buf[HALO + j:HALO + j + tm, :]

    u = sh(0)
    sums = []
    acc = None
    lo, hi = 0, 0
    for w in POOL_SIZES:
        for j in list(range(-(w // 2), lo)) + list(range(hi, w // 2)):
            acc = sh(j) if acc is None else acc + sh(j)
        lo, hi = -(w // 2), w // 2
        sums.append(acc)
    lane = lax.broadcasted_iota(jnp.int32, (tm, D_GROUP), 1)
    t = (i * tm + lax.broadcasted_iota(jnp.int32, (tm, D_GROUP), 0))
    wsum = sums[-1]
    half = jnp.full((tm, D_GROUP), POOL_SIZES[-1] // 2, jnp.int32)
    for g in range(len(POOL_SIZES) - 2, -1, -1):
        sel = lane < (g + 1) * POOL_CH
        wsum = jnp.where(sel, sums[g], wsum)
        half = jnp.where(sel, POOL_SIZES[g] // 2, half)
    cnt = (jnp.minimum(t + half, length) - jnp.maximum(t - half, 0)).astype(F32)
    d = wsum / cnt - u
    o_ref[...] = _dot(d.astype(BF16), w_ref[...]) * ps_ref[...]


def _pool_mix(u, wbd, pool_scale, tm):
    bsz, length, _ = u.shape
    specs, nt = _halo_specs(tm, length, 6, D_GROUP)
    return pl.pallas_call(
        functools.partial(_pool_kernel, tm=tm, nt=nt, length=length),
        grid=(bsz, nt),
        in_specs=specs + [pl.BlockSpec((D_GROUP, D_GROUP), lambda b, i: (0, 0)),
                          pl.BlockSpec((1, D_GROUP), lambda b, i: (0, 0))],
        out_specs=pl.BlockSpec((None, tm, D_GROUP), lambda b, i: (b, i, 0)),
        out_shape=jax.ShapeDtypeStruct((bsz, length, D_GROUP), F32),
        scratch_shapes=[pltpu.VMEM((tm + 2 * HALO, D_GROUP), F32)],
        compiler_params=_params(("parallel", "arbitrary")),
        name="pool_mix",
    )(u, u, u, wbd, pool_scale)


def _hy_short_kernel(prev_ref, cur_ref, next_ref, w_ref, b_ref, o_ref, buf, *, tm, nt):
    i = pl.program_id(1)
    _fill_halo(buf, prev_ref, cur_ref, next_ref, i, nt, tm)
    w = w_ref[...]
    y = (buf[HALO - 1:HALO - 1 + tm, :] * w[0:1] + buf[HALO:HALO + tm, :] * w[1:2]
         + buf[HALO + 1:HALO + 1 + tm, :] * w[2:3] + b_ref[...])
    o_ref[...] = y.T


def _hy_short(u, w_short, b_short, tm):
    bsz, length, _ = u.shape
    nt = length // tm
    per = tm // HALO
    last = length // HALO - 1
    c0 = 7
    in_specs = [pl.BlockSpec((None, HALO, HY_CH), lambda b, i, j: (b, jnp.maximum(i * per - 1, 0), c0 + j)),
                pl.BlockSpec((None, tm, HY_CH), lambda b, i, j: (b, i, c0 + j)),
                pl.BlockSpec((None, HALO, HY_CH), lambda b, i, j: (b, jnp.minimum((i + 1) * per, last), c0 + j)),
                pl.BlockSpec((3, HY_CH), lambda b, i, j: (0, j)),
                pl.BlockSpec((1, HY_CH), lambda b, i, j: (0, j))]
    return pl.pallas_call(
        functools.partial(_hy_short_kernel, tm=tm, nt=nt),
        grid=(bsz, nt, 3),
        in_specs=in_specs,
        out_specs=pl.BlockSpec((None, None, HY_CH, tm), lambda b, i, j: (j, b, 0, i)),
        out_shape=jax.ShapeDtypeStruct((3, bsz, HY_CH, length), F32),
        scratch_shapes=[pltpu.VMEM((tm + 2 * HALO, HY_CH), F32)],
        compiler_params=_params(("parallel", "arbitrary", "arbitrary")),
        name="hyena_short_conv",
    )(u, u, u, w_short, b_short)


HY_FEAT = 40


def _hy_filter_kernel(band_ref, w1_ref, b1_ref, w2_ref, b2_ref, w3_ref, b3_ref, dl_ref,
                      k_ref, ssq_ref, nrm_ref, *, tp, length):
    i = pl.program_id(0)
    n_i = pl.num_programs(0)
    m = i * tp + lax.broadcasted_iota(jnp.int32, (1, tp), 1)
    t = jnp.where(m <= length, m, 2 * length - m).astype(F32)
    t_norm = t / max(length - 1, 1)
    ang = ((2.0 * math.pi / length) * t) * band_ref[...]
    row = lax.broadcasted_iota(jnp.int32, (HY_FEAT, tp), 0)
    z = jnp.where(row == 0, t_norm,
                  jnp.where(row <= HY_BANDS, jnp.cos(ang), jnp.where(row < HY_EMB, jnp.sin(ang), 0.0)))
    z = jnp.concatenate([z, jnp.zeros((LANES - HY_FEAT, tp), F32)], axis=0)
    h = jnp.sin(HY_SIN_FREQ * (_dot(w1_ref[...], z, HI) + b1_ref[...]))
    h = jnp.sin(HY_SIN_FREQ * (_dot(w2_ref[...], h, HI) + b2_ref[...]))
    h = _dot(w3_ref[...], h, HI) + b3_ref[...]
    h = h * jnp.exp(-t_norm * dl_ref[...])

    @pl.when(i == 0)
    def _():
        ssq_ref[...] = jnp.zeros(ssq_ref.shape, F32)

    for o in range(2):
        fwd = h[o * 2 * HY_CH:o * 2 * HY_CH + HY_CH]
        bwd = h[o * 2 * HY_CH + HY_CH:(o + 1) * 2 * HY_CH]
        k = jnp.where(m < length, fwd, jnp.where(m == length, 0.0, bwd))
        k_ref[o] = k
        extra = jnp.where(m == 0, bwd * bwd, 0.0)
        ssq_ref[o] += jnp.sum(k * k + extra, axis=1, keepdims=True)

    @pl.when(i == n_i - 1)
    def _():
        nrm_ref[...] = lax.rsqrt(ssq_ref[...] + EPS)


def _hy_filters(length, w1, b1, w2, b2, w3, b3, tp):
    bands = jnp.linspace(1e-4, HY_BANDS - 1, HY_BANDS, dtype=F32)
    band_col = jnp.concatenate([jnp.zeros((1,), F32), bands, bands,
                                jnp.zeros((HY_FEAT - HY_EMB,), F32)])[:, None]
    deltas = jnp.abs(jnp.linspace(HY_MIN_DECAY, HY_MAX_DECAY, HY_CH, dtype=F32))
    dl_col = jnp.tile(deltas, 4)[:, None]
    w1t = jnp.pad(w1.astype(F32).T, ((0, 0), (0, LANES - HY_EMB)))
    full = lambda shape: pl.BlockSpec(shape, lambda i: (0,) * len(shape))
    n = 2 * length
    return pl.pallas_call(
        functools.partial(_hy_filter_kernel, tp=tp, length=length),
        grid=(n // tp,),
        in_specs=[full((HY_FEAT, 1)), full((HY_HIDDEN, LANES)), full((HY_HIDDEN, 1)),
                  full((HY_HIDDEN, HY_HIDDEN)), full((HY_HIDDEN, 1)),
                  full((4 * HY_CH, HY_HIDDEN)), full((4 * HY_CH, 1)), full((4 * HY_CH, 1))],
        out_specs=[pl.BlockSpec((2, HY_CH, tp), lambda i: (0, 0, i)),
                   full((2, HY_CH, 1)), full((2, HY_CH, 1))],
        out_shape=[jax.ShapeDtypeStruct((2, HY_CH, n), F32),
                   jax.ShapeDtypeStruct((2, HY_CH, 1), F32),
                   jax.ShapeDtypeStruct((2, HY_CH, 1), F32)],
        compiler_params=_params(("arbitrary",)),
        name="hyena_filters",
    )(band_col, w1t, b1.astype(F32)[:, None], w2.astype(F32).T, b2.astype(F32)[:, None],
      w3.astype(F32).T, b3.astype(F32)[:, None], dl_col)


def _dft_consts(n1, n2):
    n = n1 * n2
    a1 = 2.0 * np.pi * ((np.arange(n1)[:, None] * np.arange(n1)[None, :]) % n1) / n1
    c1, s1 = np.cos(a1), np.sin(a1)
    a2 = 2.0 * np.pi * ((np.arange(n2)[:, None] * np.arange(n2)[None, :]) % n2) / n2
    c2, s2 = np.cos(a2), np.sin(a2)
    at = 2.0 * np.pi * ((np.arange(n1)[:, None] * np.arange(n2)[None, :]) % n) / n
    f1_full = np.concatenate([c1, -s1], axis=0)
    f1_half = f1_full[:, :n1 // 2]
    g1 = np.concatenate([c1[:n1 // 2], -s1[:n1 // 2]], axis=1)
    w2f = np.block([[c2, -s2], [s2, c2]])
    w2i = np.block([[c2, s2], [-s2, c2]])
    f = lambda x: jnp.asarray(x, F32)
    return dict(f1_full=f(f1_full), f1_half=f(f1_half), g1=f(g1), w2f=f(w2f), w2i=f(w2i),
                tr=f(np.cos(at)), ti=f(-np.sin(at)))


def _fft_fwd(slabs, f1, tr, ti, w2f, stack, n1, n2):
    for c, x in enumerate(slabs):
        a = _dot(f1, x, HI)
        ar, ai = a[:n1], a[n1:]
        stack[c * n1:(c + 1) * n1, 0:n2] = ar * tr - ai * ti
        stack[c * n1:(c + 1) * n1, n2:2 * n2] = ar * ti + ai * tr
    return _dot(stack[...], w2f, HI)


def _hy_spec_kernel(nrm_ref, k_ref, f1_ref, tr_ref, ti_ref, w2f_ref, o_ref, stack, *, cg, n1, n2):
    o = pl.program_id(0)
    g = pl.program_id(1)
    x = _fft_fwd([k_ref[c] for c in range(cg)], f1_ref[...], tr_ref[...], ti_ref[...], w2f_ref[...],
                 stack, n1, n2)
    for c in range(cg):
        sc = nrm_ref[o * HY_CH + g * cg + c] * (1.0 / (n1 * n2))
        xc = x[c * n1:(c + 1) * n1] * sc
        o_ref[c, 0] = xc[:, :n2]
        o_ref[c, 1] = xc[:, n2:]


def _hy_spec(k4, nrm_flat, dc, cg, n1, n2):
    full = lambda shape: pl.BlockSpec(shape, lambda o, g: (0,) * len(shape))
    return pl.pallas_call(
        functools.partial(_hy_spec_kernel, cg=cg, n1=n1, n2=n2),
        grid=(2, HY_CH // cg),
        in_specs=[pl.BlockSpec(memory_space=pltpu.SMEM),
                  pl.BlockSpec((None, cg, n1, n2), lambda o, g: (o, g, 0, 0)),
                  full((2 * n1, n1)), full((n1, n2)), full((n1, n2)), full((2 * n2, 2 * n2))],
        out_specs=pl.BlockSpec((None, cg, 2, n1, n2), lambda o, g: (o, g, 0, 0, 0)),
        out_shape=jax.ShapeDtypeStruct((2, HY_CH, 2, n1, n2), F32),
        scratch_shapes=[pltpu.VMEM((cg * n1, 2 * n2), F32)],
        compiler_params=_params(("parallel", "arbitrary")),
        name="hyena_filter_spectrum",
    )(nrm_flat, k4, dc["f1_full"], dc["tr"], dc["ti"], dc["w2f"])


def _hy_conv_kernel(skip_ref, x_ref, ks_ref, f1_ref, g1_ref, tr_ref, ti_ref, w2f_ref, w2i_ref,
                    o_ref, stack, *, cg, n1, n2):
    g = pl.program_id(1)
    tr = tr_ref[...]
    ti = ti_ref[...]

    def conv(slabs, order):
        x = _fft_fwd(slabs, f1_ref[...], tr, ti, w2f_ref[...], stack, n1, n2)
        for c in range(cg):
            xr, xi = x[c * n1:(c + 1) * n1, :n2], x[c * n1:(c + 1) * n1, n2:]
            kr, ki = ks_ref[order, c, 0], ks_ref[order, c, 1]
            stack[c * n1:(c + 1) * n1, 0:n2] = xr * kr - xi * ki
            stack[c * n1:(c + 1) * n1, n2:2 * n2] = xr * ki + xi * kr
        bm = _dot(stack[...], w2i_ref[...], HI)
        outs = []
        for c in range(cg):
            br, bi = bm[c * n1:(c + 1) * n1, :n2], bm[c * n1:(c + 1) * n1, n2:]
            b2 = jnp.concatenate([br * tr + bi * ti, bi * tr - br * ti], axis=0)
            y = _dot(g1_ref[...], b2, HI)
            outs.append(y + slabs[c] * skip_ref[order * HY_CH + g * cg + c])
        return outs

    v = [x_ref[2, c] for c in range(cg)]
    y0 = conv(v, 0)
    z = [x_ref[0, c] * y0[c] for c in range(cg)]
    y1 = conv(z, 1)
    for c in range(cg):
        o_ref[c] = x_ref[1, c] * y1[c]


def _hy_conv(x4, kspec, skip_flat, dc, cg, n1, n2):
    bsz = x4.shape[1]
    full = lambda shape: pl.BlockSpec(shape, lambda b, g: (0,) * len(shape))
    return pl.pallas_call(
        functools.partial(_hy_conv_kernel, cg=cg, n1=n1, n2=n2),
        grid=(bsz, HY_CH // cg),
        in_specs=[pl.BlockSpec(memory_space=pltpu.SMEM),
                  pl.BlockSpec((3, None, cg, n1 // 2, n2), lambda b, g: (0, b, g, 0, 0)),
                  pl.BlockSpec((2, cg, 2, n1, n2), lambda b, g: (0, g, 0, 0, 0)),
                  full((2 * n1, n1 // 2)), full((n1 // 2, 2 * n1)), full((n1, n2)), full((n1, n2)),
                  full((2 * n2, 2 * n2)), full((2 * n2, 2 * n2))],
        out_specs=pl.BlockSpec((None, cg, n1 // 2, n2), lambda b, g: (b, g, 0, 0)),
        out_shape=jax.ShapeDtypeStruct((bsz, HY_CH, n1 // 2, n2), F32),
        scratch_shapes=[pltpu.VMEM((cg * n1, 2 * n2), F32)],
        compiler_params=_params(("parallel", "arbitrary")),
        name="hyena_long_conv",
    )(skip_flat, x4, kspec, dc["f1_half"], dc["g1"], dc["tr"], dc["ti"], dc["w2f"], dc["w2i"])


def _hy_ctx_kernel(x_ref, k_ref, nrm_ref, skip_ref, fc_ref, fs_ref, o_ref, *, c):
    fc = fc_ref[...]
    fs = fs_ref[...]
    inv_n = 1.0 / (2 * c)

    def conv(x, order):
        kk = k_ref[order]
        kr, ki = _dot(kk, fc, HI), -_dot(kk, fs, HI)
        xr, xi = _dot(x, fc[:c], HI), -_dot(x, fs[:c], HI)
        yr, yi = xr * kr - xi * ki, xr * ki + xi * kr
        y = (_dot(yr, fc[:, :c], HI) - _dot(yi, fs[:, :c], HI)) * inv_n
        return y * nrm_ref[order] + x * skip_ref[order]

    z = x_ref[0] * conv(x_ref[2], 0)
    o_ref[...] = x_ref[1] * conv(z, 1)


def _hy_ctx_conv(xt, kt, nrm, skip_col):
    _, bsz, ch, c = xt.shape
    n = 2 * c
    ang = 2.0 * np.pi * ((np.arange(n)[:, None] * np.arange(n)[None, :]) % n) / n
    fc, fs = jnp.asarray(np.cos(ang), F32), jnp.asarray(np.sin(ang), F32)
    full = lambda shape: pl.BlockSpec(shape, lambda b: (0,) * len(shape))
    return pl.pallas_call(
        functools.partial(_hy_ctx_kernel, c=c),
        grid=(bsz,),
        in_specs=[pl.BlockSpec((3, None, ch, c), lambda b: (0, b, 0, 0)),
                  full((2, ch, n)), full((2, ch, 1)), full((2, ch, 1)), full((n, n)), full((n, n))],
        out_specs=pl.BlockSpec((None, ch, c), lambda b: (b, 0, 0)),
        out_shape=jax.ShapeDtypeStruct((bsz, ch, c), F32),
        compiler_params=_params(("arbitrary",)),
        name="hyena_ctx_conv",
    )(xt, kt, nrm, skip_col, fc, fs)


def _out_proj_kernel(a_ref, b_ref, p_ref, ht_ref, x_ref, g_ref, w_ref, o_ref):
    w = D_GROUP
    acc = _dot(a_ref[...].astype(BF16), w_ref[0:w])
    acc += _dot(b_ref[...].astype(BF16), w_ref[w:2 * w])
    acc += _dot(p_ref[...].astype(BF16), w_ref[2 * w:3 * w])
    acc += _dot(ht_ref[...].T.astype(BF16), w_ref[3 * w:4 * w])
    o_ref[...] = x_ref[...] + g_ref[...] * acc


def _out_proj(a, b, p, ht, x, mod3, row_of_batch, j_gate, w_out, tm):
    bsz, t, d = x.shape
    w = D_GROUP
    tok = pl.BlockSpec((None, tm, w), lambda bb, i: (bb, i, 0))
    return pl.pallas_call(
        _out_proj_kernel,
        grid=(bsz, t // tm),
        in_specs=[tok, tok, tok,
                  pl.BlockSpec((None, w, tm), lambda bb, i: (bb, 0, i)),
                  pl.BlockSpec((None, tm, d), lambda bb, i: (bb, i, 0)),
                  pl.BlockSpec((None, 1, d), lambda bb, i: (row_of_batch(bb), 0, j_gate)),
                  pl.BlockSpec((4 * w, d), lambda bb, i: (0, 0))],
        out_specs=pl.BlockSpec((None, tm, d), lambda bb, i: (bb, i, 0)),
        out_shape=jax.ShapeDtypeStruct((bsz, t, d), F32),
        compiler_params=_params(("parallel", "arbitrary")),
        name="out_proj_residual",
    )(a, b, p, ht, x, mod3, w_out)


def _moe_kernel(x_ref, g_ref, sc_ref, sh_ref, gate_ref, rw_ref, rb_ref, w1_ref, w3_ref, w2_ref, fg_ref,
                o_ref, h_scr, gates_scr, acc_scr, *, final):
    e = pl.program_id(2)
    per_group = N_EXPERTS // N_EXPERT_GROUPS
    tm = x_ref.shape[0]

    @pl.when(e == 0)
    def _():
        x = x_ref[...]
        ms = jnp.mean(x * x, axis=-1, keepdims=True)
        h = (x * lax.rsqrt(ms + EPS)) * g_ref[...] * (1.0 + sc_ref[...]) + sh_ref[...]
        h_scr[...] = h.astype(BF16)
        acc_scr[...] = jnp.zeros(acc_scr.shape, F32)
        lane = lax.broadcasted_iota(jnp.int32, (tm, LANES), 1)
        valid = lane < N_EXPERTS
        neg = -jnp.inf
        logits = jnp.where(valid, _dot(h, rw_ref[...], HI), neg)
        mx = jnp.max(logits, axis=1, keepdims=True)
        ex = jnp.exp(logits - mx)
        scores = ex / jnp.sum(ex, axis=1, keepdims=True)
        sel = scores + rb_ref[...]
        big = jnp.int32(LANES)
        best = None
        for g in range(N_EXPERT_GROUPS):
            in_g = (lane >= g * per_group) & (lane < (g + 1) * per_group)
            v1 = jnp.max(jnp.where(in_g, sel, neg), axis=1, keepdims=True)
            i1 = jnp.min(jnp.where(in_g & (sel == v1), lane, big), axis=1, keepdims=True)
            rest = in_g & (lane != i1)
            v2 = jnp.max(jnp.where(rest, sel, neg), axis=1, keepdims=True)
            i2 = jnp.min(jnp.where(rest & (sel == v2), lane, big), axis=1, keepdims=True)
            gs = v1 + v2
            if best is None:
                best, e1, e2 = gs, i1, i2
            else:
                upd = gs > best
                best = jnp.where(upd, gs, best)
                e1 = jnp.where(upd, i1, e1)
                e2 = jnp.where(upd, i2, e2)
        w1 = jnp.sum(jnp.where(lane == e1, scores, 0.0), axis=1, keepdims=True)
        w2 = jnp.sum(jnp.where(lane == e2, scores, 0.0), axis=1, keepdims=True)
        tot = w1 + w2
        gates_scr[...] = jnp.where(lane == e1, w1 / tot, 0.0) + jnp.where(lane == e2, w2 / tot, 0.0)

    lane = lax.broadcasted_iota(jnp.int32, (tm, LANES), 1)
    ge = jnp.sum(jnp.where(lane == e, gates_scr[...], 0.0), axis=1, keepdims=True)
    hb = h_scr[...]
    a = _dot(hb, w1_ref[...])
    b = _dot(hb, w3_ref[...])
    act = (a * jax.nn.sigmoid(a)) * b
    acc_scr[...] += ge * _dot(act.astype(BF16), w2_ref[...])

    @pl.when(e == N_EXPERTS - 1)
    def _():
        y = x_ref[...] + gate_ref[...] * acc_scr[...]
        if final:
            ms = jnp.mean(y * y, axis=-1, keepdims=True)
            y = (y * lax.rsqrt(ms + EPS)) * fg_ref[...]
        o_ref[...] = y


def _moe(x, g, mod3, row_of_batch, j_shift, j_scale, j_gate, rw, rb, w1, w3, w2, final_g, final, tm):
    bsz, t, d = x.shape
    vec = lambda j: pl.BlockSpec((None, 1, d), lambda b, i, e: (row_of_batch(b), 0, j))
    full = lambda shape: pl.BlockSpec(shape, lambda b, i, e: (0,) * len(shape))
    return pl.pallas_call(
        functools.partial(_moe_kernel, final=final),
        grid=(bsz, t // tm, N_EXPERTS),
        in_specs=[pl.BlockSpec((None, tm, d), lambda b, i, e: (b, i, 0)),
                  full((1, d)), vec(j_scale), vec(j_shift), vec(j_gate),
                  full((d, LANES)), full((1, LANES)),
                  pl.BlockSpec((None, d, D_EXPERT), lambda b, i, e: (e, 0, 0)),
                  pl.BlockSpec((None, d, D_EXPERT), lambda b, i, e: (e, 0, 0)),
                  pl.BlockSpec((None, D_EXPERT, d), lambda b, i, e: (e, 0, 0)),
                  full((1, d))],
        out_specs=pl.BlockSpec((None, tm, d), lambda b, i, e: (b, i, 0)),
        out_shape=jax.ShapeDtypeStruct((bsz, t, d), F32),
        scratch_shapes=[pltpu.VMEM((tm, d), BF16), pltpu.VMEM((tm, LANES), F32), pltpu.VMEM((tm, d), F32)],
        compiler_params=_params(("parallel", "parallel", "arbitrary")),
        name="moe_final" if final else "moe",
    )(x, g, mod3, mod3, mod3, rw, rb, w1, w3, w2, final_g)


def _tile(n, pref):
    t = min(n, pref)
    assert n % t == 0
    return t


def _fft_split(n):
    n2 = LANES
    assert n % n2 == 0
    return n // n2, n2


def _mixers(u, uc, lp, li, need_ctx):
    bsz, s, _ = u.shape
    c = uc.shape[1]
    lam_init = 0.8 - 0.6 * math.exp(-0.3 * li)
    lam_vecs = lp["a_lambda"].astype(F32)
    subln = lp["a_subln_g"].astype(F32)[None, :]

    qc, kct, vc = _attn_prep(uc, _tile(c, 256), rope=False)
    ql, klt, vl = _attn_prep(u, _tile(s, 256), rope=True)
    a_l = _diff_attn(ql, kct, vc, klt, vl, lam_vecs, subln, lam_init, _tile(s, 512), _tile(s, 1024))
    bias8 = _nbr_bias(lp["b_rpb"])
    b_l = _nbr_attn(u, uc, bias8)
    wbd = jax.scipy.linalg.block_diag(*[lp["pool_w"][g] for g in range(len(POOL_SIZES))]).astype(BF16)
    pscale = lp["pool_scale"].astype(F32)[None, :]
    p_l = _pool_mix(u, wbd, pscale, _tile(s, 512))
    fargs = (lp["hy_f_w1"], lp["hy_f_b1"], lp["hy_f_w2"], lp["hy_f_b2"], lp["hy_f_w3"], lp["hy_f_b3"])
    skip = lp["hy_skip"].astype(F32)
    w_short = lp["hy_short_w"].astype(F32)
    b_short = lp["hy_short_b"].astype(F32)[None, :]
    n1, n2 = _fft_split(2 * s)
    dc = _dft_consts(n1, n2)
    cg = 8
    kt_l, _, nrm_l = _hy_filters(s, *fargs, tp=_tile(2 * s, 1024))
    kspec = _hy_spec(kt_l.reshape(2, HY_CH, n1, n2), nrm_l.reshape(2 * HY_CH), dc, cg, n1, n2)
    xt = _hy_short(u, w_short, b_short, _tile(s, 512))
    h_l = _hy_conv(xt.reshape(3, bsz, HY_CH, n1 // 2, n2), kspec, skip.reshape(2 * HY_CH), dc, cg, n1, n2)
    h_l = h_l.reshape(bsz, HY_CH, s)
    lat = (a_l, b_l, p_l, h_l)
    if not need_ctx:
        return lat, None
    a_c = _diff_attn(qc, kct, vc, None, None, lam_vecs, subln, lam_init, _tile(c, 256), None)
    b_c = _nbr_ctx_attn(uc)
    p_c = _pool_mix(uc, wbd, pscale, _tile(c, 256))
    kt_c, _, nrm_c = _hy_filters(c, *fargs, tp=_tile(2 * c, 512))
    xtc = _hy_short(uc, w_short, b_short, _tile(c, 256))
    h_c = _hy_ctx_conv(xtc, kt_c, nrm_c, skip[:, :, None])
    return lat, (a_c, b_c, p_c, h_c)


def kernel(x, c, ctx, c_ctx, norm1_g, norm2_g, ada_w, ada_b, w_in, w_out, a_lambda, a_subln_g, b_rpb, pool_w, pool_scale, hy_short_w, hy_short_b, hy_f_w1, hy_f_b1, hy_f_w2, hy_f_b2, hy_f_w3, hy_f_b3, hy_skip, router_w, router_b, moe_w1, moe_w3, moe_w2, final_g):
    depth = norm1_g.shape[0]
    bsz, s, d = x.shape
    cl = ctx.shape[1]
    assert bsz <= SUBLANES - 1
    xl, xc = x, ctx
    cpad = jnp.zeros((SUBLANES, d), F32).at[:bsz].set(c.astype(F32)).at[bsz].set(c_ctx.astype(F32))
    rw = jnp.pad(router_w.astype(F32), ((0, 0), (0, LANES - N_EXPERTS)))
    rb = jnp.pad(router_b.astype(F32), (0, LANES - N_EXPERTS))[None, :]
    lat_row = lambda b: b
    ctx_row = lambda b: bsz
    fg = final_g.astype(F32)[None, :]
    tm = _tile(s, 512)
    tmc = _tile(cl, 256)
    for li in range(depth):
        need_ctx = li < depth - 1
        lp = dict(a_lambda=a_lambda[li], a_subln_g=a_subln_g[li], b_rpb=b_rpb[li], pool_w=pool_w[li],
                  pool_scale=pool_scale[li], hy_short_w=hy_short_w[li], hy_short_b=hy_short_b[li],
                  hy_f_w1=hy_f_w1[li], hy_f_b1=hy_f_b1[li], hy_f_w2=hy_f_w2[li], hy_f_b2=hy_f_b2[li],
                  hy_f_w3=hy_f_w3[li], hy_f_b3=hy_f_b3[li], hy_skip=hy_skip[li])
        mod3 = _ada(cpad, ada_w[li].astype(F32), ada_b[li].astype(F32)[None, :]).reshape(SUBLANES, 1, 6 * d)
        n1g = norm1_g[li].astype(F32)[None, :]
        n2g = norm2_g[li].astype(F32)[None, :]
        w_in_b = w_in[li].astype(BF16)
        w_out_b = w_out[li].astype(BF16)
        u = _norm_proj(xl, n1g, mod3, lat_row, 0, 1, w_in_b, tm)
        uc = _norm_proj(xc, n1g, mod3, ctx_row, 0, 1, w_in_b, tmc)
        lat, cx = _mixers(u, uc, lp, li, need_ctx)
        xl = _out_proj(*lat, xl, mod3, lat_row, 2, w_out_b, tm)
        w1b, w3b, w2b = moe_w1[li].astype(BF16), moe_w3[li].astype(BF16), moe_w2[li].astype(BF16)
        if need_ctx:
            xc = _out_proj(*cx, xc, mod3, ctx_row, 2, w_out_b, tmc)
            xc = _moe(xc, n2g, mod3, ctx_row, 3, 4, 5, rw, rb, w1b, w3b, w2b, fg, False, tmc)
        xl = _moe(xl, n2g, mod3, lat_row, 3, 4, 5, rw, rb, w1b, w3b, w2b, fg, li == depth - 1, tm)
    return xl
```

```python
import functools
import math

import numpy as np
import jax
import jax.numpy as jnp
from jax import lax
from jax.experimental import pallas as pl
from jax.experimental.pallas import tpu as pltpu

F32 = jnp.float32
BF16 = jnp.bfloat16
HI = lax.Precision.HIGHEST

GRID_W = 64
A_HEADS = 4
A_QK = 32
A_V = 64
ROPE_BASE = 10000.0
B_HEADS = 4
B_DIM = 64
WIN_R = 8
WIN_C = 16
POOL_SIZES = (2, 4, 8, 16)
POOL_CH = 64
D_GROUP = 256
HY_CH = 256
HY_BANDS = 16
HY_EMB = 1 + 2 * HY_BANDS
HY_HIDDEN = 64
HY_SIN_FREQ = 1.0
HY_MIN_DECAY = math.log(1e-2) / 1.5
HY_MAX_DECAY = math.log(1e-2) / 0.3
N_EXPERTS = 16
N_EXPERT_GROUPS = 4
D_EXPERT = 512
EPS = 1e-6
LOG2E = 1.4426950408889634

LANES = 128
SUBLANES = 8
VMEM_LIMIT = 56 * 1024 * 1024


def _params(sem):
    return pltpu.CompilerParams(dimension_semantics=sem, vmem_limit_bytes=VMEM_LIMIT)


def _dot(a, b, prec=None):
    return jnp.dot(a, b, precision=prec, preferred_element_type=F32)


def _dot_nt(a, b):
    return lax.dot_general(a, b, (((1,), (1,)), ((), ())), preferred_element_type=F32)


def _ada_kernel(c_ref, w_ref, b_ref, o_ref):
    cf = c_ref[...]
    s = cf * jax.nn.sigmoid(cf)
    o_ref[...] = _dot(s, w_ref[...], HI) + b_ref[...]


def _ada(cpad, w, b):
    d = cpad.shape[1]
    n = w.shape[1]
    return pl.pallas_call(
        _ada_kernel,
        grid=(n // d,),
        in_specs=[pl.BlockSpec((SUBLANES, d), lambda j: (0, 0)),
                  pl.BlockSpec((d, d), lambda j: (0, j)),
                  pl.BlockSpec((1, d), lambda j: (0, j))],
        out_specs=pl.BlockSpec((SUBLANES, d), lambda j: (0, j)),
        out_shape=jax.ShapeDtypeStruct((SUBLANES, n), F32),
        compiler_params=_params(("arbitrary",)),
        name="ada_mod",
    )(cpad, w, b)


def _norm_proj_kernel(x_ref, g_ref, sc_ref, sh_ref, w_ref, o_ref):
    x = x_ref[...]
    ms = jnp.mean(x * x, axis=-1, keepdims=True)
    h = (x * lax.rsqrt(ms + EPS)) * g_ref[...] * (1.0 + sc_ref[...]) + sh_ref[...]
    o_ref[...] = _dot(h.astype(BF16), w_ref[...])


def _norm_proj(x, g, mod3, row_of_batch, j_shift, j_scale, w, tm):
    bsz, t, d = x.shape
    n = w.shape[1]
    return pl.pallas_call(
        _norm_proj_kernel,
        grid=(bsz, t // tm),
        in_specs=[pl.BlockSpec((None, tm, d), lambda b, i: (b, i, 0)),
                  pl.BlockSpec((1, d), lambda b, i: (0, 0)),
                  pl.BlockSpec((None, 1, d), lambda b, i: (row_of_batch(b), 0, j_scale)),
                  pl.BlockSpec((None, 1, d), lambda b, i: (row_of_batch(b), 0, j_shift)),
                  pl.BlockSpec((d, n), lambda b, i: (0, 0))],
        out_specs=pl.BlockSpec((None, tm, n), lambda b, i: (b, i, 0)),
        out_shape=jax.ShapeDtypeStruct((bsz, t, n), F32),
        compiler_params=_params(("parallel", "arbitrary")),
        name="norm_in_proj",
    )(x, g, mod3, mod3, w)


def _aprep_kernel(*refs, rope):
    if rope:
        u_ref, cos_ref, sin_ref, q_ref, kt_ref, v_ref = refs
    else:
        u_ref, q_ref, kt_ref, v_ref = refs
    u = u_ref[...]
    q = u[:, 0:256]
    k = u[:, 256:512]
    v = u[:, 512:768]
    if rope:
        cos_t = cos_ref[...]
        sin_t = sin_ref[...]
        lane = lax.broadcasted_iota(jnp.int32, cos_t.shape, 1)
        first = (lane % (2 * 16)) < 16

        def rot(x):
            halves = []
            for j in range(2):
                xh = x[:, j * LANES:(j + 1) * LANES]
                swap = jnp.where(first, pltpu.roll(xh, LANES - 16, axis=1), pltpu.roll(xh, 16, axis=1))
                halves.append(xh * cos_t + swap * sin_t)
            return jnp.concatenate(halves, axis=1)

        q = rot(q)
        k = rot(k)
    q = q * (A_QK ** -0.5 * LOG2E)
    kt = k.T
    for hc in range(2 * A_HEADS):
        q_ref[hc] = q[:, hc * A_QK:(hc + 1) * A_QK].astype(BF16)
        kt_ref[hc] = kt[hc * A_QK:(hc + 1) * A_QK, :].astype(BF16)
    lane = lax.broadcasted_iota(jnp.int32, (v.shape[0], LANES - A_V), 1)
    ones_col = jnp.where(lane == 0, 1.0, 0.0)
    for h in range(A_HEADS):
        v_ref[h] = jnp.concatenate([v[:, h * A_V:(h + 1) * A_V], ones_col], axis=1).astype(BF16)


def _rope_tables(length):
    n_freq = A_QK // 4
    inv = ROPE_BASE ** (-jnp.arange(n_freq, dtype=F32) / n_freq)
    t = jnp.arange(length)
    row = (t // GRID_W).astype(F32)
    col = (t % GRID_W).astype(F32)
    ang = jnp.concatenate([row[:, None] * inv, col[:, None] * inv], axis=-1)
    cos, sin = jnp.cos(ang), jnp.sin(ang)
    cos_t = jnp.tile(jnp.concatenate([cos, cos], axis=-1), (1, LANES // 32))
    sin_t = jnp.tile(jnp.concatenate([-sin, sin], axis=-1), (1, LANES // 32))
    return cos_t, sin_t


def _attn_prep(u, tm, rope):
    bsz, t, _ = u.shape
    nh = 2 * A_HEADS
    in_specs = [pl.BlockSpec((None, tm, 768), lambda b, i: (b, i, 0))]
    args = [u]
    if rope:
        cos_t, sin_t = _rope_tables(t)
        in_specs += [pl.BlockSpec((tm, LANES), lambda b, i: (i, 0))] * 2
        args += [cos_t, sin_t]
    return pl.pallas_call(
        functools.partial(_aprep_kernel, rope=rope),
        grid=(bsz, t // tm),
        in_specs=in_specs,
        out_specs=[pl.BlockSpec((None, nh, tm, A_QK), lambda b, i: (b, 0, i, 0)),
                   pl.BlockSpec((None, nh, A_QK, tm), lambda b, i: (b, 0, 0, i)),
                   pl.BlockSpec((None, A_HEADS, tm, LANES), lambda b, i: (b, 0, i, 0))],
        out_shape=[jax.ShapeDtypeStruct((bsz, nh, t, A_QK), BF16),
                   jax.ShapeDtypeStruct((bsz, nh, A_QK, t), BF16),
                   jax.ShapeDtypeStruct((bsz, A_HEADS, t, LANES), BF16)],
        compiler_params=_params(("parallel", "arbitrary")),
        name="attn_prep_rope" if rope else "attn_prep_ctx",
    )(*args)


QK_LOOKAHEAD = 3


def _dattn_kernel(*refs, lam_init, has_lat):
    if has_lat:
        lam_ref, g_ref, q_ref, kc_ref, vc_ref, k_ref, v_ref, o_ref, m_scr, acc_scr = refs
    else:
        lam_ref, g_ref, q_ref, kc_ref, vc_ref, o_ref, m_scr, acc_scr = refs
    ki = pl.program_id(2)
    nk = pl.num_programs(2)
    nh = 2 * A_HEADS

    def update(kt_r, v_r):
        scores = [_dot(q_ref[j], kt_r[j]) for j in range(QK_LOOKAHEAD)]
        for hc in range(nh):
            s = scores[hc]
            if hc + QK_LOOKAHEAD < nh:
                scores.append(_dot(q_ref[hc + QK_LOOKAHEAD], kt_r[hc + QK_LOOKAHEAD]))
            m_prev = m_scr[hc]
            m_new = jnp.maximum(m_prev, jnp.max(s, axis=1, keepdims=True))
            alpha = jnp.exp2(m_prev - m_new)
            p = jnp.exp2((s - m_new[:, :1]).astype(BF16))
            acc_scr[hc] = alpha * acc_scr[hc] + _dot(p, v_r[hc // 2])
            m_scr[hc] = m_new

    @pl.when(ki == 0)
    def _():
        m_scr[...] = jnp.full(m_scr.shape, -jnp.inf, F32)
        acc_scr[...] = jnp.zeros(acc_scr.shape, F32)
        update(kc_ref, vc_ref)

    if has_lat:
        @pl.when(ki > 0)
        def _():
            update(k_ref, v_ref)

    @pl.when(ki == nk - 1)
    def _():
        lv = lam_ref[...]
        lam = (jnp.exp(jnp.sum(lv[0:1] * lv[1:2], axis=1, keepdims=True))
               - jnp.exp(jnp.sum(lv[2:3] * lv[3:4], axis=1, keepdims=True)) + lam_init)
        for h in range(A_HEADS):
            a0 = acc_scr[2 * h]
            a1 = acc_scr[2 * h + 1]
            o = a0[:, :A_V] / a0[:, A_V:A_V + 1] - lam * (a1[:, :A_V] / a1[:, A_V:A_V + 1])
            ms = jnp.mean(o * o, axis=-1, keepdims=True)
            o_ref[:, h * A_V:(h + 1) * A_V] = (o * lax.rsqrt(ms + EPS)) * g_ref[...] * (1.0 - lam_init)


def _diff_attn(q, kct, vc, kt, v, lam_vecs, subln_g, lam_init, tq, tk):
    bsz, nh, t, _ = q.shape
    c = kct.shape[-1]
    has_lat = kt is not None
    nk = 1 + (kt.shape[-1] // tk if has_lat else 0)
    in_specs = [pl.BlockSpec((4, A_QK), lambda b, i, k: (0, 0)),
                pl.BlockSpec((1, A_V), lambda b, i, k: (0, 0)),
                pl.BlockSpec((None, nh, tq, A_QK), lambda b, i, k: (b, 0, i, 0)),
                pl.BlockSpec((None, nh, A_QK, c), lambda b, i, k: (b, 0, 0, 0)),
                pl.BlockSpec((None, A_HEADS, c, LANES), lambda b, i, k: (b, 0, 0, 0))]
    args = [lam_vecs, subln_g, q, kct, vc]
    if has_lat:
        in_specs += [pl.BlockSpec((None, nh, A_QK, tk), lambda b, i, k: (b, 0, 0, jnp.maximum(k - 1, 0))),
                     pl.BlockSpec((None, A_HEADS, tk, LANES), lambda b, i, k: (b, 0, jnp.maximum(k - 1, 0), 0))]
        args += [kt, v]
    return pl.pallas_call(
        functools.partial(_dattn_kernel, lam_init=lam_init, has_lat=has_lat),
        grid=(bsz, t // tq, nk),
        in_specs=in_specs,
        out_specs=pl.BlockSpec((None, tq, A_HEADS * A_V), lambda b, i, k: (b, i, 0)),
        out_shape=jax.ShapeDtypeStruct((bsz, t, A_HEADS * A_V), F32),
        scratch_shapes=[pltpu.VMEM((nh, tq, LANES), F32),
                        pltpu.VMEM((nh, tq, LANES), F32)],
        compiler_params=_params(("parallel", "parallel", "arbitrary")),
        name="diff_attn" if has_lat else "diff_attn_ctx",
    )(*args)


NB_ROWS = 8


def _nbr_bias(rpb):
    cols = jnp.arange(GRID_W)
    c0 = jnp.clip(cols - WIN_C // 2, 0, GRID_W - WIN_C)
    in_win = (cols[None, :] >= c0[:, None]) & (cols[None, :] < c0[:, None] + WIN_C)
    dc = jnp.clip(cols[None, :] - cols[:, None], -(WIN_C - 1), WIN_C - 1) + (WIN_C - 1)
    onehot = (dc[None] == jnp.arange(2 * WIN_C - 1)[:, None, None]).astype(F32)
    g = jnp.einsum("hab,bqk->haqk", rpb.astype(F32), onehot, precision=HI)
    g = jnp.where(in_win[None, None], g, -jnp.inf)
    b = jnp.stack([g[:, a0:a0 + WIN_R] for a0 in range(WIN_R)], axis=0)
    b = jnp.transpose(b, (0, 1, 3, 2, 4))
    return b.reshape(WIN_R, B_HEADS, GRID_W, WIN_R * GRID_W)


def _nbr_kernel(q_ref, kp_ref, kc_ref, kn_ref, vp_ref, vcur_ref, vn_ref, kctx_ref, vctx_ref, bias_ref,
                o_ref, kwin, vwin, kcx, vcx, *, n_rows):
    rb = pl.program_id(1)
    blk = NB_ROWS * GRID_W
    scale = B_DIM ** -0.5
    for h in range(B_HEADS):
        sl = slice(h * B_DIM, (h + 1) * B_DIM)
        for j, (kr, vr) in enumerate(((kp_ref, vp_ref), (kc_ref, vcur_ref), (kn_ref, vn_ref))):
            kwin[h, j * blk:(j + 1) * blk, :] = kr[:, sl].astype(BF16)
            vwin[h, j * blk:(j + 1) * blk, :] = vr[:, sl].astype(BF16)
        kcx[h] = kctx_ref[:, sl].astype(BF16)
        vcx[h] = vctx_ref[:, sl].astype(BF16)

    def window(rr):
        r = rb * NB_ROWS + rr
        r0 = jnp.clip(r - WIN_R // 2, 0, n_rows - WIN_R)
        off = pl.multiple_of((r0 - (rb - 1) * NB_ROWS) * GRID_W, GRID_W)
        return off, r0 - r + (WIN_R - 1)

    def scores(rr):
        off, a0 = window(rr)
        qrow = q_ref[rr * GRID_W:(rr + 1) * GRID_W, :]
        out = []
        for h in range(B_HEADS):
            qh = qrow[:, h * B_DIM:(h + 1) * B_DIM].astype(BF16)
            s = _dot_nt(qh, kwin[h, pl.ds(off, WIN_R * GRID_W), :]) * scale + bias_ref[a0, h]
            out.append((s, _dot_nt(qh, kcx[h]) * scale))
        return out

    nxt = scores(0)
    for rr in range(NB_ROWS):
        cur = nxt
        if rr + 1 < NB_ROWS:
            nxt = scores(rr + 1)
        off, _ = window(rr)
        outs = []
        for h in range(B_HEADS):
            s, sc = cur[h]
            m = jnp.maximum(jnp.max(s, axis=1, keepdims=True), jnp.max(sc, axis=1, keepdims=True))
            p = jnp.exp(s - m)
            pc = jnp.exp(sc - m)
            l = jnp.sum(p, axis=1, keepdims=True) + jnp.sum(pc, axis=1, keepdims=True)
            o = _dot(p.astype(BF16), vwin[h, pl.ds(off, WIN_R * GRID_W), :]) + _dot(pc.astype(BF16), vcx[h])
            outs.append(o / l)
        o_ref[rr * GRID_W:(rr + 1) * GRID_W, :] = jnp.concatenate(outs, axis=1)


def _nbr_attn(u, uc, bias8):
    bsz, s, _ = u.shape
    c = uc.shape[1]
    n_rows = s // GRID_W
    nb = n_rows // NB_ROWS
    blk = NB_ROWS * GRID_W
    w = B_HEADS * B_DIM

    def spec(col, shift):
        return pl.BlockSpec((None, blk, w), lambda b, i: (b, jnp.clip(i + shift, 0, nb - 1), col))

    return pl.pallas_call(
        functools.partial(_nbr_kernel, n_rows=n_rows),
        grid=(bsz, nb),
        in_specs=[spec(3, 0), spec(4, -1), spec(4, 0), spec(4, 1), spec(5, -1), spec(5, 0), spec(5, 1),
                  pl.BlockSpec((None, c, w), lambda b, i: (b, 0, 4)),
                  pl.BlockSpec((None, c, w), lambda b, i: (b, 0, 5)),
                  pl.BlockSpec(bias8.shape, lambda b, i: (0, 0, 0, 0))],
        out_specs=pl.BlockSpec((None, blk, w), lambda b, i: (b, i, 0)),
        out_shape=jax.ShapeDtypeStruct((bsz, s, w), F32),
        scratch_shapes=[pltpu.VMEM((B_HEADS, 3 * blk, B_DIM), BF16),
                        pltpu.VMEM((B_HEADS, 3 * blk, B_DIM), BF16),
                        pltpu.VMEM((B_HEADS, c, B_DIM), BF16),
                        pltpu.VMEM((B_HEADS, c, B_DIM), BF16)],
        compiler_params=_params(("parallel", "arbitrary")),
        name="nbr_attn",
    )(u, u, u, u, u, u, u, uc, uc, bias8)


def _nbr_ctx_kernel(q_ref, k_ref, v_ref, o_ref):
    scale = B_DIM ** -0.5
    outs = []
    for h in range(B_HEADS):
        sl = slice(h * B_DIM, (h + 1) * B_DIM)
        s = _dot_nt(q_ref[:, sl].astype(BF16), k_ref[:, sl].astype(BF16)) * scale
        m = jnp.max(s, axis=1, keepdims=True)
        p = jnp.exp(s - m)
        l = jnp.sum(p, axis=1, keepdims=True)
        outs.append(_dot(p.astype(BF16), v_ref[:, sl].astype(BF16)) / l)
    o_ref[...] = jnp.concatenate(outs, axis=1)


def _nbr_ctx_attn(uc):
    bsz, c, _ = uc.shape
    w = B_HEADS * B_DIM
    return pl.pallas_call(
        _nbr_ctx_kernel,
        grid=(bsz,),
        in_specs=[pl.BlockSpec((None, c, w), lambda b: (b, 0, 3)),
                  pl.BlockSpec((None, c, w), lambda b: (b, 0, 4)),
                  pl.BlockSpec((None, c, w), lambda b: (b, 0, 5))],
        out_specs=pl.BlockSpec((None, c, w), lambda b: (b, 0, 0)),
        out_shape=jax.ShapeDtypeStruct((bsz, c, w), F32),
        compiler_params=_params(("arbitrary",)),
        name="nbr_attn_ctx",
    )(uc, uc, uc)


HALO = SUBLANES


def _halo_specs(tm, length, col, width):
    nt = length // tm
    per = tm // HALO
    last = length // HALO - 1
    return [pl.BlockSpec((None, HALO, width), lambda b, i, *_: (b, jnp.maximum(i * per - 1, 0), col)),
            pl.BlockSpec((None, tm, width), lambda b, i, *_: (b, i, col)),
            pl.BlockSpec((None, HALO, width), lambda b, i, *_: (b, jnp.minimum((i + 1) * per, last), col))], nt


def _fill_halo(buf, prev_ref, cur_ref, next_ref, i, nt, tm):
    zero = jnp.zeros(prev_ref.shape, F32)
    buf[0:HALO, :] = jnp.where(i > 0, prev_ref[...], zero)
    buf[HALO:HALO + tm, :] = cur_ref[...]
    buf[HALO + tm:, :] = jnp.where(i < nt - 1, next_ref[...], zero)


def _pool_kernel(prev_ref, cur_ref, next_ref, w_ref, ps_ref, o_ref, buf, *, tm, nt, length):
    i = pl.program_id(1)
    _fill_halo(buf, prev_ref, cur_ref, next_ref, i, nt, tm)

    def sh(j):
        return buf[HALO + j:HALO + j + tm, :]

    u = sh(0)
    sums = []
    acc = None
    lo, hi = 0, 0
    for w in POOL_SIZES:
        for j in list(range(-(w // 2), lo)) + list(range(hi, w // 2)):
            acc = sh(j) if acc is None else acc + sh(j)
        lo, hi = -(w // 2), w // 2
        sums.append(acc)
    lane = lax.broadcasted_iota(jnp.int32, (tm, D_GROUP), 1)
    t = (i * tm + lax.broadcasted_iota(jnp.int32, (tm, D_GROUP), 0))
    wsum = sums[-1]
    half = jnp.full((tm, D_GROUP), POOL_SIZES[-1] // 2, jnp.int32)
    for g in range(len(POOL_SIZES) - 2, -1, -1):
        sel = lane < (g + 1) * POOL_CH
        wsum = jnp.where(sel, sums[g], wsum)
        half = jnp.where(sel, POOL_SIZES[g] // 2, half)
    cnt = (jnp.minimum(t + half, length) - jnp.maximum(t - half, 0)).astype(F32)
    d = wsum / cnt - u
    o_ref[...] = _dot(d.astype(BF16), w_ref[...]) * ps_ref[...]


def _pool_mix(u, wbd, pool_scale, tm):
    bsz, length, _ = u.shape
    specs, nt = _halo_specs(tm, length, 6, D_GROUP)
    return pl.pallas_call(
        functools.partial(_pool_kernel, tm=tm, nt=nt, length=length),
        grid=(bsz, nt),
        in_specs=specs + [pl.BlockSpec((D_GROUP, D_GROUP), lambda b, i: (0, 0)),
                          pl.BlockSpec((1, D_GROUP), lambda b, i: (0, 0))],
        out_specs=pl.BlockSpec((None, tm, D_GROUP), lambda b, i: (b, i, 0)),
        out_shape=jax.ShapeDtypeStruct((bsz, length, D_GROUP), F32),
        scratch_shapes=[pltpu.VMEM((tm + 2 * HALO, D_GROUP), F32)],
        compiler_params=_params(("parallel", "arbitrary")),
        name="pool_mix",
    )(u, u, u, wbd, pool_scale)


def _hy_short_kernel(prev_ref, cur_ref, next_ref, w_ref, b_ref, o_ref, buf, *, tm, nt):
    i = pl.program_id(1)
    _fill_halo(buf, prev_ref, cur_ref, next_ref, i, nt, tm)
    w = w_ref[...]
    y = (buf[HALO - 1:HALO - 1 + tm, :] * w[0:1] + buf[HALO:HALO + tm, :] * w[1:2]
         + buf[HALO + 1:HALO + 1 + tm, :] * w[2:3] + b_ref[...])
    o_ref[...] = y.T


def _hy_short(u, w_short, b_short, tm):
    bsz, length, _ = u.shape
    nt = length // tm
    per = tm // HALO
    last = length // HALO - 1
    c0 = 7
    in_specs = [pl.BlockSpec((None, HALO, HY_CH), lambda b, i, j: (b, jnp.maximum(i * per - 1, 0), c0 + j)),
                pl.BlockSpec((None, tm, HY_CH), lambda b, i, j: (b, i, c0 + j)),
                pl.BlockSpec((None, HALO, HY_CH), lambda b, i, j: (b, jnp.minimum((i + 1) * per, last), c0 + j)),
                pl.BlockSpec((3, HY_CH), lambda b, i, j: (0, j)),
                pl.BlockSpec((1, HY_CH), lambda b, i, j: (0, j))]
    return pl.pallas_call(
        functools.partial(_hy_short_kernel, tm=tm, nt=nt),
        grid=(bsz, nt, 3),
        in_specs=in_specs,
        out_specs=pl.BlockSpec((None, None, HY_CH, tm), lambda b, i, j: (j, b, 0, i)),
        out_shape=jax.ShapeDtypeStruct((3, bsz, HY_CH, length), F32),
        scratch_shapes=[pltpu.VMEM((tm + 2 * HALO, HY_CH), F32)],
        compiler_params=_params(("parallel", "arbitrary", "arbitrary")),
        name="hyena_short_conv",
    )(u, u, u, w_short, b_short)


HY_FEAT = 40


def _hy_filter_kernel(band_ref, w1_ref, b1_ref, w2_ref, b2_ref, w3_ref, b3_ref, dl_ref,
                      k_ref, ssq_ref, nrm_ref, *, tp, length):
    i = pl.program_id(0)
    n_i = pl.num_programs(0)
    m = i * tp + lax.broadcasted_iota(jnp.int32, (1, tp), 1)
    t = jnp.where(m <= length, m, 2 * length - m).astype(F32)
    t_norm = t / max(length - 1, 1)
    ang = ((2.0 * math.pi / length) * t) * band_ref[...]
    row = lax.broadcasted_iota(jnp.int32, (HY_FEAT, tp), 0)
    z = jnp.where(row == 0, t_norm,
                  jnp.where(row <= HY_BANDS, jnp.cos(ang), jnp.where(row < HY_EMB, jnp.sin(ang), 0.0)))
    z = jnp.concatenate([z, jnp.zeros((LANES - HY_FEAT, tp), F32)], axis=0)
    h = jnp.sin(HY_SIN_FREQ * (_dot(w1_ref[...], z, HI) + b1_ref[...]))
    h = jnp.sin(HY_SIN_FREQ * (_dot(w2_ref[...], h, HI) + b2_ref[...]))
    h = _dot(w3_ref[...], h, HI) + b3_ref[...]
    h = h * jnp.exp(-t_norm * dl_ref[...])

    @pl.when(i == 0)
    def _():
        ssq_ref[...] = jnp.zeros(ssq_ref.shape, F32)

    for o in range(2):
        fwd = h[o * 2 * HY_CH:o * 2 * HY_CH + HY_CH]
        bwd = h[o * 2 * HY_CH + HY_CH:(o + 1) * 2 * HY_CH]
        k = jnp.where(m < length, fwd, jnp.where(m == length, 0.0, bwd))
        k_ref[o] = k
        extra = jnp.where(m == 0, bwd * bwd, 0.0)
        ssq_ref[o] += jnp.sum(k * k + extra, axis=1, keepdims=True)

    @pl.when(i == n_i - 1)
    def _():
        nrm_ref[...] = lax.rsqrt(ssq_ref[...] + EPS)


def _hy_filters(length, w1, b1, w2, b2, w3, b3, tp):
    bands = jnp.linspace(1e-4, HY_BANDS - 1, HY_BANDS, dtype=F32)
    band_col = jnp.concatenate([jnp.zeros((1,), F32), bands, bands,
                                jnp.zeros((HY_FEAT - HY_EMB,), F32)])[:, None]
    deltas = jnp.abs(jnp.linspace(HY_MIN_DECAY, HY_MAX_DECAY, HY_CH, dtype=F32))
    dl_col = jnp.tile(deltas, 4)[:, None]
    w1t = jnp.pad(w1.astype(F32).T, ((0, 0), (0, LANES - HY_EMB)))
    full = lambda shape: pl.BlockSpec(shape, lambda i: (0,) * len(shape))
    n = 2 * length
    return pl.pallas_call(
        functools.partial(_hy_filter_kernel, tp=tp, length=length),
        grid=(n // tp,),
        in_specs=[full((HY_FEAT, 1)), full((HY_HIDDEN, LANES)), full((HY_HIDDEN, 1)),
                  full((HY_HIDDEN, HY_HIDDEN)), full((HY_HIDDEN, 1)),
                  full((4 * HY_CH, HY_HIDDEN)), full((4 * HY_CH, 1)), full((4 * HY_CH, 1))],
        out_specs=[pl.BlockSpec((2, HY_CH, tp), lambda i: (0, 0, i)),
                   full((2, HY_CH, 1)), full((2, HY_CH, 1))],
        out_shape=[jax.ShapeDtypeStruct((2, HY_CH, n), F32),
                   jax.ShapeDtypeStruct((2, HY_CH, 1), F32),
                   jax.ShapeDtypeStruct((2, HY_CH, 1), F32)],
        compiler_params=_params(("arbitrary",)),
        name="hyena_filters",
    )(band_col, w1t, b1.astype(F32)[:, None], w2.astype(F32).T, b2.astype(F32)[:, None],
      w3.astype(F32).T, b3.astype(F32)[:, None], dl_col)


def _dft_consts(n1, n2):
    n = n1 * n2
    a1 = 2.0 * np.pi * ((np.arange(n1)[:, None] * np.arange(n1)[None, :]) % n1) / n1
    c1, s1 = np.cos(a1), np.sin(a1)
    a2 = 2.0 * np.pi * ((np.arange(n2)[:, None] * np.arange(n2)[None, :]) % n2) / n2
    c2, s2 = np.cos(a2), np.sin(a2)
    at = 2.0 * np.pi * ((np.arange(n1)[:, None] * np.arange(n2)[None, :]) % n) / n
    f1_full = np.concatenate([c1, -s1], axis=0)
    f1_half = f1_full[:, :n1 // 2]
    g1 = np.concatenate([c1[:n1 // 2], -s1[:n1 // 2]], axis=1)
    w2f = np.block([[c2, -s2], [s2, c2]])
    w2i = np.block([[c2, s2], [-s2, c2]])
    f = lambda x: jnp.asarray(x, F32)
    return dict(f1_full=f(f1_full), f1_half=f(f1_half), g1=f(g1), w2f=f(w2f), w2i=f(w2i),
                tr=f(np.cos(at)), ti=f(-np.sin(at)))


def _fft_fwd(slabs, f1, tr, ti, w2f, stack, n1, n2, prec):
    for c, x in enumerate(slabs):
        a = _dot(f1, x.astype(stack.dtype), prec)
        ar, ai = a[:n1], a[n1:]
        stack[c * n1:(c + 1) * n1, 0:n2] = (ar * tr - ai * ti).astype(stack.dtype)
        stack[c * n1:(c + 1) * n1, n2:2 * n2] = (ar * ti + ai * tr).astype(stack.dtype)
    return _dot(stack[...], w2f, prec)


def _hy_spec_kernel(nrm_ref, k_ref, f1_ref, tr_ref, ti_ref, w2f_ref, o_ref, stack, *, cg, n1, n2):
    o = pl.program_id(0)
    g = pl.program_id(1)
    x = _fft_fwd([k_ref[c] for c in range(cg)], f1_ref[...], tr_ref[...], ti_ref[...], w2f_ref[...],
                 stack, n1, n2, HI)
    for c in range(cg):
        sc = nrm_ref[o * HY_CH + g * cg + c] * (1.0 / (n1 * n2))
        xc = x[c * n1:(c + 1) * n1] * sc
        o_ref[c, 0] = xc[:, :n2]
        o_ref[c, 1] = xc[:, n2:]


def _hy_spec(k4, nrm_flat, dc, cg, n1, n2):
    full = lambda shape: pl.BlockSpec(shape, lambda o, g: (0,) * len(shape))
    return pl.pallas_call(
        functools.partial(_hy_spec_kernel, cg=cg, n1=n1, n2=n2),
        grid=(2, HY_CH // cg),
        in_specs=[pl.BlockSpec(memory_space=pltpu.SMEM),
                  pl.BlockSpec((None, cg, n1, n2), lambda o, g: (o, g, 0, 0)),
                  full((2 * n1, n1)), full((n1, n2)), full((n1, n2)), full((2 * n2, 2 * n2))],
        out_specs=pl.BlockSpec((None, cg, 2, n1, n2), lambda o, g: (o, g, 0, 0, 0)),
        out_shape=jax.ShapeDtypeStruct((2, HY_CH, 2, n1, n2), F32),
        scratch_shapes=[pltpu.VMEM((cg * n1, 2 * n2), F32)],
        compiler_params=_params(("parallel", "arbitrary")),
        name="hyena_filter_spectrum",
    )(nrm_flat, k4, dc["f1_full"], dc["tr"], dc["ti"], dc["w2f"])


def _hy_conv_kernel(skip_ref, x_ref, ks_ref, f1_ref, g1_ref, tr_ref, ti_ref, w2f_ref, w2i_ref,
                    o_ref, stack, *, cg, n1, n2):
    g = pl.program_id(1)
    tr = tr_ref[...]
    ti = ti_ref[...]

    def conv(slabs, order):
        x = _fft_fwd(slabs, f1_ref[...], tr, ti, w2f_ref[...], stack, n1, n2, None)
        for c in range(cg):
            xr, xi = x[c * n1:(c + 1) * n1, :n2], x[c * n1:(c + 1) * n1, n2:]
            kr, ki = ks_ref[order, c, 0], ks_ref[order, c, 1]
            stack[c * n1:(c + 1) * n1, 0:n2] = (xr * kr - xi * ki).astype(BF16)
            stack[c * n1:(c + 1) * n1, n2:2 * n2] = (xr * ki + xi * kr).astype(BF16)
        bm = _dot(stack[...], w2i_ref[...])
        outs = []
        for c in range(cg):
            br, bi = bm[c * n1:(c + 1) * n1, :n2], bm[c * n1:(c + 1) * n1, n2:]
            b2 = jnp.concatenate([br * tr + bi * ti, bi * tr - br * ti], axis=0)
            y = _dot(g1_ref[...], b2.astype(BF16))
            outs.append(y + slabs[c] * skip_ref[order * HY_CH + g * cg + c])
        return outs

    v = [x_ref[2, c] for c in range(cg)]
    y0 = conv(v, 0)
    z = [x_ref[0, c] * y0[c] for c in range(cg)]
    y1 = conv(z, 1)
    for c in range(cg):
        o_ref[c] = x_ref[1, c] * y1[c]


def _hy_conv(x4, kspec, skip_flat, dc, cg, n1, n2):
    bsz = x4.shape[1]
    full = lambda shape: pl.BlockSpec(shape, lambda b, g: (0,) * len(shape))
    return pl.pallas_call(
        functools.partial(_hy_conv_kernel, cg=cg, n1=n1, n2=n2),
        grid=(bsz, HY_CH // cg),
        in_specs=[pl.BlockSpec(memory_space=pltpu.SMEM),
                  pl.BlockSpec((3, None, cg, n1 // 2, n2), lambda b, g: (0, b, g, 0, 0)),
                  pl.BlockSpec((2, cg, 2, n1, n2), lambda b, g: (0, g, 0, 0, 0)),
                  full((2 * n1, n1 // 2)), full((n1 // 2, 2 * n1)), full((n1, n2)), full((n1, n2)),
                  full((2 * n2, 2 * n2)), full((2 * n2, 2 * n2))],
        out_specs=pl.BlockSpec((None, cg, n1 // 2, n2), lambda b, g: (b, g, 0, 0)),
        out_shape=jax.ShapeDtypeStruct((bsz, HY_CH, n1 // 2, n2), F32),
        scratch_shapes=[pltpu.VMEM((cg * n1, 2 * n2), BF16)],
        compiler_params=_params(("parallel", "arbitrary")),
        name="hyena_long_conv",
    )(skip_flat, x4, kspec, dc["f1_half"].astype(BF16), dc["g1"].astype(BF16), dc["tr"], dc["ti"],
      dc["w2f"].astype(BF16), dc["w2i"].astype(BF16))


def _hy_ctx_kernel(x_ref, k_ref, nrm_ref, skip_ref, fc_ref, fs_ref, o_ref, *, c):
    fc = fc_ref[...]
    fs = fs_ref[...]
    inv_n = 1.0 / (2 * c)

    def conv(x, order):
        kk = k_ref[order]
        kr, ki = _dot(kk, fc, HI), -_dot(kk, fs, HI)
        xr, xi = _dot(x, fc[:c], HI), -_dot(x, fs[:c], HI)
        yr, yi = xr * kr - xi * ki, xr * ki + xi * kr
        y = (_dot(yr, fc[:, :c], HI) - _dot(yi, fs[:, :c], HI)) * inv_n
        return y * nrm_ref[order] + x * skip_ref[order]

    z = x_ref[0] * conv(x_ref[2], 0)
    o_ref[...] = x_ref[1] * conv(z, 1)


def _hy_ctx_conv(xt, kt, nrm, skip_col):
    _, bsz, ch, c = xt.shape
    n = 2 * c
    ang = 2.0 * np.pi * ((np.arange(n)[:, None] * np.arange(n)[None, :]) % n) / n
    fc, fs = jnp.asarray(np.cos(ang), F32), jnp.asarray(np.sin(ang), F32)
    full = lambda shape: pl.BlockSpec(shape, lambda b: (0,) * len(shape))
    return pl.pallas_call(
        functools.partial(_hy_ctx_kernel, c=c),
        grid=(bsz,),
        in_specs=[pl.BlockSpec((3, None, ch, c), lambda b: (0, b, 0, 0)),
                  full((2, ch, n)), full((2, ch, 1)), full((2, ch, 1)), full((n, n)), full((n, n))],
        out_specs=pl.BlockSpec((None, ch, c), lambda b: (b, 0, 0)),
        out_shape=jax.ShapeDtypeStruct((bsz, ch, c), F32),
        compiler_params=_params(("arbitrary",)),
        name="hyena_ctx_conv",
    )(xt, kt, nrm, skip_col, fc, fs)


def _out_proj_kernel(a_ref, b_ref, p_ref, ht_ref, x_ref, g_ref, w_ref, o_ref):
    w = D_GROUP
    acc = _dot(a_ref[...].astype(BF16), w_ref[0:w])
    acc += _dot(b_ref[...].astype(BF16), w_ref[w:2 * w])
    acc += _dot(p_ref[...].astype(BF16), w_ref[2 * w:3 * w])
    acc += _dot(ht_ref[...].T.astype(BF16), w_ref[3 * w:4 * w])
    o_ref[...] = x_ref[...] + g_ref[...] * acc


def _out_proj(a, b, p, ht, x, mod3, row_of_batch, j_gate, w_out, tm):
    bsz, t, d = x.shape
    w = D_GROUP
    tok = pl.BlockSpec((None, tm, w), lambda bb, i: (bb, i, 0))
    return pl.pallas_call(
        _out_proj_kernel,
        grid=(bsz, t // tm),
        in_specs=[tok, tok, tok,
                  pl.BlockSpec((None, w, tm), lambda bb, i: (bb, 0, i)),
                  pl.BlockSpec((None, tm, d), lambda bb, i: (bb, i, 0)),
                  pl.BlockSpec((None, 1, d), lambda bb, i: (row_of_batch(bb), 0, j_gate)),
                  pl.BlockSpec((4 * w, d), lambda bb, i: (0, 0))],
        out_specs=pl.BlockSpec((None, tm, d), lambda bb, i: (bb, i, 0)),
        out_shape=jax.ShapeDtypeStruct((bsz, t, d), F32),
        compiler_params=_params(("parallel", "arbitrary")),
        name="out_proj_residual",
    )(a, b, p, ht, x, mod3, w_out)


MOE_ROWS = 256


def _moe_kernel(x_ref, g_ref, sc_ref, sh_ref, gate_ref, rw_ref, rb_ref, w1_ref, w3_ref, w2_ref, fg_ref,
                o_ref, h_scr, gates_scr, acc_scr, *, final):
    e = pl.program_id(2)
    per_group = N_EXPERTS // N_EXPERT_GROUPS
    tm = x_ref.shape[0]

    @pl.when(e == 0)
    def _():
        x = x_ref[...]
        ms = jnp.mean(x * x, axis=-1, keepdims=True)
        h = (x * lax.rsqrt(ms + EPS)) * g_ref[...] * (1.0 + sc_ref[...]) + sh_ref[...]
        h_scr[...] = h.astype(BF16)
        acc_scr[...] = jnp.zeros(acc_scr.shape, F32)
        lane = lax.broadcasted_iota(jnp.int32, (tm, LANES), 1)
        valid = lane < N_EXPERTS
        neg = -jnp.inf
        logits = jnp.where(valid, _dot(h, rw_ref[...], HI), neg)
        mx = jnp.max(logits, axis=1, keepdims=True)
        ex = jnp.exp(logits - mx)
        scores = ex / jnp.sum(ex, axis=1, keepdims=True)
        sel = scores + rb_ref[...]
        big = jnp.int32(LANES)
        best = None
        for g in range(N_EXPERT_GROUPS):
            in_g = (lane >= g * per_group) & (lane < (g + 1) * per_group)
            v1 = jnp.max(jnp.where(in_g, sel, neg), axis=1, keepdims=True)
            i1 = jnp.min(jnp.where(in_g & (sel == v1), lane, big), axis=1, keepdims=True)
            rest = in_g & (lane != i1)
            v2 = jnp.max(jnp.where(rest, sel, neg), axis=1, keepdims=True)
            i2 = jnp.min(jnp.where(rest & (sel == v2), lane, big), axis=1, keepdims=True)
            gs = v1 + v2
            if best is None:
                best, e1, e2 = gs, i1, i2
            else:
                upd = gs > best
                best = jnp.where(upd, gs, best)
                e1 = jnp.where(upd, i1, e1)
                e2 = jnp.where(upd, i2, e2)
        w1 = jnp.sum(jnp.where(lane == e1, scores, 0.0), axis=1, keepdims=True)
        w2 = jnp.sum(jnp.where(lane == e2, scores, 0.0), axis=1, keepdims=True)
        tot = w1 + w2
        gates_scr[...] = jnp.where(lane == e1, w1 / tot, 0.0) + jnp.where(lane == e2, w2 / tot, 0.0)

    lane = lax.broadcasted_iota(jnp.int32, (tm, LANES), 1)
    ge = jnp.sum(jnp.where(lane == e, gates_scr[...], 0.0), axis=1, keepdims=True)

    def up(j):
        hb = h_scr[j * MOE_ROWS:(j + 1) * MOE_ROWS, :]
        return _dot(hb, w1_ref[...]), _dot(hb, w3_ref[...])

    nxt = up(0)
    for j in range(tm // MOE_ROWS):
        a, b = nxt
        if (j + 1) * MOE_ROWS < tm:
            nxt = up(j + 1)
        rows = slice(j * MOE_ROWS, (j + 1) * MOE_ROWS)
        act = (a * jax.nn.sigmoid(a)) * b
        acc_scr[rows, :] += ge[rows] * _dot(act.astype(BF16), w2_ref[...])

    @pl.when(e == N_EXPERTS - 1)
    def _():
        y = x_ref[...] + gate_ref[...] * acc_scr[...]
        if final:
            ms = jnp.mean(y * y, axis=-1, keepdims=True)
            y = (y * lax.rsqrt(ms + EPS)) * fg_ref[...]
        o_ref[...] = y


def _moe(x, g, mod3, row_of_batch, j_shift, j_scale, j_gate, rw, rb, w1, w3, w2, final_g, final, tm):
    bsz, t, d = x.shape
    vec = lambda j: pl.BlockSpec((None, 1, d), lambda b, i, e: (row_of_batch(b), 0, j))
    full = lambda shape: pl.BlockSpec(shape, lambda b, i, e: (0,) * len(shape))
    return pl.pallas_call(
        functools.partial(_moe_kernel, final=final),
        grid=(bsz, t // tm, N_EXPERTS),
        in_specs=[pl.BlockSpec((None, tm, d), lambda b, i, e: (b, i, 0)),
                  full((1, d)), vec(j_scale), vec(j_shift), vec(j_gate),
                  full((d, LANES)), full((1, LANES)),
                  pl.BlockSpec((None, d, D_EXPERT), lambda b, i, e: (e, 0, 0)),
                  pl.BlockSpec((None, d, D_EXPERT), lambda b, i, e: (e, 0, 0)),
                  pl.BlockSpec((None, D_EXPERT, d), lambda b, i, e: (e, 0, 0)),
                  full((1, d))],
        out_specs=pl.BlockSpec((None, tm, d), lambda b, i, e: (b, i, 0)),
        out_shape=jax.ShapeDtypeStruct((bsz, t, d), F32),
        scratch_shapes=[pltpu.VMEM((tm, d), BF16), pltpu.VMEM((tm, LANES), F32), pltpu.VMEM((tm, d), F32)],
        compiler_params=_params(("parallel", "parallel", "arbitrary")),
        name="moe_final" if final else "moe",
    )(x, g, mod3, mod3, mod3, rw, rb, w1, w3, w2, final_g)


def _tile(n, pref):
    t = min(n, pref)
    assert n % t == 0
    return t


def _fft_split(n):
    n2 = LANES
    assert n % n2 == 0
    return n // n2, n2


def _hyena_latent(u, lp, fargs):
    bsz, s, _ = u.shape
    n1, n2 = _fft_split(2 * s)
    dc = _dft_consts(n1, n2)
    cg = 8
    kt_l, _, nrm_l = _hy_filters(s, *fargs, tp=_tile(2 * s, 1024))
    kspec = _hy_spec(kt_l.reshape(2, HY_CH, n1, n2), nrm_l.reshape(2 * HY_CH), dc, cg, n1, n2)
    xt = _hy_short(u, lp["hy_short_w"].astype(F32), lp["hy_short_b"].astype(F32)[None, :], _tile(s, 512))
    h_l = _hy_conv(xt.reshape(3, bsz, HY_CH, n1 // 2, n2), kspec, lp["hy_skip"].astype(F32).reshape(2 * HY_CH),
                   dc, cg, n1, n2)
    return h_l.reshape(bsz, HY_CH, s)


def _mixers(u, uc, lp, li, need_ctx):
    bsz, s, _ = u.shape
    c = uc.shape[1]
    lam_init = 0.8 - 0.6 * math.exp(-0.3 * li)
    lam_vecs = lp["a_lambda"].astype(F32)
    subln = lp["a_subln_g"].astype(F32)[None, :]

    qc, kct, vc = _attn_prep(uc, _tile(c, 256), rope=False)
    ql, klt, vl = _attn_prep(u, _tile(s, 256), rope=True)
    a_l = _diff_attn(ql, kct, vc, klt, vl, lam_vecs, subln, lam_init, _tile(s, 512), _tile(s, 2048))
    bias8 = _nbr_bias(lp["b_rpb"])
    b_l = _nbr_attn(u, uc, bias8)
    wbd = jax.scipy.linalg.block_diag(*[lp["pool_w"][g] for g in range(len(POOL_SIZES))]).astype(BF16)
    pscale = lp["pool_scale"].astype(F32)[None, :]
    p_l = _pool_mix(u, wbd, pscale, _tile(s, 512))
    fargs = (lp["hy_f_w1"], lp["hy_f_b1"], lp["hy_f_w2"], lp["hy_f_b2"], lp["hy_f_w3"], lp["hy_f_b3"])
    skip = lp["hy_skip"].astype(F32)
    w_short = lp["hy_short_w"].astype(F32)
    b_short = lp["hy_short_b"].astype(F32)[None, :]
    h_l = _hyena_latent(u, lp, fargs)
    lat = (a_l, b_l, p_l, h_l)
    if not need_ctx:
        return lat, None
    a_c = _diff_attn(qc, kct, vc, None, None, lam_vecs, subln, lam_init, _tile(c, 256), None)
    b_c = _nbr_ctx_attn(uc)
    p_c = _pool_mix(uc, wbd, pscale, _tile(c, 256))
    kt_c, _, nrm_c = _hy_filters(c, *fargs, tp=_tile(2 * c, 512))
    xtc = _hy_short(uc, w_short, b_short, _tile(c, 256))
    h_c = _hy_ctx_conv(xtc, kt_c, nrm_c, skip[:, :, None])
    return lat, (a_c, b_c, p_c, h_c)


def kernel(x, c, ctx, c_ctx, norm1_g, norm2_g, ada_w, ada_b, w_in, w_out, a_lambda, a_subln_g, b_rpb, pool_w, pool_scale, hy_short_w, hy_short_b, hy_f_w1, hy_f_b1, hy_f_w2, hy_f_b2, hy_f_w3, hy_f_b3, hy_skip, router_w, router_b, moe_w1, moe_w3, moe_w2, final_g):
    depth = norm1_g.shape[0]
    bsz, s, d = x.shape
    cl = ctx.shape[1]
    assert bsz <= SUBLANES - 1
    xl, xc = x, ctx
    cpad = jnp.zeros((SUBLANES, d), F32).at[:bsz].set(c.astype(F32)).at[bsz].set(c_ctx.astype(F32))
    rw = jnp.pad(router_w.astype(F32), ((0, 0), (0, LANES - N_EXPERTS)))
    rb = jnp.pad(router_b.astype(F32), (0, LANES - N_EXPERTS))[None, :]
    lat_row = lambda b: b
    ctx_row = lambda b: bsz
    fg = final_g.astype(F32)[None, :]
    tm = _tile(s, 512)
    tmc = _tile(cl, 256)
    for li in range(depth):
        need_ctx = li < depth - 1
        lp = dict(a_lambda=a_lambda[li], a_subln_g=a_subln_g[li], b_rpb=b_rpb[li], pool_w=pool_w[li],
                  pool_scale=pool_scale[li], hy_short_w=hy_short_w[li], hy_short_b=hy_short_b[li],
                  hy_f_w1=hy_f_w1[li], hy_f_b1=hy_f_b1[li], hy_f_w2=hy_f_w2[li], hy_f_b2=hy_f_b2[li],
                  hy_f_w3=hy_f_w3[li], hy_f_b3=hy_f_b3[li], hy_skip=hy_skip[li])
        mod3 = _ada(cpad, ada_w[li].astype(F32), ada_b[li].astype(F32)[None, :]).reshape(SUBLANES, 1, 6 * d)
        n1g = norm1_g[li].astype(F32)[None, :]
        n2g = norm2_g[li].astype(F32)[None, :]
        w_in_b = w_in[li].astype(BF16)
        w_out_b = w_out[li].astype(BF16)
        u = _norm_proj(xl, n1g, mod3, lat_row, 0, 1, w_in_b, tm)
        uc = _norm_proj(xc, n1g, mod3, ctx_row, 0, 1, w_in_b, tmc)
        lat, cx = _mixers(u, uc, lp, li, need_ctx)
        xl = _out_proj(*lat, xl, mod3, lat_row, 2, w_out_b, tm)
        w1b, w3b, w2b = moe_w1[li].astype(BF16), moe_w3[li].astype(BF16), moe_w2[li].astype(BF16)
        if need_ctx:
            xc = _out_proj(*cx, xc, mod3, ctx_row, 2, w_out_b, tmc)
            xc = _moe(xc, n2g, mod3, ctx_row, 3, 4, 5, rw, rb, w1b, w3b, w2b, fg, False, tmc)
        xl = _moe(xl, n2g, mod3, lat_row, 3, 4, 5, rw, rb, w1b, w3b, w2b, fg, li == depth - 1, tm)
    return xl
```

```python
import functools
import math

import numpy as np
import jax
import jax.numpy as jnp
from jax import lax
from jax.experimental import pallas as pl
from jax.experimental.pallas import tpu as pltpu

F32 = jnp.float32
BF16 = jnp.bfloat16
HI = lax.Precision.HIGHEST

GRID_W = 64
A_HEADS = 4
A_QK = 32
A_V = 64
ROPE_BASE = 10000.0
B_HEADS = 4
B_DIM = 64
WIN_R = 8
WIN_C = 16
POOL_SIZES = (2, 4, 8, 16)
POOL_CH = 64
D_GROUP = 256
HY_CH = 256
HY_BANDS = 16
HY_EMB = 1 + 2 * HY_BANDS
HY_HIDDEN = 64
HY_SIN_FREQ = 1.0
HY_MIN_DECAY = math.log(1e-2) / 1.5
HY_MAX_DECAY = math.log(1e-2) / 0.3
N_EXPERTS = 16
N_EXPERT_GROUPS = 4
D_EXPERT = 512
EPS = 1e-6
LOG2E = 1.4426950408889634

LANES = 128
SUBLANES = 8
VMEM_LIMIT = 56 * 1024 * 1024


def _params(sem):
    return pltpu.CompilerParams(dimension_semantics=sem, vmem_limit_bytes=VMEM_LIMIT)


def _dot(a, b, prec=None):
    return jnp.dot(a, b, precision=prec, preferred_element_type=F32)


def _dot_nt(a, b):
    return lax.dot_general(a, b, (((1,), (1,)), ((), ())), preferred_element_type=F32)


def _ada_kernel(c_ref, w_ref, b_ref, o_ref):
    cf = c_ref[...]
    s = cf * jax.nn.sigmoid(cf)
    o_ref[...] = _dot(s, w_ref[...], HI) + b_ref[...]


def _ada(cpad, w, b):
    d = cpad.shape[1]
    n = w.shape[1]
    return pl.pallas_call(
        _ada_kernel,
        grid=(n // d,),
        in_specs=[pl.BlockSpec((SUBLANES, d), lambda j: (0, 0)),
                  pl.BlockSpec((d, d), lambda j: (0, j)),
                  pl.BlockSpec((1, d), lambda j: (0, j))],
        out_specs=pl.BlockSpec((SUBLANES, d), lambda j: (0, j)),
        out_shape=jax.ShapeDtypeStruct((SUBLANES, n), F32),
        compiler_params=_params(("arbitrary",)),
        name="ada_mod",
    )(cpad, w, b)


def _norm_proj_kernel(x_ref, g_ref, sc_ref, sh_ref, w_ref, o_ref):
    x = x_ref[...]
    ms = jnp.mean(x * x, axis=-1, keepdims=True)
    h = (x * lax.rsqrt(ms + EPS)) * g_ref[...] * (1.0 + sc_ref[...]) + sh_ref[...]
    o_ref[...] = _dot(h.astype(BF16), w_ref[...])


def _norm_proj(x, g, mod3, row_of_batch, j_shift, j_scale, w, tm):
    bsz, t, d = x.shape
    n = w.shape[1]
    return pl.pallas_call(
        _norm_proj_kernel,
        grid=(bsz, t // tm),
        in_specs=[pl.BlockSpec((None, tm, d), lambda b, i: (b, i, 0)),
                  pl.BlockSpec((1, d), lambda b, i: (0, 0)),
                  pl.BlockSpec((None, 1, d), lambda b, i: (row_of_batch(b), 0, j_scale)),
                  pl.BlockSpec((None, 1, d), lambda b, i: (row_of_batch(b), 0, j_shift)),
                  pl.BlockSpec((d, n), lambda b, i: (0, 0))],
        out_specs=pl.BlockSpec((None, tm, n), lambda b, i: (b, i, 0)),
        out_shape=jax.ShapeDtypeStruct((bsz, t, n), F32),
        compiler_params=_params(("parallel", "arbitrary")),
        name="norm_in_proj",
    )(x, g, mod3, mod3, w)


def _aprep_kernel(*refs, rope):
    if rope:
        u_ref, cos_ref, sin_ref, q_ref, kt_ref, v_ref = refs
    else:
        u_ref, q_ref, kt_ref, v_ref = refs
    u = u_ref[...]
    q = u[:, 0:256]
    k = u[:, 256:512]
    v = u[:, 512:768]
    if rope:
        cos_t = cos_ref[...]
        sin_t = sin_ref[...]
        lane = lax.broadcasted_iota(jnp.int32, cos_t.shape, 1)
        first = (lane % (2 * 16)) < 16

        def rot(x):
            halves = []
            for j in range(2):
                xh = x[:, j * LANES:(j + 1) * LANES]
                swap = jnp.where(first, pltpu.roll(xh, LANES - 16, axis=1), pltpu.roll(xh, 16, axis=1))
                halves.append(xh * cos_t + swap * sin_t)
            return jnp.concatenate(halves, axis=1)

        q = rot(q)
        k = rot(k)
    q = q * (A_QK ** -0.5 * LOG2E)
    kt = k.T
    for hc in range(2 * A_HEADS):
        q_ref[hc] = q[:, hc * A_QK:(hc + 1) * A_QK].astype(BF16)
        kt_ref[hc] = kt[hc * A_QK:(hc + 1) * A_QK, :].astype(BF16)
    lane = lax.broadcasted_iota(jnp.int32, (v.shape[0], LANES - A_V), 1)
    ones_col = jnp.where(lane == 0, 1.0, 0.0)
    for h in range(A_HEADS):
        v_ref[h] = jnp.concatenate([v[:, h * A_V:(h + 1) * A_V], ones_col], axis=1).astype(BF16)


def _rope_tables(length):
    n_freq = A_QK // 4
    inv = ROPE_BASE ** (-jnp.arange(n_freq, dtype=F32) / n_freq)
    t = jnp.arange(length)
    row = (t // GRID_W).astype(F32)
    col = (t % GRID_W).astype(F32)
    ang = jnp.concatenate([row[:, None] * inv, col[:, None] * inv], axis=-1)
    cos, sin = jnp.cos(ang), jnp.sin(ang)
    cos_t = jnp.tile(jnp.concatenate([cos, cos], axis=-1), (1, LANES // 32))
    sin_t = jnp.tile(jnp.concatenate([-sin, sin], axis=-1), (1, LANES // 32))
    return cos_t, sin_t


def _attn_prep(u, tm, rope):
    bsz, t, _ = u.shape
    nh = 2 * A_HEADS
    in_specs = [pl.BlockSpec((None, tm, 768), lambda b, i: (b, i, 0))]
    args = [u]
    if rope:
        cos_t, sin_t = _rope_tables(t)
        in_specs += [pl.BlockSpec((tm, LANES), lambda b, i: (i, 0))] * 2
        args += [cos_t, sin_t]
    return pl.pallas_call(
        functools.partial(_aprep_kernel, rope=rope),
        grid=(bsz, t // tm),
        in_specs=in_specs,
        out_specs=[pl.BlockSpec((None, nh, tm, A_QK), lambda b, i: (b, 0, i, 0)),
                   pl.BlockSpec((None, nh, A_QK, tm), lambda b, i: (b, 0, 0, i)),
                   pl.BlockSpec((None, A_HEADS, tm, LANES), lambda b, i: (b, 0, i, 0))],
        out_shape=[jax.ShapeDtypeStruct((bsz, nh, t, A_QK), BF16),
                   jax.ShapeDtypeStruct((bsz, nh, A_QK, t), BF16),
                   jax.ShapeDtypeStruct((bsz, A_HEADS, t, LANES), BF16)],
        compiler_params=_params(("parallel", "arbitrary")),
        name="attn_prep_rope" if rope else "attn_prep_ctx",
    )(*args)


QK_LOOKAHEAD = 3


def _dattn_kernel(*refs, lam_init, has_lat):
    if has_lat:
        lam_ref, g_ref, q_ref, kc_ref, vc_ref, k_ref, v_ref, o_ref, m_scr, acc_scr = refs
    else:
        lam_ref, g_ref, q_ref, kc_ref, vc_ref, o_ref, m_scr, acc_scr = refs
    ki = pl.program_id(2)
    nk = pl.num_programs(2)
    nh = 2 * A_HEADS

    def update(kt_r, v_r):
        scores = [_dot(q_ref[j], kt_r[j]) for j in range(QK_LOOKAHEAD)]
        for hc in range(nh):
            s = scores[hc]
            if hc + QK_LOOKAHEAD < nh:
                scores.append(_dot(q_ref[hc + QK_LOOKAHEAD], kt_r[hc + QK_LOOKAHEAD]))
            m_prev = m_scr[hc]
            m_new = jnp.maximum(m_prev, jnp.max(s, axis=1, keepdims=True))
            alpha = jnp.exp2(m_prev - m_new)
            p = jnp.exp2((s - m_new[:, :1]).astype(BF16))
            acc_scr[hc] = alpha * acc_scr[hc] + _dot(p, v_r[hc // 2])
            m_scr[hc] = m_new

    @pl.when(ki == 0)
    def _():
        m_scr[...] = jnp.full(m_scr.shape, -jnp.inf, F32)
        acc_scr[...] = jnp.zeros(acc_scr.shape, F32)
        update(kc_ref, vc_ref)

    if has_lat:
        @pl.when(ki > 0)
        def _():
            update(k_ref, v_ref)

    @pl.when(ki == nk - 1)
    def _():
        lv = lam_ref[...]
        lam = (jnp.exp(jnp.sum(lv[0:1] * lv[1:2], axis=1, keepdims=True))
               - jnp.exp(jnp.sum(lv[2:3] * lv[3:4], axis=1, keepdims=True)) + lam_init)
        for h in range(A_HEADS):
            a0 = acc_scr[2 * h]
            a1 = acc_scr[2 * h + 1]
            o = a0[:, :A_V] / a0[:, A_V:A_V + 1] - lam * (a1[:, :A_V] / a1[:, A_V:A_V + 1])
            ms = jnp.mean(o * o, axis=-1, keepdims=True)
            o_ref[:, h * A_V:(h + 1) * A_V] = (o * lax.rsqrt(ms + EPS)) * g_ref[...] * (1.0 - lam_init)


def _diff_attn(q, kct, vc, kt, v, lam_vecs, subln_g, lam_init, tq, tk):
    bsz, nh, t, _ = q.shape
    c = kct.shape[-1]
    has_lat = kt is not None
    nk = 1 + (kt.shape[-1] // tk if has_lat else 0)
    in_specs = [pl.BlockSpec((4, A_QK), lambda b, i, k: (0, 0)),
                pl.BlockSpec((1, A_V), lambda b, i, k: (0, 0)),
                pl.BlockSpec((None, nh, tq, A_QK), lambda b, i, k: (b, 0, i, 0)),
                pl.BlockSpec((None, nh, A_QK, c), lambda b, i, k: (b, 0, 0, 0)),
                pl.BlockSpec((None, A_HEADS, c, LANES), lambda b, i, k: (b, 0, 0, 0))]
    args = [lam_vecs, subln_g, q, kct, vc]
    if has_lat:
        in_specs += [pl.BlockSpec((None, nh, A_QK, tk), lambda b, i, k: (b, 0, 0, jnp.maximum(k - 1, 0))),
                     pl.BlockSpec((None, A_HEADS, tk, LANES), lambda b, i, k: (b, 0, jnp.maximum(k - 1, 0), 0))]
        args += [kt, v]
    return pl.pallas_call(
        functools.partial(_dattn_kernel, lam_init=lam_init, has_lat=has_lat),
        grid=(bsz, t // tq, nk),
        in_specs=in_specs,
        out_specs=pl.BlockSpec((None, tq, A_HEADS * A_V), lambda b, i, k: (b, i, 0)),
        out_shape=jax.ShapeDtypeStruct((bsz, t, A_HEADS * A_V), F32),
        scratch_shapes=[pltpu.VMEM((nh, tq, LANES), F32),
                        pltpu.VMEM((nh, tq, LANES), F32)],
        compiler_params=_params(("parallel", "parallel", "arbitrary")),
        name="diff_attn" if has_lat else "diff_attn_ctx",
    )(*args)


NB_ROWS = 8


def _nbr_bias(rpb):
    cols = jnp.arange(GRID_W)
    c0 = jnp.clip(cols - WIN_C // 2, 0, GRID_W - WIN_C)
    in_win = (cols[None, :] >= c0[:, None]) & (cols[None, :] < c0[:, None] + WIN_C)
    dc = jnp.clip(cols[None, :] - cols[:, None], -(WIN_C - 1), WIN_C - 1) + (WIN_C - 1)
    onehot = (dc[None] == jnp.arange(2 * WIN_C - 1)[:, None, None]).astype(F32)
    g = jnp.einsum("hab,bqk->haqk", rpb.astype(F32), onehot, precision=HI)
    g = jnp.where(in_win[None, None], g, -jnp.inf)
    b = jnp.stack([g[:, a0:a0 + WIN_R] for a0 in range(WIN_R)], axis=0)
    b = jnp.transpose(b, (0, 1, 3, 2, 4))
    return b.reshape(WIN_R, B_HEADS, GRID_W, WIN_R * GRID_W)


def _nbr_kernel(q_ref, kp_ref, kc_ref, kn_ref, vp_ref, vcur_ref, vn_ref, kctx_ref, vctx_ref, bias_ref,
                o_ref, kwin, vwin, kcx, vcx, *, n_rows):
    rb = pl.program_id(1)
    blk = NB_ROWS * GRID_W
    scale = B_DIM ** -0.5
    for h in range(B_HEADS):
        sl = slice(h * B_DIM, (h + 1) * B_DIM)
        for j, (kr, vr) in enumerate(((kp_ref, vp_ref), (kc_ref, vcur_ref), (kn_ref, vn_ref))):
            kwin[h, j * blk:(j + 1) * blk, :] = kr[:, sl].astype(BF16)
            vwin[h, j * blk:(j + 1) * blk, :] = vr[:, sl].astype(BF16)
        kcx[h] = kctx_ref[:, sl].astype(BF16)
        vcx[h] = vctx_ref[:, sl].astype(BF16)

    def window(rr):
        r = rb * NB_ROWS + rr
        r0 = jnp.clip(r - WIN_R // 2, 0, n_rows - WIN_R)
        off = pl.multiple_of((r0 - (rb - 1) * NB_ROWS) * GRID_W, GRID_W)
        return off, r0 - r + (WIN_R - 1)

    def scores(rr):
        off, a0 = window(rr)
        qrow = q_ref[rr * GRID_W:(rr + 1) * GRID_W, :]
        out = []
        for h in range(B_HEADS):
            qh = qrow[:, h * B_DIM:(h + 1) * B_DIM].astype(BF16)
            s = _dot_nt(qh, kwin[h, pl.ds(off, WIN_R * GRID_W), :]) * scale + bias_ref[a0, h]
            out.append((s, _dot_nt(qh, kcx[h]) * scale))
        return out

    nxt = scores(0)
    for rr in range(NB_ROWS):
        cur = nxt
        if rr + 1 < NB_ROWS:
            nxt = scores(rr + 1)
        off, _ = window(rr)
        outs = []
        for h in range(B_HEADS):
            s, sc = cur[h]
            m = jnp.maximum(jnp.max(s, axis=1, keepdims=True), jnp.max(sc, axis=1, keepdims=True))
            p = jnp.exp(s - m)
            pc = jnp.exp(sc - m)
            l = jnp.sum(p, axis=1, keepdims=True) + jnp.sum(pc, axis=1, keepdims=True)
            o = _dot(p.astype(BF16), vwin[h, pl.ds(off, WIN_R * GRID_W), :]) + _dot(pc.astype(BF16), vcx[h])
            outs.append(o / l)
        o_ref[rr * GRID_W:(rr + 1) * GRID_W, :] = jnp.concatenate(outs, axis=1)


def _nbr_attn(u, uc, bias8):
    bsz, s, _ = u.shape
    c = uc.shape[1]
    n_rows = s // GRID_W
    nb = n_rows // NB_ROWS
    blk = NB_ROWS * GRID_W
    w = B_HEADS * B_DIM

    def spec(col, shift):
        return pl.BlockSpec((None, blk, w), lambda b, i: (b, jnp.clip(i + shift, 0, nb - 1), col))

    return pl.pallas_call(
        functools.partial(_nbr_kernel, n_rows=n_rows),
        grid=(bsz, nb),
        in_specs=[spec(3, 0), spec(4, -1), spec(4, 0), spec(4, 1), spec(5, -1), spec(5, 0), spec(5, 1),
                  pl.BlockSpec((None, c, w), lambda b, i: (b, 0, 4)),
                  pl.BlockSpec((None, c, w), lambda b, i: (b, 0, 5)),
                  pl.BlockSpec(bias8.shape, lambda b, i: (0, 0, 0, 0))],
        out_specs=pl.BlockSpec((None, blk, w), lambda b, i: (b, i, 0)),
        out_shape=jax.ShapeDtypeStruct((bsz, s, w), F32),
        scratch_shapes=[pltpu.VMEM((B_HEADS, 3 * blk, B_DIM), BF16),
                        pltpu.VMEM((B_HEADS, 3 * blk, B_DIM), BF16),
                        pltpu.VMEM((B_HEADS, c, B_DIM), BF16),
                        pltpu.VMEM((B_HEADS, c, B_DIM), BF16)],
        compiler_params=_params(("parallel", "arbitrary")),
        name="nbr_attn",
    )(u, u, u, u, u, u, u, uc, uc, bias8)


def _nbr_ctx_kernel(q_ref, k_ref, v_ref, o_ref):
    scale = B_DIM ** -0.5
    outs = []
    for h in range(B_HEADS):
        sl = slice(h * B_DIM, (h + 1) * B_DIM)
        s = _dot_nt(q_ref[:, sl].astype(BF16), k_ref[:, sl].astype(BF16)) * scale
        m = jnp.max(s, axis=1, keepdims=True)
        p = jnp.exp(s - m)
        l = jnp.sum(p, axis=1, keepdims=True)
        outs.append(_dot(p.astype(BF16), v_ref[:, sl].astype(BF16)) / l)
    o_ref[...] = jnp.concatenate(outs, axis=1)


def _nbr_ctx_attn(uc):
    bsz, c, _ = uc.shape
    w = B_HEADS * B_DIM
    return pl.pallas_call(
        _nbr_ctx_kernel,
        grid=(bsz,),
        in_specs=[pl.BlockSpec((None, c, w), lambda b: (b, 0, 3)),
                  pl.BlockSpec((None, c, w), lambda b: (b, 0, 4)),
                  pl.BlockSpec((None, c, w), lambda b: (b, 0, 5))],
        out_specs=pl.BlockSpec((None, c, w), lambda b: (b, 0, 0)),
        out_shape=jax.ShapeDtypeStruct((bsz, c, w), F32),
        compiler_params=_params(("arbitrary",)),
        name="nbr_attn_ctx",
    )(uc, uc, uc)


HALO = SUBLANES


def _halo_specs(tm, length, col, width):
    nt = length // tm
    per = tm // HALO
    last = length // HALO - 1
    return [pl.BlockSpec((None, HALO, width), lambda b, i, *_: (b, jnp.maximum(i * per - 1, 0), col)),
            pl.BlockSpec((None, tm, width), lambda b, i, *_: (b, i, col)),
            pl.BlockSpec((None, HALO, width), lambda b, i, *_: (b, jnp.minimum((i + 1) * per, last), col))], nt


def _fill_halo(buf, prev_ref, cur_ref, next_ref, i, nt, tm):
    zero = jnp.zeros(prev_ref.shape, F32)
    buf[0:HALO, :] = jnp.where(i > 0, prev_ref[...], zero)
    buf[HALO:HALO + tm, :] = cur_ref[...]
    buf[HALO + tm:, :] = jnp.where(i < nt - 1, next_ref[...], zero)


def _pool_kernel(prev_ref, cur_ref, next_ref, w_ref, ps_ref, o_ref, buf, *, tm, nt, length):
    i = pl.program_id(1)
    _fill_halo(buf, prev_ref, cur_ref, next_ref, i, nt, tm)

    def sh(j):
        return buf[HALO + j:HALO + j + tm, :]

    u = sh(0)
    sums = []
    acc = None
    lo, hi = 0, 0
    for w in POOL_SIZES:
        for j in list(range(-(w // 2), lo)) + list(range(hi, w // 2)):
            acc = sh(j) if acc is None else acc + sh(j)
        lo, hi = -(w // 2), w // 2
        sums.append(acc)
    lane = lax.broadcasted_iota(jnp.int32, (tm, D_GROUP), 1)
    t = (i * tm + lax.broadcasted_iota(jnp.int32, (tm, D_GROUP), 0))
    wsum = sums[-1]
    half = jnp.full((tm, D_GROUP), POOL_SIZES[-1] // 2, jnp.int32)
    for g in range(len(POOL_SIZES) - 2, -1, -1):
        sel = lane < (g + 1) * POOL_CH
        wsum = jnp.where(sel, sums[g], wsum)
        half = jnp.where(sel, POOL_SIZES[g] // 2, half)
    cnt = (jnp.minimum(t + half, length) - jnp.maximum(t - half, 0)).astype(F32)
    d = wsum / cnt - u
    o_ref[...] = _dot(d.astype(BF16), w_ref[...]) * ps_ref[...]


def _pool_mix(u, wbd, pool_scale, tm):
    bsz, length, _ = u.shape
    specs, nt = _halo_specs(tm, length, 6, D_GROUP)
    return pl.pallas_call(
        functools.partial(_pool_kernel, tm=tm, nt=nt, length=length),
        grid=(bsz, nt),
        in_specs=specs + [pl.BlockSpec((D_GROUP, D_GROUP), lambda b, i: (0, 0)),
                          pl.BlockSpec((1, D_GROUP), lambda b, i: (0, 0))],
        out_specs=pl.BlockSpec((None, tm, D_GROUP), lambda b, i: (b, i, 0)),
        out_shape=jax.ShapeDtypeStruct((bsz, length, D_GROUP), F32),
        scratch_shapes=[pltpu.VMEM((tm + 2 * HALO, D_GROUP), F32)],
        compiler_params=_params(("parallel", "arbitrary")),
        name="pool_mix",
    )(u, u, u, wbd, pool_scale)


def _hy_short_kernel(prev_ref, cur_ref, next_ref, w_ref, b_ref, o_ref, buf, *, tm, nt):
    i = pl.program_id(1)
    _fill_halo(buf, prev_ref, cur_ref, next_ref, i, nt, tm)
    w = w_ref[...]
    y = (buf[HALO - 1:HALO - 1 + tm, :] * w[0:1] + buf[HALO:HALO + tm, :] * w[1:2]
         + buf[HALO + 1:HALO + 1 + tm, :] * w[2:3] + b_ref[...])
    o_ref[...] = y.T


def _hy_short(u, w_short, b_short, tm):
    bsz, length, _ = u.shape
    nt = length // tm
    per = tm // HALO
    last = length // HALO - 1
    c0 = 7
    in_specs = [pl.BlockSpec((None, HALO, HY_CH), lambda b, i, j: (b, jnp.maximum(i * per - 1, 0), c0 + j)),
                pl.BlockSpec((None, tm, HY_CH), lambda b, i, j: (b, i, c0 + j)),
                pl.BlockSpec((None, HALO, HY_CH), lambda b, i, j: (b, jnp.minimum((i + 1) * per, last), c0 + j)),
                pl.BlockSpec((3, HY_CH), lambda b, i, j: (0, j)),
                pl.BlockSpec((1, HY_CH), lambda b, i, j: (0, j))]
    return pl.pallas_call(
        functools.partial(_hy_short_kernel, tm=tm, nt=nt),
        grid=(bsz, nt, 3),
        in_specs=in_specs,
        out_specs=pl.BlockSpec((None, None, HY_CH, tm), lambda b, i, j: (j, b, 0, i)),
        out_shape=jax.ShapeDtypeStruct((3, bsz, HY_CH, length), F32),
        scratch_shapes=[pltpu.VMEM((tm + 2 * HALO, HY_CH), F32)],
        compiler_params=_params(("parallel", "arbitrary", "arbitrary")),
        name="hyena_short_conv",
    )(u, u, u, w_short, b_short)


HY_FEAT = 40


def _hy_filter_kernel(band_ref, w1_ref, b1_ref, w2_ref, b2_ref, w3_ref, b3_ref, dl_ref,
                      k_ref, ssq_ref, nrm_ref, *, tp, length):
    i = pl.program_id(0)
    n_i = pl.num_programs(0)
    m = i * tp + lax.broadcasted_iota(jnp.int32, (1, tp), 1)
    t = jnp.where(m <= length, m, 2 * length - m).astype(F32)
    t_norm = t / max(length - 1, 1)
    ang = ((2.0 * math.pi / length) * t) * band_ref[...]
    row = lax.broadcasted_iota(jnp.int32, (HY_FEAT, tp), 0)
    z = jnp.where(row == 0, t_norm,
                  jnp.where(row <= HY_BANDS, jnp.cos(ang), jnp.where(row < HY_EMB, jnp.sin(ang), 0.0)))
    z = jnp.concatenate([z, jnp.zeros((LANES - HY_FEAT, tp), F32)], axis=0)
    h = jnp.sin(HY_SIN_FREQ * (_dot(w1_ref[...], z, HI) + b1_ref[...]))
    h = jnp.sin(HY_SIN_FREQ * (_dot(w2_ref[...], h, HI) + b2_ref[...]))
    h = _dot(w3_ref[...], h, HI) + b3_ref[...]
    h = h * jnp.exp(-t_norm * dl_ref[...])

    @pl.when(i == 0)
    def _():
        ssq_ref[...] = jnp.zeros(ssq_ref.shape, F32)

    for o in range(2):
        fwd = h[o * 2 * HY_CH:o * 2 * HY_CH + HY_CH]
        bwd = h[o * 2 * HY_CH + HY_CH:(o + 1) * 2 * HY_CH]
        k = jnp.where(m < length, fwd, jnp.where(m == length, 0.0, bwd))
        k_ref[o] = k
        extra = jnp.where(m == 0, bwd * bwd, 0.0)
        ssq_ref[o] += jnp.sum(k * k + extra, axis=1, keepdims=True)

    @pl.when(i == n_i - 1)
    def _():
        nrm_ref[...] = lax.rsqrt(ssq_ref[...] + EPS)


def _hy_filters(length, w1, b1, w2, b2, w3, b3, tp):
    bands = jnp.linspace(1e-4, HY_BANDS - 1, HY_BANDS, dtype=F32)
    band_col = jnp.concatenate([jnp.zeros((1,), F32), bands, bands,
                                jnp.zeros((HY_FEAT - HY_EMB,), F32)])[:, None]
    deltas = jnp.abs(jnp.linspace(HY_MIN_DECAY, HY_MAX_DECAY, HY_CH, dtype=F32))
    dl_col = jnp.tile(deltas, 4)[:, None]
    w1t = jnp.pad(w1.astype(F32).T, ((0, 0), (0, LANES - HY_EMB)))
    full = lambda shape: pl.BlockSpec(shape, lambda i: (0,) * len(shape))
    n = 2 * length
    return pl.pallas_call(
        functools.partial(_hy_filter_kernel, tp=tp, length=length),
        grid=(n // tp,),
        in_specs=[full((HY_FEAT, 1)), full((HY_HIDDEN, LANES)), full((HY_HIDDEN, 1)),
                  full((HY_HIDDEN, HY_HIDDEN)), full((HY_HIDDEN, 1)),
                  full((4 * HY_CH, HY_HIDDEN)), full((4 * HY_CH, 1)), full((4 * HY_CH, 1))],
        out_specs=[pl.BlockSpec((2, HY_CH, tp), lambda i: (0, 0, i)),
                   full((2, HY_CH, 1)), full((2, HY_CH, 1))],
        out_shape=[jax.ShapeDtypeStruct((2, HY_CH, n), F32),
                   jax.ShapeDtypeStruct((2, HY_CH, 1), F32),
                   jax.ShapeDtypeStruct((2, HY_CH, 1), F32)],
        compiler_params=_params(("arbitrary",)),
        name="hyena_filters",
    )(band_col, w1t, b1.astype(F32)[:, None], w2.astype(F32).T, b2.astype(F32)[:, None],
      w3.astype(F32).T, b3.astype(F32)[:, None], dl_col)


def _dft_consts(n1, n2):
    n = n1 * n2
    a1 = 2.0 * np.pi * ((np.arange(n1)[:, None] * np.arange(n1)[None, :]) % n1) / n1
    c1, s1 = np.cos(a1), np.sin(a1)
    a2 = 2.0 * np.pi * ((np.arange(n2)[:, None] * np.arange(n2)[None, :]) % n2) / n2
    c2, s2 = np.cos(a2), np.sin(a2)
    at = 2.0 * np.pi * ((np.arange(n1)[:, None] * np.arange(n2)[None, :]) % n) / n
    f1_full = np.concatenate([c1, -s1], axis=0)
    f1_half = f1_full[:, :n1 // 2]
    g1 = np.concatenate([c1[:n1 // 2], -s1[:n1 // 2]], axis=1)
    w2f = np.block([[c2, -s2], [s2, c2]])
    w2i = np.block([[c2, s2], [-s2, c2]])
    f = lambda x: jnp.asarray(x, F32)
    return dict(f1_full=f(f1_full), f1_half=f(f1_half), g1=f(g1), w2f=f(w2f), w2i=f(w2i),
                tr=f(np.cos(at)), ti=f(-np.sin(at)))


def _fft_fwd(slabs, f1, tr, ti, w2f, stack, n1, n2, prec):
    for c, x in enumerate(slabs):
        a = _dot(f1, x.astype(stack.dtype), prec)
        ar, ai = a[:n1], a[n1:]
        stack[c * n1:(c + 1) * n1, 0:n2] = (ar * tr - ai * ti).astype(stack.dtype)
        stack[c * n1:(c + 1) * n1, n2:2 * n2] = (ar * ti + ai * tr).astype(stack.dtype)
    return _dot(stack[...], w2f, prec)


def _hy_spec_kernel(nrm_ref, k_ref, f1_ref, tr_ref, ti_ref, w2f_ref, o_ref, stack, *, cg, n1, n2):
    o = pl.program_id(0)
    g = pl.program_id(1)
    x = _fft_fwd([k_ref[c] for c in range(cg)], f1_ref[...], tr_ref[...], ti_ref[...], w2f_ref[...],
                 stack, n1, n2, HI)
    for c in range(cg):
        sc = nrm_ref[o * HY_CH + g * cg + c] * (1.0 / (n1 * n2))
        xc = x[c * n1:(c + 1) * n1] * sc
        o_ref[c, 0] = xc[:, :n2]
        o_ref[c, 1] = xc[:, n2:]


def _hy_spec(k4, nrm_flat, dc, cg, n1, n2):
    full = lambda shape: pl.BlockSpec(shape, lambda o, g: (0,) * len(shape))
    return pl.pallas_call(
        functools.partial(_hy_spec_kernel, cg=cg, n1=n1, n2=n2),
        grid=(2, HY_CH // cg),
        in_specs=[pl.BlockSpec(memory_space=pltpu.SMEM),
                  pl.BlockSpec((None, cg, n1, n2), lambda o, g: (o, g, 0, 0)),
                  full((2 * n1, n1)), full((n1, n2)), full((n1, n2)), full((2 * n2, 2 * n2))],
        out_specs=pl.BlockSpec((None, cg, 2, n1, n2), lambda o, g: (o, g, 0, 0, 0)),
        out_shape=jax.ShapeDtypeStruct((2, HY_CH, 2, n1, n2), F32),
        scratch_shapes=[pltpu.VMEM((cg * n1, 2 * n2), F32)],
        compiler_params=_params(("parallel", "arbitrary")),
        name="hyena_filter_spectrum",
    )(nrm_flat, k4, dc["f1_full"], dc["tr"], dc["ti"], dc["w2f"])


def _hy_conv_kernel(skip_ref, x_ref, ks_ref, f1_ref, g1_ref, tr_ref, ti_ref, w2f_ref, w2i_ref,
                    o_ref, stack, *, cg, n1, n2):
    g = pl.program_id(1)
    tr = tr_ref[...]
    ti = ti_ref[...]

    def conv(slabs, order):
        x = _fft_fwd(slabs, f1_ref[...], tr, ti, w2f_ref[...], stack, n1, n2, None)
        for c in range(cg):
            xr, xi = x[c * n1:(c + 1) * n1, :n2], x[c * n1:(c + 1) * n1, n2:]
            kr, ki = ks_ref[order, c, 0], ks_ref[order, c, 1]
            stack[c * n1:(c + 1) * n1, 0:n2] = (xr * kr - xi * ki).astype(BF16)
            stack[c * n1:(c + 1) * n1, n2:2 * n2] = (xr * ki + xi * kr).astype(BF16)
        bm = _dot(stack[...], w2i_ref[...])
        outs = []
        for c in range(cg):
            br, bi = bm[c * n1:(c + 1) * n1, :n2], bm[c * n1:(c + 1) * n1, n2:]
            b2 = jnp.concatenate([br * tr + bi * ti, bi * tr - br * ti], axis=0)
            y = _dot(g1_ref[...], b2.astype(BF16))
            outs.append(y + slabs[c] * skip_ref[order * HY_CH + g * cg + c])
        return outs

    v = [x_ref[2, c] for c in range(cg)]
    y0 = conv(v, 0)
    z = [x_ref[0, c] * y0[c] for c in range(cg)]
    y1 = conv(z, 1)
    for c in range(cg):
        o_ref[c] = x_ref[1, c] * y1[c]


def _hy_conv(x4, kspec, skip_flat, dc, cg, n1, n2):
    bsz = x4.shape[1]
    full = lambda shape: pl.BlockSpec(shape, lambda b, g: (0,) * len(shape))
    return pl.pallas_call(
        functools.partial(_hy_conv_kernel, cg=cg, n1=n1, n2=n2),
        grid=(bsz, HY_CH // cg),
        in_specs=[pl.BlockSpec(memory_space=pltpu.SMEM),
                  pl.BlockSpec((3, None, cg, n1 // 2, n2), lambda b, g: (0, b, g, 0, 0)),
                  pl.BlockSpec((2, cg, 2, n1, n2), lambda b, g: (0, g, 0, 0, 0)),
                  full((2 * n1, n1 // 2)), full((n1 // 2, 2 * n1)), full((n1, n2)), full((n1, n2)),
                  full((2 * n2, 2 * n2)), full((2 * n2, 2 * n2))],
        out_specs=pl.BlockSpec((None, cg, n1 // 2, n2), lambda b, g: (b, g, 0, 0)),
        out_shape=jax.ShapeDtypeStruct((bsz, HY_CH, n1 // 2, n2), F32),
        scratch_shapes=[pltpu.VMEM((cg * n1, 2 * n2), BF16)],
        compiler_params=_params(("parallel", "arbitrary")),
        name="hyena_long_conv",
    )(skip_flat, x4, kspec, dc["f1_half"].astype(BF16), dc["g1"].astype(BF16), dc["tr"], dc["ti"],
      dc["w2f"].astype(BF16), dc["w2i"].astype(BF16))


def _hy_ctx_kernel(x_ref, k_ref, nrm_ref, skip_ref, fc_ref, fs_ref, o_ref, *, c):
    fc = fc_ref[...]
    fs = fs_ref[...]
    inv_n = 1.0 / (2 * c)

    def conv(x, order):
        kk = k_ref[order]
        kr, ki = _dot(kk, fc, HI), -_dot(kk, fs, HI)
        xr, xi = _dot(x, fc[:c], HI), -_dot(x, fs[:c], HI)
        yr, yi = xr * kr - xi * ki, xr * ki + xi * kr
        y = (_dot(yr, fc[:, :c], HI) - _dot(yi, fs[:, :c], HI)) * inv_n
        return y * nrm_ref[order] + x * skip_ref[order]

    z = x_ref[0] * conv(x_ref[2], 0)
    o_ref[...] = x_ref[1] * conv(z, 1)


def _hy_ctx_conv(xt, kt, nrm, skip_col):
    _, bsz, ch, c = xt.shape
    n = 2 * c
    ang = 2.0 * np.pi * ((np.arange(n)[:, None] * np.arange(n)[None, :]) % n) / n
    fc, fs = jnp.asarray(np.cos(ang), F32), jnp.asarray(np.sin(ang), F32)
    full = lambda shape: pl.BlockSpec(shape, lambda b: (0,) * len(shape))
    return pl.pallas_call(
        functools.partial(_hy_ctx_kernel, c=c),
        grid=(bsz,),
        in_specs=[pl.BlockSpec((3, None, ch, c), lambda b: (0, b, 0, 0)),
                  full((2, ch, n)), full((2, ch, 1)), full((2, ch, 1)), full((n, n)), full((n, n))],
        out_specs=pl.BlockSpec((None, ch, c), lambda b: (b, 0, 0)),
        out_shape=jax.ShapeDtypeStruct((bsz, ch, c), F32),
        compiler_params=_params(("arbitrary",)),
        name="hyena_ctx_conv",
    )(xt, kt, nrm, skip_col, fc, fs)


def _out_proj_kernel(a_ref, b_ref, p_ref, ht_ref, x_ref, g_ref, w_ref, o_ref):
    w = D_GROUP
    acc = _dot(a_ref[...].astype(BF16), w_ref[0:w])
    acc += _dot(b_ref[...].astype(BF16), w_ref[w:2 * w])
    acc += _dot(p_ref[...].astype(BF16), w_ref[2 * w:3 * w])
    acc += _dot(ht_ref[...].T.astype(BF16), w_ref[3 * w:4 * w])
    o_ref[...] = x_ref[...] + g_ref[...] * acc


def _out_proj(a, b, p, ht, x, mod3, row_of_batch, j_gate, w_out, tm):
    bsz, t, d = x.shape
    w = D_GROUP
    tok = pl.BlockSpec((None, tm, w), lambda bb, i: (bb, i, 0))
    return pl.pallas_call(
        _out_proj_kernel,
        grid=(bsz, t // tm),
        in_specs=[tok, tok, tok,
                  pl.BlockSpec((None, w, tm), lambda bb, i: (bb, 0, i)),
                  pl.BlockSpec((None, tm, d), lambda bb, i: (bb, i, 0)),
                  pl.BlockSpec((None, 1, d), lambda bb, i: (row_of_batch(bb), 0, j_gate)),
                  pl.BlockSpec((4 * w, d), lambda bb, i: (0, 0))],
        out_specs=pl.BlockSpec((None, tm, d), lambda bb, i: (bb, i, 0)),
        out_shape=jax.ShapeDtypeStruct((bsz, t, d), F32),
        compiler_params=_params(("parallel", "arbitrary")),
        name="out_proj_residual",
    )(a, b, p, ht, x, mod3, w_out)


MOE_ROWS = 256


def _route(x, g, sc, sh, rw, rb):
    per_group = N_EXPERTS // N_EXPERT_GROUPS
    tm = x.shape[0]
    ms = jnp.mean(x * x, axis=-1, keepdims=True)
    h = (x * lax.rsqrt(ms + EPS)) * g * (1.0 + sc) + sh
    lane = lax.broadcasted_iota(jnp.int32, (tm, LANES), 1)
    valid = lane < N_EXPERTS
    neg = -jnp.inf
    logits = jnp.where(valid, _dot(h, rw, HI), neg)
    mx = jnp.max(logits, axis=1, keepdims=True)
    ex = jnp.exp(logits - mx)
    scores = ex / jnp.sum(ex, axis=1, keepdims=True)
    sel = scores + rb
    big = jnp.int32(LANES)
    best = None
    for grp in range(N_EXPERT_GROUPS):
        in_g = (lane >= grp * per_group) & (lane < (grp + 1) * per_group)
        v1 = jnp.max(jnp.where(in_g, sel, neg), axis=1, keepdims=True)
        i1 = jnp.min(jnp.where(in_g & (sel == v1), lane, big), axis=1, keepdims=True)
        rest = in_g & (lane != i1)
        v2 = jnp.max(jnp.where(rest, sel, neg), axis=1, keepdims=True)
        i2 = jnp.min(jnp.where(rest & (sel == v2), lane, big), axis=1, keepdims=True)
        gs = v1 + v2
        if best is None:
            best, e1, e2 = gs, i1, i2
        else:
            upd = gs > best
            best = jnp.where(upd, gs, best)
            e1 = jnp.where(upd, i1, e1)
            e2 = jnp.where(upd, i2, e2)
    w1 = jnp.sum(jnp.where(lane == e1, scores, 0.0), axis=1, keepdims=True)
    w2 = jnp.sum(jnp.where(lane == e2, scores, 0.0), axis=1, keepdims=True)
    tot = w1 + w2
    return h, jnp.where(lane == e1, w1 / tot, 0.0) + jnp.where(lane == e2, w2 / tot, 0.0)


def _moe_kernel(x_ref, g_ref, sc_ref, sh_ref, gate_ref, rw_ref, rb_ref, w1_ref, w3_ref, w2_ref, fg_ref,
                o_ref, h_scr, gates_scr, acc_scr, *, final):
    e = pl.program_id(2)
    tm = x_ref.shape[0]

    @pl.when(e == 0)
    def _():
        h, gates = _route(x_ref[...], g_ref[...], sc_ref[...], sh_ref[...], rw_ref[...], rb_ref[...])
        h_scr[...] = h.astype(BF16)
        gates_scr[...] = gates
        acc_scr[...] = jnp.zeros(acc_scr.shape, F32)

    lane = lax.broadcasted_iota(jnp.int32, (tm, LANES), 1)
    ge = jnp.sum(jnp.where(lane == e, gates_scr[...], 0.0), axis=1, keepdims=True)

    def up(j):
        hb = h_scr[j * MOE_ROWS:(j + 1) * MOE_ROWS, :]
        return _dot(hb, w1_ref[...]), _dot(hb, w3_ref[...])

    nxt = up(0)
    for j in range(tm // MOE_ROWS):
        a, b = nxt
        if (j + 1) * MOE_ROWS < tm:
            nxt = up(j + 1)
        rows = slice(j * MOE_ROWS, (j + 1) * MOE_ROWS)
        act = (a * jax.nn.sigmoid(a)) * b
        acc_scr[rows, :] += ge[rows] * _dot(act.astype(BF16), w2_ref[...])

    @pl.when(e == N_EXPERTS - 1)
    def _():
        y = x_ref[...] + gate_ref[...] * acc_scr[...]
        if final:
            ms = jnp.mean(y * y, axis=-1, keepdims=True)
            y = (y * lax.rsqrt(ms + EPS)) * fg_ref[...]
        o_ref[...] = y


def _moe(x, g, mod3, row_of_batch, j_shift, j_scale, j_gate, rw, rb, w1, w3, w2, final_g, final, tm):
    bsz, t, d = x.shape
    vec = lambda j: pl.BlockSpec((None, 1, d), lambda b, i, e: (row_of_batch(b), 0, j))
    full = lambda shape: pl.BlockSpec(shape, lambda b, i, e: (0,) * len(shape))
    return pl.pallas_call(
        functools.partial(_moe_kernel, final=final),
        grid=(bsz, t // tm, N_EXPERTS),
        in_specs=[pl.BlockSpec((None, tm, d), lambda b, i, e: (b, i, 0)),
                  full((1, d)), vec(j_scale), vec(j_shift), vec(j_gate),
                  full((d, LANES)), full((1, LANES)),
                  pl.BlockSpec((None, d, D_EXPERT), lambda b, i, e: (e, 0, 0)),
                  pl.BlockSpec((None, d, D_EXPERT), lambda b, i, e: (e, 0, 0)),
                  pl.BlockSpec((None, D_EXPERT, d), lambda b, i, e: (e, 0, 0)),
                  full((1, d))],
        out_specs=pl.BlockSpec((None, tm, d), lambda b, i, e: (b, i, 0)),
        out_shape=jax.ShapeDtypeStruct((bsz, t, d), F32),
        scratch_shapes=[pltpu.VMEM((tm, d), BF16), pltpu.VMEM((tm, LANES), F32), pltpu.VMEM((tm, d), F32)],
        compiler_params=_params(("parallel", "parallel", "arbitrary")),
        name="moe_final" if final else "moe",
    )(x, g, mod3, mod3, mod3, rw, rb, w1, w3, w2, final_g)


MOE_SUB = 512
MOE_CAP = 96
MOE_BLK = 1024


def _moe_route_kernel(x_ref, g_ref, sc_ref, sh_ref, rw_ref, rb_ref, h_ref, gates_ref, cnt_ref):
    h, gates = _route(x_ref[...], g_ref[...], sc_ref[...], sh_ref[...], rw_ref[...], rb_ref[...])
    h_ref[...] = h.astype(BF16)
    gates_ref[...] = gates
    cnt = jnp.sum(jnp.where(gates > 0.0, 1.0, 0.0), axis=0, keepdims=True)
    cnt_ref[...] = jnp.broadcast_to(cnt, cnt_ref.shape)


def _moe_route(x, g, mod3, row_of_batch, j_shift, j_scale, rw, rb):
    bsz, t, d = x.shape
    tm = MOE_SUB
    vec = lambda j: pl.BlockSpec((None, 1, d), lambda b, i: (row_of_batch(b), 0, j))
    full = lambda shape: pl.BlockSpec(shape, lambda b, i: (0,) * len(shape))
    return pl.pallas_call(
        _moe_route_kernel,
        grid=(bsz, t // tm),
        in_specs=[pl.BlockSpec((None, tm, d), lambda b, i: (b, i, 0)),
                  full((1, d)), vec(j_scale), vec(j_shift), full((d, LANES)), full((1, LANES))],
        out_specs=[pl.BlockSpec((None, tm, d), lambda b, i: (b, i, 0)),
                   pl.BlockSpec((None, tm, LANES), lambda b, i: (b, i, 0)),
                   pl.BlockSpec((None, None, SUBLANES, LANES), lambda b, i: (b, i, 0, 0))],
        out_shape=[jax.ShapeDtypeStruct((bsz, t, d), BF16),
                   jax.ShapeDtypeStruct((bsz, t, LANES), F32),
                   jax.ShapeDtypeStruct((bsz, t // tm, SUBLANES, LANES), F32)],
        compiler_params=_params(("parallel", "arbitrary")),
        name="moe_route",
    )(x, g, mod3, mod3, rw, rb)


def _moe_routed_kernel(np_ref, h_ref, gates_ref, x_ref, gate_ref, w1_ref, w3_ref, w2_ref, fg_ref,
                       o_ref, rank_scr, rankt_scr, gatet_scr, xc_scr, *, final, nblk):
    b = pl.program_id(0)
    i = pl.program_id(1)
    e = pl.program_id(2)
    nsub = MOE_BLK // MOE_SUB
    sub = lambda j: slice(j * MOE_SUB, (j + 1) * MOE_SUB)

    @pl.when(e == 0)
    def _():
        o_ref[...] = jnp.zeros(o_ref.shape, F32)
        r = lax.broadcasted_iota(jnp.int32, (MOE_SUB, MOE_SUB), 0)
        c = lax.broadcasted_iota(jnp.int32, (MOE_SUB, MOE_SUB), 1)
        ltri = jnp.where(c <= r, 1.0, 0.0).astype(BF16)
        utri = jnp.where(r <= c, 1.0, 0.0).astype(BF16)
        for j in range(nsub):
            gt = gates_ref[sub(j), :]
            rank_scr[j] = _dot(ltri, jnp.where(gt > 0.0, 1.0, 0.0).astype(BF16))
            gtt = gt.T[:N_EXPERTS]
            gatet_scr[j] = gtt
            rankt_scr[j] = _dot(jnp.where(gtt > 0.0, 1.0, 0.0).astype(BF16), utri)

    lane = lax.broadcasted_iota(jnp.int32, (MOE_SUB, LANES), 1)

    def one_pass(p, carry):
        base = (p * MOE_CAP + 1).astype(F32)
        slot_r = lax.broadcasted_iota(jnp.int32, (MOE_CAP, MOE_SUB), 0).astype(F32) + base
        slot_c = lax.broadcasted_iota(jnp.int32, (MOE_SUB, MOE_CAP), 1).astype(F32) + base
        gcs = []
        for j in range(nsub):
            rr = rankt_scr[j, pl.ds(e, 1), :]
            gr = gatet_scr[j, pl.ds(e, 1), :]
            pick = (rr == slot_r) & (gr > 0.0)
            xc_scr[j * MOE_CAP:(j + 1) * MOE_CAP, :] = _dot(
                jnp.where(pick, 1.0, 0.0).astype(BF16), h_ref[sub(j), :]).astype(BF16)
            gcs.append(jnp.sum(jnp.where(pick, gr, 0.0), axis=1, keepdims=True))
        xc = xc_scr[...]
        a = _dot(xc, w1_ref[...])
        bb = _dot(xc, w3_ref[...])
        y = _dot(((a * jax.nn.sigmoid(a)) * bb).astype(BF16), w2_ref[...])
        for j in range(nsub):
            yj = (y[j * MOE_CAP:(j + 1) * MOE_CAP] * gcs[j]).astype(BF16)
            rc = jnp.sum(jnp.where(lane == e, rank_scr[j], 0.0), axis=1, keepdims=True)
            gc = jnp.sum(jnp.where(lane == e, gates_ref[sub(j), :], 0.0), axis=1, keepdims=True)
            put = jnp.where((rc == slot_c) & (gc > 0.0), 1.0, 0.0).astype(BF16)
            o_ref[sub(j), :] += _dot(put, yj)
        return carry

    lax.fori_loop(0, np_ref[(b * nblk + i) * N_EXPERTS + e], one_pass, 0)

    @pl.when(e == N_EXPERTS - 1)
    def _():
        y = x_ref[...] + gate_ref[...] * o_ref[...]
        if final:
            ms = jnp.mean(y * y, axis=-1, keepdims=True)
            y = (y * lax.rsqrt(ms + EPS)) * fg_ref[...]
        o_ref[...] = y


def _moe_routed(x, g, mod3, row_of_batch, j_shift, j_scale, j_gate, rw, rb, w1, w3, w2, final_g, final):
    bsz, t, d = x.shape
    h, gates, cnt = _moe_route(x, g, mod3, row_of_batch, j_shift, j_scale, rw, rb)
    nblk = t // MOE_BLK
    nsub = MOE_BLK // MOE_SUB
    passes = jnp.ceil(cnt[:, :, 0, :N_EXPERTS] / MOE_CAP).astype(jnp.int32)
    passes = jnp.max(passes.reshape(bsz, nblk, nsub, N_EXPERTS), axis=2).reshape(-1)
    full = lambda shape: pl.BlockSpec(shape, lambda b, i, e, np_: (0,) * len(shape))
    grid_spec = pltpu.PrefetchScalarGridSpec(
        num_scalar_prefetch=1,
        grid=(bsz, nblk, N_EXPERTS),
        in_specs=[pl.BlockSpec((None, MOE_BLK, d), lambda b, i, e, np_: (b, i, 0)),
                  pl.BlockSpec((None, MOE_BLK, LANES), lambda b, i, e, np_: (b, i, 0)),
                  pl.BlockSpec((None, MOE_BLK, d), lambda b, i, e, np_: (b, i, 0)),
                  pl.BlockSpec((None, 1, d), lambda b, i, e, np_: (row_of_batch(b), 0, j_gate)),
                  pl.BlockSpec((None, d, D_EXPERT), lambda b, i, e, np_: (e, 0, 0)),
                  pl.BlockSpec((None, d, D_EXPERT), lambda b, i, e, np_: (e, 0, 0)),
                  pl.BlockSpec((None, D_EXPERT, d), lambda b, i, e, np_: (e, 0, 0)),
                  full((1, d))],
        out_specs=pl.BlockSpec((None, MOE_BLK, d), lambda b, i, e, np_: (b, i, 0)),
        scratch_shapes=[pltpu.VMEM((nsub, MOE_SUB, LANES), F32),
                        pltpu.VMEM((nsub, N_EXPERTS, MOE_SUB), F32),
                        pltpu.VMEM((nsub, N_EXPERTS, MOE_SUB), F32),
                        pltpu.VMEM((nsub * MOE_CAP, d), BF16)])
    return pl.pallas_call(
        functools.partial(_moe_routed_kernel, final=final, nblk=nblk),
        grid_spec=grid_spec,
        out_shape=jax.ShapeDtypeStruct((bsz, t, d), F32),
        compiler_params=_params(("parallel", "parallel", "arbitrary")),
        name="moe_routed_final" if final else "moe_routed",
    )(passes, h, gates, x, mod3, w1, w3, w2, final_g)


def _tile(n, pref):
    t = min(n, pref)
    assert n % t == 0
    return t


def _fft_split(n):
    n2 = LANES
    assert n % n2 == 0
    return n // n2, n2


def _hyena_latent(u, lp, fargs):
    bsz, s, _ = u.shape
    n1, n2 = _fft_split(2 * s)
    dc = _dft_consts(n1, n2)
    cg = 8
    kt_l, _, nrm_l = _hy_filters(s, *fargs, tp=_tile(2 * s, 1024))
    kspec = _hy_spec(kt_l.reshape(2, HY_CH, n1, n2), nrm_l.reshape(2 * HY_CH), dc, cg, n1, n2)
    xt = _hy_short(u, lp["hy_short_w"].astype(F32), lp["hy_short_b"].astype(F32)[None, :], _tile(s, 512))
    h_l = _hy_conv(xt.reshape(3, bsz, HY_CH, n1 // 2, n2), kspec, lp["hy_skip"].astype(F32).reshape(2 * HY_CH),
                   dc, cg, n1, n2)
    return h_l.reshape(bsz, HY_CH, s)


def _mixers(u, uc, lp, li, need_ctx):
    bsz, s, _ = u.shape
    c = uc.shape[1]
    lam_init = 0.8 - 0.6 * math.exp(-0.3 * li)
    lam_vecs = lp["a_lambda"].astype(F32)
    subln = lp["a_subln_g"].astype(F32)[None, :]

    qc, kct, vc = _attn_prep(uc, _tile(c, 256), rope=False)
    ql, klt, vl = _attn_prep(u, _tile(s, 256), rope=True)
    a_l = _diff_attn(ql, kct, vc, klt, vl, lam_vecs, subln, lam_init, _tile(s, 512), _tile(s, 2048))
    bias8 = _nbr_bias(lp["b_rpb"])
    b_l = _nbr_attn(u, uc, bias8)
    wbd = jax.scipy.linalg.block_diag(*[lp["pool_w"][g] for g in range(len(POOL_SIZES))]).astype(BF16)
    pscale = lp["pool_scale"].astype(F32)[None, :]
    p_l = _pool_mix(u, wbd, pscale, _tile(s, 512))
    fargs = (lp["hy_f_w1"], lp["hy_f_b1"], lp["hy_f_w2"], lp["hy_f_b2"], lp["hy_f_w3"], lp["hy_f_b3"])
    skip = lp["hy_skip"].astype(F32)
    w_short = lp["hy_short_w"].astype(F32)
    b_short = lp["hy_short_b"].astype(F32)[None, :]
    h_l = _hyena_latent(u, lp, fargs)
    lat = (a_l, b_l, p_l, h_l)
    if not need_ctx:
        return lat, None
    a_c = _diff_attn(qc, kct, vc, None, None, lam_vecs, subln, lam_init, _tile(c, 256), None)
    b_c = _nbr_ctx_attn(uc)
    p_c = _pool_mix(uc, wbd, pscale, _tile(c, 256))
    kt_c, _, nrm_c = _hy_filters(c, *fargs, tp=_tile(2 * c, 512))
    xtc = _hy_short(uc, w_short, b_short, _tile(c, 256))
    h_c = _hy_ctx_conv(xtc, kt_c, nrm_c, skip[:, :, None])
    return lat, (a_c, b_c, p_c, h_c)


def kernel(x, c, ctx, c_ctx, norm1_g, norm2_g, ada_w, ada_b, w_in, w_out, a_lambda, a_subln_g, b_rpb, pool_w, pool_scale, hy_short_w, hy_short_b, hy_f_w1, hy_f_b1, hy_f_w2, hy_f_b2, hy_f_w3, hy_f_b3, hy_skip, router_w, router_b, moe_w1, moe_w3, moe_w2, final_g):
    depth = norm1_g.shape[0]
    bsz, s, d = x.shape
    cl = ctx.shape[1]
    assert bsz <= SUBLANES - 1
    xl, xc = x, ctx
    cpad = jnp.zeros((SUBLANES, d), F32).at[:bsz].set(c.astype(F32)).at[bsz].set(c_ctx.astype(F32))
    rw = jnp.pad(router_w.astype(F32), ((0, 0), (0, LANES - N_EXPERTS)))
    rb = jnp.pad(router_b.astype(F32), (0, LANES - N_EXPERTS))[None, :]
    lat_row = lambda b: b
    ctx_row = lambda b: bsz
    fg = final_g.astype(F32)[None, :]
    tm = _tile(s, 512)
    tmc = _tile(cl, 256)
    for li in range(depth):
        need_ctx = li < depth - 1
        lp = dict(a_lambda=a_lambda[li], a_subln_g=a_subln_g[li], b_rpb=b_rpb[li], pool_w=pool_w[li],
                  pool_scale=pool_scale[li], hy_short_w=hy_short_w[li], hy_short_b=hy_short_b[li],
                  hy_f_w1=hy_f_w1[li], hy_f_b1=hy_f_b1[li], hy_f_w2=hy_f_w2[li], hy_f_b2=hy_f_b2[li],
                  hy_f_w3=hy_f_w3[li], hy_f_b3=hy_f_b3[li], hy_skip=hy_skip[li])
        mod3 = _ada(cpad, ada_w[li].astype(F32), ada_b[li].astype(F32)[None, :]).reshape(SUBLANES, 1, 6 * d)
        n1g = norm1_g[li].astype(F32)[None, :]
        n2g = norm2_g[li].astype(F32)[None, :]
        w_in_b = w_in[li].astype(BF16)
        w_out_b = w_out[li].astype(BF16)
        u = _norm_proj(xl, n1g, mod3, lat_row, 0, 1, w_in_b, tm)
        uc = _norm_proj(xc, n1g, mod3, ctx_row, 0, 1, w_in_b, tmc)
        lat, cx = _mixers(u, uc, lp, li, need_ctx)
        xl = _out_proj(*lat, xl, mod3, lat_row, 2, w_out_b, tm)
        w1b, w3b, w2b = moe_w1[li].astype(BF16), moe_w3[li].astype(BF16), moe_w2[li].astype(BF16)
        if need_ctx:
            xc = _out_proj(*cx, xc, mod3, ctx_row, 2, w_out_b, tmc)
            xc = _moe(xc, n2g, mod3, ctx_row, 3, 4, 5, rw, rb, w1b, w3b, w2b, fg, False, tmc)
        xl = _moe_routed(xl, n2g, mod3, lat_row, 3, 4, 5, rw, rb, w1b, w3b, w2b, fg, li == depth - 1)
    return xl
```

```python
import functools
import math

import numpy as np
import jax
import jax.numpy as jnp
from jax import lax
from jax.experimental import pallas as pl
from jax.experimental.pallas import tpu as pltpu

F32 = jnp.float32
BF16 = jnp.bfloat16
HI = lax.Precision.HIGHEST

GRID_W = 64
A_HEADS = 4
A_QK = 32
A_V = 64
ROPE_BASE = 10000.0
B_HEADS = 4
B_DIM = 64
WIN_R = 8
WIN_C = 16
POOL_SIZES = (2, 4, 8, 16)
POOL_CH = 64
D_GROUP = 256
HY_CH = 256
HY_BANDS = 16
HY_EMB = 1 + 2 * HY_BANDS
HY_HIDDEN = 64
HY_SIN_FREQ = 1.0
HY_MIN_DECAY = math.log(1e-2) / 1.5
HY_MAX_DECAY = math.log(1e-2) / 0.3
N_EXPERTS = 16
N_EXPERT_GROUPS = 4
D_EXPERT = 512
EPS = 1e-6
LOG2E = 1.4426950408889634

LANES = 128
SUBLANES = 8
VMEM_LIMIT = 56 * 1024 * 1024


def _params(sem):
    return pltpu.CompilerParams(dimension_semantics=sem, vmem_limit_bytes=VMEM_LIMIT)


def _dot(a, b, prec=None):
    return jnp.dot(a, b, precision=prec, preferred_element_type=F32)


def _dot_nt(a, b):
    return lax.dot_general(a, b, (((1,), (1,)), ((), ())), preferred_element_type=F32)


def _ada_kernel(c_ref, w_ref, b_ref, o_ref):
    cf = c_ref[...]
    s = cf * jax.nn.sigmoid(cf)
    o_ref[...] = _dot(s, w_ref[...], HI) + b_ref[...]


def _ada(cpad, w, b):
    d = cpad.shape[1]
    n = w.shape[1]
    return pl.pallas_call(
        _ada_kernel,
        grid=(n // d,),
        in_specs=[pl.BlockSpec((SUBLANES, d), lambda j: (0, 0)),
                  pl.BlockSpec((d, d), lambda j: (0, j)),
                  pl.BlockSpec((1, d), lambda j: (0, j))],
        out_specs=pl.BlockSpec((SUBLANES, d), lambda j: (0, j)),
        out_shape=jax.ShapeDtypeStruct((SUBLANES, n), F32),
        compiler_params=_params(("arbitrary",)),
        name="ada_mod",
    )(cpad, w, b)


def _norm_proj_kernel(x_ref, g_ref, sc_ref, sh_ref, w_ref, o_ref):
    x = x_ref[...]
    ms = jnp.mean(x * x, axis=-1, keepdims=True)
    h = (x * lax.rsqrt(ms + EPS)) * g_ref[...] * (1.0 + sc_ref[...]) + sh_ref[...]
    o_ref[...] = _dot(h.astype(BF16), w_ref[...])


def _norm_proj(x, g, mod3, row_of_batch, j_shift, j_scale, w, tm):
    bsz, t, d = x.shape
    n = w.shape[1]
    return pl.pallas_call(
        _norm_proj_kernel,
        grid=(bsz, t // tm),
        in_specs=[pl.BlockSpec((None, tm, d), lambda b, i: (b, i, 0)),
                  pl.BlockSpec((1, d), lambda b, i: (0, 0)),
                  pl.BlockSpec((None, 1, d), lambda b, i: (row_of_batch(b), 0, j_scale)),
                  pl.BlockSpec((None, 1, d), lambda b, i: (row_of_batch(b), 0, j_shift)),
                  pl.BlockSpec((d, n), lambda b, i: (0, 0))],
        out_specs=pl.BlockSpec((None, tm, n), lambda b, i: (b, i, 0)),
        out_shape=jax.ShapeDtypeStruct((bsz, t, n), F32),
        compiler_params=_params(("parallel", "arbitrary")),
        name="norm_in_proj",
    )(x, g, mod3, mod3, w)


def _aprep_kernel(*refs, rope):
    if rope:
        u_ref, cos_ref, sin_ref, q_ref, kt_ref, v_ref = refs
    else:
        u_ref, q_ref, kt_ref, v_ref = refs
    u = u_ref[...]
    q = u[:, 0:256]
    k = u[:, 256:512]
    v = u[:, 512:768]
    if rope:
        cos_t = cos_ref[...]
        sin_t = sin_ref[...]
        lane = lax.broadcasted_iota(jnp.int32, cos_t.shape, 1)
        first = (lane % (2 * 16)) < 16

        def rot(x):
            halves = []
            for j in range(2):
                xh = x[:, j * LANES:(j + 1) * LANES]
                swap = jnp.where(first, pltpu.roll(xh, LANES - 16, axis=1), pltpu.roll(xh, 16, axis=1))
                halves.append(xh * cos_t + swap * sin_t)
            return jnp.concatenate(halves, axis=1)

        q = rot(q)
        k = rot(k)
    q = q * (A_QK ** -0.5 * LOG2E)
    kt = k.T
    for hc in range(2 * A_HEADS):
        q_ref[hc] = q[:, hc * A_QK:(hc + 1) * A_QK].astype(BF16)
        kt_ref[hc] = kt[hc * A_QK:(hc + 1) * A_QK, :].astype(BF16)
    lane = lax.broadcasted_iota(jnp.int32, (v.shape[0], LANES - A_V), 1)
    ones_col = jnp.where(lane == 0, 1.0, 0.0)
    for h in range(A_HEADS):
        v_ref[h] = jnp.concatenate([v[:, h * A_V:(h + 1) * A_V], ones_col], axis=1).astype(BF16)


def _rope_tables(length):
    n_freq = A_QK // 4
    inv = ROPE_BASE ** (-jnp.arange(n_freq, dtype=F32) / n_freq)
    t = jnp.arange(length)
    row = (t // GRID_W).astype(F32)
    col = (t % GRID_W).astype(F32)
    ang = jnp.concatenate([row[:, None] * inv, col[:, None] * inv], axis=-1)
    cos, sin = jnp.cos(ang), jnp.sin(ang)
    cos_t = jnp.tile(jnp.concatenate([cos, cos], axis=-1), (1, LANES // 32))
    sin_t = jnp.tile(jnp.concatenate([-sin, sin], axis=-1), (1, LANES // 32))
    return cos_t, sin_t


def _attn_prep(u, tm, rope):
    bsz, t, _ = u.shape
    nh = 2 * A_HEADS
    in_specs = [pl.BlockSpec((None, tm, 768), lambda b, i: (b, i, 0))]
    args = [u]
    if rope:
        cos_t, sin_t = _rope_tables(t)
        in_specs += [pl.BlockSpec((tm, LANES), lambda b, i: (i, 0))] * 2
        args += [cos_t, sin_t]
    return pl.pallas_call(
        functools.partial(_aprep_kernel, rope=rope),
        grid=(bsz, t // tm),
        in_specs=in_specs,
        out_specs=[pl.BlockSpec((None, nh, tm, A_QK), lambda b, i: (b, 0, i, 0)),
                   pl.BlockSpec((None, nh, A_QK, tm), lambda b, i: (b, 0, 0, i)),
                   pl.BlockSpec((None, A_HEADS, tm, LANES), lambda b, i: (b, 0, i, 0))],
        out_shape=[jax.ShapeDtypeStruct((bsz, nh, t, A_QK), BF16),
                   jax.ShapeDtypeStruct((bsz, nh, A_QK, t), BF16),
                   jax.ShapeDtypeStruct((bsz, A_HEADS, t, LANES), BF16)],
        compiler_params=_params(("parallel", "arbitrary")),
        name="attn_prep_rope" if rope else "attn_prep_ctx",
    )(*args)


QK_LOOKAHEAD = 3


def _dattn_kernel(*refs, lam_init, has_lat):
    if has_lat:
        lam_ref, g_ref, q_ref, kc_ref, vc_ref, k_ref, v_ref, o_ref, m_scr, acc_scr = refs
    else:
        lam_ref, g_ref, q_ref, kc_ref, vc_ref, o_ref, m_scr, acc_scr = refs
    ki = pl.program_id(2)
    nk = pl.num_programs(2)
    nh = 2 * A_HEADS

    def update(kt_r, v_r):
        scores = [_dot(q_ref[j], kt_r[j]) for j in range(QK_LOOKAHEAD)]
        for hc in range(nh):
            s = scores[hc]
            if hc + QK_LOOKAHEAD < nh:
                scores.append(_dot(q_ref[hc + QK_LOOKAHEAD], kt_r[hc + QK_LOOKAHEAD]))
            m_prev = m_scr[hc]
            m_new = jnp.maximum(m_prev, jnp.max(s, axis=1, keepdims=True))
            alpha = jnp.exp2(m_prev - m_new)
            p = jnp.exp2((s - m_new[:, :1]).astype(BF16))
            acc_scr[hc] = alpha * acc_scr[hc] + _dot(p, v_r[hc // 2])
            m_scr[hc] = m_new

    @pl.when(ki == 0)
    def _():
        m_scr[...] = jnp.full(m_scr.shape, -jnp.inf, F32)
        acc_scr[...] = jnp.zeros(acc_scr.shape, F32)
        update(kc_ref, vc_ref)

    if has_lat:
        @pl.when(ki > 0)
        def _():
            update(k_ref, v_ref)

    @pl.when(ki == nk - 1)
    def _():
        lv = lam_ref[...]
        lam = (jnp.exp(jnp.sum(lv[0:1] * lv[1:2], axis=1, keepdims=True))
               - jnp.exp(jnp.sum(lv[2:3] * lv[3:4], axis=1, keepdims=True)) + lam_init)
        for h in range(A_HEADS):
            a0 = acc_scr[2 * h]
            a1 = acc_scr[2 * h + 1]
            o = a0[:, :A_V] / a0[:, A_V:A_V + 1] - lam * (a1[:, :A_V] / a1[:, A_V:A_V + 1])
            ms = jnp.mean(o * o, axis=-1, keepdims=True)
            o_ref[:, h * A_V:(h + 1) * A_V] = (o * lax.rsqrt(ms + EPS)) * g_ref[...] * (1.0 - lam_init)


def _diff_attn(q, kct, vc, kt, v, lam_vecs, subln_g, lam_init, tq, tk):
    bsz, nh, t, _ = q.shape
    c = kct.shape[-1]
    has_lat = kt is not None
    nk = 1 + (kt.shape[-1] // tk if has_lat else 0)
    in_specs = [pl.BlockSpec((4, A_QK), lambda b, i, k: (0, 0)),
                pl.BlockSpec((1, A_V), lambda b, i, k: (0, 0)),
                pl.BlockSpec((None, nh, tq, A_QK), lambda b, i, k: (b, 0, i, 0)),
                pl.BlockSpec((None, nh, A_QK, c), lambda b, i, k: (b, 0, 0, 0)),
                pl.BlockSpec((None, A_HEADS, c, LANES), lambda b, i, k: (b, 0, 0, 0))]
    args = [lam_vecs, subln_g, q, kct, vc]
    if has_lat:
        in_specs += [pl.BlockSpec((None, nh, A_QK, tk), lambda b, i, k: (b, 0, 0, jnp.maximum(k - 1, 0))),
                     pl.BlockSpec((None, A_HEADS, tk, LANES), lambda b, i, k: (b, 0, jnp.maximum(k - 1, 0), 0))]
        args += [kt, v]
    return pl.pallas_call(
        functools.partial(_dattn_kernel, lam_init=lam_init, has_lat=has_lat),
        grid=(bsz, t // tq, nk),
        in_specs=in_specs,
        out_specs=pl.BlockSpec((None, tq, A_HEADS * A_V), lambda b, i, k: (b, i, 0)),
        out_shape=jax.ShapeDtypeStruct((bsz, t, A_HEADS * A_V), F32),
        scratch_shapes=[pltpu.VMEM((nh, tq, LANES), F32),
                        pltpu.VMEM((nh, tq, LANES), F32)],
        compiler_params=_params(("parallel", "parallel", "arbitrary")),
        name="diff_attn" if has_lat else "diff_attn_ctx",
    )(*args)


NB_ROWS = 8


def _nbr_bias(rpb):
    cols = jnp.arange(GRID_W)
    c0 = jnp.clip(cols - WIN_C // 2, 0, GRID_W - WIN_C)
    in_win = (cols[None, :] >= c0[:, None]) & (cols[None, :] < c0[:, None] + WIN_C)
    dc = jnp.clip(cols[None, :] - cols[:, None], -(WIN_C - 1), WIN_C - 1) + (WIN_C - 1)
    onehot = (dc[None] == jnp.arange(2 * WIN_C - 1)[:, None, None]).astype(F32)
    g = jnp.einsum("hab,bqk->haqk", rpb.astype(F32), onehot, precision=HI)
    g = jnp.where(in_win[None, None], g, -jnp.inf)
    b = jnp.stack([g[:, a0:a0 + WIN_R] for a0 in range(WIN_R)], axis=0)
    b = jnp.transpose(b, (0, 1, 3, 2, 4))
    return b.reshape(WIN_R, B_HEADS, GRID_W, WIN_R * GRID_W)


def _nbr_kernel(q_ref, kp_ref, kc_ref, kn_ref, vp_ref, vcur_ref, vn_ref, kctx_ref, vctx_ref, bias_ref,
                o_ref, kwin, vwin, kcx, vcx, *, n_rows):
    rb = pl.program_id(1)
    blk = NB_ROWS * GRID_W
    scale = B_DIM ** -0.5
    for h in range(B_HEADS):
        sl = slice(h * B_DIM, (h + 1) * B_DIM)
        for j, (kr, vr) in enumerate(((kp_ref, vp_ref), (kc_ref, vcur_ref), (kn_ref, vn_ref))):
            kwin[h, j * blk:(j + 1) * blk, :] = kr[:, sl].astype(BF16)
            vwin[h, j * blk:(j + 1) * blk, :] = vr[:, sl].astype(BF16)
        kcx[h] = kctx_ref[:, sl].astype(BF16)
        vcx[h] = vctx_ref[:, sl].astype(BF16)

    def window(rr):
        r = rb * NB_ROWS + rr
        r0 = jnp.clip(r - WIN_R // 2, 0, n_rows - WIN_R)
        off = pl.multiple_of((r0 - (rb - 1) * NB_ROWS) * GRID_W, GRID_W)
        return off, r0 - r + (WIN_R - 1)

    def scores(rr):
        off, a0 = window(rr)
        qrow = q_ref[rr * GRID_W:(rr + 1) * GRID_W, :]
        out = []
        for h in range(B_HEADS):
            qh = qrow[:, h * B_DIM:(h + 1) * B_DIM].astype(BF16)
            s = _dot_nt(qh, kwin[h, pl.ds(off, WIN_R * GRID_W), :]) * scale + bias_ref[a0, h]
            out.append((s, _dot_nt(qh, kcx[h]) * scale))
        return out

    nxt = scores(0)
    for rr in range(NB_ROWS):
        cur = nxt
        if rr + 1 < NB_ROWS:
            nxt = scores(rr + 1)
        off, _ = window(rr)
        outs = []
        for h in range(B_HEADS):
            s, sc = cur[h]
            m = jnp.maximum(jnp.max(s, axis=1, keepdims=True), jnp.max(sc, axis=1, keepdims=True))
            p = jnp.exp(s - m)
            pc = jnp.exp(sc - m)
            l = jnp.sum(p, axis=1, keepdims=True) + jnp.sum(pc, axis=1, keepdims=True)
            o = _dot(p.astype(BF16), vwin[h, pl.ds(off, WIN_R * GRID_W), :]) + _dot(pc.astype(BF16), vcx[h])
            outs.append(o / l)
        o_ref[rr * GRID_W:(rr + 1) * GRID_W, :] = jnp.concatenate(outs, axis=1)


def _nbr_attn(u, uc, bias8):
    bsz, s, _ = u.shape
    c = uc.shape[1]
    n_rows = s // GRID_W
    nb = n_rows // NB_ROWS
    blk = NB_ROWS * GRID_W
    w = B_HEADS * B_DIM

    def spec(col, shift):
        return pl.BlockSpec((None, blk, w), lambda b, i: (b, jnp.clip(i + shift, 0, nb - 1), col))

    return pl.pallas_call(
        functools.partial(_nbr_kernel, n_rows=n_rows),
        grid=(bsz, nb),
        in_specs=[spec(3, 0), spec(4, -1), spec(4, 0), spec(4, 1), spec(5, -1), spec(5, 0), spec(5, 1),
                  pl.BlockSpec((None, c, w), lambda b, i: (b, 0, 4)),
                  pl.BlockSpec((None, c, w), lambda b, i: (b, 0, 5)),
                  pl.BlockSpec(bias8.shape, lambda b, i: (0, 0, 0, 0))],
        out_specs=pl.BlockSpec((None, blk, w), lambda b, i: (b, i, 0)),
        out_shape=jax.ShapeDtypeStruct((bsz, s, w), F32),
        scratch_shapes=[pltpu.VMEM((B_HEADS, 3 * blk, B_DIM), BF16),
                        pltpu.VMEM((B_HEADS, 3 * blk, B_DIM), BF16),
                        pltpu.VMEM((B_HEADS, c, B_DIM), BF16),
                        pltpu.VMEM((B_HEADS, c, B_DIM), BF16)],
        compiler_params=_params(("parallel", "arbitrary")),
        name="nbr_attn",
    )(u, u, u, u, u, u, u, uc, uc, bias8)


def _nbr_ctx_kernel(q_ref, k_ref, v_ref, o_ref):
    scale = B_DIM ** -0.5
    outs = []
    for h in range(B_HEADS):
        sl = slice(h * B_DIM, (h + 1) * B_DIM)
        s = _dot_nt(q_ref[:, sl].astype(BF16), k_ref[:, sl].astype(BF16)) * scale
        m = jnp.max(s, axis=1, keepdims=True)
        p = jnp.exp(s - m)
        l = jnp.sum(p, axis=1, keepdims=True)
        outs.append(_dot(p.astype(BF16), v_ref[:, sl].astype(BF16)) / l)
    o_ref[...] = jnp.concatenate(outs, axis=1)


def _nbr_ctx_attn(uc):
    bsz, c, _ = uc.shape
    w = B_HEADS * B_DIM
    return pl.pallas_call(
        _nbr_ctx_kernel,
        grid=(bsz,),
        in_specs=[pl.BlockSpec((None, c, w), lambda b: (b, 0, 3)),
                  pl.BlockSpec((None, c, w), lambda b: (b, 0, 4)),
                  pl.BlockSpec((None, c, w), lambda b: (b, 0, 5))],
        out_specs=pl.BlockSpec((None, c, w), lambda b: (b, 0, 0)),
        out_shape=jax.ShapeDtypeStruct((bsz, c, w), F32),
        compiler_params=_params(("arbitrary",)),
        name="nbr_attn_ctx",
    )(uc, uc, uc)


HALO = SUBLANES


def _halo_specs(tm, length, col, width):
    nt = length // tm
    per = tm // HALO
    last = length // HALO - 1
    return [pl.BlockSpec((None, HALO, width), lambda b, i, *_: (b, jnp.maximum(i * per - 1, 0), col)),
            pl.BlockSpec((None, tm, width), lambda b, i, *_: (b, i, col)),
            pl.BlockSpec((None, HALO, width), lambda b, i, *_: (b, jnp.minimum((i + 1) * per, last), col))], nt


def _fill_halo(buf, prev_ref, cur_ref, next_ref, i, nt, tm):
    zero = jnp.zeros(prev_ref.shape, F32)
    buf[0:HALO, :] = jnp.where(i > 0, prev_ref[...], zero)
    buf[HALO:HALO + tm, :] = cur_ref[...]
    buf[HALO + tm:, :] = jnp.where(i < nt - 1, next_ref[...], zero)


def _pool_kernel(prev_ref, cur_ref, next_ref, w_ref, ps_ref, o_ref, buf, *, tm, nt, length):
    i = pl.program_id(1)
    _fill_halo(buf, prev_ref, cur_ref, next_ref, i, nt, tm)

    def sh(j):
        return buf[HALO + j:HALO + j + tm, :]

    u = sh(0)
    sums = []
    acc = None
    lo, hi = 0, 0
    for w in POOL_SIZES:
        for j in list(range(-(w // 2), lo)) + list(range(hi, w // 2)):
            acc = sh(j) if acc is None else acc + sh(j)
        lo, hi = -(w // 2), w // 2
        sums.append(acc)
    lane = lax.broadcasted_iota(jnp.int32, (tm, D_GROUP), 1)
    t = (i * tm + lax.broadcasted_iota(jnp.int32, (tm, D_GROUP), 0))
    wsum = sums[-1]
    half = jnp.full((tm, D_GROUP), POOL_SIZES[-1] // 2, jnp.int32)
    for g in range(len(POOL_SIZES) - 2, -1, -1):
        sel = lane < (g + 1) * POOL_CH
        wsum = jnp.where(sel, sums[g], wsum)
        half = jnp.where(sel, POOL_SIZES[g] // 2, half)
    cnt = (jnp.minimum(t + half, length) - jnp.maximum(t - half, 0)).astype(F32)
    d = wsum / cnt - u
    o_ref[...] = _dot(d.astype(BF16), w_ref[...]) * ps_ref[...]


def _pool_mix(u, wbd, pool_scale, tm):
    bsz, length, _ = u.shape
    specs, nt = _halo_specs(tm, length, 6, D_GROUP)
    return pl.pallas_call(
        functools.partial(_pool_kernel, tm=tm, nt=nt, length=length),
        grid=(bsz, nt),
        in_specs=specs + [pl.BlockSpec((D_GROUP, D_GROUP), lambda b, i: (0, 0)),
                          pl.BlockSpec((1, D_GROUP), lambda b, i: (0, 0))],
        out_specs=pl.BlockSpec((None, tm, D_GROUP), lambda b, i: (b, i, 0)),
        out_shape=jax.ShapeDtypeStruct((bsz, length, D_GROUP), F32),
        scratch_shapes=[pltpu.VMEM((tm + 2 * HALO, D_GROUP), F32)],
        compiler_params=_params(("parallel", "arbitrary")),
        name="pool_mix",
    )(u, u, u, wbd, pool_scale)


def _hy_short_kernel(prev_ref, cur_ref, next_ref, w_ref, b_ref, o_ref, buf, *, tm, nt):
    i = pl.program_id(1)
    _fill_halo(buf, prev_ref, cur_ref, next_ref, i, nt, tm)
    w = w_ref[...]
    y = (buf[HALO - 1:HALO - 1 + tm, :] * w[0:1] + buf[HALO:HALO + tm, :] * w[1:2]
         + buf[HALO + 1:HALO + 1 + tm, :] * w[2:3] + b_ref[...])
    o_ref[...] = y.T


def _hy_short(u, w_short, b_short, tm):
    bsz, length, _ = u.shape
    nt = length // tm
    per = tm // HALO
    last = length // HALO - 1
    c0 = 7
    in_specs = [pl.BlockSpec((None, HALO, HY_CH), lambda b, i, j: (b, jnp.maximum(i * per - 1, 0), c0 + j)),
                pl.BlockSpec((None, tm, HY_CH), lambda b, i, j: (b, i, c0 + j)),
                pl.BlockSpec((None, HALO, HY_CH), lambda b, i, j: (b, jnp.minimum((i + 1) * per, last), c0 + j)),
                pl.BlockSpec((3, HY_CH), lambda b, i, j: (0, j)),
                pl.BlockSpec((1, HY_CH), lambda b, i, j: (0, j))]
    return pl.pallas_call(
        functools.partial(_hy_short_kernel, tm=tm, nt=nt),
        grid=(bsz, nt, 3),
        in_specs=in_specs,
        out_specs=pl.BlockSpec((None, None, HY_CH, tm), lambda b, i, j: (j, b, 0, i)),
        out_shape=jax.ShapeDtypeStruct((3, bsz, HY_CH, length), F32),
        scratch_shapes=[pltpu.VMEM((tm + 2 * HALO, HY_CH), F32)],
        compiler_params=_params(("parallel", "arbitrary", "arbitrary")),
        name="hyena_short_conv",
    )(u, u, u, w_short, b_short)


HY_FEAT = 40


def _hy_filter_kernel(band_ref, w1_ref, b1_ref, w2_ref, b2_ref, w3_ref, b3_ref, dl_ref,
                      k_ref, ssq_ref, nrm_ref, *, tp, length):
    i = pl.program_id(0)
    n_i = pl.num_programs(0)
    m = i * tp + lax.broadcasted_iota(jnp.int32, (1, tp), 1)
    t = jnp.where(m <= length, m, 2 * length - m).astype(F32)
    t_norm = t / max(length - 1, 1)
    ang = ((2.0 * math.pi / length) * t) * band_ref[...]
    row = lax.broadcasted_iota(jnp.int32, (HY_FEAT, tp), 0)
    z = jnp.where(row == 0, t_norm,
                  jnp.where(row <= HY_BANDS, jnp.cos(ang), jnp.where(row < HY_EMB, jnp.sin(ang), 0.0)))
    z = jnp.concatenate([z, jnp.zeros((LANES - HY_FEAT, tp), F32)], axis=0)
    h = jnp.sin(HY_SIN_FREQ * (_dot(w1_ref[...], z, HI) + b1_ref[...]))
    h = jnp.sin(HY_SIN_FREQ * (_dot(w2_ref[...], h, HI) + b2_ref[...]))
    hh, hl = _split_bf16(h)
    h = _dot(w3_ref[...], jnp.concatenate([hh, hl, hh], axis=0)) + b3_ref[...]
    h = h * jnp.exp(-t_norm * dl_ref[...])

    @pl.when(i == 0)
    def _():
        ssq_ref[...] = jnp.zeros(ssq_ref.shape, F32)

    for o in range(2):
        fwd = h[o * 2 * HY_CH:o * 2 * HY_CH + HY_CH]
        bwd = h[o * 2 * HY_CH + HY_CH:(o + 1) * 2 * HY_CH]
        k = jnp.where(m < length, fwd, jnp.where(m == length, 0.0, bwd))
        k_ref[o] = k
        extra = jnp.where(m == 0, bwd * bwd, 0.0)
        ssq_ref[o] += jnp.sum(k * k + extra, axis=1, keepdims=True)

    @pl.when(i == n_i - 1)
    def _():
        nrm_ref[...] = lax.rsqrt(ssq_ref[...] + EPS)


def _hy_filters(length, w1, b1, w2, b2, w3, b3, tp):
    bands = jnp.linspace(1e-4, HY_BANDS - 1, HY_BANDS, dtype=F32)
    band_col = jnp.concatenate([jnp.zeros((1,), F32), bands, bands,
                                jnp.zeros((HY_FEAT - HY_EMB,), F32)])[:, None]
    deltas = jnp.abs(jnp.linspace(HY_MIN_DECAY, HY_MAX_DECAY, HY_CH, dtype=F32))
    dl_col = jnp.tile(deltas, 4)[:, None]
    w1t = jnp.pad(w1.astype(F32).T, ((0, 0), (0, LANES - HY_EMB)))
    full = lambda shape: pl.BlockSpec(shape, lambda i: (0,) * len(shape))
    n = 2 * length
    return pl.pallas_call(
        functools.partial(_hy_filter_kernel, tp=tp, length=length),
        grid=(n // tp,),
        in_specs=[full((HY_FEAT, 1)), full((HY_HIDDEN, LANES)), full((HY_HIDDEN, 1)),
                  full((HY_HIDDEN, HY_HIDDEN)), full((HY_HIDDEN, 1)),
                  full((4 * HY_CH, 3 * HY_HIDDEN)), full((4 * HY_CH, 1)), full((4 * HY_CH, 1))],
        out_specs=[pl.BlockSpec((2, HY_CH, tp), lambda i: (0, 0, i)),
                   full((2, HY_CH, 1)), full((2, HY_CH, 1))],
        out_shape=[jax.ShapeDtypeStruct((2, HY_CH, n), F32),
                   jax.ShapeDtypeStruct((2, HY_CH, 1), F32),
                   jax.ShapeDtypeStruct((2, HY_CH, 1), F32)],
        compiler_params=_params(("arbitrary",)),
        name="hyena_filters",
    )(band_col, w1t, b1.astype(F32)[:, None], w2.astype(F32).T, b2.astype(F32)[:, None],
      _cat3(w3.astype(F32).T, 1), b3.astype(F32)[:, None], dl_col)


def _dft_consts(n1, n2):
    n = n1 * n2
    a1 = 2.0 * np.pi * ((np.arange(n1)[:, None] * np.arange(n1)[None, :]) % n1) / n1
    c1, s1 = np.cos(a1), np.sin(a1)
    a2 = 2.0 * np.pi * ((np.arange(n2)[:, None] * np.arange(n2)[None, :]) % n2) / n2
    c2, s2 = np.cos(a2), np.sin(a2)
    at = 2.0 * np.pi * ((np.arange(n1)[:, None] * np.arange(n2)[None, :]) % n) / n
    f1_full = np.concatenate([c1, -s1], axis=0)
    f1_half = f1_full[:, :n1 // 2]
    g1 = np.concatenate([c1[:n1 // 2], -s1[:n1 // 2]], axis=1)
    w2f = np.block([[c2, -s2], [s2, c2]])
    w2i = np.block([[c2, s2], [-s2, c2]])
    f = lambda x: jnp.asarray(x, F32)
    return dict(f1_full=f(f1_full), f1_half=f(f1_half), g1=f(g1), w2f=f(w2f), w2i=f(w2i),
                tr=f(np.cos(at)), ti=f(-np.sin(at)))


def _fft_fwd(slabs, f1, tr, ti, w2f, stack, n1, n2, prec):
    for c, x in enumerate(slabs):
        a = _dot(f1, x.astype(stack.dtype), prec)
        ar, ai = a[:n1], a[n1:]
        stack[c * n1:(c + 1) * n1, 0:n2] = (ar * tr - ai * ti).astype(stack.dtype)
        stack[c * n1:(c + 1) * n1, n2:2 * n2] = (ar * ti + ai * tr).astype(stack.dtype)
    return _dot(stack[...], w2f, prec)


def _split_bf16(x):
    hi = x.astype(BF16)
    return hi, (x - hi.astype(F32)).astype(BF16)


def _cat3(x, axis):
    hi, lo = _split_bf16(x)
    return jnp.concatenate([hi, hi, lo], axis=axis)


def _hy_spec_kernel(nrm_ref, k_ref, f1_ref, tr_ref, ti_ref, w2f_ref, o_ref, stack, *, cg, n1, n2):
    o = pl.program_id(0)
    g = pl.program_id(1)
    tr = tr_ref[...]
    ti = ti_ref[...]
    for c in range(cg):
        hi, lo = _split_bf16(k_ref[c])
        a = _dot(f1_ref[...], jnp.concatenate([hi, lo, hi], axis=0))
        ar, ai = a[:n1], a[n1:]
        sh, sl = _split_bf16(jnp.concatenate([ar * tr - ai * ti, ar * ti + ai * tr], axis=1))
        stack[c * n1:(c + 1) * n1, :] = jnp.concatenate([sh, sl, sh], axis=1)
    x = _dot(stack[...], w2f_ref[...])
    for c in range(cg):
        sc = nrm_ref[o * HY_CH + g * cg + c] * (1.0 / (n1 * n2))
        xc = x[c * n1:(c + 1) * n1] * sc
        o_ref[c, 0] = xc[:, :n2]
        o_ref[c, 1] = xc[:, n2:]


def _hy_spec(k4, nrm_flat, dc, cg, n1, n2):
    full = lambda shape: pl.BlockSpec(shape, lambda o, g: (0,) * len(shape))
    return pl.pallas_call(
        functools.partial(_hy_spec_kernel, cg=cg, n1=n1, n2=n2),
        grid=(2, HY_CH // cg),
        in_specs=[pl.BlockSpec(memory_space=pltpu.SMEM),
                  pl.BlockSpec((None, cg, n1, n2), lambda o, g: (o, g, 0, 0)),
                  full((2 * n1, 3 * n1)), full((n1, n2)), full((n1, n2)), full((6 * n2, 2 * n2))],
        out_specs=pl.BlockSpec((None, cg, 2, n1, n2), lambda o, g: (o, g, 0, 0, 0)),
        out_shape=jax.ShapeDtypeStruct((2, HY_CH, 2, n1, n2), F32),
        scratch_shapes=[pltpu.VMEM((cg * n1, 6 * n2), BF16)],
        compiler_params=_params(("parallel", "arbitrary")),
        name="hyena_filter_spectrum",
    )(nrm_flat, k4, _cat3(dc["f1_full"], 1), dc["tr"], dc["ti"], _cat3(dc["w2f"], 0))


def _hy_conv_kernel(skip_ref, x_ref, ks_ref, f1_ref, g1_ref, tr_ref, ti_ref, w2f_ref, w2i_ref,
                    o_ref, stack, *, cg, n1, n2):
    g = pl.program_id(1)
    tr = tr_ref[...]
    ti = ti_ref[...]

    def conv(slabs, order):
        x = _fft_fwd(slabs, f1_ref[...], tr, ti, w2f_ref[...], stack, n1, n2, None)
        for c in range(cg):
            xr, xi = x[c * n1:(c + 1) * n1, :n2], x[c * n1:(c + 1) * n1, n2:]
            kr, ki = ks_ref[order, c, 0], ks_ref[order, c, 1]
            stack[c * n1:(c + 1) * n1, 0:n2] = (xr * kr - xi * ki).astype(BF16)
            stack[c * n1:(c + 1) * n1, n2:2 * n2] = (xr * ki + xi * kr).astype(BF16)
        bm = _dot(stack[...], w2i_ref[...])
        outs = []
        for c in range(cg):
            br, bi = bm[c * n1:(c + 1) * n1, :n2], bm[c * n1:(c + 1) * n1, n2:]
            b2 = jnp.concatenate([br * tr + bi * ti, bi * tr - br * ti], axis=0)
            y = _dot(g1_ref[...], b2.astype(BF16))
            outs.append(y + slabs[c] * skip_ref[order * HY_CH + g * cg + c])
        return outs

    v = [x_ref[2, c] for c in range(cg)]
    y0 = conv(v, 0)
    z = [x_ref[0, c] * y0[c] for c in range(cg)]
    y1 = conv(z, 1)
    for c in range(cg):
        o_ref[c] = x_ref[1, c] * y1[c]


def _hy_conv(x4, kspec, skip_flat, dc, cg, n1, n2):
    bsz = x4.shape[1]
    full = lambda shape: pl.BlockSpec(shape, lambda b, g: (0,) * len(shape))
    return pl.pallas_call(
        functools.partial(_hy_conv_kernel, cg=cg, n1=n1, n2=n2),
        grid=(bsz, HY_CH // cg),
        in_specs=[pl.BlockSpec(memory_space=pltpu.SMEM),
                  pl.BlockSpec((3, None, cg, n1 // 2, n2), lambda b, g: (0, b, g, 0, 0)),
                  pl.BlockSpec((2, cg, 2, n1, n2), lambda b, g: (0, g, 0, 0, 0)),
                  full((2 * n1, n1 // 2)), full((n1 // 2, 2 * n1)), full((n1, n2)), full((n1, n2)),
                  full((2 * n2, 2 * n2)), full((2 * n2, 2 * n2))],
        out_specs=pl.BlockSpec((None, cg, n1 // 2, n2), lambda b, g: (b, g, 0, 0)),
        out_shape=jax.ShapeDtypeStruct((bsz, HY_CH, n1 // 2, n2), F32),
        scratch_shapes=[pltpu.VMEM((cg * n1, 2 * n2), BF16)],
        compiler_params=_params(("parallel", "arbitrary")),
        name="hyena_long_conv",
    )(skip_flat, x4, kspec, dc["f1_half"].astype(BF16), dc["g1"].astype(BF16), dc["tr"], dc["ti"],
      dc["w2f"].astype(BF16), dc["w2i"].astype(BF16))


def _hy_ctx_kernel(x_ref, k_ref, nrm_ref, skip_ref, fc_ref, fs_ref, o_ref, *, c):
    fc = fc_ref[...]
    fs = fs_ref[...]
    inv_n = 1.0 / (2 * c)

    def conv(x, order):
        kk = k_ref[order]
        kr, ki = _dot(kk, fc, HI), -_dot(kk, fs, HI)
        xr, xi = _dot(x, fc[:c], HI), -_dot(x, fs[:c], HI)
        yr, yi = xr * kr - xi * ki, xr * ki + xi * kr
        y = (_dot(yr, fc[:, :c], HI) - _dot(yi, fs[:, :c], HI)) * inv_n
        return y * nrm_ref[order] + x * skip_ref[order]

    z = x_ref[0] * conv(x_ref[2], 0)
    o_ref[...] = x_ref[1] * conv(z, 1)


def _hy_ctx_conv(xt, kt, nrm, skip_col):
    _, bsz, ch, c = xt.shape
    n = 2 * c
    ang = 2.0 * np.pi * ((np.arange(n)[:, None] * np.arange(n)[None, :]) % n) / n
    fc, fs = jnp.asarray(np.cos(ang), F32), jnp.asarray(np.sin(ang), F32)
    full = lambda shape: pl.BlockSpec(shape, lambda b: (0,) * len(shape))
    return pl.pallas_call(
        functools.partial(_hy_ctx_kernel, c=c),
        grid=(bsz,),
        in_specs=[pl.BlockSpec((3, None, ch, c), lambda b: (0, b, 0, 0)),
                  full((2, ch, n)), full((2, ch, 1)), full((2, ch, 1)), full((n, n)), full((n, n))],
        out_specs=pl.BlockSpec((None, ch, c), lambda b: (b, 0, 0)),
        out_shape=jax.ShapeDtypeStruct((bsz, ch, c), F32),
        compiler_params=_params(("arbitrary",)),
        name="hyena_ctx_conv",
    )(xt, kt, nrm, skip_col, fc, fs)


def _out_proj_kernel(a_ref, b_ref, p_ref, ht_ref, x_ref, g_ref, w_ref, o_ref):
    w = D_GROUP
    acc = _dot(a_ref[...].astype(BF16), w_ref[0:w])
    acc += _dot(b_ref[...].astype(BF16), w_ref[w:2 * w])
    acc += _dot(p_ref[...].astype(BF16), w_ref[2 * w:3 * w])
    acc += _dot(ht_ref[...].T.astype(BF16), w_ref[3 * w:4 * w])
    o_ref[...] = x_ref[...] + g_ref[...] * acc


def _out_proj(a, b, p, ht, x, mod3, row_of_batch, j_gate, w_out, tm):
    bsz, t, d = x.shape
    w = D_GROUP
    tok = pl.BlockSpec((None, tm, w), lambda bb, i: (bb, i, 0))
    return pl.pallas_call(
        _out_proj_kernel,
        grid=(bsz, t // tm),
        in_specs=[tok, tok, tok,
                  pl.BlockSpec((None, w, tm), lambda bb, i: (bb, 0, i)),
                  pl.BlockSpec((None, tm, d), lambda bb, i: (bb, i, 0)),
                  pl.BlockSpec((None, 1, d), lambda bb, i: (row_of_batch(bb), 0, j_gate)),
                  pl.BlockSpec((4 * w, d), lambda bb, i: (0, 0))],
        out_specs=pl.BlockSpec((None, tm, d), lambda bb, i: (bb, i, 0)),
        out_shape=jax.ShapeDtypeStruct((bsz, t, d), F32),
        compiler_params=_params(("parallel", "arbitrary")),
        name="out_proj_residual",
    )(a, b, p, ht, x, mod3, w_out)


MOE_ROWS = 256


def _route(x, g, sc, sh, rw3, rbc):
    per_group = N_EXPERTS // N_EXPERT_GROUPS
    tm = x.shape[0]
    ms = jnp.mean(x * x, axis=-1, keepdims=True)
    h = (x * lax.rsqrt(ms + EPS)) * g * (1.0 + sc) + sh
    hh, hl = _split_bf16(h)
    d = x.shape[1]
    logits = (_dot(hh, rw3[0:d]) + _dot(hl, rw3[d:2 * d]) + _dot(hh, rw3[2 * d:3 * d])).T[:N_EXPERTS]
    ex = jnp.exp(logits - jnp.max(logits, axis=0, keepdims=True))
    scores = ex / jnp.sum(ex, axis=0, keepdims=True)
    sel = scores + rbc
    srow = [sel[r:r + 1] for r in range(N_EXPERTS)]
    best = None
    for grp in range(N_EXPERT_GROUPS):
        rows = list(range(grp * per_group, (grp + 1) * per_group))
        v1 = functools.reduce(jnp.maximum, [srow[r] for r in rows])
        i1 = jnp.full((1, tm), rows[-1], jnp.int32)
        for r in reversed(rows[:-1]):
            i1 = jnp.where(srow[r] == v1, r, i1)
        rest = [jnp.where(i1 == r, -jnp.inf, srow[r]) for r in rows]
        v2 = functools.reduce(jnp.maximum, rest)
        i2 = jnp.full((1, tm), rows[-1], jnp.int32)
        for k in reversed(range(per_group - 1)):
            i2 = jnp.where(rest[k] == v2, rows[k], i2)
        gs = v1 + v2
        if best is None:
            best, e1, e2 = gs, i1, i2
        else:
            upd = gs > best
            best = jnp.where(upd, gs, best)
            e1 = jnp.where(upd, i1, e1)
            e2 = jnp.where(upd, i2, e2)
    row = lax.broadcasted_iota(jnp.int32, (N_EXPERTS, tm), 0)
    w1 = jnp.sum(jnp.where(row == e1, scores, 0.0), axis=0, keepdims=True)
    w2 = jnp.sum(jnp.where(row == e2, scores, 0.0), axis=0, keepdims=True)
    tot = w1 + w2
    gates_t = jnp.where(row == e1, w1 / tot, 0.0) + jnp.where(row == e2, w2 / tot, 0.0)
    gates_t = jnp.concatenate([gates_t, jnp.zeros((LANES - N_EXPERTS, tm), F32)], axis=0)
    return h, gates_t.T


def _moe_kernel(x_ref, g_ref, sc_ref, sh_ref, gate_ref, rw_ref, rb_ref, w1_ref, w3_ref, w2_ref, fg_ref,
                o_ref, h_scr, gates_scr, acc_scr, *, final):
    e = pl.program_id(2)
    tm = x_ref.shape[0]

    @pl.when(e == 0)
    def _():
        h, gates = _route(x_ref[...], g_ref[...], sc_ref[...], sh_ref[...], rw_ref[...], rb_ref[...])
        h_scr[...] = h.astype(BF16)
        gates_scr[...] = gates
        acc_scr[...] = jnp.zeros(acc_scr.shape, F32)

    lane = lax.broadcasted_iota(jnp.int32, (tm, LANES), 1)
    ge = jnp.sum(jnp.where(lane == e, gates_scr[...], 0.0), axis=1, keepdims=True)

    def up(j):
        hb = h_scr[j * MOE_ROWS:(j + 1) * MOE_ROWS, :]
        return _dot(hb, w1_ref[...]), _dot(hb, w3_ref[...])

    nxt = up(0)
    for j in range(tm // MOE_ROWS):
        a, b = nxt
        if (j + 1) * MOE_ROWS < tm:
            nxt = up(j + 1)
        rows = slice(j * MOE_ROWS, (j + 1) * MOE_ROWS)
        act = (a * jax.nn.sigmoid(a)) * b
        acc_scr[rows, :] += ge[rows] * _dot(act.astype(BF16), w2_ref[...])

    @pl.when(e == N_EXPERTS - 1)
    def _():
        y = x_ref[...] + gate_ref[...] * acc_scr[...]
        if final:
            ms = jnp.mean(y * y, axis=-1, keepdims=True)
            y = (y * lax.rsqrt(ms + EPS)) * fg_ref[...]
        o_ref[...] = y


def _moe(x, g, mod3, row_of_batch, j_shift, j_scale, j_gate, rw, rb, w1, w3, w2, final_g, final, tm):
    bsz, t, d = x.shape
    vec = lambda j: pl.BlockSpec((None, 1, d), lambda b, i, e: (row_of_batch(b), 0, j))
    full = lambda shape: pl.BlockSpec(shape, lambda b, i, e: (0,) * len(shape))
    return pl.pallas_call(
        functools.partial(_moe_kernel, final=final),
        grid=(bsz, t // tm, N_EXPERTS),
        in_specs=[pl.BlockSpec((None, tm, d), lambda b, i, e: (b, i, 0)),
                  full((1, d)), vec(j_scale), vec(j_shift), vec(j_gate),
                  full((3 * d, LANES)), full((N_EXPERTS, 1)),
                  pl.BlockSpec((None, d, D_EXPERT), lambda b, i, e: (e, 0, 0)),
                  pl.BlockSpec((None, d, D_EXPERT), lambda b, i, e: (e, 0, 0)),
                  pl.BlockSpec((None, D_EXPERT, d), lambda b, i, e: (e, 0, 0)),
                  full((1, d))],
        out_specs=pl.BlockSpec((None, tm, d), lambda b, i, e: (b, i, 0)),
        out_shape=jax.ShapeDtypeStruct((bsz, t, d), F32),
        scratch_shapes=[pltpu.VMEM((tm, d), BF16), pltpu.VMEM((tm, LANES), F32), pltpu.VMEM((tm, d), F32)],
        compiler_params=_params(("parallel", "parallel", "arbitrary")),
        name="moe_final" if final else "moe",
    )(x, g, mod3, mod3, mod3, rw, rb, w1, w3, w2, final_g)


MOE_SUB = 512
MOE_CAP = 96
MOE_BLK = 1024


def _moe_route_kernel(x_ref, g_ref, sc_ref, sh_ref, rw_ref, rb_ref, h_ref, gates_ref, cnt_ref):
    h, gates = _route(x_ref[...], g_ref[...], sc_ref[...], sh_ref[...], rw_ref[...], rb_ref[...])
    h_ref[...] = h.astype(BF16)
    gates_ref[...] = gates
    cnt = jnp.sum(jnp.where(gates > 0.0, 1.0, 0.0), axis=0, keepdims=True)
    cnt_ref[...] = jnp.broadcast_to(cnt, cnt_ref.shape)


def _moe_route(x, g, mod3, row_of_batch, j_shift, j_scale, rw, rb):
    bsz, t, d = x.shape
    tm = MOE_SUB
    vec = lambda j: pl.BlockSpec((None, 1, d), lambda b, i: (row_of_batch(b), 0, j))
    full = lambda shape: pl.BlockSpec(shape, lambda b, i: (0,) * len(shape))
    return pl.pallas_call(
        _moe_route_kernel,
        grid=(bsz, t // tm),
        in_specs=[pl.BlockSpec((None, tm, d), lambda b, i: (b, i, 0)),
                  full((1, d)), vec(j_scale), vec(j_shift), full((3 * d, LANES)), full((N_EXPERTS, 1))],
        out_specs=[pl.BlockSpec((None, tm, d), lambda b, i: (b, i, 0)),
                   pl.BlockSpec((None, tm, LANES), lambda b, i: (b, i, 0)),
                   pl.BlockSpec((None, None, SUBLANES, LANES), lambda b, i: (b, i, 0, 0))],
        out_shape=[jax.ShapeDtypeStruct((bsz, t, d), BF16),
                   jax.ShapeDtypeStruct((bsz, t, LANES), F32),
                   jax.ShapeDtypeStruct((bsz, t // tm, SUBLANES, LANES), F32)],
        compiler_params=_params(("parallel", "arbitrary")),
        name="moe_route",
    )(x, g, mod3, mod3, rw, rb)


def _moe_routed_kernel(np_ref, h_ref, gates_ref, x_ref, gate_ref, w1_ref, w3_ref, w2_ref, fg_ref,
                       o_ref, rank_scr, rankt_scr, gatet_scr, xc_scr, *, final, nblk):
    b = pl.program_id(0)
    i = pl.program_id(1)
    e = pl.program_id(2)
    nsub = MOE_BLK // MOE_SUB
    sub = lambda j: slice(j * MOE_SUB, (j + 1) * MOE_SUB)

    @pl.when(e == 0)
    def _():
        o_ref[...] = jnp.zeros(o_ref.shape, F32)
        r = lax.broadcasted_iota(jnp.int32, (MOE_SUB, MOE_SUB), 0)
        c = lax.broadcasted_iota(jnp.int32, (MOE_SUB, MOE_SUB), 1)
        ltri = jnp.where(c <= r, 1.0, 0.0).astype(BF16)
        utri = jnp.where(r <= c, 1.0, 0.0).astype(BF16)
        for j in range(nsub):
            gt = gates_ref[sub(j), :]
            rank_scr[j] = _dot(ltri, jnp.where(gt > 0.0, 1.0, 0.0).astype(BF16))
            gtt = gt.T[:N_EXPERTS]
            gatet_scr[j] = gtt
            rankt_scr[j] = _dot(jnp.where(gtt > 0.0, 1.0, 0.0).astype(BF16), utri)

    lane = lax.broadcasted_iota(jnp.int32, (MOE_SUB, LANES), 1)

    def one_pass(p, carry):
        base = (p * MOE_CAP + 1).astype(F32)
        slot_r = lax.broadcasted_iota(jnp.int32, (MOE_CAP, MOE_SUB), 0).astype(F32) + base
        slot_c = lax.broadcasted_iota(jnp.int32, (MOE_SUB, MOE_CAP), 1).astype(F32) + base
        gcs = []
        for j in range(nsub):
            rr = rankt_scr[j, pl.ds(e, 1), :]
            gr = gatet_scr[j, pl.ds(e, 1), :]
            pick = (rr == slot_r) & (gr > 0.0)
            xc_scr[j * MOE_CAP:(j + 1) * MOE_CAP, :] = _dot(
                jnp.where(pick, 1.0, 0.0).astype(BF16), h_ref[sub(j), :]).astype(BF16)
            gcs.append(jnp.sum(jnp.where(pick, gr, 0.0), axis=1, keepdims=True))
        xc = xc_scr[...]
        a = _dot(xc, w1_ref[...])
        bb = _dot(xc, w3_ref[...])
        y = _dot(((a * jax.nn.sigmoid(a)) * bb).astype(BF16), w2_ref[...])
        for j in range(nsub):
            yj = (y[j * MOE_CAP:(j + 1) * MOE_CAP] * gcs[j]).astype(BF16)
            rc = jnp.sum(jnp.where(lane == e, rank_scr[j], 0.0), axis=1, keepdims=True)
            gc = jnp.sum(jnp.where(lane == e, gates_ref[sub(j), :], 0.0), axis=1, keepdims=True)
            put = jnp.where((rc == slot_c) & (gc > 0.0), 1.0, 0.0).astype(BF16)
            o_ref[sub(j), :] += _dot(put, yj)
        return carry

    lax.fori_loop(0, np_ref[(b * nblk + i) * N_EXPERTS + e], one_pass, 0)

    @pl.when(e == N_EXPERTS - 1)
    def _():
        y = x_ref[...] + gate_ref[...] * o_ref[...]
        if final:
            ms = jnp.mean(y * y, axis=-1, keepdims=True)
            y = (y * lax.rsqrt(ms + EPS)) * fg_ref[...]
        o_ref[...] = y


def _moe_routed(x, g, mod3, row_of_batch, j_shift, j_scale, j_gate, rw, rb, w1, w3, w2, final_g, final):
    bsz, t, d = x.shape
    h, gates, cnt = _moe_route(x, g, mod3, row_of_batch, j_shift, j_scale, rw, rb)
    nblk = t // MOE_BLK
    nsub = MOE_BLK // MOE_SUB
    passes = jnp.ceil(cnt[:, :, 0, :N_EXPERTS] / MOE_CAP).astype(jnp.int32)
    passes = jnp.max(passes.reshape(bsz, nblk, nsub, N_EXPERTS), axis=2).reshape(-1)
    full = lambda shape: pl.BlockSpec(shape, lambda b, i, e, np_: (0,) * len(shape))
    grid_spec = pltpu.PrefetchScalarGridSpec(
        num_scalar_prefetch=1,
        grid=(bsz, nblk, N_EXPERTS),
        in_specs=[pl.BlockSpec((None, MOE_BLK, d), lambda b, i, e, np_: (b, i, 0)),
                  pl.BlockSpec((None, MOE_BLK, LANES), lambda b, i, e, np_: (b, i, 0)),
                  pl.BlockSpec((None, MOE_BLK, d), lambda b, i, e, np_: (b, i, 0)),
                  pl.BlockSpec((None, 1, d), lambda b, i, e, np_: (row_of_batch(b), 0, j_gate)),
                  pl.BlockSpec((None, d, D_EXPERT), lambda b, i, e, np_: (e, 0, 0)),
                  pl.BlockSpec((None, d, D_EXPERT), lambda b, i, e, np_: (e, 0, 0)),
                  pl.BlockSpec((None, D_EXPERT, d), lambda b, i, e, np_: (e, 0, 0)),
                  full((1, d))],
        out_specs=pl.BlockSpec((None, MOE_BLK, d), lambda b, i, e, np_: (b, i, 0)),
        scratch_shapes=[pltpu.VMEM((nsub, MOE_SUB, LANES), F32),
                        pltpu.VMEM((nsub, N_EXPERTS, MOE_SUB), F32),
                        pltpu.VMEM((nsub, N_EXPERTS, MOE_SUB), F32),
                        pltpu.VMEM((nsub * MOE_CAP, d), BF16)])
    return pl.pallas_call(
        functools.partial(_moe_routed_kernel, final=final, nblk=nblk),
        grid_spec=grid_spec,
        out_shape=jax.ShapeDtypeStruct((bsz, t, d), F32),
        compiler_params=_params(("parallel", "parallel", "arbitrary")),
        name="moe_routed_final" if final else "moe_routed",
    )(passes, h, gates, x, mod3, w1, w3, w2, final_g)


def _tile(n, pref):
    t = min(n, pref)
    assert n % t == 0
    return t


def _fft_split(n):
    n2 = LANES
    assert n % n2 == 0
    return n // n2, n2


def _hyena_latent(u, lp, fargs):
    bsz, s, _ = u.shape
    n1, n2 = _fft_split(2 * s)
    dc = _dft_consts(n1, n2)
    cg = 8
    kt_l, _, nrm_l = _hy_filters(s, *fargs, tp=_tile(2 * s, 1024))
    kspec = _hy_spec(kt_l.reshape(2, HY_CH, n1, n2), nrm_l.reshape(2 * HY_CH), dc, cg, n1, n2)
    xt = _hy_short(u, lp["hy_short_w"].astype(F32), lp["hy_short_b"].astype(F32)[None, :], _tile(s, 512))
    h_l = _hy_conv(xt.reshape(3, bsz, HY_CH, n1 // 2, n2), kspec, lp["hy_skip"].astype(F32).reshape(2 * HY_CH),
                   dc, cg, n1, n2)
    return h_l.reshape(bsz, HY_CH, s)


def _mixers(u, uc, lp, li, need_ctx):
    bsz, s, _ = u.shape
    c = uc.shape[1]
    lam_init = 0.8 - 0.6 * math.exp(-0.3 * li)
    lam_vecs = lp["a_lambda"].astype(F32)
    subln = lp["a_subln_g"].astype(F32)[None, :]

    qc, kct, vc = _attn_prep(uc, _tile(c, 256), rope=False)
    ql, klt, vl = _attn_prep(u, _tile(s, 256), rope=True)
    a_l = _diff_attn(ql, kct, vc, klt, vl, lam_vecs, subln, lam_init, _tile(s, 512), _tile(s, 2048))
    bias8 = _nbr_bias(lp["b_rpb"])
    b_l = _nbr_attn(u, uc, bias8)
    wbd = jax.scipy.linalg.block_diag(*[lp["pool_w"][g] for g in range(len(POOL_SIZES))]).astype(BF16)
    pscale = lp["pool_scale"].astype(F32)[None, :]
    p_l = _pool_mix(u, wbd, pscale, _tile(s, 512))
    fargs = (lp["hy_f_w1"], lp["hy_f_b1"], lp["hy_f_w2"], lp["hy_f_b2"], lp["hy_f_w3"], lp["hy_f_b3"])
    skip = lp["hy_skip"].astype(F32)
    w_short = lp["hy_short_w"].astype(F32)
    b_short = lp["hy_short_b"].astype(F32)[None, :]
    h_l = _hyena_latent(u, lp, fargs)
    lat = (a_l, b_l, p_l, h_l)
    if not need_ctx:
        return lat, None
    a_c = _diff_attn(qc, kct, vc, None, None, lam_vecs, subln, lam_init, _tile(c, 256), None)
    b_c = _nbr_ctx_attn(uc)
    p_c = _pool_mix(uc, wbd, pscale, _tile(c, 256))
    kt_c, _, nrm_c = _hy_filters(c, *fargs, tp=_tile(2 * c, 512))
    xtc = _hy_short(uc, w_short, b_short, _tile(c, 256))
    h_c = _hy_ctx_conv(xtc, kt_c, nrm_c, skip[:, :, None])
    return lat, (a_c, b_c, p_c, h_c)


def kernel(x, c, ctx, c_ctx, norm1_g, norm2_g, ada_w, ada_b, w_in, w_out, a_lambda, a_subln_g, b_rpb, pool_w, pool_scale, hy_short_w, hy_short_b, hy_f_w1, hy_f_b1, hy_f_w2, hy_f_b2, hy_f_w3, hy_f_b3, hy_skip, router_w, router_b, moe_w1, moe_w3, moe_w2, final_g):
    depth = norm1_g.shape[0]
    bsz, s, d = x.shape
    cl = ctx.shape[1]
    assert bsz <= SUBLANES - 1
    xl, xc = x, ctx
    cpad = jnp.zeros((SUBLANES, d), F32).at[:bsz].set(c.astype(F32)).at[bsz].set(c_ctx.astype(F32))
    rw = _cat3(jnp.pad(router_w.astype(F32), ((0, 0), (0, LANES - N_EXPERTS))), 0)
    rb = router_b.astype(F32)[:, None]
    lat_row = lambda b: b
    ctx_row = lambda b: bsz
    fg = final_g.astype(F32)[None, :]
    tm = _tile(s, 512)
    tmc = _tile(cl, 256)
    for li in range(depth):
        need_ctx = li < depth - 1
        lp = dict(a_lambda=a_lambda[li], a_subln_g=a_subln_g[li], b_rpb=b_rpb[li], pool_w=pool_w[li],
                  pool_scale=pool_scale[li], hy_short_w=hy_short_w[li], hy_short_b=hy_short_b[li],
                  hy_f_w1=hy_f_w1[li], hy_f_b1=hy_f_b1[li], hy_f_w2=hy_f_w2[li], hy_f_b2=hy_f_b2[li],
                  hy_f_w3=hy_f_w3[li], hy_f_b3=hy_f_b3[li], hy_skip=hy_skip[li])
        mod3 = _ada(cpad, ada_w[li].astype(F32), ada_b[li].astype(F32)[None, :]).reshape(SUBLANES, 1, 6 * d)
        n1g = norm1_g[li].astype(F32)[None, :]
        n2g = norm2_g[li].astype(F32)[None, :]
        w_in_b = w_in[li].astype(BF16)
        w_out_b = w_out[li].astype(BF16)
        u = _norm_proj(xl, n1g, mod3, lat_row, 0, 1, w_in_b, tm)
        uc = _norm_proj(xc, n1g, mod3, ctx_row, 0, 1, w_in_b, tmc)
        lat, cx = _mixers(u, uc, lp, li, need_ctx)
        xl = _out_proj(*lat, xl, mod3, lat_row, 2, w_out_b, tm)
        w1b, w3b, w2b = moe_w1[li].astype(BF16), moe_w3[li].astype(BF16), moe_w2[li].astype(BF16)
        if need_ctx:
            xc = _out_proj(*cx, xc, mod3, ctx_row, 2, w_out_b, tmc)
            xc = _moe(xc, n2g, mod3, ctx_row, 3, 4, 5, rw, rb, w1b, w3b, w2b, fg, False, tmc)
        xl = _moe_routed(xl, n2g, mod3, lat_row, 3, 4, 5, rw, rb, w1b, w3b, w2b, fg, li == depth - 1)
    return xl
```

```python
import functools
import math

import numpy as np
import jax
import jax.numpy as jnp
from jax import lax
from jax.experimental import pallas as pl
from jax.experimental.pallas import tpu as pltpu

F32 = jnp.float32
BF16 = jnp.bfloat16
HI = lax.Precision.HIGHEST

GRID_W = 64
A_HEADS = 4
A_QK = 32
A_V = 64
ROPE_BASE = 10000.0
B_HEADS = 4
B_DIM = 64
WIN_R = 8
WIN_C = 16
POOL_SIZES = (2, 4, 8, 16)
POOL_CH = 64
D_GROUP = 256
HY_CH = 256
HY_BANDS = 16
HY_EMB = 1 + 2 * HY_BANDS
HY_HIDDEN = 64
HY_SIN_FREQ = 1.0
HY_MIN_DECAY = math.log(1e-2) / 1.5
HY_MAX_DECAY = math.log(1e-2) / 0.3
N_EXPERTS = 16
N_EXPERT_GROUPS = 4
D_EXPERT = 512
EPS = 1e-6
LOG2E = 1.4426950408889634

LANES = 128
SUBLANES = 8
VMEM_LIMIT = 56 * 1024 * 1024


def _params(sem):
    return pltpu.CompilerParams(dimension_semantics=sem, vmem_limit_bytes=VMEM_LIMIT)


def _dot(a, b, prec=None):
    return jnp.dot(a, b, precision=prec, preferred_element_type=F32)


def _dot_nt(a, b):
    return lax.dot_general(a, b, (((1,), (1,)), ((), ())), preferred_element_type=F32)


def _ada_kernel(c_ref, w_ref, b_ref, o_ref):
    cf = c_ref[...]
    s = cf * jax.nn.sigmoid(cf)
    o_ref[...] = _dot(s, w_ref[...], HI) + b_ref[...]


def _ada(cpad, w, b):
    d = cpad.shape[1]
    n = w.shape[1]
    return pl.pallas_call(
        _ada_kernel,
        grid=(n // d,),
        in_specs=[pl.BlockSpec((SUBLANES, d), lambda j: (0, 0)),
                  pl.BlockSpec((d, d), lambda j: (0, j)),
                  pl.BlockSpec((1, d), lambda j: (0, j))],
        out_specs=pl.BlockSpec((SUBLANES, d), lambda j: (0, j)),
        out_shape=jax.ShapeDtypeStruct((SUBLANES, n), F32),
        compiler_params=_params(("arbitrary",)),
        name="ada_mod",
    )(cpad, w, b)


def _norm_proj_kernel(x_ref, g_ref, sc_ref, sh_ref, w_ref, o_ref):
    x = x_ref[...]
    ms = jnp.mean(x * x, axis=-1, keepdims=True)
    h = (x * lax.rsqrt(ms + EPS)) * g_ref[...] * (1.0 + sc_ref[...]) + sh_ref[...]
    o_ref[...] = _dot(h.astype(BF16), w_ref[...])


def _norm_proj(x, g, mod3, row_of_batch, j_shift, j_scale, w, tm):
    bsz, t, d = x.shape
    n = w.shape[1]
    return pl.pallas_call(
        _norm_proj_kernel,
        grid=(bsz, t // tm),
        in_specs=[pl.BlockSpec((None, tm, d), lambda b, i: (b, i, 0)),
                  pl.BlockSpec((1, d), lambda b, i: (0, 0)),
                  pl.BlockSpec((None, 1, d), lambda b, i: (row_of_batch(b), 0, j_scale)),
                  pl.BlockSpec((None, 1, d), lambda b, i: (row_of_batch(b), 0, j_shift)),
                  pl.BlockSpec((d, n), lambda b, i: (0, 0))],
        out_specs=pl.BlockSpec((None, tm, n), lambda b, i: (b, i, 0)),
        out_shape=jax.ShapeDtypeStruct((bsz, t, n), F32),
        compiler_params=_params(("parallel", "arbitrary")),
        name="norm_in_proj",
    )(x, g, mod3, mod3, w)


def _aprep_kernel(*refs, rope):
    if rope:
        u_ref, cos_ref, sin_ref, q_ref, kt_ref, v_ref = refs
    else:
        u_ref, q_ref, kt_ref, v_ref = refs
    u = u_ref[...]
    q = u[:, 0:256]
    k = u[:, 256:512]
    v = u[:, 512:768]
    if rope:
        cos_t = cos_ref[...]
        sin_t = sin_ref[...]
        lane = lax.broadcasted_iota(jnp.int32, cos_t.shape, 1)
        first = (lane % (2 * 16)) < 16

        def rot(x):
            halves = []
            for j in range(2):
                xh = x[:, j * LANES:(j + 1) * LANES]
                swap = jnp.where(first, pltpu.roll(xh, LANES - 16, axis=1), pltpu.roll(xh, 16, axis=1))
                halves.append(xh * cos_t + swap * sin_t)
            return jnp.concatenate(halves, axis=1)

        q = rot(q)
        k = rot(k)
    q = q * (A_QK ** -0.5 * LOG2E)
    kt = k.T
    for hc in range(2 * A_HEADS):
        q_ref[hc] = q[:, hc * A_QK:(hc + 1) * A_QK].astype(BF16)
        kt_ref[hc] = kt[hc * A_QK:(hc + 1) * A_QK, :].astype(BF16)
    lane = lax.broadcasted_iota(jnp.int32, (v.shape[0], LANES - A_V), 1)
    ones_col = jnp.where(lane == 0, 1.0, 0.0)
    for h in range(A_HEADS):
        v_ref[h] = jnp.concatenate([v[:, h * A_V:(h + 1) * A_V], ones_col], axis=1).astype(BF16)


def _rope_tables(length):
    n_freq = A_QK // 4
    inv = ROPE_BASE ** (-jnp.arange(n_freq, dtype=F32) / n_freq)
    t = jnp.arange(length)
    row = (t // GRID_W).astype(F32)
    col = (t % GRID_W).astype(F32)
    ang = jnp.concatenate([row[:, None] * inv, col[:, None] * inv], axis=-1)
    cos, sin = jnp.cos(ang), jnp.sin(ang)
    cos_t = jnp.tile(jnp.concatenate([cos, cos], axis=-1), (1, LANES // 32))
    sin_t = jnp.tile(jnp.concatenate([-sin, sin], axis=-1), (1, LANES // 32))
    return cos_t, sin_t


def _attn_prep(u, tm, rope):
    bsz, t, _ = u.shape
    nh = 2 * A_HEADS
    in_specs = [pl.BlockSpec((None, tm, 768), lambda b, i: (b, i, 0))]
    args = [u]
    if rope:
        cos_t, sin_t = _rope_tables(t)
        in_specs += [pl.BlockSpec((tm, LANES), lambda b, i: (i, 0))] * 2
        args += [cos_t, sin_t]
    return pl.pallas_call(
        functools.partial(_aprep_kernel, rope=rope),
        grid=(bsz, t // tm),
        in_specs=in_specs,
        out_specs=[pl.BlockSpec((None, nh, tm, A_QK), lambda b, i: (b, 0, i, 0)),
                   pl.BlockSpec((None, nh, A_QK, tm), lambda b, i: (b, 0, 0, i)),
                   pl.BlockSpec((None, A_HEADS, tm, LANES), lambda b, i: (b, 0, i, 0))],
        out_shape=[jax.ShapeDtypeStruct((bsz, nh, t, A_QK), BF16),
                   jax.ShapeDtypeStruct((bsz, nh, A_QK, t), BF16),
                   jax.ShapeDtypeStruct((bsz, A_HEADS, t, LANES), BF16)],
        compiler_params=_params(("parallel", "arbitrary")),
        name="attn_prep_rope" if rope else "attn_prep_ctx",
    )(*args)


QK_LOOKAHEAD = 3


def _dattn_kernel(*refs, lam_init, has_lat):
    if has_lat:
        lam_ref, g_ref, q_ref, kc_ref, vc_ref, k_ref, v_ref, o_ref, m_scr, acc_scr = refs
    else:
        lam_ref, g_ref, q_ref, kc_ref, vc_ref, o_ref, m_scr, acc_scr = refs
    ki = pl.program_id(2)
    nk = pl.num_programs(2)
    nh = 2 * A_HEADS

    def update(kt_r, v_r):
        scores = [_dot(q_ref[j], kt_r[j]) for j in range(QK_LOOKAHEAD)]
        for hc in range(nh):
            s = scores[hc]
            if hc + QK_LOOKAHEAD < nh:
                scores.append(_dot(q_ref[hc + QK_LOOKAHEAD], kt_r[hc + QK_LOOKAHEAD]))
            m_prev = m_scr[hc]
            m_new = jnp.maximum(m_prev, jnp.max(s, axis=1, keepdims=True))
            alpha = jnp.exp2(m_prev - m_new)
            p = jnp.exp2((s - m_new[:, :1]).astype(BF16))
            acc_scr[hc] = alpha * acc_scr[hc] + _dot(p, v_r[hc // 2])
            m_scr[hc] = m_new

    @pl.when(ki == 0)
    def _():
        m_scr[...] = jnp.full(m_scr.shape, -jnp.inf, F32)
        acc_scr[...] = jnp.zeros(acc_scr.shape, F32)
        update(kc_ref, vc_ref)

    if has_lat:
        @pl.when(ki > 0)
        def _():
            update(k_ref, v_ref)

    @pl.when(ki == nk - 1)
    def _():
        lv = lam_ref[...]
        lam = (jnp.exp(jnp.sum(lv[0:1] * lv[1:2], axis=1, keepdims=True))
               - jnp.exp(jnp.sum(lv[2:3] * lv[3:4], axis=1, keepdims=True)) + lam_init)
        for h in range(A_HEADS):
            a0 = acc_scr[2 * h]
            a1 = acc_scr[2 * h + 1]
            o = a0[:, :A_V] / a0[:, A_V:A_V + 1] - lam * (a1[:, :A_V] / a1[:, A_V:A_V + 1])
            ms = jnp.mean(o * o, axis=-1, keepdims=True)
            o_ref[:, h * A_V:(h + 1) * A_V] = (o * lax.rsqrt(ms + EPS)) * g_ref[...] * (1.0 - lam_init)


def _diff_attn(q, kct, vc, kt, v, lam_vecs, subln_g, lam_init, tq, tk):
    bsz, nh, t, _ = q.shape
    c = kct.shape[-1]
    has_lat = kt is not None
    nk = 1 + (kt.shape[-1] // tk if has_lat else 0)
    in_specs = [pl.BlockSpec((4, A_QK), lambda b, i, k: (0, 0)),
                pl.BlockSpec((1, A_V), lambda b, i, k: (0, 0)),
                pl.BlockSpec((None, nh, tq, A_QK), lambda b, i, k: (b, 0, i, 0)),
                pl.BlockSpec((None, nh, A_QK, c), lambda b, i, k: (b, 0, 0, 0)),
                pl.BlockSpec((None, A_HEADS, c, LANES), lambda b, i, k: (b, 0, 0, 0))]
    args = [lam_vecs, subln_g, q, kct, vc]
    if has_lat:
        in_specs += [pl.BlockSpec((None, nh, A_QK, tk), lambda b, i, k: (b, 0, 0, jnp.maximum(k - 1, 0))),
                     pl.BlockSpec((None, A_HEADS, tk, LANES), lambda b, i, k: (b, 0, jnp.maximum(k - 1, 0), 0))]
        args += [kt, v]
    return pl.pallas_call(
        functools.partial(_dattn_kernel, lam_init=lam_init, has_lat=has_lat),
        grid=(bsz, t // tq, nk),
        in_specs=in_specs,
        out_specs=pl.BlockSpec((None, tq, A_HEADS * A_V), lambda b, i, k: (b, i, 0)),
        out_shape=jax.ShapeDtypeStruct((bsz, t, A_HEADS * A_V), F32),
        scratch_shapes=[pltpu.VMEM((nh, tq, LANES), F32),
                        pltpu.VMEM((nh, tq, LANES), F32)],
        compiler_params=_params(("parallel", "parallel", "arbitrary")),
        name="diff_attn" if has_lat else "diff_attn_ctx",
    )(*args)


NB_ROWS = 8


def _nbr_bias(rpb):
    cols = jnp.arange(GRID_W)
    c0 = jnp.clip(cols - WIN_C // 2, 0, GRID_W - WIN_C)
    in_win = (cols[None, :] >= c0[:, None]) & (cols[None, :] < c0[:, None] + WIN_C)
    dc = jnp.clip(cols[None, :] - cols[:, None], -(WIN_C - 1), WIN_C - 1) + (WIN_C - 1)
    onehot = (dc[None] == jnp.arange(2 * WIN_C - 1)[:, None, None]).astype(F32)
    g = jnp.einsum("hab,bqk->haqk", rpb.astype(F32), onehot, precision=HI)
    g = jnp.where(in_win[None, None], g, -jnp.inf)
    b = jnp.stack([g[:, a0:a0 + WIN_R] for a0 in range(WIN_R)], axis=0)
    b = jnp.transpose(b, (0, 1, 3, 2, 4))
    return b.reshape(WIN_R, B_HEADS, GRID_W, WIN_R * GRID_W)


def _nbr_kernel(q_ref, kp_ref, kc_ref, kn_ref, vp_ref, vcur_ref, vn_ref, kctx_ref, vctx_ref, bias_ref,
                o_ref, kwin, vwin, kcx, vcx, *, n_rows):
    rb = pl.program_id(1)
    blk = NB_ROWS * GRID_W
    scale = B_DIM ** -0.5
    for h in range(B_HEADS):
        sl = slice(h * B_DIM, (h + 1) * B_DIM)
        for j, (kr, vr) in enumerate(((kp_ref, vp_ref), (kc_ref, vcur_ref), (kn_ref, vn_ref))):
            kwin[h, j * blk:(j + 1) * blk, :] = kr[:, sl].astype(BF16)
            vwin[h, j * blk:(j + 1) * blk, :] = vr[:, sl].astype(BF16)
        kcx[h] = kctx_ref[:, sl].astype(BF16)
        vcx[h] = vctx_ref[:, sl].astype(BF16)

    def window(rr):
        r = rb * NB_ROWS + rr
        r0 = jnp.clip(r - WIN_R // 2, 0, n_rows - WIN_R)
        off = pl.multiple_of((r0 - (rb - 1) * NB_ROWS) * GRID_W, GRID_W)
        return off, r0 - r + (WIN_R - 1)

    def scores(rr):
        off, a0 = window(rr)
        qrow = q_ref[rr * GRID_W:(rr + 1) * GRID_W, :]
        out = []
        for h in range(B_HEADS):
            qh = qrow[:, h * B_DIM:(h + 1) * B_DIM].astype(BF16)
            s = _dot_nt(qh, kwin[h, pl.ds(off, WIN_R * GRID_W), :]) * scale + bias_ref[a0, h]
            out.append((s, _dot_nt(qh, kcx[h]) * scale))
        return out

    nxt = scores(0)
    for rr in range(NB_ROWS):
        cur = nxt
        if rr + 1 < NB_ROWS:
            nxt = scores(rr + 1)
        off, _ = window(rr)
        outs = []
        for h in range(B_HEADS):
            s, sc = cur[h]
            m = jnp.maximum(jnp.max(s, axis=1, keepdims=True), jnp.max(sc, axis=1, keepdims=True))
            p = jnp.exp(s - m)
            pc = jnp.exp(sc - m)
            l = jnp.sum(p, axis=1, keepdims=True) + jnp.sum(pc, axis=1, keepdims=True)
            o = _dot(p.astype(BF16), vwin[h, pl.ds(off, WIN_R * GRID_W), :]) + _dot(pc.astype(BF16), vcx[h])
            outs.append(o / l)
        o_ref[rr * GRID_W:(rr + 1) * GRID_W, :] = jnp.concatenate(outs, axis=1)


def _nbr_attn(u, uc, bias8):
    bsz, s, _ = u.shape
    c = uc.shape[1]
    n_rows = s // GRID_W
    nb = n_rows // NB_ROWS
    blk = NB_ROWS * GRID_W
    w = B_HEADS * B_DIM

    def spec(col, shift):
        return pl.BlockSpec((None, blk, w), lambda b, i: (b, jnp.clip(i + shift, 0, nb - 1), col))

    return pl.pallas_call(
        functools.partial(_nbr_kernel, n_rows=n_rows),
        grid=(bsz, nb),
        in_specs=[spec(3, 0), spec(4, -1), spec(4, 0), spec(4, 1), spec(5, -1), spec(5, 0), spec(5, 1),
                  pl.BlockSpec((None, c, w), lambda b, i: (b, 0, 4)),
                  pl.BlockSpec((None, c, w), lambda b, i: (b, 0, 5)),
                  pl.BlockSpec(bias8.shape, lambda b, i: (0, 0, 0, 0))],
        out_specs=pl.BlockSpec((None, blk, w), lambda b, i: (b, i, 0)),
        out_shape=jax.ShapeDtypeStruct((bsz, s, w), F32),
        scratch_shapes=[pltpu.VMEM((B_HEADS, 3 * blk, B_DIM), BF16),
                        pltpu.VMEM((B_HEADS, 3 * blk, B_DIM), BF16),
                        pltpu.VMEM((B_HEADS, c, B_DIM), BF16),
                        pltpu.VMEM((B_HEADS, c, B_DIM), BF16)],
        compiler_params=_params(("parallel", "arbitrary")),
        name="nbr_attn",
    )(u, u, u, u, u, u, u, uc, uc, bias8)


def _nbr_ctx_kernel(q_ref, k_ref, v_ref, o_ref):
    scale = B_DIM ** -0.5
    outs = []
    for h in range(B_HEADS):
        sl = slice(h * B_DIM, (h + 1) * B_DIM)
        s = _dot_nt(q_ref[:, sl].astype(BF16), k_ref[:, sl].astype(BF16)) * scale
        m = jnp.max(s, axis=1, keepdims=True)
        p = jnp.exp(s - m)
        l = jnp.sum(p, axis=1, keepdims=True)
        outs.append(_dot(p.astype(BF16), v_ref[:, sl].astype(BF16)) / l)
    o_ref[...] = jnp.concatenate(outs, axis=1)


def _nbr_ctx_attn(uc):
    bsz, c, _ = uc.shape
    w = B_HEADS * B_DIM
    return pl.pallas_call(
        _nbr_ctx_kernel,
        grid=(bsz,),
        in_specs=[pl.BlockSpec((None, c, w), lambda b: (b, 0, 3)),
                  pl.BlockSpec((None, c, w), lambda b: (b, 0, 4)),
                  pl.BlockSpec((None, c, w), lambda b: (b, 0, 5))],
        out_specs=pl.BlockSpec((None, c, w), lambda b: (b, 0, 0)),
        out_shape=jax.ShapeDtypeStruct((bsz, c, w), F32),
        compiler_params=_params(("arbitrary",)),
        name="nbr_attn_ctx",
    )(uc, uc, uc)


HALO = SUBLANES


def _halo_specs(tm, length, col, width):
    nt = length // tm
    per = tm // HALO
    last = length // HALO - 1
    return [pl.BlockSpec((None, HALO, width), lambda b, i, *_: (b, jnp.maximum(i * per - 1, 0), col)),
            pl.BlockSpec((None, tm, width), lambda b, i, *_: (b, i, col)),
            pl.BlockSpec((None, HALO, width), lambda b, i, *_: (b, jnp.minimum((i + 1) * per, last), col))], nt


def _fill_halo(buf, prev_ref, cur_ref, next_ref, i, nt, tm):
    zero = jnp.zeros(prev_ref.shape, F32)
    buf[0:HALO, :] = jnp.where(i > 0, prev_ref[...], zero)
    buf[HALO:HALO + tm, :] = cur_ref[...]
    buf[HALO + tm:, :] = jnp.where(i < nt - 1, next_ref[...], zero)


def _pool_kernel(prev_ref, cur_ref, next_ref, w_ref, ps_ref, o_ref, buf, *, tm, nt, length):
    i = pl.program_id(1)
    _fill_halo(buf, prev_ref, cur_ref, next_ref, i, nt, tm)

    def sh(j):
        return buf[HALO + j:HALO + j + tm, :]

    u = sh(0)
    sums = []
    acc = None
    lo, hi = 0, 0
    for w in POOL_SIZES:
        for j in list(range(-(w // 2), lo)) + list(range(hi, w // 2)):
            acc = sh(j) if acc is None else acc + sh(j)
        lo, hi = -(w // 2), w // 2
        sums.append(acc)
    lane = lax.broadcasted_iota(jnp.int32, (tm, D_GROUP), 1)
    t = (i * tm + lax.broadcasted_iota(jnp.int32, (tm, D_GROUP), 0))
    wsum = sums[-1]
    half = jnp.full((tm, D_GROUP), POOL_SIZES[-1] // 2, jnp.int32)
    for g in range(len(POOL_SIZES) - 2, -1, -1):
        sel = lane < (g + 1) * POOL_CH
        wsum = jnp.where(sel, sums[g], wsum)
        half = jnp.where(sel, POOL_SIZES[g] // 2, half)
    cnt = (jnp.minimum(t + half, length) - jnp.maximum(t - half, 0)).astype(F32)
    d = wsum / cnt - u
    o_ref[...] = _dot(d.astype(BF16), w_ref[...]) * ps_ref[...]


def _pool_mix(u, wbd, pool_scale, tm):
    bsz, length, _ = u.shape
    specs, nt = _halo_specs(tm, length, 6, D_GROUP)
    return pl.pallas_call(
        functools.partial(_pool_kernel, tm=tm, nt=nt, length=length),
        grid=(bsz, nt),
        in_specs=specs + [pl.BlockSpec((D_GROUP, D_GROUP), lambda b, i: (0, 0)),
                          pl.BlockSpec((1, D_GROUP), lambda b, i: (0, 0))],
        out_specs=pl.BlockSpec((None, tm, D_GROUP), lambda b, i: (b, i, 0)),
        out_shape=jax.ShapeDtypeStruct((bsz, length, D_GROUP), F32),
        scratch_shapes=[pltpu.VMEM((tm + 2 * HALO, D_GROUP), F32)],
        compiler_params=_params(("parallel", "arbitrary")),
        name="pool_mix",
    )(u, u, u, wbd, pool_scale)


def _hy_short_kernel(prev_ref, cur_ref, next_ref, w_ref, b_ref, o_ref, buf, *, tm, nt):
    i = pl.program_id(1)
    _fill_halo(buf, prev_ref, cur_ref, next_ref, i, nt, tm)
    w = w_ref[...]
    y = (buf[HALO - 1:HALO - 1 + tm, :] * w[0:1] + buf[HALO:HALO + tm, :] * w[1:2]
         + buf[HALO + 1:HALO + 1 + tm, :] * w[2:3] + b_ref[...])
    o_ref[...] = y.T


def _hy_short(u, w_short, b_short, tm):
    bsz, length, _ = u.shape
    nt = length // tm
    per = tm // HALO
    last = length // HALO - 1
    c0 = 7
    in_specs = [pl.BlockSpec((None, HALO, HY_CH), lambda b, i, j: (b, jnp.maximum(i * per - 1, 0), c0 + j)),
                pl.BlockSpec((None, tm, HY_CH), lambda b, i, j: (b, i, c0 + j)),
                pl.BlockSpec((None, HALO, HY_CH), lambda b, i, j: (b, jnp.minimum((i + 1) * per, last), c0 + j)),
                pl.BlockSpec((3, HY_CH), lambda b, i, j: (0, j)),
                pl.BlockSpec((1, HY_CH), lambda b, i, j: (0, j))]
    return pl.pallas_call(
        functools.partial(_hy_short_kernel, tm=tm, nt=nt),
        grid=(bsz, nt, 3),
        in_specs=in_specs,
        out_specs=pl.BlockSpec((None, None, HY_CH, tm), lambda b, i, j: (j, b, 0, i)),
        out_shape=jax.ShapeDtypeStruct((3, bsz, HY_CH, length), F32),
        scratch_shapes=[pltpu.VMEM((tm + 2 * HALO, HY_CH), F32)],
        compiler_params=_params(("parallel", "arbitrary", "arbitrary")),
        name="hyena_short_conv",
    )(u, u, u, w_short, b_short)


HY_FEAT = 40


def _hy_filter_kernel(band_ref, w1_ref, b1_ref, w2_ref, b2_ref, w3_ref, b3_ref, dl_ref,
                      k_ref, ssq_ref, nrm_ref, *, tp, length):
    i = pl.program_id(0)
    n_i = pl.num_programs(0)
    m = i * tp + lax.broadcasted_iota(jnp.int32, (1, tp), 1)
    t = jnp.where(m <= length, m, 2 * length - m).astype(F32)
    t_norm = t / max(length - 1, 1)
    ang = ((2.0 * math.pi / length) * t) * band_ref[...]
    row = lax.broadcasted_iota(jnp.int32, (HY_FEAT, tp), 0)
    z = jnp.where(row == 0, t_norm,
                  jnp.where(row <= HY_BANDS, jnp.cos(ang), jnp.where(row < HY_EMB, jnp.sin(ang), 0.0)))
    z = jnp.concatenate([z, jnp.zeros((LANES - HY_FEAT, tp), F32)], axis=0)
    h = jnp.sin(HY_SIN_FREQ * (_dot(w1_ref[...], z, HI) + b1_ref[...]))
    h = jnp.sin(HY_SIN_FREQ * (_dot(w2_ref[...], h, HI) + b2_ref[...]))
    hh, hl = _split_bf16(h)
    h = _dot(w3_ref[...], jnp.concatenate([hh, hl, hh], axis=0)) + b3_ref[...]
    h = h * jnp.exp(-t_norm * dl_ref[...])

    @pl.when(i == 0)
    def _():
        ssq_ref[...] = jnp.zeros(ssq_ref.shape, F32)

    for o in range(2):
        fwd = h[o * 2 * HY_CH:o * 2 * HY_CH + HY_CH]
        bwd = h[o * 2 * HY_CH + HY_CH:(o + 1) * 2 * HY_CH]
        k = jnp.where(m < length, fwd, jnp.where(m == length, 0.0, bwd))
        k_ref[o] = k
        extra = jnp.where(m == 0, bwd * bwd, 0.0)
        ssq_ref[o] += jnp.sum(k * k + extra, axis=1, keepdims=True)

    @pl.when(i == n_i - 1)
    def _():
        nrm_ref[...] = lax.rsqrt(ssq_ref[...] + EPS)


def _hy_filters(length, w1, b1, w2, b2, w3, b3, tp):
    bands = jnp.linspace(1e-4, HY_BANDS - 1, HY_BANDS, dtype=F32)
    band_col = jnp.concatenate([jnp.zeros((1,), F32), bands, bands,
                                jnp.zeros((HY_FEAT - HY_EMB,), F32)])[:, None]
    deltas = jnp.abs(jnp.linspace(HY_MIN_DECAY, HY_MAX_DECAY, HY_CH, dtype=F32))
    dl_col = jnp.tile(deltas, 4)[:, None]
    w1t = jnp.pad(w1.astype(F32).T, ((0, 0), (0, LANES - HY_EMB)))
    full = lambda shape: pl.BlockSpec(shape, lambda i: (0,) * len(shape))
    n = 2 * length
    return pl.pallas_call(
        functools.partial(_hy_filter_kernel, tp=tp, length=length),
        grid=(n // tp,),
        in_specs=[full((HY_FEAT, 1)), full((HY_HIDDEN, LANES)), full((HY_HIDDEN, 1)),
                  full((HY_HIDDEN, HY_HIDDEN)), full((HY_HIDDEN, 1)),
                  full((4 * HY_CH, 3 * HY_HIDDEN)), full((4 * HY_CH, 1)), full((4 * HY_CH, 1))],
        out_specs=[pl.BlockSpec((2, HY_CH, tp), lambda i: (0, 0, i)),
                   full((2, HY_CH, 1)), full((2, HY_CH, 1))],
        out_shape=[jax.ShapeDtypeStruct((2, HY_CH, n), F32),
                   jax.ShapeDtypeStruct((2, HY_CH, 1), F32),
                   jax.ShapeDtypeStruct((2, HY_CH, 1), F32)],
        compiler_params=_params(("arbitrary",)),
        name="hyena_filters",
    )(band_col, w1t, b1.astype(F32)[:, None], w2.astype(F32).T, b2.astype(F32)[:, None],
      _cat3(w3.astype(F32).T, 1), b3.astype(F32)[:, None], dl_col)


def _dft_consts(n1, n2):
    n = n1 * n2
    a1 = 2.0 * np.pi * ((np.arange(n1)[:, None] * np.arange(n1)[None, :]) % n1) / n1
    c1, s1 = np.cos(a1), np.sin(a1)
    a2 = 2.0 * np.pi * ((np.arange(n2)[:, None] * np.arange(n2)[None, :]) % n2) / n2
    c2, s2 = np.cos(a2), np.sin(a2)
    at = 2.0 * np.pi * ((np.arange(n1)[:, None] * np.arange(n2)[None, :]) % n) / n
    f1_full = np.concatenate([c1, -s1], axis=0)
    f1_half = f1_full[:, :n1 // 2]
    g1 = np.concatenate([c1[:n1 // 2], -s1[:n1 // 2]], axis=1)
    w2f = np.block([[c2, -s2], [s2, c2]])
    w2i = np.block([[c2, s2], [-s2, c2]])
    f = lambda x: jnp.asarray(x, F32)
    return dict(f1_full=f(f1_full), f1_half=f(f1_half), g1=f(g1), w2f=f(w2f), w2i=f(w2i),
                tr=f(np.cos(at)), ti=f(-np.sin(at)))


def _fft_fwd(slabs, f1, tr, ti, w2f, stack, n1, n2, prec):
    for c, x in enumerate(slabs):
        a = _dot(f1, x.astype(stack.dtype), prec)
        ar, ai = a[:n1], a[n1:]
        stack[c * n1:(c + 1) * n1, 0:n2] = (ar * tr - ai * ti).astype(stack.dtype)
        stack[c * n1:(c + 1) * n1, n2:2 * n2] = (ar * ti + ai * tr).astype(stack.dtype)
    return _dot(stack[...], w2f, prec)


def _split_bf16(x):
    hi = x.astype(BF16)
    return hi, (x - hi.astype(F32)).astype(BF16)


def _cat3(x, axis):
    hi, lo = _split_bf16(x)
    return jnp.concatenate([hi, hi, lo], axis=axis)


def _hy_spec_kernel(nrm_ref, k_ref, f1_ref, tr_ref, ti_ref, w2f_ref, o_ref, stack, *, cg, n1, n2):
    o = pl.program_id(0)
    g = pl.program_id(1)
    tr = tr_ref[...]
    ti = ti_ref[...]
    for c in range(cg):
        hi, lo = _split_bf16(k_ref[c])
        a = _dot(f1_ref[...], jnp.concatenate([hi, lo, hi], axis=0))
        ar, ai = a[:n1], a[n1:]
        sh, sl = _split_bf16(jnp.concatenate([ar * tr - ai * ti, ar * ti + ai * tr], axis=1))
        stack[c * n1:(c + 1) * n1, :] = jnp.concatenate([sh, sl, sh], axis=1)
    x = _dot(stack[...], w2f_ref[...])
    for c in range(cg):
        sc = nrm_ref[o * HY_CH + g * cg + c] * (1.0 / (n1 * n2))
        xc = x[c * n1:(c + 1) * n1] * sc
        o_ref[c, 0] = xc[:, :n2]
        o_ref[c, 1] = xc[:, n2:]


def _hy_spec(k4, nrm_flat, dc, cg, n1, n2):
    full = lambda shape: pl.BlockSpec(shape, lambda o, g: (0,) * len(shape))
    return pl.pallas_call(
        functools.partial(_hy_spec_kernel, cg=cg, n1=n1, n2=n2),
        grid=(2, HY_CH // cg),
        in_specs=[pl.BlockSpec(memory_space=pltpu.SMEM),
                  pl.BlockSpec((None, cg, n1, n2), lambda o, g: (o, g, 0, 0)),
                  full((2 * n1, 3 * n1)), full((n1, n2)), full((n1, n2)), full((6 * n2, 2 * n2))],
        out_specs=pl.BlockSpec((None, cg, 2, n1, n2), lambda o, g: (o, g, 0, 0, 0)),
        out_shape=jax.ShapeDtypeStruct((2, HY_CH, 2, n1, n2), F32),
        scratch_shapes=[pltpu.VMEM((cg * n1, 6 * n2), BF16)],
        compiler_params=_params(("parallel", "arbitrary")),
        name="hyena_filter_spectrum",
    )(nrm_flat, k4, _cat3(dc["f1_full"], 1), dc["tr"], dc["ti"], _cat3(dc["w2f"], 0))


def _hy_conv_kernel(skip_ref, x_ref, ks_ref, f1_ref, g1_ref, tr_ref, ti_ref, w2f_ref, w2i_ref,
                    o_ref, stack, *, cg, n1, n2):
    g = pl.program_id(1)
    tr = tr_ref[...]
    ti = ti_ref[...]

    def conv(slabs, order):
        x = _fft_fwd(slabs, f1_ref[...], tr, ti, w2f_ref[...], stack, n1, n2, None)
        for c in range(cg):
            xr, xi = x[c * n1:(c + 1) * n1, :n2], x[c * n1:(c + 1) * n1, n2:]
            kr, ki = ks_ref[order, c, 0], ks_ref[order, c, 1]
            stack[c * n1:(c + 1) * n1, 0:n2] = (xr * kr - xi * ki).astype(BF16)
            stack[c * n1:(c + 1) * n1, n2:2 * n2] = (xr * ki + xi * kr).astype(BF16)
        bm = _dot(stack[...], w2i_ref[...])
        outs = []
        for c in range(cg):
            br, bi = bm[c * n1:(c + 1) * n1, :n2], bm[c * n1:(c + 1) * n1, n2:]
            b2 = jnp.concatenate([br * tr + bi * ti, bi * tr - br * ti], axis=0)
            y = _dot(g1_ref[...], b2.astype(BF16))
            outs.append(y + slabs[c] * skip_ref[order * HY_CH + g * cg + c])
        return outs

    v = [x_ref[2, c] for c in range(cg)]
    y0 = conv(v, 0)
    z = [x_ref[0, c] * y0[c] for c in range(cg)]
    y1 = conv(z, 1)
    for c in range(cg):
        o_ref[c] = x_ref[1, c] * y1[c]


def _hy_conv(x4, kspec, skip_flat, dc, cg, n1, n2):
    bsz = x4.shape[1]
    full = lambda shape: pl.BlockSpec(shape, lambda b, g: (0,) * len(shape))
    return pl.pallas_call(
        functools.partial(_hy_conv_kernel, cg=cg, n1=n1, n2=n2),
        grid=(bsz, HY_CH // cg),
        in_specs=[pl.BlockSpec(memory_space=pltpu.SMEM),
                  pl.BlockSpec((3, None, cg, n1 // 2, n2), lambda b, g: (0, b, g, 0, 0)),
                  pl.BlockSpec((2, cg, 2, n1, n2), lambda b, g: (0, g, 0, 0, 0)),
                  full((2 * n1, n1 // 2)), full((n1 // 2, 2 * n1)), full((n1, n2)), full((n1, n2)),
                  full((2 * n2, 2 * n2)), full((2 * n2, 2 * n2))],
        out_specs=pl.BlockSpec((None, cg, n1 // 2, n2), lambda b, g: (b, g, 0, 0)),
        out_shape=jax.ShapeDtypeStruct((bsz, HY_CH, n1 // 2, n2), F32),
        scratch_shapes=[pltpu.VMEM((cg * n1, 2 * n2), BF16)],
        compiler_params=_params(("parallel", "arbitrary")),
        name="hyena_long_conv",
    )(skip_flat, x4, kspec, dc["f1_half"].astype(BF16), dc["g1"].astype(BF16), dc["tr"], dc["ti"],
      dc["w2f"].astype(BF16), dc["w2i"].astype(BF16))


def _hy_ctx_kernel(x_ref, k_ref, nrm_ref, skip_ref, fc_ref, fs_ref, o_ref, *, c):
    fc = fc_ref[...]
    fs = fs_ref[...]
    inv_n = 1.0 / (2 * c)

    def conv(x, order):
        kk = k_ref[order]
        kr, ki = _dot(kk, fc, HI), -_dot(kk, fs, HI)
        xr, xi = _dot(x, fc[:c], HI), -_dot(x, fs[:c], HI)
        yr, yi = xr * kr - xi * ki, xr * ki + xi * kr
        y = (_dot(yr, fc[:, :c], HI) - _dot(yi, fs[:, :c], HI)) * inv_n
        return y * nrm_ref[order] + x * skip_ref[order]

    z = x_ref[0] * conv(x_ref[2], 0)
    o_ref[...] = x_ref[1] * conv(z, 1)


def _hy_ctx_conv(xt, kt, nrm, skip_col):
    _, bsz, ch, c = xt.shape
    n = 2 * c
    ang = 2.0 * np.pi * ((np.arange(n)[:, None] * np.arange(n)[None, :]) % n) / n
    fc, fs = jnp.asarray(np.cos(ang), F32), jnp.asarray(np.sin(ang), F32)
    full = lambda shape: pl.BlockSpec(shape, lambda b: (0,) * len(shape))
    return pl.pallas_call(
        functools.partial(_hy_ctx_kernel, c=c),
        grid=(bsz,),
        in_specs=[pl.BlockSpec((3, None, ch, c), lambda b: (0, b, 0, 0)),
                  full((2, ch, n)), full((2, ch, 1)), full((2, ch, 1)), full((n, n)), full((n, n))],
        out_specs=pl.BlockSpec((None, ch, c), lambda b: (b, 0, 0)),
        out_shape=jax.ShapeDtypeStruct((bsz, ch, c), F32),
        compiler_params=_params(("arbitrary",)),
        name="hyena_ctx_conv",
    )(xt, kt, nrm, skip_col, fc, fs)


def _out_proj_kernel(a_ref, b_ref, p_ref, ht_ref, x_ref, g_ref, w_ref, o_ref):
    w = D_GROUP
    acc = _dot(a_ref[...].astype(BF16), w_ref[0:w])
    acc += _dot(b_ref[...].astype(BF16), w_ref[w:2 * w])
    acc += _dot(p_ref[...].astype(BF16), w_ref[2 * w:3 * w])
    acc += _dot(ht_ref[...].T.astype(BF16), w_ref[3 * w:4 * w])
    o_ref[...] = x_ref[...] + g_ref[...] * acc


def _out_proj(a, b, p, ht, x, mod3, row_of_batch, j_gate, w_out, tm):
    bsz, t, d = x.shape
    w = D_GROUP
    tok = pl.BlockSpec((None, tm, w), lambda bb, i: (bb, i, 0))
    return pl.pallas_call(
        _out_proj_kernel,
        grid=(bsz, t // tm),
        in_specs=[tok, tok, tok,
                  pl.BlockSpec((None, w, tm), lambda bb, i: (bb, 0, i)),
                  pl.BlockSpec((None, tm, d), lambda bb, i: (bb, i, 0)),
                  pl.BlockSpec((None, 1, d), lambda bb, i: (row_of_batch(bb), 0, j_gate)),
                  pl.BlockSpec((4 * w, d), lambda bb, i: (0, 0))],
        out_specs=pl.BlockSpec((None, tm, d), lambda bb, i: (bb, i, 0)),
        out_shape=jax.ShapeDtypeStruct((bsz, t, d), F32),
        compiler_params=_params(("parallel", "arbitrary")),
        name="out_proj_residual",
    )(a, b, p, ht, x, mod3, w_out)


MOE_ROWS = 256


def _route(x, g, sc, sh, rw3, rbc):
    per_group = N_EXPERTS // N_EXPERT_GROUPS
    tm = x.shape[0]
    ms = jnp.mean(x * x, axis=-1, keepdims=True)
    h = (x * lax.rsqrt(ms + EPS)) * g * (1.0 + sc) + sh
    hh, hl = _split_bf16(h)
    d = x.shape[1]
    logits = (_dot(hh, rw3[0:d]) + _dot(hl, rw3[d:2 * d]) + _dot(hh, rw3[2 * d:3 * d])).T[:N_EXPERTS]
    ex = jnp.exp(logits - jnp.max(logits, axis=0, keepdims=True))
    scores = ex / jnp.sum(ex, axis=0, keepdims=True)
    sel = scores + rbc
    srow = [sel[r:r + 1] for r in range(N_EXPERTS)]
    best = None
    for grp in range(N_EXPERT_GROUPS):
        rows = list(range(grp * per_group, (grp + 1) * per_group))
        v1 = functools.reduce(jnp.maximum, [srow[r] for r in rows])
        i1 = jnp.full((1, tm), rows[-1], jnp.int32)
        for r in reversed(rows[:-1]):
            i1 = jnp.where(srow[r] == v1, r, i1)
        rest = [jnp.where(i1 == r, -jnp.inf, srow[r]) for r in rows]
        v2 = functools.reduce(jnp.maximum, rest)
        i2 = jnp.full((1, tm), rows[-1], jnp.int32)
        for k in reversed(range(per_group - 1)):
            i2 = jnp.where(rest[k] == v2, rows[k], i2)
        gs = v1 + v2
        if best is None:
            best, e1, e2 = gs, i1, i2
        else:
            upd = gs > best
            best = jnp.where(upd, gs, best)
            e1 = jnp.where(upd, i1, e1)
            e2 = jnp.where(upd, i2, e2)
    row = lax.broadcasted_iota(jnp.int32, (N_EXPERTS, tm), 0)
    w1 = jnp.sum(jnp.where(row == e1, scores, 0.0), axis=0, keepdims=True)
    w2 = jnp.sum(jnp.where(row == e2, scores, 0.0), axis=0, keepdims=True)
    tot = w1 + w2
    gates_t = jnp.where(row == e1, w1 / tot, 0.0) + jnp.where(row == e2, w2 / tot, 0.0)
    gates_t = jnp.concatenate([gates_t, jnp.zeros((LANES - N_EXPERTS, tm), F32)], axis=0)
    return h, gates_t.T


def _moe_kernel(x_ref, g_ref, sc_ref, sh_ref, gate_ref, rw_ref, rb_ref, w1_ref, w3_ref, w2_ref, fg_ref,
                o_ref, h_scr, gates_scr, acc_scr, *, final):
    e = pl.program_id(2)
    tm = x_ref.shape[0]

    @pl.when(e == 0)
    def _():
        h, gates = _route(x_ref[...], g_ref[...], sc_ref[...], sh_ref[...], rw_ref[...], rb_ref[...])
        h_scr[...] = h.astype(BF16)
        gates_scr[...] = gates
        acc_scr[...] = jnp.zeros(acc_scr.shape, F32)

    lane = lax.broadcasted_iota(jnp.int32, (tm, LANES), 1)
    ge = jnp.sum(jnp.where(lane == e, gates_scr[...], 0.0), axis=1, keepdims=True)

    def up(j):
        hb = h_scr[j * MOE_ROWS:(j + 1) * MOE_ROWS, :]
        return _dot(hb, w1_ref[...]), _dot(hb, w3_ref[...])

    nxt = up(0)
    for j in range(tm // MOE_ROWS):
        a, b = nxt
        if (j + 1) * MOE_ROWS < tm:
            nxt = up(j + 1)
        rows = slice(j * MOE_ROWS, (j + 1) * MOE_ROWS)
        act = (a * jax.nn.sigmoid(a)) * b
        acc_scr[rows, :] += ge[rows] * _dot(act.astype(BF16), w2_ref[...])

    @pl.when(e == N_EXPERTS - 1)
    def _():
        y = x_ref[...] + gate_ref[...] * acc_scr[...]
        if final:
            ms = jnp.mean(y * y, axis=-1, keepdims=True)
            y = (y * lax.rsqrt(ms + EPS)) * fg_ref[...]
        o_ref[...] = y


def _moe(x, g, mod3, row_of_batch, j_shift, j_scale, j_gate, rw, rb, w1, w3, w2, final_g, final, tm):
    bsz, t, d = x.shape
    vec = lambda j: pl.BlockSpec((None, 1, d), lambda b, i, e: (row_of_batch(b), 0, j))
    full = lambda shape: pl.BlockSpec(shape, lambda b, i, e: (0,) * len(shape))
    return pl.pallas_call(
        functools.partial(_moe_kernel, final=final),
        grid=(bsz, t // tm, N_EXPERTS),
        in_specs=[pl.BlockSpec((None, tm, d), lambda b, i, e: (b, i, 0)),
                  full((1, d)), vec(j_scale), vec(j_shift), vec(j_gate),
                  full((3 * d, LANES)), full((N_EXPERTS, 1)),
                  pl.BlockSpec((None, d, D_EXPERT), lambda b, i, e: (e, 0, 0)),
                  pl.BlockSpec((None, d, D_EXPERT), lambda b, i, e: (e, 0, 0)),
                  pl.BlockSpec((None, D_EXPERT, d), lambda b, i, e: (e, 0, 0)),
                  full((1, d))],
        out_specs=pl.BlockSpec((None, tm, d), lambda b, i, e: (b, i, 0)),
        out_shape=jax.ShapeDtypeStruct((bsz, t, d), F32),
        scratch_shapes=[pltpu.VMEM((tm, d), BF16), pltpu.VMEM((tm, LANES), F32), pltpu.VMEM((tm, d), F32)],
        compiler_params=_params(("parallel", "parallel", "arbitrary")),
        name="moe_final" if final else "moe",
    )(x, g, mod3, mod3, mod3, rw, rb, w1, w3, w2, final_g)


MOE_SUB = 512
MOE_CAP = 96
MOE_BLK = 2048
MOE_PAIR = 2


def _moe_route_kernel(x_ref, g_ref, sc_ref, sh_ref, rw_ref, rb_ref, h_ref, gates_ref, cnt_ref):
    h, gates = _route(x_ref[...], g_ref[...], sc_ref[...], sh_ref[...], rw_ref[...], rb_ref[...])
    h_ref[...] = h.astype(BF16)
    gates_ref[...] = gates
    cnt = jnp.sum(jnp.where(gates > 0.0, 1.0, 0.0), axis=0, keepdims=True)
    cnt_ref[...] = jnp.broadcast_to(cnt, cnt_ref.shape)


def _moe_route(x, g, mod3, row_of_batch, j_shift, j_scale, rw, rb):
    bsz, t, d = x.shape
    tm = MOE_SUB
    vec = lambda j: pl.BlockSpec((None, 1, d), lambda b, i: (row_of_batch(b), 0, j))
    full = lambda shape: pl.BlockSpec(shape, lambda b, i: (0,) * len(shape))
    return pl.pallas_call(
        _moe_route_kernel,
        grid=(bsz, t // tm),
        in_specs=[pl.BlockSpec((None, tm, d), lambda b, i: (b, i, 0)),
                  full((1, d)), vec(j_scale), vec(j_shift), full((3 * d, LANES)), full((N_EXPERTS, 1))],
        out_specs=[pl.BlockSpec((None, tm, d), lambda b, i: (b, i, 0)),
                   pl.BlockSpec((None, tm, LANES), lambda b, i: (b, i, 0)),
                   pl.BlockSpec((None, None, SUBLANES, LANES), lambda b, i: (b, i, 0, 0))],
        out_shape=[jax.ShapeDtypeStruct((bsz, t, d), BF16),
                   jax.ShapeDtypeStruct((bsz, t, LANES), F32),
                   jax.ShapeDtypeStruct((bsz, t // tm, SUBLANES, LANES), F32)],
        compiler_params=_params(("parallel", "arbitrary")),
        name="moe_route",
    )(x, g, mod3, mod3, rw, rb)


def _moe_routed_kernel(np_ref, h_ref, gates_ref, x_ref, gate_ref, w1_ref, w3_ref, w2_ref, fg_ref,
                       o_ref, rank_scr, rankt_scr, gatet_scr, xc_scr, *, final, nblk):
    b = pl.program_id(0)
    i = pl.program_id(1)
    e = pl.program_id(2)
    nsub = h_ref.shape[0] // MOE_SUB
    sub = lambda j: slice(j * MOE_SUB, (j + 1) * MOE_SUB)

    @pl.when(e == 0)
    def _():
        o_ref[...] = jnp.zeros(o_ref.shape, F32)
        r = lax.broadcasted_iota(jnp.int32, (MOE_SUB, MOE_SUB), 0)
        c = lax.broadcasted_iota(jnp.int32, (MOE_SUB, MOE_SUB), 1)
        ltri = jnp.where(c <= r, 1.0, 0.0).astype(BF16)
        utri = jnp.where(r <= c, 1.0, 0.0).astype(BF16)
        for j in range(nsub):
            gt = gates_ref[sub(j), :]
            rank_scr[j] = _dot(ltri, jnp.where(gt > 0.0, 1.0, 0.0).astype(BF16))
            gtt = gt.T[:N_EXPERTS]
            gatet_scr[j] = gtt
            rankt_scr[j] = _dot(jnp.where(gtt > 0.0, 1.0, 0.0).astype(BF16), utri)

    lane = lax.broadcasted_iota(jnp.int32, (MOE_SUB, LANES), 1)
    slot_lane = lax.broadcasted_iota(jnp.int32, (MOE_SUB, MOE_PAIR * MOE_CAP), 1)
    second = slot_lane >= MOE_CAP

    def one_pass(p, carry):
        base = (p * MOE_CAP + 1).astype(F32)
        slot_r = lax.broadcasted_iota(jnp.int32, (MOE_CAP, MOE_SUB), 0).astype(F32) + base
        slot_c = jnp.where(second, slot_lane - MOE_CAP, slot_lane).astype(F32) + base
        ys = []
        for k in range(MOE_PAIR):
            ex = e * MOE_PAIR + k
            gcs = []
            for j in range(nsub):
                rr = rankt_scr[j, pl.ds(ex, 1), :]
                gr = gatet_scr[j, pl.ds(ex, 1), :]
                pick = (rr == slot_r) & (gr > 0.0)
                xc_scr[k, j * MOE_CAP:(j + 1) * MOE_CAP, :] = _dot(
                    jnp.where(pick, 1.0, 0.0).astype(BF16), h_ref[sub(j), :]).astype(BF16)
                gcs.append(jnp.sum(jnp.where(pick, gr, 0.0), axis=1, keepdims=True))
            xc = xc_scr[k]
            a = _dot(xc, w1_ref[k])
            bb = _dot(xc, w3_ref[k])
            y = _dot(((a * jax.nn.sigmoid(a)) * bb).astype(BF16), w2_ref[k])
            ys.append([(y[j * MOE_CAP:(j + 1) * MOE_CAP] * gcs[j]).astype(BF16) for j in range(nsub)])
        for j in range(nsub):
            cols = []
            for k in range(MOE_PAIR):
                ex = e * MOE_PAIR + k
                rc = jnp.sum(jnp.where(lane == ex, rank_scr[j], 0.0), axis=1, keepdims=True)
                gc = jnp.sum(jnp.where(lane == ex, gates_ref[sub(j), :], 0.0), axis=1, keepdims=True)
                cols.append(jnp.where(gc > 0.0, rc, 0.0))
            put = jnp.where(jnp.where(second, cols[1], cols[0]) == slot_c, 1.0, 0.0).astype(BF16)
            o_ref[sub(j), :] += _dot(put, jnp.concatenate([ys[0][j], ys[1][j]], axis=0))
        return carry

    lax.fori_loop(0, np_ref[(b * nblk + i) * (N_EXPERTS // MOE_PAIR) + e], one_pass, 0)

    @pl.when(e == N_EXPERTS // MOE_PAIR - 1)
    def _():
        y = x_ref[...] + gate_ref[...] * o_ref[...]
        if final:
            ms = jnp.mean(y * y, axis=-1, keepdims=True)
            y = (y * lax.rsqrt(ms + EPS)) * fg_ref[...]
        o_ref[...] = y


def _moe_routed(x, g, mod3, row_of_batch, j_shift, j_scale, j_gate, rw, rb, w1, w3, w2, final_g, final):
    bsz, t, d = x.shape
    h, gates, cnt = _moe_route(x, g, mod3, row_of_batch, j_shift, j_scale, rw, rb)
    tb = _tile(t, MOE_BLK)
    nblk = t // tb
    nsub = tb // MOE_SUB
    npair = N_EXPERTS // MOE_PAIR
    passes = jnp.ceil(cnt[:, :, 0, :N_EXPERTS] / MOE_CAP).astype(jnp.int32)
    passes = jnp.max(passes.reshape(bsz, nblk, nsub, npair, MOE_PAIR), axis=(2, 4)).reshape(-1)
    full = lambda shape: pl.BlockSpec(shape, lambda b, i, e, np_: (0,) * len(shape))
    blk = lambda w: pl.BlockSpec((None, tb, w), lambda b, i, e, np_: (b, i, 0), pipeline_mode=pl.Buffered(1))
    grid_spec = pltpu.PrefetchScalarGridSpec(
        num_scalar_prefetch=1,
        grid=(bsz, nblk, npair),
        in_specs=[blk(d), blk(LANES), blk(d),
                  pl.BlockSpec((None, 1, d), lambda b, i, e, np_: (row_of_batch(b), 0, j_gate)),
                  pl.BlockSpec((MOE_PAIR, d, D_EXPERT), lambda b, i, e, np_: (e, 0, 0)),
                  pl.BlockSpec((MOE_PAIR, d, D_EXPERT), lambda b, i, e, np_: (e, 0, 0)),
                  pl.BlockSpec((MOE_PAIR, D_EXPERT, d), lambda b, i, e, np_: (e, 0, 0)),
                  full((1, d))],
        out_specs=pl.BlockSpec((None, tb, d), lambda b, i, e, np_: (b, i, 0)),
        scratch_shapes=[pltpu.VMEM((nsub, MOE_SUB, LANES), F32),
                        pltpu.VMEM((nsub, N_EXPERTS, MOE_SUB), F32),
                        pltpu.VMEM((nsub, N_EXPERTS, MOE_SUB), F32),
                        pltpu.VMEM((MOE_PAIR, nsub * MOE_CAP, d), BF16)])
    return pl.pallas_call(
        functools.partial(_moe_routed_kernel, final=final, nblk=nblk),
        grid_spec=grid_spec,
        out_shape=jax.ShapeDtypeStruct((bsz, t, d), F32),
        compiler_params=_params(("parallel", "parallel", "arbitrary")),
        name="moe_routed_final" if final else "moe_routed",
    )(passes, h, gates, x, mod3, w1, w3, w2, final_g)


def _tile(n, pref):
    t = min(n, pref)
    assert n % t == 0
    return t


def _fft_split(n):
    n2 = LANES
    assert n % n2 == 0
    return n // n2, n2


def _hyena_latent(u, lp, fargs):
    bsz, s, _ = u.shape
    n1, n2 = _fft_split(2 * s)
    dc = _dft_consts(n1, n2)
    cg = 8
    kt_l, _, nrm_l = _hy_filters(s, *fargs, tp=_tile(2 * s, 1024))
    kspec = _hy_spec(kt_l.reshape(2, HY_CH, n1, n2), nrm_l.reshape(2 * HY_CH), dc, cg, n1, n2)
    xt = _hy_short(u, lp["hy_short_w"].astype(F32), lp["hy_short_b"].astype(F32)[None, :], _tile(s, 512))
    h_l = _hy_conv(xt.reshape(3, bsz, HY_CH, n1 // 2, n2), kspec, lp["hy_skip"].astype(F32).reshape(2 * HY_CH),
                   dc, cg, n1, n2)
    return h_l.reshape(bsz, HY_CH, s)


def _mixers(u, uc, lp, li, need_ctx):
    bsz, s, _ = u.shape
    c = uc.shape[1]
    lam_init = 0.8 - 0.6 * math.exp(-0.3 * li)
    lam_vecs = lp["a_lambda"].astype(F32)
    subln = lp["a_subln_g"].astype(F32)[None, :]

    qc, kct, vc = _attn_prep(uc, _tile(c, 256), rope=False)
    ql, klt, vl = _attn_prep(u, _tile(s, 256), rope=True)
    a_l = _diff_attn(ql, kct, vc, klt, vl, lam_vecs, subln, lam_init, _tile(s, 512), _tile(s, 2048))
    bias8 = _nbr_bias(lp["b_rpb"])
    b_l = _nbr_attn(u, uc, bias8)
    wbd = jax.scipy.linalg.block_diag(*[lp["pool_w"][g] for g in range(len(POOL_SIZES))]).astype(BF16)
    pscale = lp["pool_scale"].astype(F32)[None, :]
    p_l = _pool_mix(u, wbd, pscale, _tile(s, 512))
    fargs = (lp["hy_f_w1"], lp["hy_f_b1"], lp["hy_f_w2"], lp["hy_f_b2"], lp["hy_f_w3"], lp["hy_f_b3"])
    skip = lp["hy_skip"].astype(F32)
    w_short = lp["hy_short_w"].astype(F32)
    b_short = lp["hy_short_b"].astype(F32)[None, :]
    h_l = _hyena_latent(u, lp, fargs)
    lat = (a_l, b_l, p_l, h_l)
    if not need_ctx:
        return lat, None
    a_c = _diff_attn(qc, kct, vc, None, None, lam_vecs, subln, lam_init, _tile(c, 256), None)
    b_c = _nbr_ctx_attn(uc)
    p_c = _pool_mix(uc, wbd, pscale, _tile(c, 256))
    kt_c, _, nrm_c = _hy_filters(c, *fargs, tp=_tile(2 * c, 512))
    xtc = _hy_short(uc, w_short, b_short, _tile(c, 256))
    h_c = _hy_ctx_conv(xtc, kt_c, nrm_c, skip[:, :, None])
    return lat, (a_c, b_c, p_c, h_c)


def kernel(x, c, ctx, c_ctx, norm1_g, norm2_g, ada_w, ada_b, w_in, w_out, a_lambda, a_subln_g, b_rpb, pool_w, pool_scale, hy_short_w, hy_short_b, hy_f_w1, hy_f_b1, hy_f_w2, hy_f_b2, hy_f_w3, hy_f_b3, hy_skip, router_w, router_b, moe_w1, moe_w3, moe_w2, final_g):
    depth = norm1_g.shape[0]
    bsz, s, d = x.shape
    cl = ctx.shape[1]
    assert bsz <= SUBLANES - 1
    xl, xc = x, ctx
    cpad = jnp.zeros((SUBLANES, d), F32).at[:bsz].set(c.astype(F32)).at[bsz].set(c_ctx.astype(F32))
    rw = _cat3(jnp.pad(router_w.astype(F32), ((0, 0), (0, LANES - N_EXPERTS))), 0)
    rb = router_b.astype(F32)[:, None]
    lat_row = lambda b: b
    ctx_row = lambda b: bsz
    fg = final_g.astype(F32)[None, :]
    tm = _tile(s, 512)
    tmc = _tile(cl, 256)
    for li in range(depth):
        need_ctx = li < depth - 1
        lp = dict(a_lambda=a_lambda[li], a_subln_g=a_subln_g[li], b_rpb=b_rpb[li], pool_w=pool_w[li],
                  pool_scale=pool_scale[li], hy_short_w=hy_short_w[li], hy_short_b=hy_short_b[li],
                  hy_f_w1=hy_f_w1[li], hy_f_b1=hy_f_b1[li], hy_f_w2=hy_f_w2[li], hy_f_b2=hy_f_b2[li],
                  hy_f_w3=hy_f_w3[li], hy_f_b3=hy_f_b3[li], hy_skip=hy_skip[li])
        mod3 = _ada(cpad, ada_w[li].astype(F32), ada_b[li].astype(F32)[None, :]).reshape(SUBLANES, 1, 6 * d)
        n1g = norm1_g[li].astype(F32)[None, :]
        n2g = norm2_g[li].astype(F32)[None, :]
        w_in_b = w_in[li].astype(BF16)
        w_out_b = w_out[li].astype(BF16)
        u = _norm_proj(xl, n1g, mod3, lat_row, 0, 1, w_in_b, tm)
        uc = _norm_proj(xc, n1g, mod3, ctx_row, 0, 1, w_in_b, tmc)
        lat, cx = _mixers(u, uc, lp, li, need_ctx)
        xl = _out_proj(*lat, xl, mod3, lat_row, 2, w_out_b, tm)
        w1b, w3b, w2b = moe_w1[li].astype(BF16), moe_w3[li].astype(BF16), moe_w2[li].astype(BF16)
        if need_ctx:
            xc = _out_proj(*cx, xc, mod3, ctx_row, 2, w_out_b, tmc)
            xc = _moe(xc, n2g, mod3, ctx_row, 3, 4, 5, rw, rb, w1b, w3b, w2b, fg, False, tmc)
        xl = _moe_routed(xl, n2g, mod3, lat_row, 3, 4, 5, rw, rb, w1b, w3b, w2b, fg, li == depth - 1)
    return xl
```

```python
import functools
import math

import numpy as np
import jax
import jax.numpy as jnp
from jax import lax
from jax.experimental import pallas as pl
from jax.experimental.pallas import tpu as pltpu

F32 = jnp.float32
BF16 = jnp.bfloat16
HI = lax.Precision.HIGHEST

GRID_W = 64
A_HEADS = 4
A_QK = 32
A_V = 64
ROPE_BASE = 10000.0
B_HEADS = 4
B_DIM = 64
WIN_R = 8
WIN_C = 16
POOL_SIZES = (2, 4, 8, 16)
POOL_CH = 64
D_GROUP = 256
HY_CH = 256
HY_BANDS = 16
HY_EMB = 1 + 2 * HY_BANDS
HY_HIDDEN = 64
HY_SIN_FREQ = 1.0
HY_MIN_DECAY = math.log(1e-2) / 1.5
HY_MAX_DECAY = math.log(1e-2) / 0.3
N_EXPERTS = 16
N_EXPERT_GROUPS = 4
D_EXPERT = 512
EPS = 1e-6
LOG2E = 1.4426950408889634

LANES = 128
SUBLANES = 8
VMEM_LIMIT = 56 * 1024 * 1024


def _params(sem):
    return pltpu.CompilerParams(dimension_semantics=sem, vmem_limit_bytes=VMEM_LIMIT)


def _dot(a, b, prec=None):
    return jnp.dot(a, b, precision=prec, preferred_element_type=F32)


def _dot_nt(a, b):
    return lax.dot_general(a, b, (((1,), (1,)), ((), ())), preferred_element_type=F32)


def _ada_kernel(c_ref, w_ref, b_ref, o_ref):
    cf = c_ref[...]
    s = cf * jax.nn.sigmoid(cf)
    o_ref[...] = _dot(s, w_ref[...], HI) + b_ref[...]


def _ada(cpad, w, b):
    d = cpad.shape[1]
    n = w.shape[1]
    return pl.pallas_call(
        _ada_kernel,
        grid=(n // d,),
        in_specs=[pl.BlockSpec((SUBLANES, d), lambda j: (0, 0)),
                  pl.BlockSpec((d, d), lambda j: (0, j)),
                  pl.BlockSpec((1, d), lambda j: (0, j))],
        out_specs=pl.BlockSpec((SUBLANES, d), lambda j: (0, j)),
        out_shape=jax.ShapeDtypeStruct((SUBLANES, n), F32),
        compiler_params=_params(("arbitrary",)),
        name="ada_mod",
    )(cpad, w, b)


def _norm_proj_kernel(x_ref, g_ref, sc_ref, sh_ref, w_ref, o_ref):
    x = x_ref[...]
    ms = jnp.mean(x * x, axis=-1, keepdims=True)
    h = (x * lax.rsqrt(ms + EPS)) * g_ref[...] * (1.0 + sc_ref[...]) + sh_ref[...]
    o_ref[...] = _dot(h.astype(BF16), w_ref[...])


def _norm_proj(x, g, mod3, row_of_batch, j_shift, j_scale, w, tm):
    bsz, t, d = x.shape
    n = w.shape[1]
    return pl.pallas_call(
        _norm_proj_kernel,
        grid=(bsz, t // tm),
        in_specs=[pl.BlockSpec((None, tm, d), lambda b, i: (b, i, 0)),
                  pl.BlockSpec((1, d), lambda b, i: (0, 0)),
                  pl.BlockSpec((None, 1, d), lambda b, i: (row_of_batch(b), 0, j_scale)),
                  pl.BlockSpec((None, 1, d), lambda b, i: (row_of_batch(b), 0, j_shift)),
                  pl.BlockSpec((d, n), lambda b, i: (0, 0))],
        out_specs=pl.BlockSpec((None, tm, n), lambda b, i: (b, i, 0)),
        out_shape=jax.ShapeDtypeStruct((bsz, t, n), F32),
        compiler_params=_params(("parallel", "arbitrary")),
        name="norm_in_proj",
    )(x, g, mod3, mod3, w)


def _aprep_kernel(*refs, rope):
    if rope:
        u_ref, cos_ref, sin_ref, q_ref, kt_ref, v_ref = refs
    else:
        u_ref, q_ref, kt_ref, v_ref = refs
    u = u_ref[...]
    q = u[:, 0:256]
    k = u[:, 256:512]
    v = u[:, 512:768]
    if rope:
        cos_t = cos_ref[...]
        sin_t = sin_ref[...]
        lane = lax.broadcasted_iota(jnp.int32, cos_t.shape, 1)
        first = (lane % (2 * 16)) < 16

        def rot(x):
            halves = []
            for j in range(2):
                xh = x[:, j * LANES:(j + 1) * LANES]
                swap = jnp.where(first, pltpu.roll(xh, LANES - 16, axis=1), pltpu.roll(xh, 16, axis=1))
                halves.append(xh * cos_t + swap * sin_t)
            return jnp.concatenate(halves, axis=1)

        q = rot(q)
        k = rot(k)
    q = q * (A_QK ** -0.5 * LOG2E)
    kt = k.T
    for hc in range(2 * A_HEADS):
        q_ref[hc] = q[:, hc * A_QK:(hc + 1) * A_QK].astype(BF16)
        kt_ref[hc] = kt[hc * A_QK:(hc + 1) * A_QK, :].astype(BF16)
    lane = lax.broadcasted_iota(jnp.int32, (v.shape[0], LANES - A_V), 1)
    ones_col = jnp.where(lane == 0, 1.0, 0.0)
    for h in range(A_HEADS):
        v_ref[h] = jnp.concatenate([v[:, h * A_V:(h + 1) * A_V], ones_col], axis=1).astype(BF16)


def _rope_tables(length):
    n_freq = A_QK // 4
    inv = ROPE_BASE ** (-jnp.arange(n_freq, dtype=F32) / n_freq)
    t = jnp.arange(length)
    row = (t // GRID_W).astype(F32)
    col = (t % GRID_W).astype(F32)
    ang = jnp.concatenate([row[:, None] * inv, col[:, None] * inv], axis=-1)
    cos, sin = jnp.cos(ang), jnp.sin(ang)
    cos_t = jnp.tile(jnp.concatenate([cos, cos], axis=-1), (1, LANES // 32))
    sin_t = jnp.tile(jnp.concatenate([-sin, sin], axis=-1), (1, LANES // 32))
    return cos_t, sin_t


def _attn_prep(u, tm, rope):
    bsz, t, _ = u.shape
    nh = 2 * A_HEADS
    in_specs = [pl.BlockSpec((None, tm, 768), lambda b, i: (b, i, 0))]
    args = [u]
    if rope:
        cos_t, sin_t = _rope_tables(t)
        in_specs += [pl.BlockSpec((tm, LANES), lambda b, i: (i, 0))] * 2
        args += [cos_t, sin_t]
    return pl.pallas_call(
        functools.partial(_aprep_kernel, rope=rope),
        grid=(bsz, t // tm),
        in_specs=in_specs,
        out_specs=[pl.BlockSpec((None, nh, tm, A_QK), lambda b, i: (b, 0, i, 0)),
                   pl.BlockSpec((None, nh, A_QK, tm), lambda b, i: (b, 0, 0, i)),
                   pl.BlockSpec((None, A_HEADS, tm, LANES), lambda b, i: (b, 0, i, 0))],
        out_shape=[jax.ShapeDtypeStruct((bsz, nh, t, A_QK), BF16),
                   jax.ShapeDtypeStruct((bsz, nh, A_QK, t), BF16),
                   jax.ShapeDtypeStruct((bsz, A_HEADS, t, LANES), BF16)],
        compiler_params=_params(("parallel", "arbitrary")),
        name="attn_prep_rope" if rope else "attn_prep_ctx",
    )(*args)


QK_LOOKAHEAD = 3


def _dattn_kernel(*refs, lam_init, has_lat):
    if has_lat:
        lam_ref, g_ref, q_ref, kc_ref, vc_ref, k_ref, v_ref, o_ref, m_scr, acc_scr = refs
    else:
        lam_ref, g_ref, q_ref, kc_ref, vc_ref, o_ref, m_scr, acc_scr = refs
    ki = pl.program_id(2)
    nk = pl.num_programs(2)
    nh = 2 * A_HEADS

    def update(kt_r, v_r):
        scores = [_dot(q_ref[j], kt_r[j]) for j in range(QK_LOOKAHEAD)]
        for hc in range(nh):
            s = scores[hc]
            if hc + QK_LOOKAHEAD < nh:
                scores.append(_dot(q_ref[hc + QK_LOOKAHEAD], kt_r[hc + QK_LOOKAHEAD]))
            m_prev = m_scr[hc]
            m_new = jnp.maximum(m_prev, jnp.max(s, axis=1, keepdims=True))
            alpha = jnp.exp2(m_prev - m_new)
            p = jnp.exp2((s - m_new[:, :1]).astype(BF16))
            acc_scr[hc] = alpha * acc_scr[hc] + _dot(p, v_r[hc // 2])
            m_scr[hc] = m_new

    @pl.when(ki == 0)
    def _():
        m_scr[...] = jnp.full(m_scr.shape, -jnp.inf, F32)
        acc_scr[...] = jnp.zeros(acc_scr.shape, F32)
        update(kc_ref, vc_ref)

    if has_lat:
        @pl.when(ki > 0)
        def _():
            update(k_ref, v_ref)

    @pl.when(ki == nk - 1)
    def _():
        lv = lam_ref[...]
        lam = (jnp.exp(jnp.sum(lv[0:1] * lv[1:2], axis=1, keepdims=True))
               - jnp.exp(jnp.sum(lv[2:3] * lv[3:4], axis=1, keepdims=True)) + lam_init)
        for h in range(A_HEADS):
            a0 = acc_scr[2 * h]
            a1 = acc_scr[2 * h + 1]
            o = a0[:, :A_V] / a0[:, A_V:A_V + 1] - lam * (a1[:, :A_V] / a1[:, A_V:A_V + 1])
            ms = jnp.mean(o * o, axis=-1, keepdims=True)
            o_ref[:, h * A_V:(h + 1) * A_V] = (o * lax.rsqrt(ms + EPS)) * g_ref[...] * (1.0 - lam_init)


def _diff_attn(q, kct, vc, kt, v, lam_vecs, subln_g, lam_init, tq, tk):
    bsz, nh, t, _ = q.shape
    c = kct.shape[-1]
    has_lat = kt is not None
    nk = 1 + (kt.shape[-1] // tk if has_lat else 0)
    in_specs = [pl.BlockSpec((4, A_QK), lambda b, i, k: (0, 0)),
                pl.BlockSpec((1, A_V), lambda b, i, k: (0, 0)),
                pl.BlockSpec((None, nh, tq, A_QK), lambda b, i, k: (b, 0, i, 0)),
                pl.BlockSpec((None, nh, A_QK, c), lambda b, i, k: (b, 0, 0, 0)),
                pl.BlockSpec((None, A_HEADS, c, LANES), lambda b, i, k: (b, 0, 0, 0))]
    args = [lam_vecs, subln_g, q, kct, vc]
    if has_lat:
        in_specs += [pl.BlockSpec((None, nh, A_QK, tk), lambda b, i, k: (b, 0, 0, jnp.maximum(k - 1, 0))),
                     pl.BlockSpec((None, A_HEADS, tk, LANES), lambda b, i, k: (b, 0, jnp.maximum(k - 1, 0), 0))]
        args += [kt, v]
    return pl.pallas_call(
        functools.partial(_dattn_kernel, lam_init=lam_init, has_lat=has_lat),
        grid=(bsz, t // tq, nk),
        in_specs=in_specs,
        out_specs=pl.BlockSpec((None, tq, A_HEADS * A_V), lambda b, i, k: (b, i, 0)),
        out_shape=jax.ShapeDtypeStruct((bsz, t, A_HEADS * A_V), F32),
        scratch_shapes=[pltpu.VMEM((nh, tq, LANES), F32),
                        pltpu.VMEM((nh, tq, LANES), F32)],
        compiler_params=_params(("parallel", "parallel", "arbitrary")),
        name="diff_attn" if has_lat else "diff_attn_ctx",
    )(*args)


NB_ROWS = 8


def _nbr_bias(rpb):
    cols = jnp.arange(GRID_W)
    c0 = jnp.clip(cols - WIN_C // 2, 0, GRID_W - WIN_C)
    in_win = (cols[None, :] >= c0[:, None]) & (cols[None, :] < c0[:, None] + WIN_C)
    dc = jnp.clip(cols[None, :] - cols[:, None], -(WIN_C - 1), WIN_C - 1) + (WIN_C - 1)
    onehot = (dc[None] == jnp.arange(2 * WIN_C - 1)[:, None, None]).astype(F32)
    g = jnp.einsum("hab,bqk->haqk", rpb.astype(F32), onehot, precision=HI)
    g = jnp.where(in_win[None, None], g, -jnp.inf)
    b = jnp.stack([g[:, a0:a0 + WIN_R] for a0 in range(WIN_R)], axis=0)
    b = jnp.transpose(b, (0, 1, 3, 2, 4))
    return b.reshape(WIN_R, B_HEADS, GRID_W, WIN_R * GRID_W)


def _nbr_kernel(q_ref, kp_ref, kc_ref, kn_ref, vp_ref, vcur_ref, vn_ref, kctx_ref, vctx_ref, bias_ref,
                o_ref, kwin, vwin, kcx, vcx, *, n_rows):
    rb = pl.program_id(1)
    blk = NB_ROWS * GRID_W
    scale = B_DIM ** -0.5
    for h in range(B_HEADS):
        sl = slice(h * B_DIM, (h + 1) * B_DIM)
        for j, (kr, vr) in enumerate(((kp_ref, vp_ref), (kc_ref, vcur_ref), (kn_ref, vn_ref))):
            kwin[h, j * blk:(j + 1) * blk, :] = kr[:, sl].astype(BF16)
            vwin[h, j * blk:(j + 1) * blk, :] = vr[:, sl].astype(BF16)
        kcx[h] = kctx_ref[:, sl].astype(BF16)
        vcx[h] = vctx_ref[:, sl].astype(BF16)

    def window(rr):
        r = rb * NB_ROWS + rr
        r0 = jnp.clip(r - WIN_R // 2, 0, n_rows - WIN_R)
        off = pl.multiple_of((r0 - (rb - 1) * NB_ROWS) * GRID_W, GRID_W)
        return off, r0 - r + (WIN_R - 1)

    def scores(rr):
        off, a0 = window(rr)
        qrow = q_ref[rr * GRID_W:(rr + 1) * GRID_W, :]
        out = []
        for h in range(B_HEADS):
            qh = qrow[:, h * B_DIM:(h + 1) * B_DIM].astype(BF16)
            s = _dot_nt(qh, kwin[h, pl.ds(off, WIN_R * GRID_W), :]) * scale + bias_ref[a0, h]
            out.append((s, _dot_nt(qh, kcx[h]) * scale))
        return out

    nxt = scores(0)
    for rr in range(NB_ROWS):
        cur = nxt
        if rr + 1 < NB_ROWS:
            nxt = scores(rr + 1)
        off, _ = window(rr)
        outs = []
        for h in range(B_HEADS):
            s, sc = cur[h]
            m = jnp.maximum(jnp.max(s, axis=1, keepdims=True), jnp.max(sc, axis=1, keepdims=True))
            p = jnp.exp(s - m)
            pc = jnp.exp(sc - m)
            l = jnp.sum(p, axis=1, keepdims=True) + jnp.sum(pc, axis=1, keepdims=True)
            o = _dot(p.astype(BF16), vwin[h, pl.ds(off, WIN_R * GRID_W), :]) + _dot(pc.astype(BF16), vcx[h])
            outs.append(o / l)
        o_ref[rr * GRID_W:(rr + 1) * GRID_W, :] = jnp.concatenate(outs, axis=1)


def _nbr_attn(u, uc, bias8):
    bsz, s, _ = u.shape
    c = uc.shape[1]
    n_rows = s // GRID_W
    nb = n_rows // NB_ROWS
    blk = NB_ROWS * GRID_W
    w = B_HEADS * B_DIM

    def spec(col, shift):
        return pl.BlockSpec((None, blk, w), lambda b, i: (b, jnp.clip(i + shift, 0, nb - 1), col))

    return pl.pallas_call(
        functools.partial(_nbr_kernel, n_rows=n_rows),
        grid=(bsz, nb),
        in_specs=[spec(3, 0), spec(4, -1), spec(4, 0), spec(4, 1), spec(5, -1), spec(5, 0), spec(5, 1),
                  pl.BlockSpec((None, c, w), lambda b, i: (b, 0, 4)),
                  pl.BlockSpec((None, c, w), lambda b, i: (b, 0, 5)),
                  pl.BlockSpec(bias8.shape, lambda b, i: (0, 0, 0, 0))],
        out_specs=pl.BlockSpec((None, blk, w), lambda b, i: (b, i, 0)),
        out_shape=jax.ShapeDtypeStruct((bsz, s, w), F32),
        scratch_shapes=[pltpu.VMEM((B_HEADS, 3 * blk, B_DIM), BF16),
                        pltpu.VMEM((B_HEADS, 3 * blk, B_DIM), BF16),
                        pltpu.VMEM((B_HEADS, c, B_DIM), BF16),
                        pltpu.VMEM((B_HEADS, c, B_DIM), BF16)],
        compiler_params=_params(("parallel", "arbitrary")),
        name="nbr_attn",
    )(u, u, u, u, u, u, u, uc, uc, bias8)


def _nbr_ctx_kernel(q_ref, k_ref, v_ref, o_ref):
    scale = B_DIM ** -0.5
    outs = []
    for h in range(B_HEADS):
        sl = slice(h * B_DIM, (h + 1) * B_DIM)
        s = _dot_nt(q_ref[:, sl].astype(BF16), k_ref[:, sl].astype(BF16)) * scale
        m = jnp.max(s, axis=1, keepdims=True)
        p = jnp.exp(s - m)
        l = jnp.sum(p, axis=1, keepdims=True)
        outs.append(_dot(p.astype(BF16), v_ref[:, sl].astype(BF16)) / l)
    o_ref[...] = jnp.concatenate(outs, axis=1)


def _nbr_ctx_attn(uc):
    bsz, c, _ = uc.shape
    w = B_HEADS * B_DIM
    return pl.pallas_call(
        _nbr_ctx_kernel,
        grid=(bsz,),
        in_specs=[pl.BlockSpec((None, c, w), lambda b: (b, 0, 3)),
                  pl.BlockSpec((None, c, w), lambda b: (b, 0, 4)),
                  pl.BlockSpec((None, c, w), lambda b: (b, 0, 5))],
        out_specs=pl.BlockSpec((None, c, w), lambda b: (b, 0, 0)),
        out_shape=jax.ShapeDtypeStruct((bsz, c, w), F32),
        compiler_params=_params(("arbitrary",)),
        name="nbr_attn_ctx",
    )(uc, uc, uc)


HALO = SUBLANES


def _halo_specs(tm, length, col, width):
    nt = length // tm
    per = tm // HALO
    last = length // HALO - 1
    return [pl.BlockSpec((None, HALO, width), lambda b, i, *_: (b, jnp.maximum(i * per - 1, 0), col)),
            pl.BlockSpec((None, tm, width), lambda b, i, *_: (b, i, col)),
            pl.BlockSpec((None, HALO, width), lambda b, i, *_: (b, jnp.minimum((i + 1) * per, last), col))], nt


def _fill_halo(buf, prev_ref, cur_ref, next_ref, i, nt, tm):
    zero = jnp.zeros(prev_ref.shape, F32)
    buf[0:HALO, :] = jnp.where(i > 0, prev_ref[...], zero)
    buf[HALO:HALO + tm, :] = cur_ref[...]
    buf[HALO + tm:, :] = jnp.where(i < nt - 1, next_ref[...], zero)


def _pool_kernel(prev_ref, cur_ref, next_ref, w_ref, ps_ref, o_ref, buf, *, tm, nt, length):
    i = pl.program_id(1)
    _fill_halo(buf, prev_ref, cur_ref, next_ref, i, nt, tm)

    def sh(j):
        return buf[HALO + j:HALO + j + tm, :]

    u = sh(0)
    sums = []
    acc = None
    lo, hi = 0, 0
    for w in POOL_SIZES:
        for j in list(range(-(w // 2), lo)) + list(range(hi, w // 2)):
            acc = sh(j) if acc is None else acc + sh(j)
        lo, hi = -(w // 2), w // 2
        sums.append(acc)
    lane = lax.broadcasted_iota(jnp.int32, (tm, D_GROUP), 1)
    t = (i * tm + lax.broadcasted_iota(jnp.int32, (tm, D_GROUP), 0))
    wsum = sums[-1]
    half = jnp.full((tm, D_GROUP), POOL_SIZES[-1] // 2, jnp.int32)
    for g in range(len(POOL_SIZES) - 2, -1, -1):
        sel = lane < (g + 1) * POOL_CH
        wsum = jnp.where(sel, sums[g], wsum)
        half = jnp.where(sel, POOL_SIZES[g] // 2, half)
    cnt = (jnp.minimum(t + half, length) - jnp.maximum(t - half, 0)).astype(F32)
    d = wsum / cnt - u
    o_ref[...] = _dot(d.astype(BF16), w_ref[...]) * ps_ref[...]


def _pool_mix(u, wbd, pool_scale, tm):
    bsz, length, _ = u.shape
    specs, nt = _halo_specs(tm, length, 6, D_GROUP)
    return pl.pallas_call(
        functools.partial(_pool_kernel, tm=tm, nt=nt, length=length),
        grid=(bsz, nt),
        in_specs=specs + [pl.BlockSpec((D_GROUP, D_GROUP), lambda b, i: (0, 0)),
                          pl.BlockSpec((1, D_GROUP), lambda b, i: (0, 0))],
        out_specs=pl.BlockSpec((None, tm, D_GROUP), lambda b, i: (b, i, 0)),
        out_shape=jax.ShapeDtypeStruct((bsz, length, D_GROUP), F32),
        scratch_shapes=[pltpu.VMEM((tm + 2 * HALO, D_GROUP), F32)],
        compiler_params=_params(("parallel", "arbitrary")),
        name="pool_mix",
    )(u, u, u, wbd, pool_scale)


def _hy_short_kernel(prev_ref, cur_ref, next_ref, w_ref, b_ref, o_ref, buf, *, tm, nt):
    i = pl.program_id(1)
    _fill_halo(buf, prev_ref, cur_ref, next_ref, i, nt, tm)
    w = w_ref[...]
    y = (buf[HALO - 1:HALO - 1 + tm, :] * w[0:1] + buf[HALO:HALO + tm, :] * w[1:2]
         + buf[HALO + 1:HALO + 1 + tm, :] * w[2:3] + b_ref[...])
    o_ref[...] = y.T


def _hy_short(u, w_short, b_short, tm):
    bsz, length, _ = u.shape
    nt = length // tm
    per = tm // HALO
    last = length // HALO - 1
    c0 = 7
    in_specs = [pl.BlockSpec((None, HALO, HY_CH), lambda b, i, j: (b, jnp.maximum(i * per - 1, 0), c0 + j)),
                pl.BlockSpec((None, tm, HY_CH), lambda b, i, j: (b, i, c0 + j)),
                pl.BlockSpec((None, HALO, HY_CH), lambda b, i, j: (b, jnp.minimum((i + 1) * per, last), c0 + j)),
                pl.BlockSpec((3, HY_CH), lambda b, i, j: (0, j)),
                pl.BlockSpec((1, HY_CH), lambda b, i, j: (0, j))]
    return pl.pallas_call(
        functools.partial(_hy_short_kernel, tm=tm, nt=nt),
        grid=(bsz, nt, 3),
        in_specs=in_specs,
        out_specs=pl.BlockSpec((None, None, HY_CH, tm), lambda b, i, j: (j, b, 0, i)),
        out_shape=jax.ShapeDtypeStruct((3, bsz, HY_CH, length), F32),
        scratch_shapes=[pltpu.VMEM((tm + 2 * HALO, HY_CH), F32)],
        compiler_params=_params(("parallel", "arbitrary", "arbitrary")),
        name="hyena_short_conv",
    )(u, u, u, w_short, b_short)


HY_FEAT = 40


def _hy_filter_kernel(band_ref, w1_ref, b1_ref, w2_ref, b2_ref, w3_ref, b3_ref, dl_ref,
                      k_ref, ssq_ref, nrm_ref, *, tp, length):
    i = pl.program_id(0)
    n_i = pl.num_programs(0)
    m = i * tp + lax.broadcasted_iota(jnp.int32, (1, tp), 1)
    t = jnp.where(m <= length, m, 2 * length - m).astype(F32)
    t_norm = t / max(length - 1, 1)
    ang = ((2.0 * math.pi / length) * t) * band_ref[...]
    row = lax.broadcasted_iota(jnp.int32, (HY_FEAT, tp), 0)
    z = jnp.where(row == 0, t_norm,
                  jnp.where(row <= HY_BANDS, jnp.cos(ang), jnp.where(row < HY_EMB, jnp.sin(ang), 0.0)))
    z = jnp.concatenate([z, jnp.zeros((LANES - HY_FEAT, tp), F32)], axis=0)
    h = jnp.sin(HY_SIN_FREQ * (_dot(w1_ref[...], z, HI) + b1_ref[...]))
    h = jnp.sin(HY_SIN_FREQ * (_dot(w2_ref[...], h, HI) + b2_ref[...]))
    hh, hl = _split_bf16(h)
    h = _dot(w3_ref[...], jnp.concatenate([hh, hl, hh], axis=0)) + b3_ref[...]
    h = h * jnp.exp(-t_norm * dl_ref[...])

    @pl.when(i == 0)
    def _():
        ssq_ref[...] = jnp.zeros(ssq_ref.shape, F32)

    for o in range(2):
        fwd = h[o * 2 * HY_CH:o * 2 * HY_CH + HY_CH]
        bwd = h[o * 2 * HY_CH + HY_CH:(o + 1) * 2 * HY_CH]
        k = jnp.where(m < length, fwd, jnp.where(m == length, 0.0, bwd))
        k_ref[o] = k
        extra = jnp.where(m == 0, bwd * bwd, 0.0)
        ssq_ref[o] += jnp.sum(k * k + extra, axis=1, keepdims=True)

    @pl.when(i == n_i - 1)
    def _():
        nrm_ref[...] = lax.rsqrt(ssq_ref[...] + EPS)


def _hy_filters(length, w1, b1, w2, b2, w3, b3, tp):
    bands = jnp.linspace(1e-4, HY_BANDS - 1, HY_BANDS, dtype=F32)
    band_col = jnp.concatenate([jnp.zeros((1,), F32), bands, bands,
                                jnp.zeros((HY_FEAT - HY_EMB,), F32)])[:, None]
    deltas = jnp.abs(jnp.linspace(HY_MIN_DECAY, HY_MAX_DECAY, HY_CH, dtype=F32))
    dl_col = jnp.tile(deltas, 4)[:, None]
    w1t = jnp.pad(w1.astype(F32).T, ((0, 0), (0, LANES - HY_EMB)))
    full = lambda shape: pl.BlockSpec(shape, lambda i: (0,) * len(shape))
    n = 2 * length
    return pl.pallas_call(
        functools.partial(_hy_filter_kernel, tp=tp, length=length),
        grid=(n // tp,),
        in_specs=[full((HY_FEAT, 1)), full((HY_HIDDEN, LANES)), full((HY_HIDDEN, 1)),
                  full((HY_HIDDEN, HY_HIDDEN)), full((HY_HIDDEN, 1)),
                  full((4 * HY_CH, 3 * HY_HIDDEN)), full((4 * HY_CH, 1)), full((4 * HY_CH, 1))],
        out_specs=[pl.BlockSpec((2, HY_CH, tp), lambda i: (0, 0, i)),
                   full((2, HY_CH, 1)), full((2, HY_CH, 1))],
        out_shape=[jax.ShapeDtypeStruct((2, HY_CH, n), F32),
                   jax.ShapeDtypeStruct((2, HY_CH, 1), F32),
                   jax.ShapeDtypeStruct((2, HY_CH, 1), F32)],
        compiler_params=_params(("arbitrary",)),
        name="hyena_filters",
    )(band_col, w1t, b1.astype(F32)[:, None], w2.astype(F32).T, b2.astype(F32)[:, None],
      _cat3(w3.astype(F32).T, 1), b3.astype(F32)[:, None], dl_col)


def _dft_consts(n1, n2):
    n = n1 * n2
    a1 = 2.0 * np.pi * ((np.arange(n1)[:, None] * np.arange(n1)[None, :]) % n1) / n1
    c1, s1 = np.cos(a1), np.sin(a1)
    a2 = 2.0 * np.pi * ((np.arange(n2)[:, None] * np.arange(n2)[None, :]) % n2) / n2
    c2, s2 = np.cos(a2), np.sin(a2)
    at = 2.0 * np.pi * ((np.arange(n1)[:, None] * np.arange(n2)[None, :]) % n) / n
    f1_full = np.concatenate([c1, -s1], axis=0)
    f1_half = f1_full[:, :n1 // 2]
    g1 = np.concatenate([c1[:n1 // 2], -s1[:n1 // 2]], axis=1)
    w2f = np.block([[c2, -s2], [s2, c2]])
    w2i = np.block([[c2, s2], [-s2, c2]])
    f = lambda x: jnp.asarray(x, F32)
    return dict(f1_full=f(f1_full), f1_half=f(f1_half), g1=f(g1), w2f=f(w2f), w2i=f(w2i),
                tr=f(np.cos(at)), ti=f(-np.sin(at)))


def _fft_fwd(slabs, f1, tr, ti, w2f, stack, n1, n2, prec):
    for c, x in enumerate(slabs):
        a = _dot(f1, x.astype(stack.dtype), prec)
        ar, ai = a[:n1], a[n1:]
        stack[c * n1:(c + 1) * n1, 0:n2] = (ar * tr - ai * ti).astype(stack.dtype)
        stack[c * n1:(c + 1) * n1, n2:2 * n2] = (ar * ti + ai * tr).astype(stack.dtype)
    return _dot(stack[...], w2f, prec)


def _split_bf16(x):
    hi = x.astype(BF16)
    return hi, (x - hi.astype(F32)).astype(BF16)


def _cat3(x, axis):
    hi, lo = _split_bf16(x)
    return jnp.concatenate([hi, hi, lo], axis=axis)


def _hy_spec_kernel(nrm_ref, k_ref, f1_ref, tr_ref, ti_ref, w2f_ref, o_ref, stack, *, cg, n1, n2):
    o = pl.program_id(0)
    g = pl.program_id(1)
    tr = tr_ref[...]
    ti = ti_ref[...]
    for c in range(cg):
        hi, lo = _split_bf16(k_ref[c])
        a = _dot(f1_ref[...], jnp.concatenate([hi, lo, hi], axis=0))
        ar, ai = a[:n1], a[n1:]
        sh, sl = _split_bf16(jnp.concatenate([ar * tr - ai * ti, ar * ti + ai * tr], axis=1))
        stack[c * n1:(c + 1) * n1, :] = jnp.concatenate([sh, sl, sh], axis=1)
    x = _dot(stack[...], w2f_ref[...])
    for c in range(cg):
        sc = nrm_ref[o * HY_CH + g * cg + c] * (1.0 / (n1 * n2))
        xc = x[c * n1:(c + 1) * n1] * sc
        o_ref[c, 0] = xc[:, :n2]
        o_ref[c, 1] = xc[:, n2:]


def _hy_spec(k4, nrm_flat, dc, cg, n1, n2):
    full = lambda shape: pl.BlockSpec(shape, lambda o, g: (0,) * len(shape))
    return pl.pallas_call(
        functools.partial(_hy_spec_kernel, cg=cg, n1=n1, n2=n2),
        grid=(2, HY_CH // cg),
        in_specs=[pl.BlockSpec(memory_space=pltpu.SMEM),
                  pl.BlockSpec((None, cg, n1, n2), lambda o, g: (o, g, 0, 0)),
                  full((2 * n1, 3 * n1)), full((n1, n2)), full((n1, n2)), full((6 * n2, 2 * n2))],
        out_specs=pl.BlockSpec((None, cg, 2, n1, n2), lambda o, g: (o, g, 0, 0, 0)),
        out_shape=jax.ShapeDtypeStruct((2, HY_CH, 2, n1, n2), F32),
        scratch_shapes=[pltpu.VMEM((cg * n1, 6 * n2), BF16)],
        compiler_params=_params(("parallel", "arbitrary")),
        name="hyena_filter_spectrum",
    )(nrm_flat, k4, _cat3(dc["f1_full"], 1), dc["tr"], dc["ti"], _cat3(dc["w2f"], 0))


def _hy_conv_kernel(skip_ref, x_ref, ks_ref, f1_ref, g1_ref, tr_ref, ti_ref, w2f_ref, w2i_ref,
                    o_ref, stack, *, cg, n1, n2):
    g = pl.program_id(1)
    tr = tr_ref[...]
    ti = ti_ref[...]

    def conv(slabs, order):
        x = _fft_fwd(slabs, f1_ref[...], tr, ti, w2f_ref[...], stack, n1, n2, None)
        for c in range(cg):
            xr, xi = x[c * n1:(c + 1) * n1, :n2], x[c * n1:(c + 1) * n1, n2:]
            kr, ki = ks_ref[order, c, 0], ks_ref[order, c, 1]
            stack[c * n1:(c + 1) * n1, 0:n2] = (xr * kr - xi * ki).astype(BF16)
            stack[c * n1:(c + 1) * n1, n2:2 * n2] = (xr * ki + xi * kr).astype(BF16)
        bm = _dot(stack[...], w2i_ref[...])
        outs = []
        for c in range(cg):
            br, bi = bm[c * n1:(c + 1) * n1, :n2], bm[c * n1:(c + 1) * n1, n2:]
            b2 = jnp.concatenate([br * tr + bi * ti, bi * tr - br * ti], axis=0)
            y = _dot(g1_ref[...], b2.astype(BF16))
            outs.append(y + slabs[c] * skip_ref[order * HY_CH + g * cg + c])
        return outs

    v = [x_ref[2, c] for c in range(cg)]
    y0 = conv(v, 0)
    z = [x_ref[0, c] * y0[c] for c in range(cg)]
    y1 = conv(z, 1)
    for c in range(cg):
        o_ref[c] = x_ref[1, c] * y1[c]


def _hy_conv(x4, kspec, skip_flat, dc, cg, n1, n2):
    bsz = x4.shape[1]
    full = lambda shape: pl.BlockSpec(shape, lambda b, g: (0,) * len(shape))
    return pl.pallas_call(
        functools.partial(_hy_conv_kernel, cg=cg, n1=n1, n2=n2),
        grid=(bsz, HY_CH // cg),
        in_specs=[pl.BlockSpec(memory_space=pltpu.SMEM),
                  pl.BlockSpec((3, None, cg, n1 // 2, n2), lambda b, g: (0, b, g, 0, 0)),
                  pl.BlockSpec((2, cg, 2, n1, n2), lambda b, g: (0, g, 0, 0, 0)),
                  full((2 * n1, n1 // 2)), full((n1 // 2, 2 * n1)), full((n1, n2)), full((n1, n2)),
                  full((2 * n2, 2 * n2)), full((2 * n2, 2 * n2))],
        out_specs=pl.BlockSpec((None, cg, n1 // 2, n2), lambda b, g: (b, g, 0, 0)),
        out_shape=jax.ShapeDtypeStruct((bsz, HY_CH, n1 // 2, n2), F32),
        scratch_shapes=[pltpu.VMEM((cg * n1, 2 * n2), BF16)],
        compiler_params=_params(("parallel", "arbitrary")),
        name="hyena_long_conv",
    )(skip_flat, x4, kspec, dc["f1_half"].astype(BF16), dc["g1"].astype(BF16), dc["tr"], dc["ti"],
      dc["w2f"].astype(BF16), dc["w2i"].astype(BF16))


def _hy_ctx_kernel(x_ref, k_ref, nrm_ref, skip_ref, fc_ref, fs_ref, o_ref, *, c):
    fc = fc_ref[...]
    fs = fs_ref[...]
    inv_n = 1.0 / (2 * c)

    def conv(x, order):
        kk = k_ref[order]
        kr, ki = _dot(kk, fc, HI), -_dot(kk, fs, HI)
        xr, xi = _dot(x, fc[:c], HI), -_dot(x, fs[:c], HI)
        yr, yi = xr * kr - xi * ki, xr * ki + xi * kr
        y = (_dot(yr, fc[:, :c], HI) - _dot(yi, fs[:, :c], HI)) * inv_n
        return y * nrm_ref[order] + x * skip_ref[order]

    z = x_ref[0] * conv(x_ref[2], 0)
    o_ref[...] = x_ref[1] * conv(z, 1)


def _hy_ctx_conv(xt, kt, nrm, skip_col):
    _, bsz, ch, c = xt.shape
    n = 2 * c
    ang = 2.0 * np.pi * ((np.arange(n)[:, None] * np.arange(n)[None, :]) % n) / n
    fc, fs = jnp.asarray(np.cos(ang), F32), jnp.asarray(np.sin(ang), F32)
    full = lambda shape: pl.BlockSpec(shape, lambda b: (0,) * len(shape))
    return pl.pallas_call(
        functools.partial(_hy_ctx_kernel, c=c),
        grid=(bsz,),
        in_specs=[pl.BlockSpec((3, None, ch, c), lambda b: (0, b, 0, 0)),
                  full((2, ch, n)), full((2, ch, 1)), full((2, ch, 1)), full((n, n)), full((n, n))],
        out_specs=pl.BlockSpec((None, ch, c), lambda b: (b, 0, 0)),
        out_shape=jax.ShapeDtypeStruct((bsz, ch, c), F32),
        compiler_params=_params(("arbitrary",)),
        name="hyena_ctx_conv",
    )(xt, kt, nrm, skip_col, fc, fs)


def _out_proj_kernel(a_ref, b_ref, p_ref, ht_ref, x_ref, g_ref, w_ref, o_ref):
    w = D_GROUP
    acc = _dot(a_ref[...].astype(BF16), w_ref[0:w])
    acc += _dot(b_ref[...].astype(BF16), w_ref[w:2 * w])
    acc += _dot(p_ref[...].astype(BF16), w_ref[2 * w:3 * w])
    acc += _dot(ht_ref[...].T.astype(BF16), w_ref[3 * w:4 * w])
    o_ref[...] = x_ref[...] + g_ref[...] * acc


def _out_proj(a, b, p, ht, x, mod3, row_of_batch, j_gate, w_out, tm):
    bsz, t, d = x.shape
    w = D_GROUP
    tok = pl.BlockSpec((None, tm, w), lambda bb, i: (bb, i, 0))
    return pl.pallas_call(
        _out_proj_kernel,
        grid=(bsz, t // tm),
        in_specs=[tok, tok, tok,
                  pl.BlockSpec((None, w, tm), lambda bb, i: (bb, 0, i)),
                  pl.BlockSpec((None, tm, d), lambda bb, i: (bb, i, 0)),
                  pl.BlockSpec((None, 1, d), lambda bb, i: (row_of_batch(bb), 0, j_gate)),
                  pl.BlockSpec((4 * w, d), lambda bb, i: (0, 0))],
        out_specs=pl.BlockSpec((None, tm, d), lambda bb, i: (bb, i, 0)),
        out_shape=jax.ShapeDtypeStruct((bsz, t, d), F32),
        compiler_params=_params(("parallel", "arbitrary")),
        name="out_proj_residual",
    )(a, b, p, ht, x, mod3, w_out)


MOE_ROWS = 256


def _route(x, g, sc, sh, rw3, rbc):
    per_group = N_EXPERTS // N_EXPERT_GROUPS
    tm = x.shape[0]
    ms = jnp.mean(x * x, axis=-1, keepdims=True)
    h = (x * lax.rsqrt(ms + EPS)) * g * (1.0 + sc) + sh
    hh, hl = _split_bf16(h)
    d = x.shape[1]
    logits = (_dot(hh, rw3[0:d]) + _dot(hl, rw3[d:2 * d]) + _dot(hh, rw3[2 * d:3 * d])).T[:N_EXPERTS]
    ex = jnp.exp(logits - jnp.max(logits, axis=0, keepdims=True))
    scores = ex / jnp.sum(ex, axis=0, keepdims=True)
    sel = scores + rbc
    srow = [sel[r:r + 1] for r in range(N_EXPERTS)]
    best = None
    for grp in range(N_EXPERT_GROUPS):
        rows = list(range(grp * per_group, (grp + 1) * per_group))
        v1 = functools.reduce(jnp.maximum, [srow[r] for r in rows])
        i1 = jnp.full((1, tm), rows[-1], jnp.int32)
        for r in reversed(rows[:-1]):
            i1 = jnp.where(srow[r] == v1, r, i1)
        rest = [jnp.where(i1 == r, -jnp.inf, srow[r]) for r in rows]
        v2 = functools.reduce(jnp.maximum, rest)
        i2 = jnp.full((1, tm), rows[-1], jnp.int32)
        for k in reversed(range(per_group - 1)):
            i2 = jnp.where(rest[k] == v2, rows[k], i2)
        gs = v1 + v2
        if best is None:
            best, e1, e2 = gs, i1, i2
        else:
            upd = gs > best
            best = jnp.where(upd, gs, best)
            e1 = jnp.where(upd, i1, e1)
            e2 = jnp.where(upd, i2, e2)
    row = lax.broadcasted_iota(jnp.int32, (N_EXPERTS, tm), 0)
    w1 = jnp.sum(jnp.where(row == e1, scores, 0.0), axis=0, keepdims=True)
    w2 = jnp.sum(jnp.where(row == e2, scores, 0.0), axis=0, keepdims=True)
    tot = w1 + w2
    gates_t = jnp.where(row == e1, w1 / tot, 0.0) + jnp.where(row == e2, w2 / tot, 0.0)
    gates_t = jnp.concatenate([gates_t, jnp.zeros((LANES - N_EXPERTS, tm), F32)], axis=0)
    return h, gates_t.T


def _moe_kernel(x_ref, g_ref, sc_ref, sh_ref, gate_ref, rw_ref, rb_ref, w1_ref, w3_ref, w2_ref, fg_ref,
                o_ref, h_scr, gates_scr, acc_scr, *, final):
    e = pl.program_id(2)
    tm = x_ref.shape[0]

    @pl.when(e == 0)
    def _():
        h, gates = _route(x_ref[...], g_ref[...], sc_ref[...], sh_ref[...], rw_ref[...], rb_ref[...])
        h_scr[...] = h.astype(BF16)
        gates_scr[...] = gates
        acc_scr[...] = jnp.zeros(acc_scr.shape, F32)

    lane = lax.broadcasted_iota(jnp.int32, (tm, LANES), 1)
    ge = jnp.sum(jnp.where(lane == e, gates_scr[...], 0.0), axis=1, keepdims=True)

    def up(j):
        hb = h_scr[j * MOE_ROWS:(j + 1) * MOE_ROWS, :]
        return _dot(hb, w1_ref[...]), _dot(hb, w3_ref[...])

    nxt = up(0)
    for j in range(tm // MOE_ROWS):
        a, b = nxt
        if (j + 1) * MOE_ROWS < tm:
            nxt = up(j + 1)
        rows = slice(j * MOE_ROWS, (j + 1) * MOE_ROWS)
        act = (a * jax.nn.sigmoid(a)) * b
        acc_scr[rows, :] += ge[rows] * _dot(act.astype(BF16), w2_ref[...])

    @pl.when(e == N_EXPERTS - 1)
    def _():
        y = x_ref[...] + gate_ref[...] * acc_scr[...]
        if final:
            ms = jnp.mean(y * y, axis=-1, keepdims=True)
            y = (y * lax.rsqrt(ms + EPS)) * fg_ref[...]
        o_ref[...] = y


def _moe(x, g, mod3, row_of_batch, j_shift, j_scale, j_gate, rw, rb, w1, w3, w2, final_g, final, tm):
    bsz, t, d = x.shape
    vec = lambda j: pl.BlockSpec((None, 1, d), lambda b, i, e: (row_of_batch(b), 0, j))
    full = lambda shape: pl.BlockSpec(shape, lambda b, i, e: (0,) * len(shape))
    return pl.pallas_call(
        functools.partial(_moe_kernel, final=final),
        grid=(bsz, t // tm, N_EXPERTS),
        in_specs=[pl.BlockSpec((None, tm, d), lambda b, i, e: (b, i, 0)),
                  full((1, d)), vec(j_scale), vec(j_shift), vec(j_gate),
                  full((3 * d, LANES)), full((N_EXPERTS, 1)),
                  pl.BlockSpec((None, d, D_EXPERT), lambda b, i, e: (e, 0, 0)),
                  pl.BlockSpec((None, d, D_EXPERT), lambda b, i, e: (e, 0, 0)),
                  pl.BlockSpec((None, D_EXPERT, d), lambda b, i, e: (e, 0, 0)),
                  full((1, d))],
        out_specs=pl.BlockSpec((None, tm, d), lambda b, i, e: (b, i, 0)),
        out_shape=jax.ShapeDtypeStruct((bsz, t, d), F32),
        scratch_shapes=[pltpu.VMEM((tm, d), BF16), pltpu.VMEM((tm, LANES), F32), pltpu.VMEM((tm, d), F32)],
        compiler_params=_params(("parallel", "parallel", "arbitrary")),
        name="moe_final" if final else "moe",
    )(x, g, mod3, mod3, mod3, rw, rb, w1, w3, w2, final_g)


MOE_SUB = 512
MOE_CAP = 128
MOE_BLK = 2048
MOE_PAIR = 2


def _moe_route_kernel(x_ref, g_ref, sc_ref, sh_ref, rw_ref, rb_ref, h_ref, gates_ref, cnt_ref):
    h, gates = _route(x_ref[...], g_ref[...], sc_ref[...], sh_ref[...], rw_ref[...], rb_ref[...])
    h_ref[...] = h.astype(BF16)
    gates_ref[...] = gates
    cnt = jnp.sum(jnp.where(gates > 0.0, 1.0, 0.0), axis=0, keepdims=True)
    cnt_ref[...] = jnp.broadcast_to(cnt, cnt_ref.shape)


def _moe_route(x, g, mod3, row_of_batch, j_shift, j_scale, rw, rb):
    bsz, t, d = x.shape
    tm = MOE_SUB
    vec = lambda j: pl.BlockSpec((None, 1, d), lambda b, i: (row_of_batch(b), 0, j))
    full = lambda shape: pl.BlockSpec(shape, lambda b, i: (0,) * len(shape))
    return pl.pallas_call(
        _moe_route_kernel,
        grid=(bsz, t // tm),
        in_specs=[pl.BlockSpec((None, tm, d), lambda b, i: (b, i, 0)),
                  full((1, d)), vec(j_scale), vec(j_shift), full((3 * d, LANES)), full((N_EXPERTS, 1))],
        out_specs=[pl.BlockSpec((None, tm, d), lambda b, i: (b, i, 0)),
                   pl.BlockSpec((None, tm, LANES), lambda b, i: (b, i, 0)),
                   pl.BlockSpec((None, None, SUBLANES, LANES), lambda b, i: (b, i, 0, 0))],
        out_shape=[jax.ShapeDtypeStruct((bsz, t, d), BF16),
                   jax.ShapeDtypeStruct((bsz, t, LANES), F32),
                   jax.ShapeDtypeStruct((bsz, t // tm, SUBLANES, LANES), F32)],
        compiler_params=_params(("parallel", "arbitrary")),
        name="moe_route",
    )(x, g, mod3, mod3, rw, rb)


def _moe_routed_kernel(np_ref, h_ref, gates_ref, x_ref, gate_ref, w1_ref, w3_ref, w2_ref, fg_ref,
                       o_ref, rank_scr, rankt_scr, gatet_scr, xc_scr, *, final, nblk):
    b = pl.program_id(0)
    i = pl.program_id(1)
    e = pl.program_id(2)
    nsub = h_ref.shape[0] // MOE_SUB
    sub = lambda j: slice(j * MOE_SUB, (j + 1) * MOE_SUB)

    @pl.when(e == 0)
    def _():
        o_ref[...] = jnp.zeros(o_ref.shape, F32)
        r = lax.broadcasted_iota(jnp.int32, (MOE_SUB, MOE_SUB), 0)
        c = lax.broadcasted_iota(jnp.int32, (MOE_SUB, MOE_SUB), 1)
        ltri = jnp.where(c <= r, 1.0, 0.0).astype(BF16)
        utri = jnp.where(r <= c, 1.0, 0.0).astype(BF16)
        for j in range(nsub):
            gt = gates_ref[sub(j), :]
            rank_scr[j] = _dot(ltri, jnp.where(gt > 0.0, 1.0, 0.0).astype(BF16))
            gtt = gt.T[:N_EXPERTS]
            gatet_scr[j] = gtt
            rankt_scr[j] = _dot(jnp.where(gtt > 0.0, 1.0, 0.0).astype(BF16), utri)

    lane = lax.broadcasted_iota(jnp.int32, (MOE_SUB, LANES), 1)
    slot_lane = lax.broadcasted_iota(jnp.int32, (MOE_SUB, MOE_PAIR * MOE_CAP), 1)
    second = slot_lane >= MOE_CAP

    def one_pass(p, carry):
        base = (p * MOE_CAP + 1).astype(F32)
        slot_r = lax.broadcasted_iota(jnp.int32, (MOE_CAP, MOE_SUB), 0).astype(F32) + base
        slot_c = jnp.where(second, slot_lane - MOE_CAP, slot_lane).astype(F32) + base
        ys = []
        for k in range(MOE_PAIR):
            ex = e * MOE_PAIR + k
            gcs = []
            for j in range(nsub):
                rr = rankt_scr[j, pl.ds(ex, 1), :]
                gr = gatet_scr[j, pl.ds(ex, 1), :]
                pick = (rr == slot_r) & (gr > 0.0)
                xc_scr[k, j * MOE_CAP:(j + 1) * MOE_CAP, :] = _dot(
                    jnp.where(pick, 1.0, 0.0).astype(BF16), h_ref[sub(j), :]).astype(BF16)
                gcs.append(jnp.sum(jnp.where(pick, gr, 0.0), axis=1, keepdims=True))
            xc = xc_scr[k]
            a = _dot(xc, w1_ref[k])
            bb = _dot(xc, w3_ref[k])
            y = _dot(((a * jax.nn.sigmoid(a)) * bb).astype(BF16), w2_ref[k])
            ys.append([(y[j * MOE_CAP:(j + 1) * MOE_CAP] * gcs[j]).astype(BF16) for j in range(nsub)])
        for j in range(nsub):
            cols = []
            for k in range(MOE_PAIR):
                ex = e * MOE_PAIR + k
                rc = jnp.sum(jnp.where(lane == ex, rank_scr[j], 0.0), axis=1, keepdims=True)
                gc = jnp.sum(jnp.where(lane == ex, gates_ref[sub(j), :], 0.0), axis=1, keepdims=True)
                cols.append(jnp.where(gc > 0.0, rc, 0.0))
            put = jnp.where(jnp.where(second, cols[1], cols[0]) == slot_c, 1.0, 0.0).astype(BF16)
            o_ref[sub(j), :] += _dot(put, jnp.concatenate([ys[0][j], ys[1][j]], axis=0))
        return carry

    lax.fori_loop(0, np_ref[(b * nblk + i) * (N_EXPERTS // MOE_PAIR) + e], one_pass, 0)

    @pl.when(e == N_EXPERTS // MOE_PAIR - 1)
    def _():
        y = x_ref[...] + gate_ref[...] * o_ref[...]
        if final:
            ms = jnp.mean(y * y, axis=-1, keepdims=True)
            y = (y * lax.rsqrt(ms + EPS)) * fg_ref[...]
        o_ref[...] = y


def _moe_routed(x, g, mod3, row_of_batch, j_shift, j_scale, j_gate, rw, rb, w1, w3, w2, final_g, final):
    bsz, t, d = x.shape
    h, gates, cnt = _moe_route(x, g, mod3, row_of_batch, j_shift, j_scale, rw, rb)
    tb = _tile(t, MOE_BLK)
    nblk = t // tb
    nsub = tb // MOE_SUB
    npair = N_EXPERTS // MOE_PAIR
    passes = jnp.ceil(cnt[:, :, 0, :N_EXPERTS] / MOE_CAP).astype(jnp.int32)
    passes = jnp.max(passes.reshape(bsz, nblk, nsub, npair, MOE_PAIR), axis=(2, 4)).reshape(-1)
    full = lambda shape: pl.BlockSpec(shape, lambda b, i, e, np_: (0,) * len(shape))
    blk = lambda w: pl.BlockSpec((None, tb, w), lambda b, i, e, np_: (b, i, 0), pipeline_mode=pl.Buffered(1))
    grid_spec = pltpu.PrefetchScalarGridSpec(
        num_scalar_prefetch=1,
        grid=(bsz, nblk, npair),
        in_specs=[blk(d), blk(LANES), blk(d),
                  pl.BlockSpec((None, 1, d), lambda b, i, e, np_: (row_of_batch(b), 0, j_gate)),
                  pl.BlockSpec((MOE_PAIR, d, D_EXPERT), lambda b, i, e, np_: (e, 0, 0)),
                  pl.BlockSpec((MOE_PAIR, d, D_EXPERT), lambda b, i, e, np_: (e, 0, 0)),
                  pl.BlockSpec((MOE_PAIR, D_EXPERT, d), lambda b, i, e, np_: (e, 0, 0)),
                  full((1, d))],
        out_specs=pl.BlockSpec((None, tb, d), lambda b, i, e, np_: (b, i, 0)),
        scratch_shapes=[pltpu.VMEM((nsub, MOE_SUB, LANES), F32),
                        pltpu.VMEM((nsub, N_EXPERTS, MOE_SUB), F32),
                        pltpu.VMEM((nsub, N_EXPERTS, MOE_SUB), F32),
                        pltpu.VMEM((MOE_PAIR, nsub * MOE_CAP, d), BF16)])
    return pl.pallas_call(
        functools.partial(_moe_routed_kernel, final=final, nblk=nblk),
        grid_spec=grid_spec,
        out_shape=jax.ShapeDtypeStruct((bsz, t, d), F32),
        compiler_params=_params(("parallel", "parallel", "arbitrary")),
        name="moe_routed_final" if final else "moe_routed",
    )(passes, h, gates, x, mod3, w1, w3, w2, final_g)


def _tile(n, pref):
    t = min(n, pref)
    assert n % t == 0
    return t


def _fft_split(n):
    n2 = LANES
    assert n % n2 == 0
    return n // n2, n2


def _hyena_latent(u, lp, fargs):
    bsz, s, _ = u.shape
    n1, n2 = _fft_split(2 * s)
    dc = _dft_consts(n1, n2)
    cg = 8
    kt_l, _, nrm_l = _hy_filters(s, *fargs, tp=_tile(2 * s, 1024))
    kspec = _hy_spec(kt_l.reshape(2, HY_CH, n1, n2), nrm_l.reshape(2 * HY_CH), dc, cg, n1, n2)
    xt = _hy_short(u, lp["hy_short_w"].astype(F32), lp["hy_short_b"].astype(F32)[None, :], _tile(s, 512))
    h_l = _hy_conv(xt.reshape(3, bsz, HY_CH, n1 // 2, n2), kspec, lp["hy_skip"].astype(F32).reshape(2 * HY_CH),
                   dc, cg, n1, n2)
    return h_l.reshape(bsz, HY_CH, s)


def _mixers(u, uc, lp, li, need_ctx):
    bsz, s, _ = u.shape
    c = uc.shape[1]
    lam_init = 0.8 - 0.6 * math.exp(-0.3 * li)
    lam_vecs = lp["a_lambda"].astype(F32)
    subln = lp["a_subln_g"].astype(F32)[None, :]

    qc, kct, vc = _attn_prep(uc, _tile(c, 256), rope=False)
    ql, klt, vl = _attn_prep(u, _tile(s, 256), rope=True)
    a_l = _diff_attn(ql, kct, vc, klt, vl, lam_vecs, subln, lam_init, _tile(s, 512), _tile(s, 2048))
    bias8 = _nbr_bias(lp["b_rpb"])
    b_l = _nbr_attn(u, uc, bias8)
    wbd = jax.scipy.linalg.block_diag(*[lp["pool_w"][g] for g in range(len(POOL_SIZES))]).astype(BF16)
    pscale = lp["pool_scale"].astype(F32)[None, :]
    p_l = _pool_mix(u, wbd, pscale, _tile(s, 512))
    fargs = (lp["hy_f_w1"], lp["hy_f_b1"], lp["hy_f_w2"], lp["hy_f_b2"], lp["hy_f_w3"], lp["hy_f_b3"])
    skip = lp["hy_skip"].astype(F32)
    w_short = lp["hy_short_w"].astype(F32)
    b_short = lp["hy_short_b"].astype(F32)[None, :]
    h_l = _hyena_latent(u, lp, fargs)
    lat = (a_l, b_l, p_l, h_l)
    if not need_ctx:
        return lat, None
    a_c = _diff_attn(qc, kct, vc, None, None, lam_vecs, subln, lam_init, _tile(c, 256), None)
    b_c = _nbr_ctx_attn(uc)
    p_c = _pool_mix(uc, wbd, pscale, _tile(c, 256))
    kt_c, _, nrm_c = _hy_filters(c, *fargs, tp=_tile(2 * c, 512))
    xtc = _hy_short(uc, w_short, b_short, _tile(c, 256))
    h_c = _hy_ctx_conv(xtc, kt_c, nrm_c, skip[:, :, None])
    return lat, (a_c, b_c, p_c, h_c)


def kernel(x, c, ctx, c_ctx, norm1_g, norm2_g, ada_w, ada_b, w_in, w_out, a_lambda, a_subln_g, b_rpb, pool_w, pool_scale, hy_short_w, hy_short_b, hy_f_w1, hy_f_b1, hy_f_w2, hy_f_b2, hy_f_w3, hy_f_b3, hy_skip, router_w, router_b, moe_w1, moe_w3, moe_w2, final_g):
    depth = norm1_g.shape[0]
    bsz, s, d = x.shape
    cl = ctx.shape[1]
    assert bsz <= SUBLANES - 1
    xl, xc = x, ctx
    cpad = jnp.zeros((SUBLANES, d), F32).at[:bsz].set(c.astype(F32)).at[bsz].set(c_ctx.astype(F32))
    rw = _cat3(jnp.pad(router_w.astype(F32), ((0, 0), (0, LANES - N_EXPERTS))), 0)
    rb = router_b.astype(F32)[:, None]
    lat_row = lambda b: b
    ctx_row = lambda b: bsz
    fg = final_g.astype(F32)[None, :]
    tm = _tile(s, 512)
    tmc = _tile(cl, 256)
    for li in range(depth):
        need_ctx = li < depth - 1
        lp = dict(a_lambda=a_lambda[li], a_subln_g=a_subln_g[li], b_rpb=b_rpb[li], pool_w=pool_w[li],
                  pool_scale=pool_scale[li], hy_short_w=hy_short_w[li], hy_short_b=hy_short_b[li],
                  hy_f_w1=hy_f_w1[li], hy_f_b1=hy_f_b1[li], hy_f_w2=hy_f_w2[li], hy_f_b2=hy_f_b2[li],
                  hy_f_w3=hy_f_w3[li], hy_f_b3=hy_f_b3[li], hy_skip=hy_skip[li])
        mod3 = _ada(cpad, ada_w[li].astype(F32), ada_b[li].astype(F32)[None, :]).reshape(SUBLANES, 1, 6 * d)
        n1g = norm1_g[li].astype(F32)[None, :]
        n2g = norm2_g[li].astype(F32)[None, :]
        w_in_b = w_in[li].astype(BF16)
        w_out_b = w_out[li].astype(BF16)
        u = _norm_proj(xl, n1g, mod3, lat_row, 0, 1, w_in_b, tm)
        uc = _norm_proj(xc, n1g, mod3, ctx_row, 0, 1, w_in_b, tmc)
        lat, cx = _mixers(u, uc, lp, li, need_ctx)
        xl = _out_proj(*lat, xl, mod3, lat_row, 2, w_out_b, tm)
        w1b, w3b, w2b = moe_w1[li].astype(BF16), moe_w3[li].astype(BF16), moe_w2[li].astype(BF16)
        if need_ctx:
            xc = _out_proj(*cx, xc, mod3, ctx_row, 2, w_out_b, tmc)
            xc = _moe(xc, n2g, mod3, ctx_row, 3, 4, 5, rw, rb, w1b, w3b, w2b, fg, False, tmc)
        xl = _moe_routed(xl, n2g, mod3, lat_row, 3, 4, 5, rw, rb, w1b, w3b, w2b, fg, li == depth - 1)
    return xl
```

```python
import functools
import math

import numpy as np
import jax
import jax.numpy as jnp
from jax import lax
from jax.experimental import pallas as pl
from jax.experimental.pallas import tpu as pltpu

F32 = jnp.float32
BF16 = jnp.bfloat16
HI = lax.Precision.HIGHEST

GRID_W = 64
A_HEADS = 4
A_QK = 32
A_V = 64
ROPE_BASE = 10000.0
B_HEADS = 4
B_DIM = 64
WIN_R = 8
WIN_C = 16
POOL_SIZES = (2, 4, 8, 16)
POOL_CH = 64
D_GROUP = 256
HY_CH = 256
HY_BANDS = 16
HY_EMB = 1 + 2 * HY_BANDS
HY_HIDDEN = 64
HY_SIN_FREQ = 1.0
HY_MIN_DECAY = math.log(1e-2) / 1.5
HY_MAX_DECAY = math.log(1e-2) / 0.3
N_EXPERTS = 16
N_EXPERT_GROUPS = 4
D_EXPERT = 512
EPS = 1e-6
LOG2E = 1.4426950408889634

LANES = 128
SUBLANES = 8
VMEM_LIMIT = 56 * 1024 * 1024


def _params(sem):
    return pltpu.CompilerParams(dimension_semantics=sem, vmem_limit_bytes=VMEM_LIMIT)


def _dot(a, b, prec=None):
    return jnp.dot(a, b, precision=prec, preferred_element_type=F32)


def _dot_nt(a, b):
    return lax.dot_general(a, b, (((1,), (1,)), ((), ())), preferred_element_type=F32)


def _ada_kernel(c_ref, w_ref, b_ref, o_ref):
    cf = c_ref[...]
    s = cf * jax.nn.sigmoid(cf)
    o_ref[...] = _dot(s, w_ref[...], HI) + b_ref[...]


def _ada(cpad, w, b):
    d = cpad.shape[1]
    n = w.shape[1]
    return pl.pallas_call(
        _ada_kernel,
        grid=(n // d,),
        in_specs=[pl.BlockSpec((SUBLANES, d), lambda j: (0, 0)),
                  pl.BlockSpec((d, d), lambda j: (0, j)),
                  pl.BlockSpec((1, d), lambda j: (0, j))],
        out_specs=pl.BlockSpec((SUBLANES, d), lambda j: (0, j)),
        out_shape=jax.ShapeDtypeStruct((SUBLANES, n), F32),
        compiler_params=_params(("arbitrary",)),
        name="ada_mod",
    )(cpad, w, b)


def _norm_proj_kernel(x_ref, g_ref, sc_ref, sh_ref, w_ref, o_ref):
    x = x_ref[...]
    ms = jnp.mean(x * x, axis=-1, keepdims=True)
    h = (x * lax.rsqrt(ms + EPS)) * g_ref[...] * (1.0 + sc_ref[...]) + sh_ref[...]
    o_ref[...] = _dot(h.astype(BF16), w_ref[...])


def _norm_proj(x, g, mod3, row_of_batch, j_shift, j_scale, w, tm):
    bsz, t, d = x.shape
    n = w.shape[1]
    return pl.pallas_call(
        _norm_proj_kernel,
        grid=(bsz, t // tm),
        in_specs=[pl.BlockSpec((None, tm, d), lambda b, i: (b, i, 0)),
                  pl.BlockSpec((1, d), lambda b, i: (0, 0)),
                  pl.BlockSpec((None, 1, d), lambda b, i: (row_of_batch(b), 0, j_scale)),
                  pl.BlockSpec((None, 1, d), lambda b, i: (row_of_batch(b), 0, j_shift)),
                  pl.BlockSpec((d, n), lambda b, i: (0, 0))],
        out_specs=pl.BlockSpec((None, tm, n), lambda b, i: (b, i, 0)),
        out_shape=jax.ShapeDtypeStruct((bsz, t, n), F32),
        compiler_params=_params(("parallel", "arbitrary")),
        name="norm_in_proj",
    )(x, g, mod3, mod3, w)


def _aprep_kernel(*refs, rope):
    if rope:
        u_ref, cos_ref, sin_ref, q_ref, kt_ref, v_ref = refs
    else:
        u_ref, q_ref, kt_ref, v_ref = refs
    u = u_ref[...]
    q = u[:, 0:256]
    k = u[:, 256:512]
    v = u[:, 512:768]
    if rope:
        cos_t = cos_ref[...]
        sin_t = sin_ref[...]
        lane = lax.broadcasted_iota(jnp.int32, cos_t.shape, 1)
        first = (lane % (2 * 16)) < 16

        def rot(x):
            halves = []
            for j in range(2):
                xh = x[:, j * LANES:(j + 1) * LANES]
                swap = jnp.where(first, pltpu.roll(xh, LANES - 16, axis=1), pltpu.roll(xh, 16, axis=1))
                halves.append(xh * cos_t + swap * sin_t)
            return jnp.concatenate(halves, axis=1)

        q = rot(q)
        k = rot(k)
    q = q * (A_QK ** -0.5 * LOG2E)
    kt = k.T
    for hc in range(2 * A_HEADS):
        q_ref[hc] = q[:, hc * A_QK:(hc + 1) * A_QK].astype(BF16)
        kt_ref[hc] = kt[hc * A_QK:(hc + 1) * A_QK, :].astype(BF16)
    lane = lax.broadcasted_iota(jnp.int32, (v.shape[0], LANES - A_V), 1)
    ones_col = jnp.where(lane == 0, 1.0, 0.0)
    for h in range(A_HEADS):
        v_ref[h] = jnp.concatenate([v[:, h * A_V:(h + 1) * A_V], ones_col], axis=1).astype(BF16)


def _rope_tables(length):
    n_freq = A_QK // 4
    inv = ROPE_BASE ** (-jnp.arange(n_freq, dtype=F32) / n_freq)
    t = jnp.arange(length)
    row = (t // GRID_W).astype(F32)
    col = (t % GRID_W).astype(F32)
    ang = jnp.concatenate([row[:, None] * inv, col[:, None] * inv], axis=-1)
    cos, sin = jnp.cos(ang), jnp.sin(ang)
    cos_t = jnp.tile(jnp.concatenate([cos, cos], axis=-1), (1, LANES // 32))
    sin_t = jnp.tile(jnp.concatenate([-sin, sin], axis=-1), (1, LANES // 32))
    return cos_t, sin_t


def _attn_prep(u, tm, rope):
    bsz, t, _ = u.shape
    nh = 2 * A_HEADS
    in_specs = [pl.BlockSpec((None, tm, 768), lambda b, i: (b, i, 0))]
    args = [u]
    if rope:
        cos_t, sin_t = _rope_tables(t)
        in_specs += [pl.BlockSpec((tm, LANES), lambda b, i: (i, 0))] * 2
        args += [cos_t, sin_t]
    return pl.pallas_call(
        functools.partial(_aprep_kernel, rope=rope),
        grid=(bsz, t // tm),
        in_specs=in_specs,
        out_specs=[pl.BlockSpec((None, nh, tm, A_QK), lambda b, i: (b, 0, i, 0)),
                   pl.BlockSpec((None, nh, A_QK, tm), lambda b, i: (b, 0, 0, i)),
                   pl.BlockSpec((None, A_HEADS, tm, LANES), lambda b, i: (b, 0, i, 0))],
        out_shape=[jax.ShapeDtypeStruct((bsz, nh, t, A_QK), BF16),
                   jax.ShapeDtypeStruct((bsz, nh, A_QK, t), BF16),
                   jax.ShapeDtypeStruct((bsz, A_HEADS, t, LANES), BF16)],
        compiler_params=_params(("parallel", "arbitrary")),
        name="attn_prep_rope" if rope else "attn_prep_ctx",
    )(*args)


QK_LOOKAHEAD = 3


def _dattn_kernel(*refs, lam_init, has_lat):
    if has_lat:
        lam_ref, g_ref, q_ref, kc_ref, vc_ref, k_ref, v_ref, o_ref, m_scr, acc_scr = refs
    else:
        lam_ref, g_ref, q_ref, kc_ref, vc_ref, o_ref, m_scr, acc_scr = refs
    ki = pl.program_id(2)
    nk = pl.num_programs(2)
    nh = 2 * A_HEADS

    def update(kt_r, v_r):
        scores = [_dot(q_ref[j], kt_r[j]) for j in range(QK_LOOKAHEAD)]
        for hc in range(nh):
            s = scores[hc]
            if hc + QK_LOOKAHEAD < nh:
                scores.append(_dot(q_ref[hc + QK_LOOKAHEAD], kt_r[hc + QK_LOOKAHEAD]))
            m_prev = m_scr[hc]
            m_new = jnp.maximum(m_prev, jnp.max(s, axis=1, keepdims=True))
            alpha = jnp.exp2(m_prev - m_new)
            p = jnp.exp2((s - m_new[:, :1]).astype(BF16))
            acc_scr[hc] = alpha * acc_scr[hc] + _dot(p, v_r[hc // 2])
            m_scr[hc] = m_new

    @pl.when(ki == 0)
    def _():
        m_scr[...] = jnp.full(m_scr.shape, -jnp.inf, F32)
        acc_scr[...] = jnp.zeros(acc_scr.shape, F32)
        update(kc_ref, vc_ref)

    if has_lat:
        @pl.when(ki > 0)
        def _():
            update(k_ref, v_ref)

    @pl.when(ki == nk - 1)
    def _():
        lv = lam_ref[...]
        lam = (jnp.exp(jnp.sum(lv[0:1] * lv[1:2], axis=1, keepdims=True))
               - jnp.exp(jnp.sum(lv[2:3] * lv[3:4], axis=1, keepdims=True)) + lam_init)
        for h in range(A_HEADS):
            a0 = acc_scr[2 * h]
            a1 = acc_scr[2 * h + 1]
            o = a0[:, :A_V] / a0[:, A_V:A_V + 1] - lam * (a1[:, :A_V] / a1[:, A_V:A_V + 1])
            ms = jnp.mean(o * o, axis=-1, keepdims=True)
            o_ref[:, h * A_V:(h + 1) * A_V] = (o * lax.rsqrt(ms + EPS)) * g_ref[...] * (1.0 - lam_init)


def _diff_attn(q, kct, vc, kt, v, lam_vecs, subln_g, lam_init, tq, tk):
    bsz, nh, t, _ = q.shape
    c = kct.shape[-1]
    has_lat = kt is not None
    nk = 1 + (kt.shape[-1] // tk if has_lat else 0)
    in_specs = [pl.BlockSpec((4, A_QK), lambda b, i, k: (0, 0)),
                pl.BlockSpec((1, A_V), lambda b, i, k: (0, 0)),
                pl.BlockSpec((None, nh, tq, A_QK), lambda b, i, k: (b, 0, i, 0)),
                pl.BlockSpec((None, nh, A_QK, c), lambda b, i, k: (b, 0, 0, 0)),
                pl.BlockSpec((None, A_HEADS, c, LANES), lambda b, i, k: (b, 0, 0, 0))]
    args = [lam_vecs, subln_g, q, kct, vc]
    if has_lat:
        in_specs += [pl.BlockSpec((None, nh, A_QK, tk), lambda b, i, k: (b, 0, 0, jnp.maximum(k - 1, 0))),
                     pl.BlockSpec((None, A_HEADS, tk, LANES), lambda b, i, k: (b, 0, jnp.maximum(k - 1, 0), 0))]
        args += [kt, v]
    return pl.pallas_call(
        functools.partial(_dattn_kernel, lam_init=lam_init, has_lat=has_lat),
        grid=(bsz, t // tq, nk),
        in_specs=in_specs,
        out_specs=pl.BlockSpec((None, tq, A_HEADS * A_V), lambda b, i, k: (b, i, 0)),
        out_shape=jax.ShapeDtypeStruct((bsz, t, A_HEADS * A_V), F32),
        scratch_shapes=[pltpu.VMEM((nh, tq, LANES), F32),
                        pltpu.VMEM((nh, tq, LANES), F32)],
        compiler_params=_params(("parallel", "parallel", "arbitrary")),
        name="diff_attn" if has_lat else "diff_attn_ctx",
    )(*args)


NB_ROWS = 8


def _nbr_bias(rpb):
    cols = jnp.arange(GRID_W)
    c0 = jnp.clip(cols - WIN_C // 2, 0, GRID_W - WIN_C)
    in_win = (cols[None, :] >= c0[:, None]) & (cols[None, :] < c0[:, None] + WIN_C)
    dc = jnp.clip(cols[None, :] - cols[:, None], -(WIN_C - 1), WIN_C - 1) + (WIN_C - 1)
    onehot = (dc[None] == jnp.arange(2 * WIN_C - 1)[:, None, None]).astype(F32)
    g = jnp.einsum("hab,bqk->haqk", rpb.astype(F32), onehot, precision=HI)
    g = jnp.where(in_win[None, None], g, -jnp.inf)
    b = jnp.stack([g[:, a0:a0 + WIN_R] for a0 in range(WIN_R)], axis=0)
    b = jnp.transpose(b, (0, 1, 3, 2, 4))
    return b.reshape(WIN_R, B_HEADS, GRID_W, WIN_R * GRID_W)


def _nbr_kernel(q_ref, kp_ref, kc_ref, kn_ref, vp_ref, vcur_ref, vn_ref, kctx_ref, vctx_ref, bias_ref,
                o_ref, kwin, vwin, kcx, vcx, *, n_rows):
    rb = pl.program_id(1)
    blk = NB_ROWS * GRID_W
    scale = B_DIM ** -0.5
    for h in range(B_HEADS):
        sl = slice(h * B_DIM, (h + 1) * B_DIM)
        for j, (kr, vr) in enumerate(((kp_ref, vp_ref), (kc_ref, vcur_ref), (kn_ref, vn_ref))):
            kwin[h, j * blk:(j + 1) * blk, :] = kr[:, sl].astype(BF16)
            vwin[h, j * blk:(j + 1) * blk, :] = vr[:, sl].astype(BF16)
        kcx[h] = kctx_ref[:, sl].astype(BF16)
        vcx[h] = vctx_ref[:, sl].astype(BF16)

    def window(rr):
        r = rb * NB_ROWS + rr
        r0 = jnp.clip(r - WIN_R // 2, 0, n_rows - WIN_R)
        off = pl.multiple_of((r0 - (rb - 1) * NB_ROWS) * GRID_W, GRID_W)
        return off, r0 - r + (WIN_R - 1)

    def scores(rr):
        off, a0 = window(rr)
        qrow = q_ref[rr * GRID_W:(rr + 1) * GRID_W, :]
        out = []
        for h in range(B_HEADS):
            qh = qrow[:, h * B_DIM:(h + 1) * B_DIM].astype(BF16)
            s = _dot_nt(qh, kwin[h, pl.ds(off, WIN_R * GRID_W), :]) * scale + bias_ref[a0, h]
            out.append((s, _dot_nt(qh, kcx[h]) * scale))
        return out

    nxt = scores(0)
    for rr in range(NB_ROWS):
        cur = nxt
        if rr + 1 < NB_ROWS:
            nxt = scores(rr + 1)
        off, _ = window(rr)
        outs = []
        for h in range(B_HEADS):
            s, sc = cur[h]
            m = jnp.maximum(jnp.max(s, axis=1, keepdims=True), jnp.max(sc, axis=1, keepdims=True))
            p = jnp.exp(s - m)
            pc = jnp.exp(sc - m)
            l = jnp.sum(p, axis=1, keepdims=True) + jnp.sum(pc, axis=1, keepdims=True)
            o = _dot(p.astype(BF16), vwin[h, pl.ds(off, WIN_R * GRID_W), :]) + _dot(pc.astype(BF16), vcx[h])
            outs.append(o / l)
        o_ref[rr * GRID_W:(rr + 1) * GRID_W, :] = jnp.concatenate(outs, axis=1)


def _nbr_attn(u, uc, bias8):
    bsz, s, _ = u.shape
    c = uc.shape[1]
    n_rows = s // GRID_W
    nb = n_rows // NB_ROWS
    blk = NB_ROWS * GRID_W
    w = B_HEADS * B_DIM

    def spec(col, shift):
        return pl.BlockSpec((None, blk, w), lambda b, i: (b, jnp.clip(i + shift, 0, nb - 1), col))

    return pl.pallas_call(
        functools.partial(_nbr_kernel, n_rows=n_rows),
        grid=(bsz, nb),
        in_specs=[spec(3, 0), spec(4, -1), spec(4, 0), spec(4, 1), spec(5, -1), spec(5, 0), spec(5, 1),
                  pl.BlockSpec((None, c, w), lambda b, i: (b, 0, 4)),
                  pl.BlockSpec((None, c, w), lambda b, i: (b, 0, 5)),
                  pl.BlockSpec(bias8.shape, lambda b, i: (0, 0, 0, 0))],
        out_specs=pl.BlockSpec((None, blk, w), lambda b, i: (b, i, 0)),
        out_shape=jax.ShapeDtypeStruct((bsz, s, w), F32),
        scratch_shapes=[pltpu.VMEM((B_HEADS, 3 * blk, B_DIM), BF16),
                        pltpu.VMEM((B_HEADS, 3 * blk, B_DIM), BF16),
                        pltpu.VMEM((B_HEADS, c, B_DIM), BF16),
                        pltpu.VMEM((B_HEADS, c, B_DIM), BF16)],
        compiler_params=_params(("parallel", "arbitrary")),
        name="nbr_attn",
    )(u, u, u, u, u, u, u, uc, uc, bias8)


def _nbr_ctx_kernel(q_ref, k_ref, v_ref, o_ref):
    scale = B_DIM ** -0.5
    outs = []
    for h in range(B_HEADS):
        sl = slice(h * B_DIM, (h + 1) * B_DIM)
        s = _dot_nt(q_ref[:, sl].astype(BF16), k_ref[:, sl].astype(BF16)) * scale
        m = jnp.max(s, axis=1, keepdims=True)
        p = jnp.exp(s - m)
        l = jnp.sum(p, axis=1, keepdims=True)
        outs.append(_dot(p.astype(BF16), v_ref[:, sl].astype(BF16)) / l)
    o_ref[...] = jnp.concatenate(outs, axis=1)


def _nbr_ctx_attn(uc):
    bsz, c, _ = uc.shape
    w = B_HEADS * B_DIM
    return pl.pallas_call(
        _nbr_ctx_kernel,
        grid=(bsz,),
        in_specs=[pl.BlockSpec((None, c, w), lambda b: (b, 0, 3)),
                  pl.BlockSpec((None, c, w), lambda b: (b, 0, 4)),
                  pl.BlockSpec((None, c, w), lambda b: (b, 0, 5))],
        out_specs=pl.BlockSpec((None, c, w), lambda b: (b, 0, 0)),
        out_shape=jax.ShapeDtypeStruct((bsz, c, w), F32),
        compiler_params=_params(("arbitrary",)),
        name="nbr_attn_ctx",
    )(uc, uc, uc)


HALO = SUBLANES


def _halo_specs(tm, length, col, width):
    nt = length // tm
    per = tm // HALO
    last = length // HALO - 1
    return [pl.BlockSpec((None, HALO, width), lambda b, i, *_: (b, jnp.maximum(i * per - 1, 0), col)),
            pl.BlockSpec((None, tm, width), lambda b, i, *_: (b, i, col)),
            pl.BlockSpec((None, HALO, width), lambda b, i, *_: (b, jnp.minimum((i + 1) * per, last), col))], nt


def _fill_halo(buf, prev_ref, cur_ref, next_ref, i, nt, tm):
    zero = jnp.zeros(prev_ref.shape, F32)
    buf[0:HALO, :] = jnp.where(i > 0, prev_ref[...], zero)
    buf[HALO:HALO + tm, :] = cur_ref[...]
    buf[HALO + tm:, :] = jnp.where(i < nt - 1, next_ref[...], zero)


def _pool_kernel(prev_ref, cur_ref, next_ref, w_ref, ps_ref, o_ref, buf, *, tm, nt, length):
    i = pl.program_id(1)
    _fill_halo(buf, prev_ref, cur_ref, next_ref, i, nt, tm)

    def sh(j):
        return buf[HALO + j:HALO + j + tm, :]

    u = sh(0)
    sums = []
    acc = None
    lo, hi = 0, 0
    for w in POOL_SIZES:
        for j in list(range(-(w // 2), lo)) + list(range(hi, w // 2)):
            acc = sh(j) if acc is None else acc + sh(j)
        lo, hi = -(w // 2), w // 2
        sums.append(acc)
    lane = lax.broadcasted_iota(jnp.int32, (tm, D_GROUP), 1)
    t = (i * tm + lax.broadcasted_iota(jnp.int32, (tm, D_GROUP), 0))
    wsum = sums[-1]
    half = jnp.full((tm, D_GROUP), POOL_SIZES[-1] // 2, jnp.int32)
    for g in range(len(POOL_SIZES) - 2, -1, -1):
        sel = lane < (g + 1) * POOL_CH
        wsum = jnp.where(sel, sums[g], wsum)
        half = jnp.where(sel, POOL_SIZES[g] // 2, half)
    cnt = (jnp.minimum(t + half, length) - jnp.maximum(t - half, 0)).astype(F32)
    d = wsum / cnt - u
    o_ref[...] = _dot(d.astype(BF16), w_ref[...]) * ps_ref[...]


def _pool_mix(u, wbd, pool_scale, tm):
    bsz, length, _ = u.shape
    specs, nt = _halo_specs(tm, length, 6, D_GROUP)
    return pl.pallas_call(
        functools.partial(_pool_kernel, tm=tm, nt=nt, length=length),
        grid=(bsz, nt),
        in_specs=specs + [pl.BlockSpec((D_GROUP, D_GROUP), lambda b, i: (0, 0)),
                          pl.BlockSpec((1, D_GROUP), lambda b, i: (0, 0))],
        out_specs=pl.BlockSpec((None, tm, D_GROUP), lambda b, i: (b, i, 0)),
        out_shape=jax.ShapeDtypeStruct((bsz, length, D_GROUP), F32),
        scratch_shapes=[pltpu.VMEM((tm + 2 * HALO, D_GROUP), F32)],
        compiler_params=_params(("parallel", "arbitrary")),
        name="pool_mix",
    )(u, u, u, wbd, pool_scale)


def _hy_short_kernel(prev_ref, cur_ref, next_ref, w_ref, b_ref, o_ref, buf, *, tm, nt):
    i = pl.program_id(1)
    _fill_halo(buf, prev_ref, cur_ref, next_ref, i, nt, tm)
    w = w_ref[...]
    y = (buf[HALO - 1:HALO - 1 + tm, :] * w[0:1] + buf[HALO:HALO + tm, :] * w[1:2]
         + buf[HALO + 1:HALO + 1 + tm, :] * w[2:3] + b_ref[...])
    o_ref[...] = y.T


def _hy_short(u, w_short, b_short, tm):
    bsz, length, _ = u.shape
    nt = length // tm
    per = tm // HALO
    last = length // HALO - 1
    c0 = 7
    in_specs = [pl.BlockSpec((None, HALO, HY_CH), lambda b, i, j: (b, jnp.maximum(i * per - 1, 0), c0 + j)),
                pl.BlockSpec((None, tm, HY_CH), lambda b, i, j: (b, i, c0 + j)),
                pl.BlockSpec((None, HALO, HY_CH), lambda b, i, j: (b, jnp.minimum((i + 1) * per, last), c0 + j)),
                pl.BlockSpec((3, HY_CH), lambda b, i, j: (0, j)),
                pl.BlockSpec((1, HY_CH), lambda b, i, j: (0, j))]
    return pl.pallas_call(
        functools.partial(_hy_short_kernel, tm=tm, nt=nt),
        grid=(bsz, nt, 3),
        in_specs=in_specs,
        out_specs=pl.BlockSpec((None, None, HY_CH, tm), lambda b, i, j: (j, b, 0, i)),
        out_shape=jax.ShapeDtypeStruct((3, bsz, HY_CH, length), F32),
        scratch_shapes=[pltpu.VMEM((tm + 2 * HALO, HY_CH), F32)],
        compiler_params=_params(("parallel", "arbitrary", "arbitrary")),
        name="hyena_short_conv",
    )(u, u, u, w_short, b_short)


HY_FEAT = 40


def _hy_filter_kernel(band_ref, w1_ref, b1_ref, w2_ref, b2_ref, w3_ref, b3_ref, dl_ref,
                      k_ref, ssq_ref, nrm_ref, *, tp, length):
    i = pl.program_id(0)
    n_i = pl.num_programs(0)
    m = i * tp + lax.broadcasted_iota(jnp.int32, (1, tp), 1)
    t = jnp.where(m <= length, m, 2 * length - m).astype(F32)
    t_norm = t / max(length - 1, 1)
    ang = ((2.0 * math.pi / length) * t) * band_ref[...]
    row = lax.broadcasted_iota(jnp.int32, (HY_FEAT, tp), 0)
    z = jnp.where(row == 0, t_norm,
                  jnp.where(row <= HY_BANDS, jnp.cos(ang), jnp.where(row < HY_EMB, jnp.sin(ang), 0.0)))
    z = jnp.concatenate([z, jnp.zeros((LANES - HY_FEAT, tp), F32)], axis=0)
    h = jnp.sin(HY_SIN_FREQ * (_dot(w1_ref[...], z, HI) + b1_ref[...]))
    h = jnp.sin(HY_SIN_FREQ * (_dot(w2_ref[...], h, HI) + b2_ref[...]))
    hh, hl = _split_bf16(h)
    h = _dot(w3_ref[...], jnp.concatenate([hh, hl, hh], axis=0)) + b3_ref[...]
    h = h * jnp.exp(-t_norm * dl_ref[...])

    @pl.when(i == 0)
    def _():
        ssq_ref[...] = jnp.zeros(ssq_ref.shape, F32)

    for o in range(2):
        fwd = h[o * 2 * HY_CH:o * 2 * HY_CH + HY_CH]
        bwd = h[o * 2 * HY_CH + HY_CH:(o + 1) * 2 * HY_CH]
        k = jnp.where(m < length, fwd, jnp.where(m == length, 0.0, bwd))
        k_ref[o] = k
        extra = jnp.where(m == 0, bwd * bwd, 0.0)
        ssq_ref[o] += jnp.sum(k * k + extra, axis=1, keepdims=True)

    @pl.when(i == n_i - 1)
    def _():
        nrm_ref[...] = lax.rsqrt(ssq_ref[...] + EPS)


def _hy_filters(length, w1, b1, w2, b2, w3, b3, tp):
    bands = jnp.linspace(1e-4, HY_BANDS - 1, HY_BANDS, dtype=F32)
    band_col = jnp.concatenate([jnp.zeros((1,), F32), bands, bands,
                                jnp.zeros((HY_FEAT - HY_EMB,), F32)])[:, None]
    deltas = jnp.abs(jnp.linspace(HY_MIN_DECAY, HY_MAX_DECAY, HY_CH, dtype=F32))
    dl_col = jnp.tile(deltas, 4)[:, None]
    w1t = jnp.pad(w1.astype(F32).T, ((0, 0), (0, LANES - HY_EMB)))
    full = lambda shape: pl.BlockSpec(shape, lambda i: (0,) * len(shape))
    n = 2 * length
    return pl.pallas_call(
        functools.partial(_hy_filter_kernel, tp=tp, length=length),
        grid=(n // tp,),
        in_specs=[full((HY_FEAT, 1)), full((HY_HIDDEN, LANES)), full((HY_HIDDEN, 1)),
                  full((HY_HIDDEN, HY_HIDDEN)), full((HY_HIDDEN, 1)),
                  full((4 * HY_CH, 3 * HY_HIDDEN)), full((4 * HY_CH, 1)), full((4 * HY_CH, 1))],
        out_specs=[pl.BlockSpec((2, HY_CH, tp), lambda i: (0, 0, i)),
                   full((2, HY_CH, 1)), full((2, HY_CH, 1))],
        out_shape=[jax.ShapeDtypeStruct((2, HY_CH, n), F32),
                   jax.ShapeDtypeStruct((2, HY_CH, 1), F32),
                   jax.ShapeDtypeStruct((2, HY_CH, 1), F32)],
        compiler_params=_params(("arbitrary",)),
        name="hyena_filters",
    )(band_col, w1t, b1.astype(F32)[:, None], w2.astype(F32).T, b2.astype(F32)[:, None],
      _cat3(w3.astype(F32).T, 1), b3.astype(F32)[:, None], dl_col)


def _dft_consts(n1, n2):
    n = n1 * n2
    a1 = 2.0 * np.pi * ((np.arange(n1)[:, None] * np.arange(n1)[None, :]) % n1) / n1
    c1, s1 = np.cos(a1), np.sin(a1)
    a2 = 2.0 * np.pi * ((np.arange(n2)[:, None] * np.arange(n2)[None, :]) % n2) / n2
    c2, s2 = np.cos(a2), np.sin(a2)
    at = 2.0 * np.pi * ((np.arange(n1)[:, None] * np.arange(n2)[None, :]) % n) / n
    f1_full = np.concatenate([c1, -s1], axis=0)
    f1_half = f1_full[:, :n1 // 2]
    g1 = np.concatenate([c1[:n1 // 2], -s1[:n1 // 2]], axis=1)
    w2f = np.block([[c2, -s2], [s2, c2]])
    w2i = np.block([[c2, s2], [-s2, c2]])
    f = lambda x: jnp.asarray(x, F32)
    return dict(f1_full=f(f1_full), f1_half=f(f1_half), g1=f(g1), w2f=f(w2f), w2i=f(w2i),
                tr=f(np.cos(at)), ti=f(-np.sin(at)))


def _fft_fwd(slabs, f1, tr, ti, w2f, stack, n1, n2, prec):
    for c, x in enumerate(slabs):
        a = _dot(f1, x.astype(stack.dtype), prec)
        ar, ai = a[:n1], a[n1:]
        stack[c * n1:(c + 1) * n1, 0:n2] = (ar * tr - ai * ti).astype(stack.dtype)
        stack[c * n1:(c + 1) * n1, n2:2 * n2] = (ar * ti + ai * tr).astype(stack.dtype)
    return _dot(stack[...], w2f, prec)


def _split_bf16(x):
    hi = x.astype(BF16)
    return hi, (x - hi.astype(F32)).astype(BF16)


def _cat3(x, axis):
    hi, lo = _split_bf16(x)
    return jnp.concatenate([hi, hi, lo], axis=axis)


def _hy_spec_kernel(nrm_ref, k_ref, f1_ref, tr_ref, ti_ref, w2f_ref, o_ref, stack, *, cg, n1, n2):
    o = pl.program_id(0)
    g = pl.program_id(1)
    tr = tr_ref[...]
    ti = ti_ref[...]
    for c in range(cg):
        hi, lo = _split_bf16(k_ref[c])
        a = _dot(f1_ref[...], jnp.concatenate([hi, lo, hi], axis=0))
        ar, ai = a[:n1], a[n1:]
        sh, sl = _split_bf16(jnp.concatenate([ar * tr - ai * ti, ar * ti + ai * tr], axis=1))
        stack[c * n1:(c + 1) * n1, :] = jnp.concatenate([sh, sl, sh], axis=1)
    x = _dot(stack[...], w2f_ref[...])
    for c in range(cg):
        sc = nrm_ref[o * HY_CH + g * cg + c] * (1.0 / (n1 * n2))
        xc = x[c * n1:(c + 1) * n1] * sc
        o_ref[c, 0] = xc[:, :n2]
        o_ref[c, 1] = xc[:, n2:]


def _hy_spec(k4, nrm_flat, dc, cg, n1, n2):
    full = lambda shape: pl.BlockSpec(shape, lambda o, g: (0,) * len(shape))
    return pl.pallas_call(
        functools.partial(_hy_spec_kernel, cg=cg, n1=n1, n2=n2),
        grid=(2, HY_CH // cg),
        in_specs=[pl.BlockSpec(memory_space=pltpu.SMEM),
                  pl.BlockSpec((None, cg, n1, n2), lambda o, g: (o, g, 0, 0)),
                  full((2 * n1, 3 * n1)), full((n1, n2)), full((n1, n2)), full((6 * n2, 2 * n2))],
        out_specs=pl.BlockSpec((None, cg, 2, n1, n2), lambda o, g: (o, g, 0, 0, 0)),
        out_shape=jax.ShapeDtypeStruct((2, HY_CH, 2, n1, n2), F32),
        scratch_shapes=[pltpu.VMEM((cg * n1, 6 * n2), BF16)],
        compiler_params=_params(("parallel", "arbitrary")),
        name="hyena_filter_spectrum",
    )(nrm_flat, k4, _cat3(dc["f1_full"], 1), dc["tr"], dc["ti"], _cat3(dc["w2f"], 0))


def _hy_conv_kernel(skip_ref, x_ref, ks_ref, f1_ref, g1_ref, tr_ref, ti_ref, w2f_ref, w2i_ref,
                    o_ref, stack, *, cg, n1, n2):
    g = pl.program_id(1)
    tr = tr_ref[...]
    ti = ti_ref[...]

    def conv(slabs, order):
        x = _fft_fwd(slabs, f1_ref[...], tr, ti, w2f_ref[...], stack, n1, n2, None)
        for c in range(cg):
            xr, xi = x[c * n1:(c + 1) * n1, :n2], x[c * n1:(c + 1) * n1, n2:]
            kr, ki = ks_ref[order, c, 0], ks_ref[order, c, 1]
            stack[c * n1:(c + 1) * n1, 0:n2] = (xr * kr - xi * ki).astype(BF16)
            stack[c * n1:(c + 1) * n1, n2:2 * n2] = (xr * ki + xi * kr).astype(BF16)
        bm = _dot(stack[...], w2i_ref[...])
        outs = []
        for c in range(cg):
            br, bi = bm[c * n1:(c + 1) * n1, :n2], bm[c * n1:(c + 1) * n1, n2:]
            b2 = jnp.concatenate([br * tr + bi * ti, bi * tr - br * ti], axis=0)
            y = _dot(g1_ref[...], b2.astype(BF16))
            outs.append(y + slabs[c] * skip_ref[order * HY_CH + g * cg + c])
        return outs

    v = [x_ref[2, c] for c in range(cg)]
    y0 = conv(v, 0)
    z = [x_ref[0, c] * y0[c] for c in range(cg)]
    y1 = conv(z, 1)
    for c in range(cg):
        o_ref[c] = x_ref[1, c] * y1[c]


def _hy_conv(x4, kspec, skip_flat, dc, cg, n1, n2):
    bsz = x4.shape[1]
    full = lambda shape: pl.BlockSpec(shape, lambda b, g: (0,) * len(shape))
    return pl.pallas_call(
        functools.partial(_hy_conv_kernel, cg=cg, n1=n1, n2=n2),
        grid=(bsz, HY_CH // cg),
        in_specs=[pl.BlockSpec(memory_space=pltpu.SMEM),
                  pl.BlockSpec((3, None, cg, n1 // 2, n2), lambda b, g: (0, b, g, 0, 0)),
                  pl.BlockSpec((2, cg, 2, n1, n2), lambda b, g: (0, g, 0, 0, 0)),
                  full((2 * n1, n1 // 2)), full((n1 // 2, 2 * n1)), full((n1, n2)), full((n1, n2)),
                  full((2 * n2, 2 * n2)), full((2 * n2, 2 * n2))],
        out_specs=pl.BlockSpec((None, cg, n1 // 2, n2), lambda b, g: (b, g, 0, 0)),
        out_shape=jax.ShapeDtypeStruct((bsz, HY_CH, n1 // 2, n2), F32),
        scratch_shapes=[pltpu.VMEM((cg * n1, 2 * n2), BF16)],
        compiler_params=_params(("parallel", "arbitrary")),
        name="hyena_long_conv",
    )(skip_flat, x4, kspec, dc["f1_half"].astype(BF16), dc["g1"].astype(BF16), dc["tr"], dc["ti"],
      dc["w2f"].astype(BF16), dc["w2i"].astype(BF16))


def _hy_ctx_kernel(x_ref, k_ref, nrm_ref, skip_ref, fc_ref, fs_ref, o_ref, *, c):
    fc = fc_ref[...]
    fs = fs_ref[...]
    inv_n = 1.0 / (2 * c)

    def conv(x, order):
        kk = k_ref[order]
        kr, ki = _dot(kk, fc, HI), -_dot(kk, fs, HI)
        xr, xi = _dot(x, fc[:c], HI), -_dot(x, fs[:c], HI)
        yr, yi = xr * kr - xi * ki, xr * ki + xi * kr
        y = (_dot(yr, fc[:, :c], HI) - _dot(yi, fs[:, :c], HI)) * inv_n
        return y * nrm_ref[order] + x * skip_ref[order]

    z = x_ref[0] * conv(x_ref[2], 0)
    o_ref[...] = x_ref[1] * conv(z, 1)


def _hy_ctx_conv(xt, kt, nrm, skip_col):
    _, bsz, ch, c = xt.shape
    n = 2 * c
    ang = 2.0 * np.pi * ((np.arange(n)[:, None] * np.arange(n)[None, :]) % n) / n
    fc, fs = jnp.asarray(np.cos(ang), F32), jnp.asarray(np.sin(ang), F32)
    full = lambda shape: pl.BlockSpec(shape, lambda b: (0,) * len(shape))
    return pl.pallas_call(
        functools.partial(_hy_ctx_kernel, c=c),
        grid=(bsz,),
        in_specs=[pl.BlockSpec((3, None, ch, c), lambda b: (0, b, 0, 0)),
                  full((2, ch, n)), full((2, ch, 1)), full((2, ch, 1)), full((n, n)), full((n, n))],
        out_specs=pl.BlockSpec((None, ch, c), lambda b: (b, 0, 0)),
        out_shape=jax.ShapeDtypeStruct((bsz, ch, c), F32),
        compiler_params=_params(("arbitrary",)),
        name="hyena_ctx_conv",
    )(xt, kt, nrm, skip_col, fc, fs)


def _out_proj_kernel(a_ref, b_ref, p_ref, ht_ref, x_ref, g_ref, w_ref, o_ref):
    w = D_GROUP
    acc = _dot(a_ref[...].astype(BF16), w_ref[0:w])
    acc += _dot(b_ref[...].astype(BF16), w_ref[w:2 * w])
    acc += _dot(p_ref[...].astype(BF16), w_ref[2 * w:3 * w])
    acc += _dot(ht_ref[...].T.astype(BF16), w_ref[3 * w:4 * w])
    o_ref[...] = x_ref[...] + g_ref[...] * acc


def _out_proj(a, b, p, ht, x, mod3, row_of_batch, j_gate, w_out, tm):
    bsz, t, d = x.shape
    w = D_GROUP
    tok = pl.BlockSpec((None, tm, w), lambda bb, i: (bb, i, 0))
    return pl.pallas_call(
        _out_proj_kernel,
        grid=(bsz, t // tm),
        in_specs=[tok, tok, tok,
                  pl.BlockSpec((None, w, tm), lambda bb, i: (bb, 0, i)),
                  pl.BlockSpec((None, tm, d), lambda bb, i: (bb, i, 0)),
                  pl.BlockSpec((None, 1, d), lambda bb, i: (row_of_batch(bb), 0, j_gate)),
                  pl.BlockSpec((4 * w, d), lambda bb, i: (0, 0))],
        out_specs=pl.BlockSpec((None, tm, d), lambda bb, i: (bb, i, 0)),
        out_shape=jax.ShapeDtypeStruct((bsz, t, d), F32),
        compiler_params=_params(("parallel", "arbitrary")),
        name="out_proj_residual",
    )(a, b, p, ht, x, mod3, w_out)


MOE_ROWS = 256


def _route(x, g, sc, sh, rw3, rbc):
    per_group = N_EXPERTS // N_EXPERT_GROUPS
    tm = x.shape[0]
    ms = jnp.mean(x * x, axis=-1, keepdims=True)
    h = (x * lax.rsqrt(ms + EPS)) * g * (1.0 + sc) + sh
    hh, hl = _split_bf16(h)
    d = x.shape[1]
    logits = (_dot(hh, rw3[0:d]) + _dot(hl, rw3[d:2 * d]) + _dot(hh, rw3[2 * d:3 * d])).T[:N_EXPERTS]
    ex = jnp.exp(logits - jnp.max(logits, axis=0, keepdims=True))
    scores = ex / jnp.sum(ex, axis=0, keepdims=True)
    sel = scores + rbc
    srow = [sel[r:r + 1] for r in range(N_EXPERTS)]
    best = None
    for grp in range(N_EXPERT_GROUPS):
        rows = list(range(grp * per_group, (grp + 1) * per_group))
        v1 = functools.reduce(jnp.maximum, [srow[r] for r in rows])
        i1 = jnp.full((1, tm), rows[-1], jnp.int32)
        for r in reversed(rows[:-1]):
            i1 = jnp.where(srow[r] == v1, r, i1)
        rest = [jnp.where(i1 == r, -jnp.inf, srow[r]) for r in rows]
        v2 = functools.reduce(jnp.maximum, rest)
        i2 = jnp.full((1, tm), rows[-1], jnp.int32)
        for k in reversed(range(per_group - 1)):
            i2 = jnp.where(rest[k] == v2, rows[k], i2)
        gs = v1 + v2
        if best is None:
            best, e1, e2 = gs, i1, i2
        else:
            upd = gs > best
            best = jnp.where(upd, gs, best)
            e1 = jnp.where(upd, i1, e1)
            e2 = jnp.where(upd, i2, e2)
    row = lax.broadcasted_iota(jnp.int32, (N_EXPERTS, tm), 0)
    w1 = jnp.sum(jnp.where(row == e1, scores, 0.0), axis=0, keepdims=True)
    w2 = jnp.sum(jnp.where(row == e2, scores, 0.0), axis=0, keepdims=True)
    tot = w1 + w2
    gates_t = jnp.where(row == e1, w1 / tot, 0.0) + jnp.where(row == e2, w2 / tot, 0.0)
    gates_t = jnp.concatenate([gates_t, jnp.zeros((LANES - N_EXPERTS, tm), F32)], axis=0)
    return h, gates_t.T


def _moe_kernel(x_ref, g_ref, sc_ref, sh_ref, gate_ref, rw_ref, rb_ref, w1_ref, w3_ref, w2_ref, fg_ref,
                o_ref, h_scr, gates_scr, acc_scr, *, final):
    e = pl.program_id(2)
    tm = x_ref.shape[0]

    @pl.when(e == 0)
    def _():
        h, gates = _route(x_ref[...], g_ref[...], sc_ref[...], sh_ref[...], rw_ref[...], rb_ref[...])
        h_scr[...] = h.astype(BF16)
        gates_scr[...] = gates
        acc_scr[...] = jnp.zeros(acc_scr.shape, F32)

    lane = lax.broadcasted_iota(jnp.int32, (tm, LANES), 1)
    ge = jnp.sum(jnp.where(lane == e, gates_scr[...], 0.0), axis=1, keepdims=True)

    def up(j):
        hb = h_scr[j * MOE_ROWS:(j + 1) * MOE_ROWS, :]
        return _dot(hb, w1_ref[...]), _dot(hb, w3_ref[...])

    nxt = up(0)
    for j in range(tm // MOE_ROWS):
        a, b = nxt
        if (j + 1) * MOE_ROWS < tm:
            nxt = up(j + 1)
        rows = slice(j * MOE_ROWS, (j + 1) * MOE_ROWS)
        act = (a * jax.nn.sigmoid(a)) * b
        acc_scr[rows, :] += ge[rows] * _dot(act.astype(BF16), w2_ref[...])

    @pl.when(e == N_EXPERTS - 1)
    def _():
        y = x_ref[...] + gate_ref[...] * acc_scr[...]
        if final:
            ms = jnp.mean(y * y, axis=-1, keepdims=True)
            y = (y * lax.rsqrt(ms + EPS)) * fg_ref[...]
        o_ref[...] = y


def _moe(x, g, mod3, row_of_batch, j_shift, j_scale, j_gate, rw, rb, w1, w3, w2, final_g, final, tm):
    bsz, t, d = x.shape
    vec = lambda j: pl.BlockSpec((None, 1, d), lambda b, i, e: (row_of_batch(b), 0, j))
    full = lambda shape: pl.BlockSpec(shape, lambda b, i, e: (0,) * len(shape))
    return pl.pallas_call(
        functools.partial(_moe_kernel, final=final),
        grid=(bsz, t // tm, N_EXPERTS),
        in_specs=[pl.BlockSpec((None, tm, d), lambda b, i, e: (b, i, 0)),
                  full((1, d)), vec(j_scale), vec(j_shift), vec(j_gate),
                  full((3 * d, LANES)), full((N_EXPERTS, 1)),
                  pl.BlockSpec((None, d, D_EXPERT), lambda b, i, e: (e, 0, 0)),
                  pl.BlockSpec((None, d, D_EXPERT), lambda b, i, e: (e, 0, 0)),
                  pl.BlockSpec((None, D_EXPERT, d), lambda b, i, e: (e, 0, 0)),
                  full((1, d))],
        out_specs=pl.BlockSpec((None, tm, d), lambda b, i, e: (b, i, 0)),
        out_shape=jax.ShapeDtypeStruct((bsz, t, d), F32),
        scratch_shapes=[pltpu.VMEM((tm, d), BF16), pltpu.VMEM((tm, LANES), F32), pltpu.VMEM((tm, d), F32)],
        compiler_params=_params(("parallel", "parallel", "arbitrary")),
        name="moe_final" if final else "moe",
    )(x, g, mod3, mod3, mod3, rw, rb, w1, w3, w2, final_g)


MOE_SUB = 512
MOE_CAP = 128
MOE_BLK = 2048
MOE_PAIR = 2


def _moe_route_kernel(x_ref, g_ref, sc_ref, sh_ref, rw_ref, rb_ref, h_ref, gates_ref, cnt_ref):
    h, gates = _route(x_ref[...], g_ref[...], sc_ref[...], sh_ref[...], rw_ref[...], rb_ref[...])
    h_ref[...] = h.astype(BF16)
    gates_ref[...] = gates
    cnt = jnp.sum(jnp.where(gates > 0.0, 1.0, 0.0), axis=0, keepdims=True)
    cnt_ref[...] = jnp.broadcast_to(cnt, cnt_ref.shape)


def _moe_route(x, g, mod3, row_of_batch, j_shift, j_scale, rw, rb):
    bsz, t, d = x.shape
    tm = MOE_SUB
    vec = lambda j: pl.BlockSpec((None, 1, d), lambda b, i: (row_of_batch(b), 0, j))
    full = lambda shape: pl.BlockSpec(shape, lambda b, i: (0,) * len(shape))
    return pl.pallas_call(
        _moe_route_kernel,
        grid=(bsz, t // tm),
        in_specs=[pl.BlockSpec((None, tm, d), lambda b, i: (b, i, 0)),
                  full((1, d)), vec(j_scale), vec(j_shift), full((3 * d, LANES)), full((N_EXPERTS, 1))],
        out_specs=[pl.BlockSpec((None, tm, d), lambda b, i: (b, i, 0)),
                   pl.BlockSpec((None, tm, LANES), lambda b, i: (b, i, 0)),
                   pl.BlockSpec((None, None, SUBLANES, LANES), lambda b, i: (b, i, 0, 0))],
        out_shape=[jax.ShapeDtypeStruct((bsz, t, d), BF16),
                   jax.ShapeDtypeStruct((bsz, t, LANES), F32),
                   jax.ShapeDtypeStruct((bsz, t // tm, SUBLANES, LANES), F32)],
        compiler_params=_params(("parallel", "arbitrary")),
        name="moe_route",
    )(x, g, mod3, mod3, rw, rb)


def _moe_routed_kernel(np_ref, h_ref, gates_ref, x_ref, gate_ref, w1_ref, w3_ref, w2_ref, fg_ref,
                       o_ref, rank_scr, rankt_scr, gatet_scr, xc_scr, *, final, nblk):
    b = pl.program_id(0)
    i = pl.program_id(1)
    e = pl.program_id(2)
    nsub = h_ref.shape[0] // MOE_SUB
    sub = lambda j: slice(j * MOE_SUB, (j + 1) * MOE_SUB)

    @pl.when(e == 0)
    def _():
        o_ref[...] = jnp.zeros(o_ref.shape, F32)
        r = lax.broadcasted_iota(jnp.int32, (MOE_SUB, MOE_SUB), 0)
        c = lax.broadcasted_iota(jnp.int32, (MOE_SUB, MOE_SUB), 1)
        ltri = jnp.where(c <= r, 1.0, 0.0).astype(BF16)
        utri = jnp.where(r <= c, 1.0, 0.0).astype(BF16)
        for j in range(nsub):
            gt = gates_ref[sub(j), :]
            rank_scr[j] = _dot(ltri, jnp.where(gt > 0.0, 1.0, 0.0).astype(BF16))
            gtt = gt.T[:N_EXPERTS]
            gatet_scr[j] = gtt
            rankt_scr[j] = _dot(jnp.where(gtt > 0.0, 1.0, 0.0).astype(BF16), utri)

    lane = lax.broadcasted_iota(jnp.int32, (MOE_SUB, LANES), 1)
    slot_lane = lax.broadcasted_iota(jnp.int32, (MOE_SUB, MOE_PAIR * MOE_CAP), 1)
    second = slot_lane >= MOE_CAP

    def one_pass(p, carry):
        base = (p * MOE_CAP + 1).astype(F32)
        slot_r = lax.broadcasted_iota(jnp.int32, (MOE_CAP, MOE_SUB), 0).astype(F32) + base
        slot_c = jnp.where(second, slot_lane - MOE_CAP, slot_lane).astype(F32) + base
        ys = []
        gcs = [[], []]
        for j in range(nsub):
            picks = []
            for k in range(MOE_PAIR):
                ex = e * MOE_PAIR + k
                rr = rankt_scr[j, pl.ds(ex, 1), :]
                gr = gatet_scr[j, pl.ds(ex, 1), :]
                pick = (rr == slot_r) & (gr > 0.0)
                picks.append(jnp.where(pick, 1.0, 0.0).astype(BF16))
                gcs[k].append(jnp.sum(jnp.where(pick, gr, 0.0), axis=1, keepdims=True))
            xcj = _dot(jnp.concatenate(picks, axis=0), h_ref[sub(j), :]).astype(BF16)
            for k in range(MOE_PAIR):
                xc_scr[k, j * MOE_CAP:(j + 1) * MOE_CAP, :] = xcj[k * MOE_CAP:(k + 1) * MOE_CAP]
        for k in range(MOE_PAIR):
            xc = xc_scr[k]
            a = _dot(xc, w1_ref[k])
            bb = _dot(xc, w3_ref[k])
            y = _dot(((a * jax.nn.sigmoid(a)) * bb).astype(BF16), w2_ref[k])
            ys.append([(y[j * MOE_CAP:(j + 1) * MOE_CAP] * gcs[k][j]).astype(BF16) for j in range(nsub)])
        for j in range(nsub):
            cols = []
            for k in range(MOE_PAIR):
                ex = e * MOE_PAIR + k
                rc = jnp.sum(jnp.where(lane == ex, rank_scr[j], 0.0), axis=1, keepdims=True)
                gc = jnp.sum(jnp.where(lane == ex, gates_ref[sub(j), :], 0.0), axis=1, keepdims=True)
                cols.append(jnp.where(gc > 0.0, rc, 0.0))
            put = jnp.where(jnp.where(second, cols[1], cols[0]) == slot_c, 1.0, 0.0).astype(BF16)
            o_ref[sub(j), :] += _dot(put, jnp.concatenate([ys[0][j], ys[1][j]], axis=0))
        return carry

    lax.fori_loop(0, np_ref[(b * nblk + i) * (N_EXPERTS // MOE_PAIR) + e], one_pass, 0)

    @pl.when(e == N_EXPERTS // MOE_PAIR - 1)
    def _():
        y = x_ref[...] + gate_ref[...] * o_ref[...]
        if final:
            ms = jnp.mean(y * y, axis=-1, keepdims=True)
            y = (y * lax.rsqrt(ms + EPS)) * fg_ref[...]
        o_ref[...] = y


def _moe_routed(x, g, mod3, row_of_batch, j_shift, j_scale, j_gate, rw, rb, w1, w3, w2, final_g, final):
    bsz, t, d = x.shape
    h, gates, cnt = _moe_route(x, g, mod3, row_of_batch, j_shift, j_scale, rw, rb)
    tb = _tile(t, MOE_BLK)
    nblk = t // tb
    nsub = tb // MOE_SUB
    npair = N_EXPERTS // MOE_PAIR
    passes = jnp.ceil(cnt[:, :, 0, :N_EXPERTS] / MOE_CAP).astype(jnp.int32)
    passes = jnp.max(passes.reshape(bsz, nblk, nsub, npair, MOE_PAIR), axis=(2, 4)).reshape(-1)
    full = lambda shape: pl.BlockSpec(shape, lambda b, i, e, np_: (0,) * len(shape))
    blk = lambda w: pl.BlockSpec((None, tb, w), lambda b, i, e, np_: (b, i, 0), pipeline_mode=pl.Buffered(1))
    grid_spec = pltpu.PrefetchScalarGridSpec(
        num_scalar_prefetch=1,
        grid=(bsz, nblk, npair),
        in_specs=[blk(d), blk(LANES), blk(d),
                  pl.BlockSpec((None, 1, d), lambda b, i, e, np_: (row_of_batch(b), 0, j_gate)),
                  pl.BlockSpec((MOE_PAIR, d, D_EXPERT), lambda b, i, e, np_: (e, 0, 0)),
                  pl.BlockSpec((MOE_PAIR, d, D_EXPERT), lambda b, i, e, np_: (e, 0, 0)),
                  pl.BlockSpec((MOE_PAIR, D_EXPERT, d), lambda b, i, e, np_: (e, 0, 0)),
                  full((1, d))],
        out_specs=pl.BlockSpec((None, tb, d), lambda b, i, e, np_: (b, i, 0)),
        scratch_shapes=[pltpu.VMEM((nsub, MOE_SUB, LANES), F32),
                        pltpu.VMEM((nsub, N_EXPERTS, MOE_SUB), F32),
                        pltpu.VMEM((nsub, N_EXPERTS, MOE_SUB), F32),
                        pltpu.VMEM((MOE_PAIR, nsub * MOE_CAP, d), BF16)])
    return pl.pallas_call(
        functools.partial(_moe_routed_kernel, final=final, nblk=nblk),
        grid_spec=grid_spec,
        out_shape=jax.ShapeDtypeStruct((bsz, t, d), F32),
        compiler_params=_params(("parallel", "parallel", "arbitrary")),
        name="moe_routed_final" if final else "moe_routed",
    )(passes, h, gates, x, mod3, w1, w3, w2, final_g)


def _tile(n, pref):
    t = min(n, pref)
    assert n % t == 0
    return t


def _fft_split(n):
    n2 = LANES
    assert n % n2 == 0
    return n // n2, n2


def _hyena_latent(u, lp, fargs):
    bsz, s, _ = u.shape
    n1, n2 = _fft_split(2 * s)
    dc = _dft_consts(n1, n2)
    cg = 8
    kt_l, _, nrm_l = _hy_filters(s, *fargs, tp=_tile(2 * s, 1024))
    kspec = _hy_spec(kt_l.reshape(2, HY_CH, n1, n2), nrm_l.reshape(2 * HY_CH), dc, cg, n1, n2)
    xt = _hy_short(u, lp["hy_short_w"].astype(F32), lp["hy_short_b"].astype(F32)[None, :], _tile(s, 512))
    h_l = _hy_conv(xt.reshape(3, bsz, HY_CH, n1 // 2, n2), kspec, lp["hy_skip"].astype(F32).reshape(2 * HY_CH),
                   dc, cg, n1, n2)
    return h_l.reshape(bsz, HY_CH, s)


def _mixers(u, uc, lp, li, need_ctx):
    bsz, s, _ = u.shape
    c = uc.shape[1]
    lam_init = 0.8 - 0.6 * math.exp(-0.3 * li)
    lam_vecs = lp["a_lambda"].astype(F32)
    subln = lp["a_subln_g"].astype(F32)[None, :]

    qc, kct, vc = _attn_prep(uc, _tile(c, 256), rope=False)
    ql, klt, vl = _attn_prep(u, _tile(s, 256), rope=True)
    a_l = _diff_attn(ql, kct, vc, klt, vl, lam_vecs, subln, lam_init, _tile(s, 512), _tile(s, 2048))
    bias8 = _nbr_bias(lp["b_rpb"])
    b_l = _nbr_attn(u, uc, bias8)
    wbd = jax.scipy.linalg.block_diag(*[lp["pool_w"][g] for g in range(len(POOL_SIZES))]).astype(BF16)
    pscale = lp["pool_scale"].astype(F32)[None, :]
    p_l = _pool_mix(u, wbd, pscale, _tile(s, 512))
    fargs = (lp["hy_f_w1"], lp["hy_f_b1"], lp["hy_f_w2"], lp["hy_f_b2"], lp["hy_f_w3"], lp["hy_f_b3"])
    skip = lp["hy_skip"].astype(F32)
    w_short = lp["hy_short_w"].astype(F32)
    b_short = lp["hy_short_b"].astype(F32)[None, :]
    h_l = _hyena_latent(u, lp, fargs)
    lat = (a_l, b_l, p_l, h_l)
    if not need_ctx:
        return lat, None
    a_c = _diff_attn(qc, kct, vc, None, None, lam_vecs, subln, lam_init, _tile(c, 256), None)
    b_c = _nbr_ctx_attn(uc)
    p_c = _pool_mix(uc, wbd, pscale, _tile(c, 256))
    kt_c, _, nrm_c = _hy_filters(c, *fargs, tp=_tile(2 * c, 512))
    xtc = _hy_short(uc, w_short, b_short, _tile(c, 256))
    h_c = _hy_ctx_conv(xtc, kt_c, nrm_c, skip[:, :, None])
    return lat, (a_c, b_c, p_c, h_c)


def kernel(x, c, ctx, c_ctx, norm1_g, norm2_g, ada_w, ada_b, w_in, w_out, a_lambda, a_subln_g, b_rpb, pool_w, pool_scale, hy_short_w, hy_short_b, hy_f_w1, hy_f_b1, hy_f_w2, hy_f_b2, hy_f_w3, hy_f_b3, hy_skip, router_w, router_b, moe_w1, moe_w3, moe_w2, final_g):
    depth = norm1_g.shape[0]
    bsz, s, d = x.shape
    cl = ctx.shape[1]
    assert bsz <= SUBLANES - 1
    xl, xc = x, ctx
    cpad = jnp.zeros((SUBLANES, d), F32).at[:bsz].set(c.astype(F32)).at[bsz].set(c_ctx.astype(F32))
    rw = _cat3(jnp.pad(router_w.astype(F32), ((0, 0), (0, LANES - N_EXPERTS))), 0)
    rb = router_b.astype(F32)[:, None]
    lat_row = lambda b: b
    ctx_row = lambda b: bsz
    fg = final_g.astype(F32)[None, :]
    tm = _tile(s, 512)
    tmc = _tile(cl, 256)
    for li in range(depth):
        need_ctx = li < depth - 1
        lp = dict(a_lambda=a_lambda[li], a_subln_g=a_subln_g[li], b_rpb=b_rpb[li], pool_w=pool_w[li],
                  pool_scale=pool_scale[li], hy_short_w=hy_short_w[li], hy_short_b=hy_short_b[li],
                  hy_f_w1=hy_f_w1[li], hy_f_b1=hy_f_b1[li], hy_f_w2=hy_f_w2[li], hy_f_b2=hy_f_b2[li],
                  hy_f_w3=hy_f_w3[li], hy_f_b3=hy_f_b3[li], hy_skip=hy_skip[li])
        mod3 = _ada(cpad, ada_w[li].astype(F32), ada_b[li].astype(F32)[None, :]).reshape(SUBLANES, 1, 6 * d)
        n1g = norm1_g[li].astype(F32)[None, :]
        n2g = norm2_g[li].astype(F32)[None, :]
        w_in_b = w_in[li].astype(BF16)
        w_out_b = w_out[li].astype(BF16)
        u = _norm_proj(xl, n1g, mod3, lat_row, 0, 1, w_in_b, tm)
        uc = _norm_proj(xc, n1g, mod3, ctx_row, 0, 1, w_in_b, tmc)
        lat, cx = _mixers(u, uc, lp, li, need_ctx)
        xl = _out_proj(*lat, xl, mod3, lat_row, 2, w_out_b, tm)
        w1b, w3b, w2b = moe_w1[li].astype(BF16), moe_w3[li].astype(BF16), moe_w2[li].astype(BF16)
        if need_ctx:
            xc = _out_proj(*cx, xc, mod3, ctx_row, 2, w_out_b, tmc)
            xc = _moe(xc, n2g, mod3, ctx_row, 3, 4, 5, rw, rb, w1b, w3b, w2b, fg, False, tmc)
        xl = _moe_routed(xl, n2g, mod3, lat_row, 3, 4, 5, rw, rb, w1b, w3b, w2b, fg, li == depth - 1)
    return xl
```

```python
import functools
import math

import numpy as np
import jax
import jax.numpy as jnp
from jax import lax
from jax.experimental import pallas as pl
from jax.experimental.pallas import tpu as pltpu

F32 = jnp.float32
BF16 = jnp.bfloat16
HI = lax.Precision.HIGHEST

GRID_W = 64
A_HEADS = 4
A_QK = 32
A_V = 64
ROPE_BASE = 10000.0
B_HEADS = 4
B_DIM = 64
WIN_R = 8
WIN_C = 16
POOL_SIZES = (2, 4, 8, 16)
POOL_CH = 64
D_GROUP = 256
HY_CH = 256
HY_BANDS = 16
HY_EMB = 1 + 2 * HY_BANDS
HY_HIDDEN = 64
HY_SIN_FREQ = 1.0
HY_MIN_DECAY = math.log(1e-2) / 1.5
HY_MAX_DECAY = math.log(1e-2) / 0.3
N_EXPERTS = 16
N_EXPERT_GROUPS = 4
D_EXPERT = 512
EPS = 1e-6
LOG2E = 1.4426950408889634

LANES = 128
SUBLANES = 8
VMEM_LIMIT = 56 * 1024 * 1024


def _params(sem):
    return pltpu.CompilerParams(dimension_semantics=sem, vmem_limit_bytes=VMEM_LIMIT)


def _dot(a, b, prec=None):
    return jnp.dot(a, b, precision=prec, preferred_element_type=F32)


def _dot_nt(a, b):
    return lax.dot_general(a, b, (((1,), (1,)), ((), ())), preferred_element_type=F32)


def _ada_kernel(c_ref, w_ref, b_ref, o_ref):
    cf = c_ref[...]
    s = cf * jax.nn.sigmoid(cf)
    o_ref[...] = _dot(s, w_ref[...], HI) + b_ref[...]


def _ada(cpad, w, b):
    d = cpad.shape[1]
    n = w.shape[1]
    return pl.pallas_call(
        _ada_kernel,
        grid=(n // d,),
        in_specs=[pl.BlockSpec((SUBLANES, d), lambda j: (0, 0)),
                  pl.BlockSpec((d, d), lambda j: (0, j)),
                  pl.BlockSpec((1, d), lambda j: (0, j))],
        out_specs=pl.BlockSpec((SUBLANES, d), lambda j: (0, j)),
        out_shape=jax.ShapeDtypeStruct((SUBLANES, n), F32),
        compiler_params=_params(("arbitrary",)),
        name="ada_mod",
    )(cpad, w, b)


def _norm_proj_kernel(x_ref, g_ref, sc_ref, sh_ref, w_ref, o_ref):
    x = x_ref[...]
    ms = jnp.mean(x * x, axis=-1, keepdims=True)
    h = (x * lax.rsqrt(ms + EPS)) * g_ref[...] * (1.0 + sc_ref[...]) + sh_ref[...]
    o_ref[...] = _dot(h.astype(BF16), w_ref[...])


def _norm_proj(x, g, mod3, row_of_batch, j_shift, j_scale, w, tm):
    bsz, t, d = x.shape
    n = w.shape[1]
    return pl.pallas_call(
        _norm_proj_kernel,
        grid=(bsz, t // tm),
        in_specs=[pl.BlockSpec((None, tm, d), lambda b, i: (b, i, 0)),
                  pl.BlockSpec((1, d), lambda b, i: (0, 0)),
                  pl.BlockSpec((None, 1, d), lambda b, i: (row_of_batch(b), 0, j_scale)),
                  pl.BlockSpec((None, 1, d), lambda b, i: (row_of_batch(b), 0, j_shift)),
                  pl.BlockSpec((d, n), lambda b, i: (0, 0))],
        out_specs=pl.BlockSpec((None, tm, n), lambda b, i: (b, i, 0)),
        out_shape=jax.ShapeDtypeStruct((bsz, t, n), F32),
        compiler_params=_params(("parallel", "arbitrary")),
        name="norm_in_proj",
    )(x, g, mod3, mod3, w)


def _aprep_kernel(*refs, rope):
    if rope:
        u_ref, cos_ref, sin_ref, q_ref, kt_ref, v_ref = refs
    else:
        u_ref, q_ref, kt_ref, v_ref = refs
    u = u_ref[...]
    q = u[:, 0:256]
    k = u[:, 256:512]
    v = u[:, 512:768]
    if rope:
        cos_t = cos_ref[...]
        sin_t = sin_ref[...]
        lane = lax.broadcasted_iota(jnp.int32, cos_t.shape, 1)
        first = (lane % (2 * 16)) < 16

        def rot(x):
            halves = []
            for j in range(2):
                xh = x[:, j * LANES:(j + 1) * LANES]
                swap = jnp.where(first, pltpu.roll(xh, LANES - 16, axis=1), pltpu.roll(xh, 16, axis=1))
                halves.append(xh * cos_t + swap * sin_t)
            return jnp.concatenate(halves, axis=1)

        q = rot(q)
        k = rot(k)
    q = q * (A_QK ** -0.5 * LOG2E)
    kt = k.T
    for hc in range(2 * A_HEADS):
        q_ref[hc] = q[:, hc * A_QK:(hc + 1) * A_QK].astype(BF16)
        kt_ref[hc] = kt[hc * A_QK:(hc + 1) * A_QK, :].astype(BF16)
    lane = lax.broadcasted_iota(jnp.int32, (v.shape[0], LANES - A_V), 1)
    ones_col = jnp.where(lane == 0, 1.0, 0.0)
    for h in range(A_HEADS):
        v_ref[h] = jnp.concatenate([v[:, h * A_V:(h + 1) * A_V], ones_col], axis=1).astype(BF16)


def _rope_tables(length):
    n_freq = A_QK // 4
    inv = ROPE_BASE ** (-jnp.arange(n_freq, dtype=F32) / n_freq)
    t = jnp.arange(length)
    row = (t // GRID_W).astype(F32)
    col = (t % GRID_W).astype(F32)
    ang = jnp.concatenate([row[:, None] * inv, col[:, None] * inv], axis=-1)
    cos, sin = jnp.cos(ang), jnp.sin(ang)
    cos_t = jnp.tile(jnp.concatenate([cos, cos], axis=-1), (1, LANES // 32))
    sin_t = jnp.tile(jnp.concatenate([-sin, sin], axis=-1), (1, LANES // 32))
    return cos_t, sin_t


def _attn_prep(u, tm, rope):
    bsz, t, _ = u.shape
    nh = 2 * A_HEADS
    in_specs = [pl.BlockSpec((None, tm, 768), lambda b, i: (b, i, 0))]
    args = [u]
    if rope:
        cos_t, sin_t = _rope_tables(t)
        in_specs += [pl.BlockSpec((tm, LANES), lambda b, i: (i, 0))] * 2
        args += [cos_t, sin_t]
    return pl.pallas_call(
        functools.partial(_aprep_kernel, rope=rope),
        grid=(bsz, t // tm),
        in_specs=in_specs,
        out_specs=[pl.BlockSpec((None, nh, tm, A_QK), lambda b, i: (b, 0, i, 0)),
                   pl.BlockSpec((None, nh, A_QK, tm), lambda b, i: (b, 0, 0, i)),
                   pl.BlockSpec((None, A_HEADS, tm, LANES), lambda b, i: (b, 0, i, 0))],
        out_shape=[jax.ShapeDtypeStruct((bsz, nh, t, A_QK), BF16),
                   jax.ShapeDtypeStruct((bsz, nh, A_QK, t), BF16),
                   jax.ShapeDtypeStruct((bsz, A_HEADS, t, LANES), BF16)],
        compiler_params=_params(("parallel", "arbitrary")),
        name="attn_prep_rope" if rope else "attn_prep_ctx",
    )(*args)


QK_LOOKAHEAD = 2


def _dattn_kernel(*refs, lam_init, has_lat):
    if has_lat:
        lam_ref, g_ref, q_ref, kc_ref, vc_ref, k_ref, v_ref, o_ref, m_scr, acc_scr = refs
    else:
        lam_ref, g_ref, q_ref, kc_ref, vc_ref, o_ref, m_scr, acc_scr = refs
    ki = pl.program_id(2)
    nk = pl.num_programs(2)
    nh = 2 * A_HEADS

    def update(kt_r, v_r):
        scores = [_dot(q_ref[j], kt_r[j]) for j in range(QK_LOOKAHEAD)]
        for hc in range(nh):
            s = scores[hc]
            if hc + QK_LOOKAHEAD < nh:
                scores.append(_dot(q_ref[hc + QK_LOOKAHEAD], kt_r[hc + QK_LOOKAHEAD]))
            m_prev = m_scr[hc]
            m_new = jnp.maximum(m_prev, jnp.max(s, axis=1, keepdims=True))
            alpha = jnp.exp2(m_prev - m_new)
            p = jnp.exp2((s - m_new[:, :1]).astype(BF16))
            acc_scr[hc] = alpha * acc_scr[hc] + _dot(p, v_r[hc // 2])
            m_scr[hc] = m_new

    @pl.when(ki == 0)
    def _():
        m_scr[...] = jnp.full(m_scr.shape, -jnp.inf, F32)
        acc_scr[...] = jnp.zeros(acc_scr.shape, F32)
        update(kc_ref, vc_ref)

    if has_lat:
        @pl.when(ki > 0)
        def _():
            update(k_ref, v_ref)

    @pl.when(ki == nk - 1)
    def _():
        lv = lam_ref[...]
        lam = (jnp.exp(jnp.sum(lv[0:1] * lv[1:2], axis=1, keepdims=True))
               - jnp.exp(jnp.sum(lv[2:3] * lv[3:4], axis=1, keepdims=True)) + lam_init)
        for h in range(A_HEADS):
            a0 = acc_scr[2 * h]
            a1 = acc_scr[2 * h + 1]
            o = a0[:, :A_V] / a0[:, A_V:A_V + 1] - lam * (a1[:, :A_V] / a1[:, A_V:A_V + 1])
            ms = jnp.mean(o * o, axis=-1, keepdims=True)
            o_ref[:, h * A_V:(h + 1) * A_V] = (o * lax.rsqrt(ms + EPS)) * g_ref[...] * (1.0 - lam_init)


def _diff_attn(q, kct, vc, kt, v, lam_vecs, subln_g, lam_init, tq, tk):
    bsz, nh, t, _ = q.shape
    c = kct.shape[-1]
    has_lat = kt is not None
    nk = 1 + (kt.shape[-1] // tk if has_lat else 0)
    in_specs = [pl.BlockSpec((4, A_QK), lambda b, i, k: (0, 0)),
                pl.BlockSpec((1, A_V), lambda b, i, k: (0, 0)),
                pl.BlockSpec((None, nh, tq, A_QK), lambda b, i, k: (b, 0, i, 0)),
                pl.BlockSpec((None, nh, A_QK, c), lambda b, i, k: (b, 0, 0, 0)),
                pl.BlockSpec((None, A_HEADS, c, LANES), lambda b, i, k: (b, 0, 0, 0))]
    args = [lam_vecs, subln_g, q, kct, vc]
    if has_lat:
        in_specs += [pl.BlockSpec((None, nh, A_QK, tk), lambda b, i, k: (b, 0, 0, jnp.maximum(k - 1, 0))),
                     pl.BlockSpec((None, A_HEADS, tk, LANES), lambda b, i, k: (b, 0, jnp.maximum(k - 1, 0), 0))]
        args += [kt, v]
    return pl.pallas_call(
        functools.partial(_dattn_kernel, lam_init=lam_init, has_lat=has_lat),
        grid=(bsz, t // tq, nk),
        in_specs=in_specs,
        out_specs=pl.BlockSpec((None, tq, A_HEADS * A_V), lambda b, i, k: (b, i, 0)),
        out_shape=jax.ShapeDtypeStruct((bsz, t, A_HEADS * A_V), F32),
        scratch_shapes=[pltpu.VMEM((nh, tq, LANES), F32),
                        pltpu.VMEM((nh, tq, LANES), F32)],
        compiler_params=_params(("parallel", "parallel", "arbitrary")),
        name="diff_attn" if has_lat else "diff_attn_ctx",
    )(*args)


NB_ROWS = 8


def _nbr_bias(rpb):
    cols = jnp.arange(GRID_W)
    c0 = jnp.clip(cols - WIN_C // 2, 0, GRID_W - WIN_C)
    in_win = (cols[None, :] >= c0[:, None]) & (cols[None, :] < c0[:, None] + WIN_C)
    dc = jnp.clip(cols[None, :] - cols[:, None], -(WIN_C - 1), WIN_C - 1) + (WIN_C - 1)
    onehot = (dc[None] == jnp.arange(2 * WIN_C - 1)[:, None, None]).astype(F32)
    g = jnp.einsum("hab,bqk->haqk", rpb.astype(F32), onehot, precision=HI)
    g = jnp.where(in_win[None, None], g, -jnp.inf)
    b = jnp.stack([g[:, a0:a0 + WIN_R] for a0 in range(WIN_R)], axis=0)
    b = jnp.transpose(b, (0, 1, 3, 2, 4))
    return b.reshape(WIN_R, B_HEADS, GRID_W, WIN_R * GRID_W)


def _nbr_kernel(q_ref, kp_ref, kc_ref, kn_ref, vp_ref, vcur_ref, vn_ref, kctx_ref, vctx_ref, bias_ref,
                o_ref, kwin, vwin, kcx, vcx, *, n_rows):
    rb = pl.program_id(1)
    blk = NB_ROWS * GRID_W
    scale = B_DIM ** -0.5
    for h in range(B_HEADS):
        sl = slice(h * B_DIM, (h + 1) * B_DIM)
        for j, (kr, vr) in enumerate(((kp_ref, vp_ref), (kc_ref, vcur_ref), (kn_ref, vn_ref))):
            kwin[h, j * blk:(j + 1) * blk, :] = kr[:, sl].astype(BF16)
            vwin[h, j * blk:(j + 1) * blk, :] = vr[:, sl].astype(BF16)
        kcx[h] = kctx_ref[:, sl].astype(BF16)
        vcx[h] = vctx_ref[:, sl].astype(BF16)

    def window(rr):
        r = rb * NB_ROWS + rr
        r0 = jnp.clip(r - WIN_R // 2, 0, n_rows - WIN_R)
        off = pl.multiple_of((r0 - (rb - 1) * NB_ROWS) * GRID_W, GRID_W)
        return off, r0 - r + (WIN_R - 1)

    def scores(rr):
        off, a0 = window(rr)
        qrow = q_ref[rr * GRID_W:(rr + 1) * GRID_W, :]
        out = []
        for h in range(B_HEADS):
            qh = qrow[:, h * B_DIM:(h + 1) * B_DIM].astype(BF16)
            s = _dot_nt(qh, kwin[h, pl.ds(off, WIN_R * GRID_W), :]) * scale + bias_ref[a0, h]
            out.append((s, _dot_nt(qh, kcx[h]) * scale))
        return out

    nxt = scores(0)
    for rr in range(NB_ROWS):
        cur = nxt
        if rr + 1 < NB_ROWS:
            nxt = scores(rr + 1)
        off, _ = window(rr)
        outs = []
        for h in range(B_HEADS):
            s, sc = cur[h]
            m = jnp.maximum(jnp.max(s, axis=1, keepdims=True), jnp.max(sc, axis=1, keepdims=True))
            p = jnp.exp(s - m)
            pc = jnp.exp(sc - m)
            l = jnp.sum(p, axis=1, keepdims=True) + jnp.sum(pc, axis=1, keepdims=True)
            o = _dot(p.astype(BF16), vwin[h, pl.ds(off, WIN_R * GRID_W), :]) + _dot(pc.astype(BF16), vcx[h])
            outs.append(o / l)
        o_ref[rr * GRID_W:(rr + 1) * GRID_W, :] = jnp.concatenate(outs, axis=1)


def _nbr_attn(u, uc, bias8):
    bsz, s, _ = u.shape
    c = uc.shape[1]
    n_rows = s // GRID_W
    nb = n_rows // NB_ROWS
    blk = NB_ROWS * GRID_W
    w = B_HEADS * B_DIM

    def spec(col, shift):
        return pl.BlockSpec((None, blk, w), lambda b, i: (b, jnp.clip(i + shift, 0, nb - 1), col))

    return pl.pallas_call(
        functools.partial(_nbr_kernel, n_rows=n_rows),
        grid=(bsz, nb),
        in_specs=[spec(3, 0), spec(4, -1), spec(4, 0), spec(4, 1), spec(5, -1), spec(5, 0), spec(5, 1),
                  pl.BlockSpec((None, c, w), lambda b, i: (b, 0, 4)),
                  pl.BlockSpec((None, c, w), lambda b, i: (b, 0, 5)),
                  pl.BlockSpec(bias8.shape, lambda b, i: (0, 0, 0, 0))],
        out_specs=pl.BlockSpec((None, blk, w), lambda b, i: (b, i, 0)),
        out_shape=jax.ShapeDtypeStruct((bsz, s, w), F32),
        scratch_shapes=[pltpu.VMEM((B_HEADS, 3 * blk, B_DIM), BF16),
                        pltpu.VMEM((B_HEADS, 3 * blk, B_DIM), BF16),
                        pltpu.VMEM((B_HEADS, c, B_DIM), BF16),
                        pltpu.VMEM((B_HEADS, c, B_DIM), BF16)],
        compiler_params=_params(("parallel", "arbitrary")),
        name="nbr_attn",
    )(u, u, u, u, u, u, u, uc, uc, bias8)


def _nbr_ctx_kernel(q_ref, k_ref, v_ref, o_ref):
    scale = B_DIM ** -0.5
    outs = []
    for h in range(B_HEADS):
        sl = slice(h * B_DIM, (h + 1) * B_DIM)
        s = _dot_nt(q_ref[:, sl].astype(BF16), k_ref[:, sl].astype(BF16)) * scale
        m = jnp.max(s, axis=1, keepdims=True)
        p = jnp.exp(s - m)
        l = jnp.sum(p, axis=1, keepdims=True)
        outs.append(_dot(p.astype(BF16), v_ref[:, sl].astype(BF16)) / l)
    o_ref[...] = jnp.concatenate(outs, axis=1)


def _nbr_ctx_attn(uc):
    bsz, c, _ = uc.shape
    w = B_HEADS * B_DIM
    return pl.pallas_call(
        _nbr_ctx_kernel,
        grid=(bsz,),
        in_specs=[pl.BlockSpec((None, c, w), lambda b: (b, 0, 3)),
                  pl.BlockSpec((None, c, w), lambda b: (b, 0, 4)),
                  pl.BlockSpec((None, c, w), lambda b: (b, 0, 5))],
        out_specs=pl.BlockSpec((None, c, w), lambda b: (b, 0, 0)),
        out_shape=jax.ShapeDtypeStruct((bsz, c, w), F32),
        compiler_params=_params(("arbitrary",)),
        name="nbr_attn_ctx",
    )(uc, uc, uc)


HALO = SUBLANES


def _halo_specs(tm, length, col, width):
    nt = length // tm
    per = tm // HALO
    last = length // HALO - 1
    return [pl.BlockSpec((None, HALO, width), lambda b, i, *_: (b, jnp.maximum(i * per - 1, 0), col)),
            pl.BlockSpec((None, tm, width), lambda b, i, *_: (b, i, col)),
            pl.BlockSpec((None, HALO, width), lambda b, i, *_: (b, jnp.minimum((i + 1) * per, last), col))], nt


def _fill_halo(buf, prev_ref, cur_ref, next_ref, i, nt, tm):
    zero = jnp.zeros(prev_ref.shape, F32)
    buf[0:HALO, :] = jnp.where(i > 0, prev_ref[...], zero)
    buf[HALO:HALO + tm, :] = cur_ref[...]
    buf[HALO + tm:, :] = jnp.where(i < nt - 1, next_ref[...], zero)


def _pool_kernel(prev_ref, cur_ref, next_ref, w_ref, ps_ref, o_ref, buf, *, tm, nt, length):
    i = pl.program_id(1)
    _fill_halo(buf, prev_ref, cur_ref, next_ref, i, nt, tm)

    def sh(j):
        return buf[HALO + j:HALO + j + tm, :]

    u = sh(0)
    sums = []
    acc = None
    lo, hi = 0, 0
    for w in POOL_SIZES:
        for j in list(range(-(w // 2), lo)) + list(range(hi, w // 2)):
            acc = sh(j) if acc is None else acc + sh(j)
        lo, hi = -(w // 2), w // 2
        sums.append(acc)
    lane = lax.broadcasted_iota(jnp.int32, (tm, D_GROUP), 1)
    t = (i * tm + lax.broadcasted_iota(jnp.int32, (tm, D_GROUP), 0))
    wsum = sums[-1]
    half = jnp.full((tm, D_GROUP), POOL_SIZES[-1] // 2, jnp.int32)
    for g in range(len(POOL_SIZES) - 2, -1, -1):
        sel = lane < (g + 1) * POOL_CH
        wsum = jnp.where(sel, sums[g], wsum)
        half = jnp.where(sel, POOL_SIZES[g] // 2, half)
    cnt = (jnp.minimum(t + half, length) - jnp.maximum(t - half, 0)).astype(F32)
    d = wsum / cnt - u
    o_ref[...] = _dot(d.astype(BF16), w_ref[...]) * ps_ref[...]


def _pool_mix(u, wbd, pool_scale, tm):
    bsz, length, _ = u.shape
    specs, nt = _halo_specs(tm, length, 6, D_GROUP)
    return pl.pallas_call(
        functools.partial(_pool_kernel, tm=tm, nt=nt, length=length),
        grid=(bsz, nt),
        in_specs=specs + [pl.BlockSpec((D_GROUP, D_GROUP), lambda b, i: (0, 0)),
                          pl.BlockSpec((1, D_GROUP), lambda b, i: (0, 0))],
        out_specs=pl.BlockSpec((None, tm, D_GROUP), lambda b, i: (b, i, 0)),
        out_shape=jax.ShapeDtypeStruct((bsz, length, D_GROUP), F32),
        scratch_shapes=[pltpu.VMEM((tm + 2 * HALO, D_GROUP), F32)],
        compiler_params=_params(("parallel", "arbitrary")),
        name="pool_mix",
    )(u, u, u, wbd, pool_scale)


def _hy_short_kernel(prev_ref, cur_ref, next_ref, w_ref, b_ref, o_ref, buf, *, tm, nt):
    i = pl.program_id(1)
    _fill_halo(buf, prev_ref, cur_ref, next_ref, i, nt, tm)
    w = w_ref[...]
    y = (buf[HALO - 1:HALO - 1 + tm, :] * w[0:1] + buf[HALO:HALO + tm, :] * w[1:2]
         + buf[HALO + 1:HALO + 1 + tm, :] * w[2:3] + b_ref[...])
    o_ref[...] = y.T


def _hy_short(u, w_short, b_short, tm):
    bsz, length, _ = u.shape
    nt = length // tm
    per = tm // HALO
    last = length // HALO - 1
    c0 = 7
    in_specs = [pl.BlockSpec((None, HALO, HY_CH), lambda b, i, j: (b, jnp.maximum(i * per - 1, 0), c0 + j)),
                pl.BlockSpec((None, tm, HY_CH), lambda b, i, j: (b, i, c0 + j)),
                pl.BlockSpec((None, HALO, HY_CH), lambda b, i, j: (b, jnp.minimum((i + 1) * per, last), c0 + j)),
                pl.BlockSpec((3, HY_CH), lambda b, i, j: (0, j)),
                pl.BlockSpec((1, HY_CH), lambda b, i, j: (0, j))]
    return pl.pallas_call(
        functools.partial(_hy_short_kernel, tm=tm, nt=nt),
        grid=(bsz, nt, 3),
        in_specs=in_specs,
        out_specs=pl.BlockSpec((None, None, HY_CH, tm), lambda b, i, j: (j, b, 0, i)),
        out_shape=jax.ShapeDtypeStruct((3, bsz, HY_CH, length), F32),
        scratch_shapes=[pltpu.VMEM((tm + 2 * HALO, HY_CH), F32)],
        compiler_params=_params(("parallel", "arbitrary", "arbitrary")),
        name="hyena_short_conv",
    )(u, u, u, w_short, b_short)


HY_FEAT = 40


def _hy_filter_kernel(band_ref, w1_ref, b1_ref, w2_ref, b2_ref, w3_ref, b3_ref, dl_ref,
                      k_ref, ssq_ref, nrm_ref, *, tp, length):
    i = pl.program_id(0)
    n_i = pl.num_programs(0)
    m = i * tp + lax.broadcasted_iota(jnp.int32, (1, tp), 1)
    t = jnp.where(m <= length, m, 2 * length - m).astype(F32)
    t_norm = t / max(length - 1, 1)
    ang = ((2.0 * math.pi / length) * t) * band_ref[...]
    row = lax.broadcasted_iota(jnp.int32, (HY_FEAT, tp), 0)
    z = jnp.where(row == 0, t_norm,
                  jnp.where(row <= HY_BANDS, jnp.cos(ang), jnp.where(row < HY_EMB, jnp.sin(ang), 0.0)))
    z = jnp.concatenate([z, jnp.zeros((LANES - HY_FEAT, tp), F32)], axis=0)
    h = jnp.sin(HY_SIN_FREQ * (_dot(w1_ref[...], z, HI) + b1_ref[...]))
    h = jnp.sin(HY_SIN_FREQ * (_dot(w2_ref[...], h, HI) + b2_ref[...]))
    hh, hl = _split_bf16(h)
    h = _dot(w3_ref[...], jnp.concatenate([hh, hl, hh], axis=0)) + b3_ref[...]
    h = h * jnp.exp(-t_norm * dl_ref[...])

    @pl.when(i == 0)
    def _():
        ssq_ref[...] = jnp.zeros(ssq_ref.shape, F32)

    for o in range(2):
        fwd = h[o * 2 * HY_CH:o * 2 * HY_CH + HY_CH]
        bwd = h[o * 2 * HY_CH + HY_CH:(o + 1) * 2 * HY_CH]
        k = jnp.where(m < length, fwd, jnp.where(m == length, 0.0, bwd))
        k_ref[o] = k
        extra = jnp.where(m == 0, bwd * bwd, 0.0)
        ssq_ref[o] += jnp.sum(k * k + extra, axis=1, keepdims=True)

    @pl.when(i == n_i - 1)
    def _():
        nrm_ref[...] = lax.rsqrt(ssq_ref[...] + EPS)


def _hy_filters(length, w1, b1, w2, b2, w3, b3, tp):
    bands = jnp.linspace(1e-4, HY_BANDS - 1, HY_BANDS, dtype=F32)
    band_col = jnp.concatenate([jnp.zeros((1,), F32), bands, bands,
                                jnp.zeros((HY_FEAT - HY_EMB,), F32)])[:, None]
    deltas = jnp.abs(jnp.linspace(HY_MIN_DECAY, HY_MAX_DECAY, HY_CH, dtype=F32))
    dl_col = jnp.tile(deltas, 4)[:, None]
    w1t = jnp.pad(w1.astype(F32).T, ((0, 0), (0, LANES - HY_EMB)))
    full = lambda shape: pl.BlockSpec(shape, lambda i: (0,) * len(shape))
    n = 2 * length
    return pl.pallas_call(
        functools.partial(_hy_filter_kernel, tp=tp, length=length),
        grid=(n // tp,),
        in_specs=[full((HY_FEAT, 1)), full((HY_HIDDEN, LANES)), full((HY_HIDDEN, 1)),
                  full((HY_HIDDEN, HY_HIDDEN)), full((HY_HIDDEN, 1)),
                  full((4 * HY_CH, 3 * HY_HIDDEN)), full((4 * HY_CH, 1)), full((4 * HY_CH, 1))],
        out_specs=[pl.BlockSpec((2, HY_CH, tp), lambda i: (0, 0, i)),
                   full((2, HY_CH, 1)), full((2, HY_CH, 1))],
        out_shape=[jax.ShapeDtypeStruct((2, HY_CH, n), F32),
                   jax.ShapeDtypeStruct((2, HY_CH, 1), F32),
                   jax.ShapeDtypeStruct((2, HY_CH, 1), F32)],
        compiler_params=_params(("arbitrary",)),
        name="hyena_filters",
    )(band_col, w1t, b1.astype(F32)[:, None], w2.astype(F32).T, b2.astype(F32)[:, None],
      _cat3(w3.astype(F32).T, 1), b3.astype(F32)[:, None], dl_col)


def _dft_consts(n1, n2):
    n = n1 * n2
    a1 = 2.0 * np.pi * ((np.arange(n1)[:, None] * np.arange(n1)[None, :]) % n1) / n1
    c1, s1 = np.cos(a1), np.sin(a1)
    a2 = 2.0 * np.pi * ((np.arange(n2)[:, None] * np.arange(n2)[None, :]) % n2) / n2
    c2, s2 = np.cos(a2), np.sin(a2)
    at = 2.0 * np.pi * ((np.arange(n1)[:, None] * np.arange(n2)[None, :]) % n) / n
    f1_full = np.concatenate([c1, -s1], axis=0)
    f1_half = f1_full[:, :n1 // 2]
    g1 = np.concatenate([c1[:n1 // 2], -s1[:n1 // 2]], axis=1)
    w2f = np.block([[c2, -s2], [s2, c2]])
    w2i = np.block([[c2, s2], [-s2, c2]])
    f = lambda x: jnp.asarray(x, F32)
    return dict(f1_full=f(f1_full), f1_half=f(f1_half), g1=f(g1), w2f=f(w2f), w2i=f(w2i),
                tr=f(np.cos(at)), ti=f(-np.sin(at)))


def _fft_fwd(slabs, f1, tr, ti, w2f, stack, n1, n2, prec):
    for c, x in enumerate(slabs):
        a = _dot(f1, x.astype(stack.dtype), prec)
        ar, ai = a[:n1], a[n1:]
        stack[c * n1:(c + 1) * n1, 0:n2] = (ar * tr - ai * ti).astype(stack.dtype)
        stack[c * n1:(c + 1) * n1, n2:2 * n2] = (ar * ti + ai * tr).astype(stack.dtype)
    return _dot(stack[...], w2f, prec)


def _split_bf16(x):
    hi = x.astype(BF16)
    return hi, (x - hi.astype(F32)).astype(BF16)


def _cat3(x, axis):
    hi, lo = _split_bf16(x)
    return jnp.concatenate([hi, hi, lo], axis=axis)


def _hy_spec_kernel(nrm_ref, k_ref, f1_ref, tr_ref, ti_ref, w2f_ref, o_ref, stack, *, cg, n1, n2):
    o = pl.program_id(0)
    g = pl.program_id(1)
    tr = tr_ref[...]
    ti = ti_ref[...]
    for c in range(cg):
        hi, lo = _split_bf16(k_ref[c])
        a = _dot(f1_ref[...], jnp.concatenate([hi, lo, hi], axis=0))
        ar, ai = a[:n1], a[n1:]
        sh, sl = _split_bf16(jnp.concatenate([ar * tr - ai * ti, ar * ti + ai * tr], axis=1))
        stack[c * n1:(c + 1) * n1, :] = jnp.concatenate([sh, sl, sh], axis=1)
    x = _dot(stack[...], w2f_ref[...])
    for c in range(cg):
        sc = nrm_ref[o * HY_CH + g * cg + c] * (1.0 / (n1 * n2))
        xc = x[c * n1:(c + 1) * n1] * sc
        o_ref[c, 0] = xc[:, :n2]
        o_ref[c, 1] = xc[:, n2:]


def _hy_spec(k4, nrm_flat, dc, cg, n1, n2):
    full = lambda shape: pl.BlockSpec(shape, lambda o, g: (0,) * len(shape))
    return pl.pallas_call(
        functools.partial(_hy_spec_kernel, cg=cg, n1=n1, n2=n2),
        grid=(2, HY_CH // cg),
        in_specs=[pl.BlockSpec(memory_space=pltpu.SMEM),
                  pl.BlockSpec((None, cg, n1, n2), lambda o, g: (o, g, 0, 0)),
                  full((2 * n1, 3 * n1)), full((n1, n2)), full((n1, n2)), full((6 * n2, 2 * n2))],
        out_specs=pl.BlockSpec((None, cg, 2, n1, n2), lambda o, g: (o, g, 0, 0, 0)),
        out_shape=jax.ShapeDtypeStruct((2, HY_CH, 2, n1, n2), F32),
        scratch_shapes=[pltpu.VMEM((cg * n1, 6 * n2), BF16)],
        compiler_params=_params(("parallel", "arbitrary")),
        name="hyena_filter_spectrum",
    )(nrm_flat, k4, _cat3(dc["f1_full"], 1), dc["tr"], dc["ti"], _cat3(dc["w2f"], 0))


def _hy_conv_kernel(skip_ref, x_ref, ks_ref, f1_ref, g1_ref, tr_ref, ti_ref, w2f_ref, w2i_ref,
                    o_ref, stack, *, cg, n1, n2):
    g = pl.program_id(1)
    tr = tr_ref[...]
    ti = ti_ref[...]

    def conv(slabs, order):
        x = _fft_fwd(slabs, f1_ref[...], tr, ti, w2f_ref[...], stack, n1, n2, None)
        for c in range(cg):
            xr, xi = x[c * n1:(c + 1) * n1, :n2], x[c * n1:(c + 1) * n1, n2:]
            kr, ki = ks_ref[order, c, 0], ks_ref[order, c, 1]
            stack[c * n1:(c + 1) * n1, 0:n2] = (xr * kr - xi * ki).astype(BF16)
            stack[c * n1:(c + 1) * n1, n2:2 * n2] = (xr * ki + xi * kr).astype(BF16)
        bm = _dot(stack[...], w2i_ref[...])
        outs = []
        for c in range(cg):
            br, bi = bm[c * n1:(c + 1) * n1, :n2], bm[c * n1:(c + 1) * n1, n2:]
            b2 = jnp.concatenate([br * tr + bi * ti, bi * tr - br * ti], axis=0)
            y = _dot(g1_ref[...], b2.astype(BF16))
            outs.append(y + slabs[c] * skip_ref[order * HY_CH + g * cg + c])
        return outs

    v = [x_ref[2, c] for c in range(cg)]
    y0 = conv(v, 0)
    z = [x_ref[0, c] * y0[c] for c in range(cg)]
    y1 = conv(z, 1)
    for c in range(cg):
        o_ref[c] = x_ref[1, c] * y1[c]


def _hy_conv(x4, kspec, skip_flat, dc, cg, n1, n2):
    bsz = x4.shape[1]
    full = lambda shape: pl.BlockSpec(shape, lambda b, g: (0,) * len(shape))
    return pl.pallas_call(
        functools.partial(_hy_conv_kernel, cg=cg, n1=n1, n2=n2),
        grid=(bsz, HY_CH // cg),
        in_specs=[pl.BlockSpec(memory_space=pltpu.SMEM),
                  pl.BlockSpec((3, None, cg, n1 // 2, n2), lambda b, g: (0, b, g, 0, 0)),
                  pl.BlockSpec((2, cg, 2, n1, n2), lambda b, g: (0, g, 0, 0, 0)),
                  full((2 * n1, n1 // 2)), full((n1 // 2, 2 * n1)), full((n1, n2)), full((n1, n2)),
                  full((2 * n2, 2 * n2)), full((2 * n2, 2 * n2))],
        out_specs=pl.BlockSpec((None, cg, n1 // 2, n2), lambda b, g: (b, g, 0, 0)),
        out_shape=jax.ShapeDtypeStruct((bsz, HY_CH, n1 // 2, n2), F32),
        scratch_shapes=[pltpu.VMEM((cg * n1, 2 * n2), BF16)],
        compiler_params=_params(("parallel", "arbitrary")),
        name="hyena_long_conv",
    )(skip_flat, x4, kspec, dc["f1_half"].astype(BF16), dc["g1"].astype(BF16), dc["tr"], dc["ti"],
      dc["w2f"].astype(BF16), dc["w2i"].astype(BF16))


def _hy_ctx_kernel(x_ref, k_ref, nrm_ref, skip_ref, fc_ref, fs_ref, o_ref, *, c):
    fc = fc_ref[...]
    fs = fs_ref[...]
    inv_n = 1.0 / (2 * c)

    def conv(x, order):
        kk = k_ref[order]
        kr, ki = _dot(kk, fc, HI), -_dot(kk, fs, HI)
        xr, xi = _dot(x, fc[:c], HI), -_dot(x, fs[:c], HI)
        yr, yi = xr * kr - xi * ki, xr * ki + xi * kr
        y = (_dot(yr, fc[:, :c], HI) - _dot(yi, fs[:, :c], HI)) * inv_n
        return y * nrm_ref[order] + x * skip_ref[order]

    z = x_ref[0] * conv(x_ref[2], 0)
    o_ref[...] = x_ref[1] * conv(z, 1)


def _hy_ctx_conv(xt, kt, nrm, skip_col):
    _, bsz, ch, c = xt.shape
    n = 2 * c
    ang = 2.0 * np.pi * ((np.arange(n)[:, None] * np.arange(n)[None, :]) % n) / n
    fc, fs = jnp.asarray(np.cos(ang), F32), jnp.asarray(np.sin(ang), F32)
    full = lambda shape: pl.BlockSpec(shape, lambda b: (0,) * len(shape))
    return pl.pallas_call(
        functools.partial(_hy_ctx_kernel, c=c),
        grid=(bsz,),
        in_specs=[pl.BlockSpec((3, None, ch, c), lambda b: (0, b, 0, 0)),
                  full((2, ch, n)), full((2, ch, 1)), full((2, ch, 1)), full((n, n)), full((n, n))],
        out_specs=pl.BlockSpec((None, ch, c), lambda b: (b, 0, 0)),
        out_shape=jax.ShapeDtypeStruct((bsz, ch, c), F32),
        compiler_params=_params(("arbitrary",)),
        name="hyena_ctx_conv",
    )(xt, kt, nrm, skip_col, fc, fs)


def _out_proj_kernel(a_ref, b_ref, p_ref, ht_ref, x_ref, g_ref, w_ref, o_ref):
    w = D_GROUP
    acc = _dot(a_ref[...].astype(BF16), w_ref[0:w])
    acc += _dot(b_ref[...].astype(BF16), w_ref[w:2 * w])
    acc += _dot(p_ref[...].astype(BF16), w_ref[2 * w:3 * w])
    acc += _dot(ht_ref[...].T.astype(BF16), w_ref[3 * w:4 * w])
    o_ref[...] = x_ref[...] + g_ref[...] * acc


def _out_proj(a, b, p, ht, x, mod3, row_of_batch, j_gate, w_out, tm):
    bsz, t, d = x.shape
    w = D_GROUP
    tok = pl.BlockSpec((None, tm, w), lambda bb, i: (bb, i, 0))
    return pl.pallas_call(
        _out_proj_kernel,
        grid=(bsz, t // tm),
        in_specs=[tok, tok, tok,
                  pl.BlockSpec((None, w, tm), lambda bb, i: (bb, 0, i)),
                  pl.BlockSpec((None, tm, d), lambda bb, i: (bb, i, 0)),
                  pl.BlockSpec((None, 1, d), lambda bb, i: (row_of_batch(bb), 0, j_gate)),
                  pl.BlockSpec((4 * w, d), lambda bb, i: (0, 0))],
        out_specs=pl.BlockSpec((None, tm, d), lambda bb, i: (bb, i, 0)),
        out_shape=jax.ShapeDtypeStruct((bsz, t, d), F32),
        compiler_params=_params(("parallel", "arbitrary")),
        name="out_proj_residual",
    )(a, b, p, ht, x, mod3, w_out)


MOE_ROWS = 256


def _route(x, g, sc, sh, rw3, rbc):
    per_group = N_EXPERTS // N_EXPERT_GROUPS
    tm = x.shape[0]
    ms = jnp.mean(x * x, axis=-1, keepdims=True)
    h = (x * lax.rsqrt(ms + EPS)) * g * (1.0 + sc) + sh
    hh, hl = _split_bf16(h)
    d = x.shape[1]
    logits = (_dot(hh, rw3[0:d]) + _dot(hl, rw3[d:2 * d]) + _dot(hh, rw3[2 * d:3 * d])).T[:N_EXPERTS]
    ex = jnp.exp(logits - jnp.max(logits, axis=0, keepdims=True))
    scores = ex / jnp.sum(ex, axis=0, keepdims=True)
    sel = scores + rbc
    srow = [sel[r:r + 1] for r in range(N_EXPERTS)]
    best = None
    for grp in range(N_EXPERT_GROUPS):
        rows = list(range(grp * per_group, (grp + 1) * per_group))
        v1 = functools.reduce(jnp.maximum, [srow[r] for r in rows])
        i1 = jnp.full((1, tm), rows[-1], jnp.int32)
        for r in reversed(rows[:-1]):
            i1 = jnp.where(srow[r] == v1, r, i1)
        rest = [jnp.where(i1 == r, -jnp.inf, srow[r]) for r in rows]
        v2 = functools.reduce(jnp.maximum, rest)
        i2 = jnp.full((1, tm), rows[-1], jnp.int32)
        for k in reversed(range(per_group - 1)):
            i2 = jnp.where(rest[k] == v2, rows[k], i2)
        gs = v1 + v2
        if best is None:
            best, e1, e2 = gs, i1, i2
        else:
            upd = gs > best
            best = jnp.where(upd, gs, best)
            e1 = jnp.where(upd, i1, e1)
            e2 = jnp.where(upd, i2, e2)
    row = lax.broadcasted_iota(jnp.int32, (N_EXPERTS, tm), 0)
    w1 = jnp.sum(jnp.where(row == e1, scores, 0.0), axis=0, keepdims=True)
    w2 = jnp.sum(jnp.where(row == e2, scores, 0.0), axis=0, keepdims=True)
    tot = w1 + w2
    gates_t = jnp.where(row == e1, w1 / tot, 0.0) + jnp.where(row == e2, w2 / tot, 0.0)
    gates_t = jnp.concatenate([gates_t, jnp.zeros((LANES - N_EXPERTS, tm), F32)], axis=0)
    return h, gates_t.T


def _moe_kernel(x_ref, g_ref, sc_ref, sh_ref, gate_ref, rw_ref, rb_ref, w1_ref, w3_ref, w2_ref, fg_ref,
                o_ref, h_scr, gates_scr, acc_scr, *, final):
    e = pl.program_id(2)
    tm = x_ref.shape[0]

    @pl.when(e == 0)
    def _():
        h, gates = _route(x_ref[...], g_ref[...], sc_ref[...], sh_ref[...], rw_ref[...], rb_ref[...])
        h_scr[...] = h.astype(BF16)
        gates_scr[...] = gates
        acc_scr[...] = jnp.zeros(acc_scr.shape, F32)

    lane = lax.broadcasted_iota(jnp.int32, (tm, LANES), 1)
    ge = jnp.sum(jnp.where(lane == e, gates_scr[...], 0.0), axis=1, keepdims=True)

    def up(j):
        hb = h_scr[j * MOE_ROWS:(j + 1) * MOE_ROWS, :]
        return _dot(hb, w1_ref[...]), _dot(hb, w3_ref[...])

    nxt = up(0)
    for j in range(tm // MOE_ROWS):
        a, b = nxt
        if (j + 1) * MOE_ROWS < tm:
            nxt = up(j + 1)
        rows = slice(j * MOE_ROWS, (j + 1) * MOE_ROWS)
        act = (a * jax.nn.sigmoid(a)) * b
        acc_scr[rows, :] += ge[rows] * _dot(act.astype(BF16), w2_ref[...])

    @pl.when(e == N_EXPERTS - 1)
    def _():
        y = x_ref[...] + gate_ref[...] * acc_scr[...]
        if final:
            ms = jnp.mean(y * y, axis=-1, keepdims=True)
            y = (y * lax.rsqrt(ms + EPS)) * fg_ref[...]
        o_ref[...] = y


def _moe(x, g, mod3, row_of_batch, j_shift, j_scale, j_gate, rw, rb, w1, w3, w2, final_g, final, tm):
    bsz, t, d = x.shape
    vec = lambda j: pl.BlockSpec((None, 1, d), lambda b, i, e: (row_of_batch(b), 0, j))
    full = lambda shape: pl.BlockSpec(shape, lambda b, i, e: (0,) * len(shape))
    return pl.pallas_call(
        functools.partial(_moe_kernel, final=final),
        grid=(bsz, t // tm, N_EXPERTS),
        in_specs=[pl.BlockSpec((None, tm, d), lambda b, i, e: (b, i, 0)),
                  full((1, d)), vec(j_scale), vec(j_shift), vec(j_gate),
                  full((3 * d, LANES)), full((N_EXPERTS, 1)),
                  pl.BlockSpec((None, d, D_EXPERT), lambda b, i, e: (e, 0, 0)),
                  pl.BlockSpec((None, d, D_EXPERT), lambda b, i, e: (e, 0, 0)),
                  pl.BlockSpec((None, D_EXPERT, d), lambda b, i, e: (e, 0, 0)),
                  full((1, d))],
        out_specs=pl.BlockSpec((None, tm, d), lambda b, i, e: (b, i, 0)),
        out_shape=jax.ShapeDtypeStruct((bsz, t, d), F32),
        scratch_shapes=[pltpu.VMEM((tm, d), BF16), pltpu.VMEM((tm, LANES), F32), pltpu.VMEM((tm, d), F32)],
        compiler_params=_params(("parallel", "parallel", "arbitrary")),
        name="moe_final" if final else "moe",
    )(x, g, mod3, mod3, mod3, rw, rb, w1, w3, w2, final_g)


MOE_SUB = 512
MOE_CAP = 128
MOE_BLK = 2048
MOE_PAIR = 2


def _moe_route_kernel(x_ref, g_ref, sc_ref, sh_ref, rw_ref, rb_ref, h_ref, gates_ref, cnt_ref):
    h, gates = _route(x_ref[...], g_ref[...], sc_ref[...], sh_ref[...], rw_ref[...], rb_ref[...])
    h_ref[...] = h.astype(BF16)
    gates_ref[...] = gates
    cnt = jnp.sum(jnp.where(gates > 0.0, 1.0, 0.0), axis=0, keepdims=True)
    cnt_ref[...] = jnp.broadcast_to(cnt, cnt_ref.shape)


def _moe_route(x, g, mod3, row_of_batch, j_shift, j_scale, rw, rb):
    bsz, t, d = x.shape
    tm = MOE_SUB
    vec = lambda j: pl.BlockSpec((None, 1, d), lambda b, i: (row_of_batch(b), 0, j))
    full = lambda shape: pl.BlockSpec(shape, lambda b, i: (0,) * len(shape))
    return pl.pallas_call(
        _moe_route_kernel,
        grid=(bsz, t // tm),
        in_specs=[pl.BlockSpec((None, tm, d), lambda b, i: (b, i, 0)),
                  full((1, d)), vec(j_scale), vec(j_shift), full((3 * d, LANES)), full((N_EXPERTS, 1))],
        out_specs=[pl.BlockSpec((None, tm, d), lambda b, i: (b, i, 0)),
                   pl.BlockSpec((None, tm, LANES), lambda b, i: (b, i, 0)),
                   pl.BlockSpec((None, None, SUBLANES, LANES), lambda b, i: (b, i, 0, 0))],
        out_shape=[jax.ShapeDtypeStruct((bsz, t, d), BF16),
                   jax.ShapeDtypeStruct((bsz, t, LANES), F32),
                   jax.ShapeDtypeStruct((bsz, t // tm, SUBLANES, LANES), F32)],
        compiler_params=_params(("parallel", "arbitrary")),
        name="moe_route",
    )(x, g, mod3, mod3, rw, rb)


def _moe_routed_kernel(np_ref, h_ref, gates_ref, x_ref, gate_ref, w1_ref, w3_ref, w2_ref, fg_ref,
                       o_ref, rank_scr, rankt_scr, gatet_scr, xc_scr, *, final, nblk):
    b = pl.program_id(0)
    i = pl.program_id(1)
    e = pl.program_id(2)
    nsub = h_ref.shape[0] // MOE_SUB
    sub = lambda j: slice(j * MOE_SUB, (j + 1) * MOE_SUB)

    @pl.when(e == 0)
    def _():
        o_ref[...] = jnp.zeros(o_ref.shape, F32)
        r = lax.broadcasted_iota(jnp.int32, (MOE_SUB, MOE_SUB), 0)
        c = lax.broadcasted_iota(jnp.int32, (MOE_SUB, MOE_SUB), 1)
        ltri = jnp.where(c <= r, 1.0, 0.0).astype(BF16)
        utri = jnp.where(r <= c, 1.0, 0.0).astype(BF16)
        for j in range(nsub):
            gt = gates_ref[sub(j), :]
            rank_scr[j] = _dot(ltri, jnp.where(gt > 0.0, 1.0, 0.0).astype(BF16))
            gtt = gt.T[:N_EXPERTS]
            gatet_scr[j] = gtt
            rankt_scr[j] = _dot(jnp.where(gtt > 0.0, 1.0, 0.0).astype(BF16), utri)

    lane = lax.broadcasted_iota(jnp.int32, (MOE_SUB, LANES), 1)
    slot_lane = lax.broadcasted_iota(jnp.int32, (MOE_SUB, MOE_PAIR * MOE_CAP), 1)
    second = slot_lane >= MOE_CAP

    def one_pass(p, carry):
        base = (p * MOE_CAP + 1).astype(F32)
        slot_r = lax.broadcasted_iota(jnp.int32, (MOE_CAP, MOE_SUB), 0).astype(F32) + base
        slot_c = jnp.where(second, slot_lane - MOE_CAP, slot_lane).astype(F32) + base
        ys = []
        gcs = [[], []]
        for j in range(nsub):
            picks = []
            for k in range(MOE_PAIR):
                ex = e * MOE_PAIR + k
                rr = rankt_scr[j, pl.ds(ex, 1), :]
                gr = gatet_scr[j, pl.ds(ex, 1), :]
                pick = (rr == slot_r) & (gr > 0.0)
                picks.append(jnp.where(pick, 1.0, 0.0).astype(BF16))
                gcs[k].append(jnp.sum(jnp.where(pick, gr, 0.0), axis=1, keepdims=True))
            xcj = _dot(jnp.concatenate(picks, axis=0), h_ref[sub(j), :]).astype(BF16)
            for k in range(MOE_PAIR):
                xc_scr[k, j * MOE_CAP:(j + 1) * MOE_CAP, :] = xcj[k * MOE_CAP:(k + 1) * MOE_CAP]
        for k in range(MOE_PAIR):
            xc = xc_scr[k]
            a = _dot(xc, w1_ref[k])
            bb = _dot(xc, w3_ref[k])
            y = _dot(((a * jax.nn.sigmoid(a)) * bb).astype(BF16), w2_ref[k])
            ys.append([(y[j * MOE_CAP:(j + 1) * MOE_CAP] * gcs[k][j]).astype(BF16) for j in range(nsub)])
        for j in range(nsub):
            cols = []
            for k in range(MOE_PAIR):
                ex = e * MOE_PAIR + k
                rc = jnp.sum(jnp.where(lane == ex, rank_scr[j], 0.0), axis=1, keepdims=True)
                gc = jnp.sum(jnp.where(lane == ex, gates_ref[sub(j), :], 0.0), axis=1, keepdims=True)
                cols.append(jnp.where(gc > 0.0, rc, 0.0))
            put = jnp.where(jnp.where(second, cols[1], cols[0]) == slot_c, 1.0, 0.0).astype(BF16)
            o_ref[sub(j), :] += _dot(put, jnp.concatenate([ys[0][j], ys[1][j]], axis=0))
        return carry

    lax.fori_loop(0, np_ref[(b * nblk + i) * (N_EXPERTS // MOE_PAIR) + e], one_pass, 0)

    @pl.when(e == N_EXPERTS // MOE_PAIR - 1)
    def _():
        y = x_ref[...] + gate_ref[...] * o_ref[...]
        if final:
            ms = jnp.mean(y * y, axis=-1, keepdims=True)
            y = (y * lax.rsqrt(ms + EPS)) * fg_ref[...]
        o_ref[...] = y


def _moe_routed(x, g, mod3, row_of_batch, j_shift, j_scale, j_gate, rw, rb, w1, w3, w2, final_g, final):
    bsz, t, d = x.shape
    h, gates, cnt = _moe_route(x, g, mod3, row_of_batch, j_shift, j_scale, rw, rb)
    tb = _tile(t, MOE_BLK)
    nblk = t // tb
    nsub = tb // MOE_SUB
    npair = N_EXPERTS // MOE_PAIR
    passes = jnp.ceil(cnt[:, :, 0, :N_EXPERTS] / MOE_CAP).astype(jnp.int32)
    passes = jnp.max(passes.reshape(bsz, nblk, nsub, npair, MOE_PAIR), axis=(2, 4)).reshape(-1)
    full = lambda shape: pl.BlockSpec(shape, lambda b, i, e, np_: (0,) * len(shape))
    blk = lambda w: pl.BlockSpec((None, tb, w), lambda b, i, e, np_: (b, i, 0), pipeline_mode=pl.Buffered(1))
    grid_spec = pltpu.PrefetchScalarGridSpec(
        num_scalar_prefetch=1,
        grid=(bsz, nblk, npair),
        in_specs=[blk(d), blk(LANES), blk(d),
                  pl.BlockSpec((None, 1, d), lambda b, i, e, np_: (row_of_batch(b), 0, j_gate)),
                  pl.BlockSpec((MOE_PAIR, d, D_EXPERT), lambda b, i, e, np_: (e, 0, 0)),
                  pl.BlockSpec((MOE_PAIR, d, D_EXPERT), lambda b, i, e, np_: (e, 0, 0)),
                  pl.BlockSpec((MOE_PAIR, D_EXPERT, d), lambda b, i, e, np_: (e, 0, 0)),
                  full((1, d))],
        out_specs=pl.BlockSpec((None, tb, d), lambda b, i, e, np_: (b, i, 0)),
        scratch_shapes=[pltpu.VMEM((nsub, MOE_SUB, LANES), F32),
                        pltpu.VMEM((nsub, N_EXPERTS, MOE_SUB), F32),
                        pltpu.VMEM((nsub, N_EXPERTS, MOE_SUB), F32),
                        pltpu.VMEM((MOE_PAIR, nsub * MOE_CAP, d), BF16)])
    return pl.pallas_call(
        functools.partial(_moe_routed_kernel, final=final, nblk=nblk),
        grid_spec=grid_spec,
        out_shape=jax.ShapeDtypeStruct((bsz, t, d), F32),
        compiler_params=_params(("parallel", "parallel", "arbitrary")),
        name="moe_routed_final" if final else "moe_routed",
    )(passes, h, gates, x, mod3, w1, w3, w2, final_g)


def _tile(n, pref):
    t = min(n, pref)
    assert n % t == 0
    return t


def _fft_split(n):
    n2 = LANES
    assert n % n2 == 0
    return n // n2, n2


def _hyena_latent(u, lp, fargs):
    bsz, s, _ = u.shape
    n1, n2 = _fft_split(2 * s)
    dc = _dft_consts(n1, n2)
    cg = 8
    kt_l, _, nrm_l = _hy_filters(s, *fargs, tp=_tile(2 * s, 1024))
    kspec = _hy_spec(kt_l.reshape(2, HY_CH, n1, n2), nrm_l.reshape(2 * HY_CH), dc, cg, n1, n2)
    xt = _hy_short(u, lp["hy_short_w"].astype(F32), lp["hy_short_b"].astype(F32)[None, :], _tile(s, 512))
    h_l = _hy_conv(xt.reshape(3, bsz, HY_CH, n1 // 2, n2), kspec, lp["hy_skip"].astype(F32).reshape(2 * HY_CH),
                   dc, cg, n1, n2)
    return h_l.reshape(bsz, HY_CH, s)


def _mixers(u, uc, lp, li, need_ctx):
    bsz, s, _ = u.shape
    c = uc.shape[1]
    lam_init = 0.8 - 0.6 * math.exp(-0.3 * li)
    lam_vecs = lp["a_lambda"].astype(F32)
    subln = lp["a_subln_g"].astype(F32)[None, :]

    qc, kct, vc = _attn_prep(uc, _tile(c, 256), rope=False)
    ql, klt, vl = _attn_prep(u, _tile(s, 256), rope=True)
    a_l = _diff_attn(ql, kct, vc, klt, vl, lam_vecs, subln, lam_init, _tile(s, 512), _tile(s, 4096))
    bias8 = _nbr_bias(lp["b_rpb"])
    b_l = _nbr_attn(u, uc, bias8)
    wbd = jax.scipy.linalg.block_diag(*[lp["pool_w"][g] for g in range(len(POOL_SIZES))]).astype(BF16)
    pscale = lp["pool_scale"].astype(F32)[None, :]
    p_l = _pool_mix(u, wbd, pscale, _tile(s, 512))
    fargs = (lp["hy_f_w1"], lp["hy_f_b1"], lp["hy_f_w2"], lp["hy_f_b2"], lp["hy_f_w3"], lp["hy_f_b3"])
    skip = lp["hy_skip"].astype(F32)
    w_short = lp["hy_short_w"].astype(F32)
    b_short = lp["hy_short_b"].astype(F32)[None, :]
    h_l = _hyena_latent(u, lp, fargs)
    lat = (a_l, b_l, p_l, h_l)
    if not need_ctx:
        return lat, None
    a_c = _diff_attn(qc, kct, vc, None, None, lam_vecs, subln, lam_init, _tile(c, 256), None)
    b_c = _nbr_ctx_attn(uc)
    p_c = _pool_mix(uc, wbd, pscale, _tile(c, 256))
    kt_c, _, nrm_c = _hy_filters(c, *fargs, tp=_tile(2 * c, 512))
    xtc = _hy_short(uc, w_short, b_short, _tile(c, 256))
    h_c = _hy_ctx_conv(xtc, kt_c, nrm_c, skip[:, :, None])
    return lat, (a_c, b_c, p_c, h_c)


def kernel(x, c, ctx, c_ctx, norm1_g, norm2_g, ada_w, ada_b, w_in, w_out, a_lambda, a_subln_g, b_rpb, pool_w, pool_scale, hy_short_w, hy_short_b, hy_f_w1, hy_f_b1, hy_f_w2, hy_f_b2, hy_f_w3, hy_f_b3, hy_skip, router_w, router_b, moe_w1, moe_w3, moe_w2, final_g):
    depth = norm1_g.shape[0]
    bsz, s, d = x.shape
    cl = ctx.shape[1]
    assert bsz <= SUBLANES - 1
    xl, xc = x, ctx
    cpad = jnp.zeros((SUBLANES, d), F32).at[:bsz].set(c.astype(F32)).at[bsz].set(c_ctx.astype(F32))
    rw = _cat3(jnp.pad(router_w.astype(F32), ((0, 0), (0, LANES - N_EXPERTS))), 0)
    rb = router_b.astype(F32)[:, None]
    lat_row = lambda b: b
    ctx_row = lambda b: bsz
    fg = final_g.astype(F32)[None, :]
    tm = _tile(s, 512)
    tmc = _tile(cl, 256)
    for li in range(depth):
        need_ctx = li < depth - 1
        lp = dict(a_lambda=a_lambda[li], a_subln_g=a_subln_g[li], b_rpb=b_rpb[li], pool_w=pool_w[li],
                  pool_scale=pool_scale[li], hy_short_w=hy_short_w[li], hy_short_b=hy_short_b[li],
                  hy_f_w1=hy_f_w1[li], hy_f_b1=hy_f_b1[li], hy_f_w2=hy_f_w2[li], hy_f_b2=hy_f_b2[li],
                  hy_f_w3=hy_f_w3[li], hy_f_b3=hy_f_b3[li], hy_skip=hy_skip[li])
        mod3 = _ada(cpad, ada_w[li].astype(F32), ada_b[li].astype(F32)[None, :]).reshape(SUBLANES, 1, 6 * d)
        n1g = norm1_g[li].astype(F32)[None, :]
        n2g = norm2_g[li].astype(F32)[None, :]
        w_in_b = w_in[li].astype(BF16)
        w_out_b = w_out[li].astype(BF16)
        u = _norm_proj(xl, n1g, mod3, lat_row, 0, 1, w_in_b, tm)
        uc = _norm_proj(xc, n1g, mod3, ctx_row, 0, 1, w_in_b, tmc)
        lat, cx = _mixers(u, uc, lp, li, need_ctx)
        xl = _out_proj(*lat, xl, mod3, lat_row, 2, w_out_b, tm)
        w1b, w3b, w2b = moe_w1[li].astype(BF16), moe_w3[li].astype(BF16), moe_w2[li].astype(BF16)
        if need_ctx:
            xc = _out_proj(*cx, xc, mod3, ctx_row, 2, w_out_b, tmc)
            xc = _moe(xc, n2g, mod3, ctx_row, 3, 4, 5, rw, rb, w1b, w3b, w2b, fg, False, tmc)
        xl = _moe_routed(xl, n2g, mod3, lat_row, 3, 4, 5, rw, rb, w1b, w3b, w2b, fg, li == depth - 1)
    return xl
```

```python
import functools
import math

import numpy as np
import jax
import jax.numpy as jnp
from jax import lax
from jax.experimental import pallas as pl
from jax.experimental.pallas import tpu as pltpu

F32 = jnp.float32
BF16 = jnp.bfloat16
HI = lax.Precision.HIGHEST

GRID_W = 64
A_HEADS = 4
A_QK = 32
A_V = 64
ROPE_BASE = 10000.0
B_HEADS = 4
B_DIM = 64
WIN_R = 8
WIN_C = 16
POOL_SIZES = (2, 4, 8, 16)
POOL_CH = 64
D_GROUP = 256
HY_CH = 256
HY_BANDS = 16
HY_EMB = 1 + 2 * HY_BANDS
HY_HIDDEN = 64
HY_SIN_FREQ = 1.0
HY_MIN_DECAY = math.log(1e-2) / 1.5
HY_MAX_DECAY = math.log(1e-2) / 0.3
N_EXPERTS = 16
N_EXPERT_GROUPS = 4
D_EXPERT = 512
EPS = 1e-6
LOG2E = 1.4426950408889634

LANES = 128
SUBLANES = 8
VMEM_LIMIT = 56 * 1024 * 1024


def _params(sem):
    return pltpu.CompilerParams(dimension_semantics=sem, vmem_limit_bytes=VMEM_LIMIT)


def _dot(a, b, prec=None):
    return jnp.dot(a, b, precision=prec, preferred_element_type=F32)


def _dot_nt(a, b):
    return lax.dot_general(a, b, (((1,), (1,)), ((), ())), preferred_element_type=F32)


def _ada_kernel(c_ref, w_ref, b_ref, o_ref):
    cf = c_ref[...]
    s = cf * jax.nn.sigmoid(cf)
    o_ref[...] = _dot(s, w_ref[...], HI) + b_ref[...]


def _ada(cpad, w, b):
    d = cpad.shape[1]
    n = w.shape[1]
    return pl.pallas_call(
        _ada_kernel,
        grid=(n // d,),
        in_specs=[pl.BlockSpec((SUBLANES, d), lambda j: (0, 0)),
                  pl.BlockSpec((d, d), lambda j: (0, j)),
                  pl.BlockSpec((1, d), lambda j: (0, j))],
        out_specs=pl.BlockSpec((SUBLANES, d), lambda j: (0, j)),
        out_shape=jax.ShapeDtypeStruct((SUBLANES, n), F32),
        compiler_params=_params(("arbitrary",)),
        name="ada_mod",
    )(cpad, w, b)


A_COLS = 768


def _attn_layout(u, cos_t, sin_t, q_ref, kt_ref, v_ref):
    q = u[:, 0:256]
    k = u[:, 256:512]
    v = u[:, 512:768]
    if cos_t is not None:
        lane = lax.broadcasted_iota(jnp.int32, cos_t.shape, 1)
        first = (lane % (2 * 16)) < 16

        def rot(x):
            halves = []
            for j in range(2):
                xh = x[:, j * LANES:(j + 1) * LANES]
                swap = jnp.where(first, pltpu.roll(xh, LANES - 16, axis=1), pltpu.roll(xh, 16, axis=1))
                halves.append(xh * cos_t + swap * sin_t)
            return jnp.concatenate(halves, axis=1)

        q = rot(q)
        k = rot(k)
    q = q * (A_QK ** -0.5 * LOG2E)
    kt = k.T
    for hc in range(2 * A_HEADS):
        q_ref[hc] = q[:, hc * A_QK:(hc + 1) * A_QK].astype(BF16)
        kt_ref[hc] = kt[hc * A_QK:(hc + 1) * A_QK, :].astype(BF16)
    lane = lax.broadcasted_iota(jnp.int32, (v.shape[0], LANES - A_V), 1)
    ones_col = jnp.where(lane == 0, 1.0, 0.0)
    for h in range(A_HEADS):
        v_ref[h] = jnp.concatenate([v[:, h * A_V:(h + 1) * A_V], ones_col], axis=1).astype(BF16)


def _rope_tables(length):
    n_freq = A_QK // 4
    inv = ROPE_BASE ** (-jnp.arange(n_freq, dtype=F32) / n_freq)
    t = jnp.arange(length)
    row = (t // GRID_W).astype(F32)
    col = (t % GRID_W).astype(F32)
    ang = jnp.concatenate([row[:, None] * inv, col[:, None] * inv], axis=-1)
    cos, sin = jnp.cos(ang), jnp.sin(ang)
    cos_t = jnp.tile(jnp.concatenate([cos, cos], axis=-1), (1, LANES // 32))
    sin_t = jnp.tile(jnp.concatenate([-sin, sin], axis=-1), (1, LANES // 32))
    return cos_t, sin_t


def _norm_proj_kernel(*refs, rope):
    if rope:
        x_ref, g_ref, sc_ref, sh_ref, w_ref, cos_ref, sin_ref, u_ref, q_ref, kt_ref, v_ref = refs
    else:
        x_ref, g_ref, sc_ref, sh_ref, w_ref, u_ref, q_ref, kt_ref, v_ref = refs
    x = x_ref[...]
    ms = jnp.mean(x * x, axis=-1, keepdims=True)
    h = (x * lax.rsqrt(ms + EPS)) * g_ref[...] * (1.0 + sc_ref[...]) + sh_ref[...]
    u = _dot(h.astype(BF16), w_ref[...])
    u_ref[...] = u[:, A_COLS:]
    _attn_layout(u[:, :A_COLS], cos_ref[...] if rope else None, sin_ref[...] if rope else None, q_ref, kt_ref, v_ref)


def _norm_proj(x, g, mod3, row_of_batch, j_shift, j_scale, w, tm, rope):
    bsz, t, d = x.shape
    n = w.shape[1]
    nh = 2 * A_HEADS
    in_specs = [pl.BlockSpec((None, tm, d), lambda b, i: (b, i, 0)),
                pl.BlockSpec((1, d), lambda b, i: (0, 0)),
                pl.BlockSpec((None, 1, d), lambda b, i: (row_of_batch(b), 0, j_scale)),
                pl.BlockSpec((None, 1, d), lambda b, i: (row_of_batch(b), 0, j_shift)),
                pl.BlockSpec((d, n), lambda b, i: (0, 0))]
    args = [x, g, mod3, mod3, w]
    if rope:
        in_specs += [pl.BlockSpec((tm, LANES), lambda b, i: (i, 0))] * 2
        args += list(_rope_tables(t))
    return pl.pallas_call(
        functools.partial(_norm_proj_kernel, rope=rope),
        grid=(bsz, t // tm),
        in_specs=in_specs,
        out_specs=[pl.BlockSpec((None, tm, n - A_COLS), lambda b, i: (b, i, 0)),
                   pl.BlockSpec((None, nh, tm, A_QK), lambda b, i: (b, 0, i, 0)),
                   pl.BlockSpec((None, nh, A_QK, tm), lambda b, i: (b, 0, 0, i)),
                   pl.BlockSpec((None, A_HEADS, tm, LANES), lambda b, i: (b, 0, i, 0))],
        out_shape=[jax.ShapeDtypeStruct((bsz, t, n - A_COLS), F32),
                   jax.ShapeDtypeStruct((bsz, nh, t, A_QK), BF16),
                   jax.ShapeDtypeStruct((bsz, nh, A_QK, t), BF16),
                   jax.ShapeDtypeStruct((bsz, A_HEADS, t, LANES), BF16)],
        compiler_params=_params(("parallel", "arbitrary")),
        name="norm_in_proj_rope" if rope else "norm_in_proj_ctx",
    )(*args)


QK_LOOKAHEAD = 2


def _dattn_kernel(*refs, lam_init, has_lat):
    if has_lat:
        lam_ref, g_ref, q_ref, kc_ref, vc_ref, k_ref, v_ref, o_ref, m_scr, acc_scr = refs
    else:
        lam_ref, g_ref, q_ref, kc_ref, vc_ref, o_ref, m_scr, acc_scr = refs
    ki = pl.program_id(2)
    nk = pl.num_programs(2)
    nh = 2 * A_HEADS

    def update(kt_r, v_r):
        scores = [_dot(q_ref[j], kt_r[j]) for j in range(QK_LOOKAHEAD)]
        for hc in range(nh):
            s = scores[hc]
            if hc + QK_LOOKAHEAD < nh:
                scores.append(_dot(q_ref[hc + QK_LOOKAHEAD], kt_r[hc + QK_LOOKAHEAD]))
            m_prev = m_scr[hc]
            m_new = jnp.maximum(m_prev, jnp.max(s, axis=1, keepdims=True))
            alpha = jnp.exp2(m_prev - m_new)
            p = jnp.exp2((s - m_new[:, :1]).astype(BF16))
            acc_scr[hc] = alpha * acc_scr[hc] + _dot(p, v_r[hc // 2])
            m_scr[hc] = m_new

    @pl.when(ki == 0)
    def _():
        m_scr[...] = jnp.full(m_scr.shape, -jnp.inf, F32)
        acc_scr[...] = jnp.zeros(acc_scr.shape, F32)
        update(kc_ref, vc_ref)

    if has_lat:
        @pl.when(ki > 0)
        def _():
            update(k_ref, v_ref)

    @pl.when(ki == nk - 1)
    def _():
        lv = lam_ref[...]
        lam = (jnp.exp(jnp.sum(lv[0:1] * lv[1:2], axis=1, keepdims=True))
               - jnp.exp(jnp.sum(lv[2:3] * lv[3:4], axis=1, keepdims=True)) + lam_init)
        for h in range(A_HEADS):
            a0 = acc_scr[2 * h]
            a1 = acc_scr[2 * h + 1]
            o = a0[:, :A_V] / a0[:, A_V:A_V + 1] - lam * (a1[:, :A_V] / a1[:, A_V:A_V + 1])
            ms = jnp.mean(o * o, axis=-1, keepdims=True)
            o_ref[:, h * A_V:(h + 1) * A_V] = (o * lax.rsqrt(ms + EPS)) * g_ref[...] * (1.0 - lam_init)


def _diff_attn(q, kct, vc, kt, v, lam_vecs, subln_g, lam_init, tq, tk):
    bsz, nh, t, _ = q.shape
    c = kct.shape[-1]
    has_lat = kt is not None
    nk = 1 + (kt.shape[-1] // tk if has_lat else 0)
    in_specs = [pl.BlockSpec((4, A_QK), lambda b, i, k: (0, 0)),
                pl.BlockSpec((1, A_V), lambda b, i, k: (0, 0)),
                pl.BlockSpec((None, nh, tq, A_QK), lambda b, i, k: (b, 0, i, 0)),
                pl.BlockSpec((None, nh, A_QK, c), lambda b, i, k: (b, 0, 0, 0)),
                pl.BlockSpec((None, A_HEADS, c, LANES), lambda b, i, k: (b, 0, 0, 0))]
    args = [lam_vecs, subln_g, q, kct, vc]
    if has_lat:
        in_specs += [pl.BlockSpec((None, nh, A_QK, tk), lambda b, i, k: (b, 0, 0, jnp.maximum(k - 1, 0))),
                     pl.BlockSpec((None, A_HEADS, tk, LANES), lambda b, i, k: (b, 0, jnp.maximum(k - 1, 0), 0))]
        args += [kt, v]
    return pl.pallas_call(
        functools.partial(_dattn_kernel, lam_init=lam_init, has_lat=has_lat),
        grid=(bsz, t // tq, nk),
        in_specs=in_specs,
        out_specs=pl.BlockSpec((None, tq, A_HEADS * A_V), lambda b, i, k: (b, i, 0)),
        out_shape=jax.ShapeDtypeStruct((bsz, t, A_HEADS * A_V), F32),
        scratch_shapes=[pltpu.VMEM((nh, tq, LANES), F32),
                        pltpu.VMEM((nh, tq, LANES), F32)],
        compiler_params=_params(("parallel", "parallel", "arbitrary")),
        name="diff_attn" if has_lat else "diff_attn_ctx",
    )(*args)


NB_ROWS = 8


def _nbr_bias(rpb):
    cols = jnp.arange(GRID_W)
    c0 = jnp.clip(cols - WIN_C // 2, 0, GRID_W - WIN_C)
    in_win = (cols[None, :] >= c0[:, None]) & (cols[None, :] < c0[:, None] + WIN_C)
    dc = jnp.clip(cols[None, :] - cols[:, None], -(WIN_C - 1), WIN_C - 1) + (WIN_C - 1)
    onehot = (dc[None] == jnp.arange(2 * WIN_C - 1)[:, None, None]).astype(F32)
    g = jnp.einsum("hab,bqk->haqk", rpb.astype(F32), onehot, precision=HI)
    g = jnp.where(in_win[None, None], g, -jnp.inf)
    b = jnp.stack([g[:, a0:a0 + WIN_R] for a0 in range(WIN_R)], axis=0)
    b = jnp.transpose(b, (0, 1, 3, 2, 4))
    return b.reshape(WIN_R, B_HEADS, GRID_W, WIN_R * GRID_W)


def _nbr_kernel(q_ref, kp_ref, kc_ref, kn_ref, vp_ref, vcur_ref, vn_ref, kctx_ref, vctx_ref, bias_ref,
                o_ref, kwin, vwin, kcx, vcx, *, n_rows):
    rb = pl.program_id(1)
    blk = NB_ROWS * GRID_W
    scale = B_DIM ** -0.5
    for h in range(B_HEADS):
        sl = slice(h * B_DIM, (h + 1) * B_DIM)
        for j, (kr, vr) in enumerate(((kp_ref, vp_ref), (kc_ref, vcur_ref), (kn_ref, vn_ref))):
            kwin[h, j * blk:(j + 1) * blk, :] = kr[:, sl].astype(BF16)
            vwin[h, j * blk:(j + 1) * blk, :] = vr[:, sl].astype(BF16)
        kcx[h] = kctx_ref[:, sl].astype(BF16)
        vcx[h] = vctx_ref[:, sl].astype(BF16)

    def window(rr):
        r = rb * NB_ROWS + rr
        r0 = jnp.clip(r - WIN_R // 2, 0, n_rows - WIN_R)
        off = pl.multiple_of((r0 - (rb - 1) * NB_ROWS) * GRID_W, GRID_W)
        return off, r0 - r + (WIN_R - 1)

    def scores(rr):
        off, a0 = window(rr)
        qrow = q_ref[rr * GRID_W:(rr + 1) * GRID_W, :]
        out = []
        for h in range(B_HEADS):
            qh = qrow[:, h * B_DIM:(h + 1) * B_DIM].astype(BF16)
            s = _dot_nt(qh, kwin[h, pl.ds(off, WIN_R * GRID_W), :]) * scale + bias_ref[a0, h]
            out.append((s, _dot_nt(qh, kcx[h]) * scale))
        return out

    nxt = scores(0)
    for rr in range(NB_ROWS):
        cur = nxt
        if rr + 1 < NB_ROWS:
            nxt = scores(rr + 1)
        off, _ = window(rr)
        outs = []
        for h in range(B_HEADS):
            s, sc = cur[h]
            m = jnp.maximum(jnp.max(s, axis=1, keepdims=True), jnp.max(sc, axis=1, keepdims=True))
            p = jnp.exp(s - m)
            pc = jnp.exp(sc - m)
            l = jnp.sum(p, axis=1, keepdims=True) + jnp.sum(pc, axis=1, keepdims=True)
            o = _dot(p.astype(BF16), vwin[h, pl.ds(off, WIN_R * GRID_W), :]) + _dot(pc.astype(BF16), vcx[h])
            outs.append(o / l)
        o_ref[rr * GRID_W:(rr + 1) * GRID_W, :] = jnp.concatenate(outs, axis=1)


def _nbr_attn(u, uc, bias8):
    bsz, s, _ = u.shape
    c = uc.shape[1]
    n_rows = s // GRID_W
    nb = n_rows // NB_ROWS
    blk = NB_ROWS * GRID_W
    w = B_HEADS * B_DIM

    def spec(col, shift):
        return pl.BlockSpec((None, blk, w), lambda b, i: (b, jnp.clip(i + shift, 0, nb - 1), col))

    return pl.pallas_call(
        functools.partial(_nbr_kernel, n_rows=n_rows),
        grid=(bsz, nb),
        in_specs=[spec(0, 0), spec(1, -1), spec(1, 0), spec(1, 1), spec(2, -1), spec(2, 0), spec(2, 1),
                  pl.BlockSpec((None, c, w), lambda b, i: (b, 0, 1)),
                  pl.BlockSpec((None, c, w), lambda b, i: (b, 0, 2)),
                  pl.BlockSpec(bias8.shape, lambda b, i: (0, 0, 0, 0))],
        out_specs=pl.BlockSpec((None, blk, w), lambda b, i: (b, i, 0)),
        out_shape=jax.ShapeDtypeStruct((bsz, s, w), F32),
        scratch_shapes=[pltpu.VMEM((B_HEADS, 3 * blk, B_DIM), BF16),
                        pltpu.VMEM((B_HEADS, 3 * blk, B_DIM), BF16),
                        pltpu.VMEM((B_HEADS, c, B_DIM), BF16),
                        pltpu.VMEM((B_HEADS, c, B_DIM), BF16)],
        compiler_params=_params(("parallel", "arbitrary")),
        name="nbr_attn",
    )(u, u, u, u, u, u, u, uc, uc, bias8)


def _nbr_ctx_kernel(q_ref, k_ref, v_ref, o_ref):
    scale = B_DIM ** -0.5
    outs = []
    for h in range(B_HEADS):
        sl = slice(h * B_DIM, (h + 1) * B_DIM)
        s = _dot_nt(q_ref[:, sl].astype(BF16), k_ref[:, sl].astype(BF16)) * scale
        m = jnp.max(s, axis=1, keepdims=True)
        p = jnp.exp(s - m)
        l = jnp.sum(p, axis=1, keepdims=True)
        outs.append(_dot(p.astype(BF16), v_ref[:, sl].astype(BF16)) / l)
    o_ref[...] = jnp.concatenate(outs, axis=1)


def _nbr_ctx_attn(uc):
    bsz, c, _ = uc.shape
    w = B_HEADS * B_DIM
    return pl.pallas_call(
        _nbr_ctx_kernel,
        grid=(bsz,),
        in_specs=[pl.BlockSpec((None, c, w), lambda b: (b, 0, 0)),
                  pl.BlockSpec((None, c, w), lambda b: (b, 0, 1)),
                  pl.BlockSpec((None, c, w), lambda b: (b, 0, 2))],
        out_specs=pl.BlockSpec((None, c, w), lambda b: (b, 0, 0)),
        out_shape=jax.ShapeDtypeStruct((bsz, c, w), F32),
        compiler_params=_params(("arbitrary",)),
        name="nbr_attn_ctx",
    )(uc, uc, uc)


HALO = SUBLANES


def _halo_specs(tm, length, col, width):
    nt = length // tm
    per = tm // HALO
    last = length // HALO - 1
    return [pl.BlockSpec((None, HALO, width), lambda b, i, *_: (b, jnp.maximum(i * per - 1, 0), col)),
            pl.BlockSpec((None, tm, width), lambda b, i, *_: (b, i, col)),
            pl.BlockSpec((None, HALO, width), lambda b, i, *_: (b, jnp.minimum((i + 1) * per, last), col))], nt


def _fill_halo(buf, prev_ref, cur_ref, next_ref, i, nt, tm):
    zero = jnp.zeros(prev_ref.shape, F32)
    buf[0:HALO, :] = jnp.where(i > 0, prev_ref[...], zero)
    buf[HALO:HALO + tm, :] = cur_ref[...]
    buf[HALO + tm:, :] = jnp.where(i < nt - 1, next_ref[...], zero)


def _pool_kernel(prev_ref, cur_ref, next_ref, w_ref, ps_ref, o_ref, buf, *, tm, nt, length):
    i = pl.program_id(1)
    _fill_halo(buf, prev_ref, cur_ref, next_ref, i, nt, tm)

    def sh(j):
        return buf[HALO + j:HALO + j + tm, :]

    u = sh(0)
    sums = []
    acc = None
    lo, hi = 0, 0
    for w in POOL_SIZES:
        for j in list(range(-(w // 2), lo)) + list(range(hi, w // 2)):
            acc = sh(j) if acc is None else acc + sh(j)
        lo, hi = -(w // 2), w // 2
        sums.append(acc)
    lane = lax.broadcasted_iota(jnp.int32, (tm, D_GROUP), 1)
    t = (i * tm + lax.broadcasted_iota(jnp.int32, (tm, D_GROUP), 0))
    wsum = sums[-1]
    half = jnp.full((tm, D_GROUP), POOL_SIZES[-1] // 2, jnp.int32)
    for g in range(len(POOL_SIZES) - 2, -1, -1):
        sel = lane < (g + 1) * POOL_CH
        wsum = jnp.where(sel, sums[g], wsum)
        half = jnp.where(sel, POOL_SIZES[g] // 2, half)
    cnt = (jnp.minimum(t + half, length) - jnp.maximum(t - half, 0)).astype(F32)
    d = wsum / cnt - u
    o_ref[...] = _dot(d.astype(BF16), w_ref[...]) * ps_ref[...]


def _pool_mix(u, wbd, pool_scale, tm):
    bsz, length, _ = u.shape
    specs, nt = _halo_specs(tm, length, 3, D_GROUP)
    return pl.pallas_call(
        functools.partial(_pool_kernel, tm=tm, nt=nt, length=length),
        grid=(bsz, nt),
        in_specs=specs + [pl.BlockSpec((D_GROUP, D_GROUP), lambda b, i: (0, 0)),
                          pl.BlockSpec((1, D_GROUP), lambda b, i: (0, 0))],
        out_specs=pl.BlockSpec((None, tm, D_GROUP), lambda b, i: (b, i, 0)),
        out_shape=jax.ShapeDtypeStruct((bsz, length, D_GROUP), F32),
        scratch_shapes=[pltpu.VMEM((tm + 2 * HALO, D_GROUP), F32)],
        compiler_params=_params(("parallel", "arbitrary")),
        name="pool_mix",
    )(u, u, u, wbd, pool_scale)


def _hy_short_kernel(prev_ref, cur_ref, next_ref, w_ref, b_ref, o_ref, buf, *, tm, nt):
    i = pl.program_id(1)
    _fill_halo(buf, prev_ref, cur_ref, next_ref, i, nt, tm)
    w = w_ref[...]
    y = (buf[HALO - 1:HALO - 1 + tm, :] * w[0:1] + buf[HALO:HALO + tm, :] * w[1:2]
         + buf[HALO + 1:HALO + 1 + tm, :] * w[2:3] + b_ref[...])
    o_ref[...] = y.T


def _hy_short(u, w_short, b_short, tm):
    bsz, length, _ = u.shape
    nt = length // tm
    per = tm // HALO
    last = length // HALO - 1
    c0 = 4
    in_specs = [pl.BlockSpec((None, HALO, HY_CH), lambda b, i, j: (b, jnp.maximum(i * per - 1, 0), c0 + j)),
                pl.BlockSpec((None, tm, HY_CH), lambda b, i, j: (b, i, c0 + j)),
                pl.BlockSpec((None, HALO, HY_CH), lambda b, i, j: (b, jnp.minimum((i + 1) * per, last), c0 + j)),
                pl.BlockSpec((3, HY_CH), lambda b, i, j: (0, j)),
                pl.BlockSpec((1, HY_CH), lambda b, i, j: (0, j))]
    return pl.pallas_call(
        functools.partial(_hy_short_kernel, tm=tm, nt=nt),
        grid=(bsz, nt, 3),
        in_specs=in_specs,
        out_specs=pl.BlockSpec((None, None, HY_CH, tm), lambda b, i, j: (j, b, 0, i)),
        out_shape=jax.ShapeDtypeStruct((3, bsz, HY_CH, length), F32),
        scratch_shapes=[pltpu.VMEM((tm + 2 * HALO, HY_CH), F32)],
        compiler_params=_params(("parallel", "arbitrary", "arbitrary")),
        name="hyena_short_conv",
    )(u, u, u, w_short, b_short)


HY_FEAT = 40


def _hy_filter_kernel(band_ref, w1_ref, b1_ref, w2_ref, b2_ref, w3_ref, b3_ref, dl_ref,
                      k_ref, ssq_ref, nrm_ref, *, tp, length):
    i = pl.program_id(0)
    n_i = pl.num_programs(0)
    m = i * tp + lax.broadcasted_iota(jnp.int32, (1, tp), 1)
    t = jnp.where(m <= length, m, 2 * length - m).astype(F32)
    t_norm = t / max(length - 1, 1)
    ang = ((2.0 * math.pi / length) * t) * band_ref[...]
    row = lax.broadcasted_iota(jnp.int32, (HY_FEAT, tp), 0)
    z = jnp.where(row == 0, t_norm,
                  jnp.where(row <= HY_BANDS, jnp.cos(ang), jnp.where(row < HY_EMB, jnp.sin(ang), 0.0)))
    z = jnp.concatenate([z, jnp.zeros((LANES - HY_FEAT, tp), F32)], axis=0)
    h = jnp.sin(HY_SIN_FREQ * (_dot(w1_ref[...], z, HI) + b1_ref[...]))
    h = jnp.sin(HY_SIN_FREQ * (_dot(w2_ref[...], h, HI) + b2_ref[...]))
    hh, hl = _split_bf16(h)
    h = _dot(w3_ref[...], jnp.concatenate([hh, hl, hh], axis=0)) + b3_ref[...]
    h = h * jnp.exp(-t_norm * dl_ref[...])

    @pl.when(i == 0)
    def _():
        ssq_ref[...] = jnp.zeros(ssq_ref.shape, F32)

    for o in range(2):
        fwd = h[o * 2 * HY_CH:o * 2 * HY_CH + HY_CH]
        bwd = h[o * 2 * HY_CH + HY_CH:(o + 1) * 2 * HY_CH]
        k = jnp.where(m < length, fwd, jnp.where(m == length, 0.0, bwd))
        k_ref[o] = k
        extra = jnp.where(m == 0, bwd * bwd, 0.0)
        ssq_ref[o] += jnp.sum(k * k + extra, axis=1, keepdims=True)

    @pl.when(i == n_i - 1)
    def _():
        nrm_ref[...] = lax.rsqrt(ssq_ref[...] + EPS)


def _hy_filters(length, w1, b1, w2, b2, w3, b3, tp):
    bands = jnp.linspace(1e-4, HY_BANDS - 1, HY_BANDS, dtype=F32)
    band_col = jnp.concatenate([jnp.zeros((1,), F32), bands, bands,
                                jnp.zeros((HY_FEAT - HY_EMB,), F32)])[:, None]
    deltas = jnp.abs(jnp.linspace(HY_MIN_DECAY, HY_MAX_DECAY, HY_CH, dtype=F32))
    dl_col = jnp.tile(deltas, 4)[:, None]
    w1t = jnp.pad(w1.astype(F32).T, ((0, 0), (0, LANES - HY_EMB)))
    full = lambda shape: pl.BlockSpec(shape, lambda i: (0,) * len(shape))
    n = 2 * length
    return pl.pallas_call(
        functools.partial(_hy_filter_kernel, tp=tp, length=length),
        grid=(n // tp,),
        in_specs=[full((HY_FEAT, 1)), full((HY_HIDDEN, LANES)), full((HY_HIDDEN, 1)),
                  full((HY_HIDDEN, HY_HIDDEN)), full((HY_HIDDEN, 1)),
                  full((4 * HY_CH, 3 * HY_HIDDEN)), full((4 * HY_CH, 1)), full((4 * HY_CH, 1))],
        out_specs=[pl.BlockSpec((2, HY_CH, tp), lambda i: (0, 0, i)),
                   full((2, HY_CH, 1)), full((2, HY_CH, 1))],
        out_shape=[jax.ShapeDtypeStruct((2, HY_CH, n), F32),
                   jax.ShapeDtypeStruct((2, HY_CH, 1), F32),
                   jax.ShapeDtypeStruct((2, HY_CH, 1), F32)],
        compiler_params=_params(("arbitrary",)),
        name="hyena_filters",
    )(band_col, w1t, b1.astype(F32)[:, None], w2.astype(F32).T, b2.astype(F32)[:, None],
      _cat3(w3.astype(F32).T, 1), b3.astype(F32)[:, None], dl_col)


def _dft_consts(n1, n2):
    n = n1 * n2
    a1 = 2.0 * np.pi * ((np.arange(n1)[:, None] * np.arange(n1)[None, :]) % n1) / n1
    c1, s1 = np.cos(a1), np.sin(a1)
    a2 = 2.0 * np.pi * ((np.arange(n2)[:, None] * np.arange(n2)[None, :]) % n2) / n2
    c2, s2 = np.cos(a2), np.sin(a2)
    at = 2.0 * np.pi * ((np.arange(n1)[:, None] * np.arange(n2)[None, :]) % n) / n
    f1_full = np.concatenate([c1, -s1], axis=0)
    f1_half = f1_full[:, :n1 // 2]
    g1 = np.concatenate([c1[:n1 // 2], -s1[:n1 // 2]], axis=1)
    w2f = np.block([[c2, -s2], [s2, c2]])
    w2i = np.block([[c2, s2], [-s2, c2]])
    f = lambda x: jnp.asarray(x, F32)
    return dict(f1_full=f(f1_full), f1_half=f(f1_half), g1=f(g1), w2f=f(w2f), w2i=f(w2i),
                tr=f(np.cos(at)), ti=f(-np.sin(at)))


def _fft_fwd(slabs, f1, tr, ti, w2f, stack, n1, n2, prec):
    for c, x in enumerate(slabs):
        a = _dot(f1, x.astype(stack.dtype), prec)
        ar, ai = a[:n1], a[n1:]
        stack[c * n1:(c + 1) * n1, 0:n2] = (ar * tr - ai * ti).astype(stack.dtype)
        stack[c * n1:(c + 1) * n1, n2:2 * n2] = (ar * ti + ai * tr).astype(stack.dtype)
    return _dot(stack[...], w2f, prec)


def _split_bf16(x):
    hi = x.astype(BF16)
    return hi, (x - hi.astype(F32)).astype(BF16)


def _cat3(x, axis):
    hi, lo = _split_bf16(x)
    return jnp.concatenate([hi, hi, lo], axis=axis)


def _hy_spec_kernel(nrm_ref, k_ref, f1_ref, tr_ref, ti_ref, w2f_ref, o_ref, stack, *, cg, n1, n2):
    o = pl.program_id(0)
    g = pl.program_id(1)
    tr = tr_ref[...]
    ti = ti_ref[...]
    for c in range(cg):
        hi, lo = _split_bf16(k_ref[c])
        a = _dot(f1_ref[...], jnp.concatenate([hi, lo, hi], axis=0))
        ar, ai = a[:n1], a[n1:]
        sh, sl = _split_bf16(jnp.concatenate([ar * tr - ai * ti, ar * ti + ai * tr], axis=1))
        stack[c * n1:(c + 1) * n1, :] = jnp.concatenate([sh, sl, sh], axis=1)
    x = _dot(stack[...], w2f_ref[...])
    for c in range(cg):
        sc = nrm_ref[o * HY_CH + g * cg + c] * (1.0 / (n1 * n2))
        xc = x[c * n1:(c + 1) * n1] * sc
        o_ref[c, 0] = xc[:, :n2]
        o_ref[c, 1] = xc[:, n2:]


def _hy_spec(k4, nrm_flat, dc, cg, n1, n2):
    full = lambda shape: pl.BlockSpec(shape, lambda o, g: (0,) * len(shape))
    return pl.pallas_call(
        functools.partial(_hy_spec_kernel, cg=cg, n1=n1, n2=n2),
        grid=(2, HY_CH // cg),
        in_specs=[pl.BlockSpec(memory_space=pltpu.SMEM),
                  pl.BlockSpec((None, cg, n1, n2), lambda o, g: (o, g, 0, 0)),
                  full((2 * n1, 3 * n1)), full((n1, n2)), full((n1, n2)), full((6 * n2, 2 * n2))],
        out_specs=pl.BlockSpec((None, cg, 2, n1, n2), lambda o, g: (o, g, 0, 0, 0)),
        out_shape=jax.ShapeDtypeStruct((2, HY_CH, 2, n1, n2), F32),
        scratch_shapes=[pltpu.VMEM((cg * n1, 6 * n2), BF16)],
        compiler_params=_params(("parallel", "arbitrary")),
        name="hyena_filter_spectrum",
    )(nrm_flat, k4, _cat3(dc["f1_full"], 1), dc["tr"], dc["ti"], _cat3(dc["w2f"], 0))


def _hy_conv_kernel(skip_ref, x_ref, ks_ref, f1_ref, g1_ref, tr_ref, ti_ref, w2f_ref, w2i_ref,
                    o_ref, stack, *, cg, n1, n2):
    g = pl.program_id(1)
    tr = tr_ref[...]
    ti = ti_ref[...]

    def conv(slabs, order):
        x = _fft_fwd(slabs, f1_ref[...], tr, ti, w2f_ref[...], stack, n1, n2, None)
        for c in range(cg):
            xr, xi = x[c * n1:(c + 1) * n1, :n2], x[c * n1:(c + 1) * n1, n2:]
            kr, ki = ks_ref[order, c, 0], ks_ref[order, c, 1]
            stack[c * n1:(c + 1) * n1, 0:n2] = (xr * kr - xi * ki).astype(BF16)
            stack[c * n1:(c + 1) * n1, n2:2 * n2] = (xr * ki + xi * kr).astype(BF16)
        bm = _dot(stack[...], w2i_ref[...])
        outs = []
        for c in range(cg):
            br, bi = bm[c * n1:(c + 1) * n1, :n2], bm[c * n1:(c + 1) * n1, n2:]
            b2 = jnp.concatenate([br * tr + bi * ti, bi * tr - br * ti], axis=0)
            y = _dot(g1_ref[...], b2.astype(BF16))
            outs.append(y + slabs[c] * skip_ref[order * HY_CH + g * cg + c])
        return outs

    v = [x_ref[2, c] for c in range(cg)]
    y0 = conv(v, 0)
    z = [x_ref[0, c] * y0[c] for c in range(cg)]
    y1 = conv(z, 1)
    for c in range(cg):
        o_ref[c] = x_ref[1, c] * y1[c]


def _hy_conv(x4, kspec, skip_flat, dc, cg, n1, n2):
    bsz = x4.shape[1]
    full = lambda shape: pl.BlockSpec(shape, lambda b, g: (0,) * len(shape))
    return pl.pallas_call(
        functools.partial(_hy_conv_kernel, cg=cg, n1=n1, n2=n2),
        grid=(bsz, HY_CH // cg),
        in_specs=[pl.BlockSpec(memory_space=pltpu.SMEM),
                  pl.BlockSpec((3, None, cg, n1 // 2, n2), lambda b, g: (0, b, g, 0, 0)),
                  pl.BlockSpec((2, cg, 2, n1, n2), lambda b, g: (0, g, 0, 0, 0)),
                  full((2 * n1, n1 // 2)), full((n1 // 2, 2 * n1)), full((n1, n2)), full((n1, n2)),
                  full((2 * n2, 2 * n2)), full((2 * n2, 2 * n2))],
        out_specs=pl.BlockSpec((None, cg, n1 // 2, n2), lambda b, g: (b, g, 0, 0)),
        out_shape=jax.ShapeDtypeStruct((bsz, HY_CH, n1 // 2, n2), F32),
        scratch_shapes=[pltpu.VMEM((cg * n1, 2 * n2), BF16)],
        compiler_params=_params(("parallel", "arbitrary")),
        name="hyena_long_conv",
    )(skip_flat, x4, kspec, dc["f1_half"].astype(BF16), dc["g1"].astype(BF16), dc["tr"], dc["ti"],
      dc["w2f"].astype(BF16), dc["w2i"].astype(BF16))


def _hy_ctx_kernel(x_ref, k_ref, nrm_ref, skip_ref, fc_ref, fs_ref, o_ref, *, c):
    fc = fc_ref[...]
    fs = fs_ref[...]
    inv_n = 1.0 / (2 * c)

    def conv(x, order):
        kk = k_ref[order]
        kr, ki = _dot(kk, fc, HI), -_dot(kk, fs, HI)
        xr, xi = _dot(x, fc[:c], HI), -_dot(x, fs[:c], HI)
        yr, yi = xr * kr - xi * ki, xr * ki + xi * kr
        y = (_dot(yr, fc[:, :c], HI) - _dot(yi, fs[:, :c], HI)) * inv_n
        return y * nrm_ref[order] + x * skip_ref[order]

    z = x_ref[0] * conv(x_ref[2], 0)
    o_ref[...] = x_ref[1] * conv(z, 1)


def _hy_ctx_conv(xt, kt, nrm, skip_col):
    _, bsz, ch, c = xt.shape
    n = 2 * c
    ang = 2.0 * np.pi * ((np.arange(n)[:, None] * np.arange(n)[None, :]) % n) / n
    fc, fs = jnp.asarray(np.cos(ang), F32), jnp.asarray(np.sin(ang), F32)
    full = lambda shape: pl.BlockSpec(shape, lambda b: (0,) * len(shape))
    return pl.pallas_call(
        functools.partial(_hy_ctx_kernel, c=c),
        grid=(bsz,),
        in_specs=[pl.BlockSpec((3, None, ch, c), lambda b: (0, b, 0, 0)),
                  full((2, ch, n)), full((2, ch, 1)), full((2, ch, 1)), full((n, n)), full((n, n))],
        out_specs=pl.BlockSpec((None, ch, c), lambda b: (b, 0, 0)),
        out_shape=jax.ShapeDtypeStruct((bsz, ch, c), F32),
        compiler_params=_params(("arbitrary",)),
        name="hyena_ctx_conv",
    )(xt, kt, nrm, skip_col, fc, fs)


def _out_proj_kernel(a_ref, b_ref, p_ref, ht_ref, x_ref, g_ref, w_ref, o_ref):
    w = D_GROUP
    acc = _dot(a_ref[...].astype(BF16), w_ref[0:w])
    acc += _dot(b_ref[...].astype(BF16), w_ref[w:2 * w])
    acc += _dot(p_ref[...].astype(BF16), w_ref[2 * w:3 * w])
    acc += _dot(ht_ref[...].T.astype(BF16), w_ref[3 * w:4 * w])
    o_ref[...] = x_ref[...] + g_ref[...] * acc


def _out_proj(a, b, p, ht, x, mod3, row_of_batch, j_gate, w_out, tm):
    bsz, t, d = x.shape
    w = D_GROUP
    tok = pl.BlockSpec((None, tm, w), lambda bb, i: (bb, i, 0))
    return pl.pallas_call(
        _out_proj_kernel,
        grid=(bsz, t // tm),
        in_specs=[tok, tok, tok,
                  pl.BlockSpec((None, w, tm), lambda bb, i: (bb, 0, i)),
                  pl.BlockSpec((None, tm, d), lambda bb, i: (bb, i, 0)),
                  pl.BlockSpec((None, 1, d), lambda bb, i: (row_of_batch(bb), 0, j_gate)),
                  pl.BlockSpec((4 * w, d), lambda bb, i: (0, 0))],
        out_specs=pl.BlockSpec((None, tm, d), lambda bb, i: (bb, i, 0)),
        out_shape=jax.ShapeDtypeStruct((bsz, t, d), F32),
        compiler_params=_params(("parallel", "arbitrary")),
        name="out_proj_residual",
    )(a, b, p, ht, x, mod3, w_out)


MOE_ROWS = 256


def _route(x, g, sc, sh, rw3, rbc):
    per_group = N_EXPERTS // N_EXPERT_GROUPS
    tm = x.shape[0]
    ms = jnp.mean(x * x, axis=-1, keepdims=True)
    h = (x * lax.rsqrt(ms + EPS)) * g * (1.0 + sc) + sh
    hh, hl = _split_bf16(h)
    d = x.shape[1]
    logits = (_dot(hh, rw3[0:d]) + _dot(hl, rw3[d:2 * d]) + _dot(hh, rw3[2 * d:3 * d])).T[:N_EXPERTS]
    ex = jnp.exp(logits - jnp.max(logits, axis=0, keepdims=True))
    scores = ex / jnp.sum(ex, axis=0, keepdims=True)
    sel = scores + rbc
    srow = [sel[r:r + 1] for r in range(N_EXPERTS)]
    best = None
    for grp in range(N_EXPERT_GROUPS):
        rows = list(range(grp * per_group, (grp + 1) * per_group))
        v1 = functools.reduce(jnp.maximum, [srow[r] for r in rows])
        i1 = jnp.full((1, tm), rows[-1], jnp.int32)
        for r in reversed(rows[:-1]):
            i1 = jnp.where(srow[r] == v1, r, i1)
        rest = [jnp.where(i1 == r, -jnp.inf, srow[r]) for r in rows]
        v2 = functools.reduce(jnp.maximum, rest)
        i2 = jnp.full((1, tm), rows[-1], jnp.int32)
        for k in reversed(range(per_group - 1)):
            i2 = jnp.where(rest[k] == v2, rows[k], i2)
        gs = v1 + v2
        if best is None:
            best, e1, e2 = gs, i1, i2
        else:
            upd = gs > best
            best = jnp.where(upd, gs, best)
            e1 = jnp.where(upd, i1, e1)
            e2 = jnp.where(upd, i2, e2)
    row = lax.broadcasted_iota(jnp.int32, (N_EXPERTS, tm), 0)
    w1 = jnp.sum(jnp.where(row == e1, scores, 0.0), axis=0, keepdims=True)
    w2 = jnp.sum(jnp.where(row == e2, scores, 0.0), axis=0, keepdims=True)
    tot = w1 + w2
    gates_t = jnp.where(row == e1, w1 / tot, 0.0) + jnp.where(row == e2, w2 / tot, 0.0)
    gates_t = jnp.concatenate([gates_t, jnp.zeros((LANES - N_EXPERTS, tm), F32)], axis=0)
    return h, gates_t.T


def _moe_kernel(x_ref, g_ref, sc_ref, sh_ref, gate_ref, rw_ref, rb_ref, w1_ref, w3_ref, w2_ref, fg_ref,
                o_ref, h_scr, gates_scr, acc_scr, *, final):
    e = pl.program_id(2)
    tm = x_ref.shape[0]

    @pl.when(e == 0)
    def _():
        h, gates = _route(x_ref[...], g_ref[...], sc_ref[...], sh_ref[...], rw_ref[...], rb_ref[...])
        h_scr[...] = h.astype(BF16)
        gates_scr[...] = gates
        acc_scr[...] = jnp.zeros(acc_scr.shape, F32)

    lane = lax.broadcasted_iota(jnp.int32, (tm, LANES), 1)
    ge = jnp.sum(jnp.where(lane == e, gates_scr[...], 0.0), axis=1, keepdims=True)

    def up(j):
        hb = h_scr[j * MOE_ROWS:(j + 1) * MOE_ROWS, :]
        return _dot(hb, w1_ref[...]), _dot(hb, w3_ref[...])

    nxt = up(0)
    for j in range(tm // MOE_ROWS):
        a, b = nxt
        if (j + 1) * MOE_ROWS < tm:
            nxt = up(j + 1)
        rows = slice(j * MOE_ROWS, (j + 1) * MOE_ROWS)
        act = (a * jax.nn.sigmoid(a)) * b
        acc_scr[rows, :] += ge[rows] * _dot(act.astype(BF16), w2_ref[...])

    @pl.when(e == N_EXPERTS - 1)
    def _():
        y = x_ref[...] + gate_ref[...] * acc_scr[...]
        if final:
            ms = jnp.mean(y * y, axis=-1, keepdims=True)
            y = (y * lax.rsqrt(ms + EPS)) * fg_ref[...]
        o_ref[...] = y


def _moe(x, g, mod3, row_of_batch, j_shift, j_scale, j_gate, rw, rb, w1, w3, w2, final_g, final, tm):
    bsz, t, d = x.shape
    vec = lambda j: pl.BlockSpec((None, 1, d), lambda b, i, e: (row_of_batch(b), 0, j))
    full = lambda shape: pl.BlockSpec(shape, lambda b, i, e: (0,) * len(shape))
    return pl.pallas_call(
        functools.partial(_moe_kernel, final=final),
        grid=(bsz, t // tm, N_EXPERTS),
        in_specs=[pl.BlockSpec((None, tm, d), lambda b, i, e: (b, i, 0)),
                  full((1, d)), vec(j_scale), vec(j_shift), vec(j_gate),
                  full((3 * d, LANES)), full((N_EXPERTS, 1)),
                  pl.BlockSpec((None, d, D_EXPERT), lambda b, i, e: (e, 0, 0)),
                  pl.BlockSpec((None, d, D_EXPERT), lambda b, i, e: (e, 0, 0)),
                  pl.BlockSpec((None, D_EXPERT, d), lambda b, i, e: (e, 0, 0)),
                  full((1, d))],
        out_specs=pl.BlockSpec((None, tm, d), lambda b, i, e: (b, i, 0)),
        out_shape=jax.ShapeDtypeStruct((bsz, t, d), F32),
        scratch_shapes=[pltpu.VMEM((tm, d), BF16), pltpu.VMEM((tm, LANES), F32), pltpu.VMEM((tm, d), F32)],
        compiler_params=_params(("parallel", "parallel", "arbitrary")),
        name="moe_final" if final else "moe",
    )(x, g, mod3, mod3, mod3, rw, rb, w1, w3, w2, final_g)


MOE_SUB = 512
MOE_CAP = 128
MOE_BLK = 2048
MOE_PAIR = 2


def _moe_route_kernel(x_ref, g_ref, sc_ref, sh_ref, rw_ref, rb_ref, h_ref, gates_ref, cnt_ref):
    h, gates = _route(x_ref[...], g_ref[...], sc_ref[...], sh_ref[...], rw_ref[...], rb_ref[...])
    h_ref[...] = h.astype(BF16)
    gates_ref[...] = gates
    cnt = jnp.sum(jnp.where(gates > 0.0, 1.0, 0.0), axis=0, keepdims=True)
    cnt_ref[...] = jnp.broadcast_to(cnt, cnt_ref.shape)


def _moe_route(x, g, mod3, row_of_batch, j_shift, j_scale, rw, rb):
    bsz, t, d = x.shape
    tm = MOE_SUB
    vec = lambda j: pl.BlockSpec((None, 1, d), lambda b, i: (row_of_batch(b), 0, j))
    full = lambda shape: pl.BlockSpec(shape, lambda b, i: (0,) * len(shape))
    return pl.pallas_call(
        _moe_route_kernel,
        grid=(bsz, t // tm),
        in_specs=[pl.BlockSpec((None, tm, d), lambda b, i: (b, i, 0)),
                  full((1, d)), vec(j_scale), vec(j_shift), full((3 * d, LANES)), full((N_EXPERTS, 1))],
        out_specs=[pl.BlockSpec((None, tm, d), lambda b, i: (b, i, 0)),
                   pl.BlockSpec((None, tm, LANES), lambda b, i: (b, i, 0)),
                   pl.BlockSpec((None, None, SUBLANES, LANES), lambda b, i: (b, i, 0, 0))],
        out_shape=[jax.ShapeDtypeStruct((bsz, t, d), BF16),
                   jax.ShapeDtypeStruct((bsz, t, LANES), F32),
                   jax.ShapeDtypeStruct((bsz, t // tm, SUBLANES, LANES), F32)],
        compiler_params=_params(("parallel", "arbitrary")),
        name="moe_route",
    )(x, g, mod3, mod3, rw, rb)


def _moe_routed_kernel(np_ref, h_ref, gates_ref, x_ref, gate_ref, w1_ref, w3_ref, w2_ref, fg_ref,
                       o_ref, rank_scr, rankt_scr, gatet_scr, xc_scr, *, final, nblk):
    b = pl.program_id(0)
    i = pl.program_id(1)
    e = pl.program_id(2)
    nsub = h_ref.shape[0] // MOE_SUB
    sub = lambda j: slice(j * MOE_SUB, (j + 1) * MOE_SUB)

    @pl.when(e == 0)
    def _():
        o_ref[...] = jnp.zeros(o_ref.shape, F32)
        r = lax.broadcasted_iota(jnp.int32, (MOE_SUB, MOE_SUB), 0)
        c = lax.broadcasted_iota(jnp.int32, (MOE_SUB, MOE_SUB), 1)
        ltri = jnp.where(c <= r, 1.0, 0.0).astype(BF16)
        utri = jnp.where(r <= c, 1.0, 0.0).astype(BF16)
        for j in range(nsub):
            gt = gates_ref[sub(j), :]
            rank_scr[j] = _dot(ltri, jnp.where(gt > 0.0, 1.0, 0.0).astype(BF16))
            gtt = gt.T[:N_EXPERTS]
            gatet_scr[j] = gtt
            rankt_scr[j] = _dot(jnp.where(gtt > 0.0, 1.0, 0.0).astype(BF16), utri)

    lane = lax.broadcasted_iota(jnp.int32, (MOE_SUB, LANES), 1)
    slot_lane = lax.broadcasted_iota(jnp.int32, (MOE_SUB, MOE_PAIR * MOE_CAP), 1)
    second = slot_lane >= MOE_CAP

    def one_pass(p, carry):
        base = (p * MOE_CAP + 1).astype(F32)
        slot_r = lax.broadcasted_iota(jnp.int32, (MOE_CAP, MOE_SUB), 0).astype(F32) + base
        slot_c = jnp.where(second, slot_lane - MOE_CAP, slot_lane).astype(F32) + base
        ys = []
        gcs = [[], []]
        for j in range(nsub):
            picks = []
            for k in range(MOE_PAIR):
                ex = e * MOE_PAIR + k
                rr = rankt_scr[j, pl.ds(ex, 1), :]
                gr = gatet_scr[j, pl.ds(ex, 1), :]
                pick = (rr == slot_r) & (gr > 0.0)
                picks.append(jnp.where(pick, 1.0, 0.0).astype(BF16))
                gcs[k].append(jnp.sum(jnp.where(pick, gr, 0.0), axis=1, keepdims=True))
            xcj = _dot(jnp.concatenate(picks, axis=0), h_ref[sub(j), :]).astype(BF16)
            for k in range(MOE_PAIR):
                xc_scr[k, j * MOE_CAP:(j + 1) * MOE_CAP, :] = xcj[k * MOE_CAP:(k + 1) * MOE_CAP]
        for k in range(MOE_PAIR):
            xc = xc_scr[k]
            a = _dot(xc, w1_ref[k])
            bb = _dot(xc, w3_ref[k])
            y = _dot(((a * jax.nn.sigmoid(a)) * bb).astype(BF16), w2_ref[k])
            ys.append([(y[j * MOE_CAP:(j + 1) * MOE_CAP] * gcs[k][j]).astype(BF16) for j in range(nsub)])
        for j in range(nsub):
            cols = []
            for k in range(MOE_PAIR):
                ex = e * MOE_PAIR + k
                rc = jnp.sum(jnp.where(lane == ex, rank_scr[j], 0.0), axis=1, keepdims=True)
                gc = jnp.sum(jnp.where(lane == ex, gates_ref[sub(j), :], 0.0), axis=1, keepdims=True)
                cols.append(jnp.where(gc > 0.0, rc, 0.0))
            put = jnp.where(jnp.where(second, cols[1], cols[0]) == slot_c, 1.0, 0.0).astype(BF16)
            o_ref[sub(j), :] += _dot(put, jnp.concatenate([ys[0][j], ys[1][j]], axis=0))
        return carry

    lax.fori_loop(0, np_ref[(b * nblk + i) * (N_EXPERTS // MOE_PAIR) + e], one_pass, 0)

    @pl.when(e == N_EXPERTS // MOE_PAIR - 1)
    def _():
        y = x_ref[...] + gate_ref[...] * o_ref[...]
        if final:
            ms = jnp.mean(y * y, axis=-1, keepdims=True)
            y = (y * lax.rsqrt(ms + EPS)) * fg_ref[...]
        o_ref[...] = y


def _moe_routed(x, g, mod3, row_of_batch, j_shift, j_scale, j_gate, rw, rb, w1, w3, w2, final_g, final):
    bsz, t, d = x.shape
    h, gates, cnt = _moe_route(x, g, mod3, row_of_batch, j_shift, j_scale, rw, rb)
    tb = _tile(t, MOE_BLK)
    nblk = t // tb
    nsub = tb // MOE_SUB
    npair = N_EXPERTS // MOE_PAIR
    passes = jnp.ceil(cnt[:, :, 0, :N_EXPERTS] / MOE_CAP).astype(jnp.int32)
    passes = jnp.max(passes.reshape(bsz, nblk, nsub, npair, MOE_PAIR), axis=(2, 4)).reshape(-1)
    full = lambda shape: pl.BlockSpec(shape, lambda b, i, e, np_: (0,) * len(shape))
    blk = lambda w: pl.BlockSpec((None, tb, w), lambda b, i, e, np_: (b, i, 0), pipeline_mode=pl.Buffered(1))
    grid_spec = pltpu.PrefetchScalarGridSpec(
        num_scalar_prefetch=1,
        grid=(bsz, nblk, npair),
        in_specs=[blk(d), blk(LANES), blk(d),
                  pl.BlockSpec((None, 1, d), lambda b, i, e, np_: (row_of_batch(b), 0, j_gate)),
                  pl.BlockSpec((MOE_PAIR, d, D_EXPERT), lambda b, i, e, np_: (e, 0, 0)),
                  pl.BlockSpec((MOE_PAIR, d, D_EXPERT), lambda b, i, e, np_: (e, 0, 0)),
                  pl.BlockSpec((MOE_PAIR, D_EXPERT, d), lambda b, i, e, np_: (e, 0, 0)),
                  full((1, d))],
        out_specs=pl.BlockSpec((None, tb, d), lambda b, i, e, np_: (b, i, 0)),
        scratch_shapes=[pltpu.VMEM((nsub, MOE_SUB, LANES), F32),
                        pltpu.VMEM((nsub, N_EXPERTS, MOE_SUB), F32),
                        pltpu.VMEM((nsub, N_EXPERTS, MOE_SUB), F32),
                        pltpu.VMEM((MOE_PAIR, nsub * MOE_CAP, d), BF16)])
    return pl.pallas_call(
        functools.partial(_moe_routed_kernel, final=final, nblk=nblk),
        grid_spec=grid_spec,
        out_shape=jax.ShapeDtypeStruct((bsz, t, d), F32),
        compiler_params=_params(("parallel", "parallel", "arbitrary")),
        name="moe_routed_final" if final else "moe_routed",
    )(passes, h, gates, x, mod3, w1, w3, w2, final_g)


def _tile(n, pref):
    t = min(n, pref)
    assert n % t == 0
    return t


def _fft_split(n):
    n2 = LANES
    assert n % n2 == 0
    return n // n2, n2


def _hyena_latent(u, lp, fargs):
    bsz, s, _ = u.shape
    n1, n2 = _fft_split(2 * s)
    dc = _dft_consts(n1, n2)
    cg = 8
    kt_l, _, nrm_l = _hy_filters(s, *fargs, tp=_tile(2 * s, 1024))
    kspec = _hy_spec(kt_l.reshape(2, HY_CH, n1, n2), nrm_l.reshape(2 * HY_CH), dc, cg, n1, n2)
    xt = _hy_short(u, lp["hy_short_w"].astype(F32), lp["hy_short_b"].astype(F32)[None, :], _tile(s, 512))
    h_l = _hy_conv(xt.reshape(3, bsz, HY_CH, n1 // 2, n2), kspec, lp["hy_skip"].astype(F32).reshape(2 * HY_CH),
                   dc, cg, n1, n2)
    return h_l.reshape(bsz, HY_CH, s)


def _mixers(lat_in, ctx_in, lp, li, need_ctx):
    u, ql, klt, vl = lat_in
    uc, qc, kct, vc = ctx_in
    bsz, s, _ = u.shape
    c = uc.shape[1]
    lam_init = 0.8 - 0.6 * math.exp(-0.3 * li)
    lam_vecs = lp["a_lambda"].astype(F32)
    subln = lp["a_subln_g"].astype(F32)[None, :]

    a_l = _diff_attn(ql, kct, vc, klt, vl, lam_vecs, subln, lam_init, _tile(s, 512), _tile(s, 4096))
    bias8 = _nbr_bias(lp["b_rpb"])
    b_l = _nbr_attn(u, uc, bias8)
    wbd = jax.scipy.linalg.block_diag(*[lp["pool_w"][g] for g in range(len(POOL_SIZES))]).astype(BF16)
    pscale = lp["pool_scale"].astype(F32)[None, :]
    p_l = _pool_mix(u, wbd, pscale, _tile(s, 512))
    fargs = (lp["hy_f_w1"], lp["hy_f_b1"], lp["hy_f_w2"], lp["hy_f_b2"], lp["hy_f_w3"], lp["hy_f_b3"])
    skip = lp["hy_skip"].astype(F32)
    w_short = lp["hy_short_w"].astype(F32)
    b_short = lp["hy_short_b"].astype(F32)[None, :]
    h_l = _hyena_latent(u, lp, fargs)
    lat = (a_l, b_l, p_l, h_l)
    if not need_ctx:
        return lat, None
    a_c = _diff_attn(qc, kct, vc, None, None, lam_vecs, subln, lam_init, _tile(c, 256), None)
    b_c = _nbr_ctx_attn(uc)
    p_c = _pool_mix(uc, wbd, pscale, _tile(c, 256))
    kt_c, _, nrm_c = _hy_filters(c, *fargs, tp=_tile(2 * c, 512))
    xtc = _hy_short(uc, w_short, b_short, _tile(c, 256))
    h_c = _hy_ctx_conv(xtc, kt_c, nrm_c, skip[:, :, None])
    return lat, (a_c, b_c, p_c, h_c)


def kernel(x, c, ctx, c_ctx, norm1_g, norm2_g, ada_w, ada_b, w_in, w_out, a_lambda, a_subln_g, b_rpb, pool_w, pool_scale, hy_short_w, hy_short_b, hy_f_w1, hy_f_b1, hy_f_w2, hy_f_b2, hy_f_w3, hy_f_b3, hy_skip, router_w, router_b, moe_w1, moe_w3, moe_w2, final_g):
    depth = norm1_g.shape[0]
    bsz, s, d = x.shape
    cl = ctx.shape[1]
    assert bsz <= SUBLANES - 1
    xl, xc = x, ctx
    cpad = jnp.zeros((SUBLANES, d), F32).at[:bsz].set(c.astype(F32)).at[bsz].set(c_ctx.astype(F32))
    rw = _cat3(jnp.pad(router_w.astype(F32), ((0, 0), (0, LANES - N_EXPERTS))), 0)
    rb = router_b.astype(F32)[:, None]
    lat_row = lambda b: b
    ctx_row = lambda b: bsz
    fg = final_g.astype(F32)[None, :]
    tm = _tile(s, 512)
    tmc = _tile(cl, 256)
    for li in range(depth):
        need_ctx = li < depth - 1
        lp = dict(a_lambda=a_lambda[li], a_subln_g=a_subln_g[li], b_rpb=b_rpb[li], pool_w=pool_w[li],
                  pool_scale=pool_scale[li], hy_short_w=hy_short_w[li], hy_short_b=hy_short_b[li],
                  hy_f_w1=hy_f_w1[li], hy_f_b1=hy_f_b1[li], hy_f_w2=hy_f_w2[li], hy_f_b2=hy_f_b2[li],
                  hy_f_w3=hy_f_w3[li], hy_f_b3=hy_f_b3[li], hy_skip=hy_skip[li])
        mod3 = _ada(cpad, ada_w[li].astype(F32), ada_b[li].astype(F32)[None, :]).reshape(SUBLANES, 1, 6 * d)
        n1g = norm1_g[li].astype(F32)[None, :]
        n2g = norm2_g[li].astype(F32)[None, :]
        w_in_b = w_in[li].astype(BF16)
        w_out_b = w_out[li].astype(BF16)
        lat_in = _norm_proj(xl, n1g, mod3, lat_row, 0, 1, w_in_b, tm, True)
        ctx_in = _norm_proj(xc, n1g, mod3, ctx_row, 0, 1, w_in_b, tmc, False)
        lat, cx = _mixers(lat_in, ctx_in, lp, li, need_ctx)
        xl = _out_proj(*lat, xl, mod3, lat_row, 2, w_out_b, tm)
        w1b, w3b, w2b = moe_w1[li].astype(BF16), moe_w3[li].astype(BF16), moe_w2[li].astype(BF16)
        if need_ctx:
            xc = _out_proj(*cx, xc, mod3, ctx_row, 2, w_out_b, tmc)
            xc = _moe(xc, n2g, mod3, ctx_row, 3, 4, 5, rw, rb, w1b, w3b, w2b, fg, False, tmc)
        xl = _moe_routed(xl, n2g, mod3, lat_row, 3, 4, 5, rw, rb, w1b, w3b, w2b, fg, li == depth - 1)
    return xl
```

```python
import functools
import math

import numpy as np
import jax
import jax.numpy as jnp
from jax import lax
from jax.experimental import pallas as pl
from jax.experimental.pallas import tpu as pltpu

F32 = jnp.float32
BF16 = jnp.bfloat16
HI = lax.Precision.HIGHEST

GRID_W = 64
A_HEADS = 4
A_QK = 32
A_V = 64
ROPE_BASE = 10000.0
B_HEADS = 4
B_DIM = 64
WIN_R = 8
WIN_C = 16
POOL_SIZES = (2, 4, 8, 16)
POOL_CH = 64
D_GROUP = 256
HY_CH = 256
HY_BANDS = 16
HY_EMB = 1 + 2 * HY_BANDS
HY_HIDDEN = 64
HY_SIN_FREQ = 1.0
HY_MIN_DECAY = math.log(1e-2) / 1.5
HY_MAX_DECAY = math.log(1e-2) / 0.3
N_EXPERTS = 16
N_EXPERT_GROUPS = 4
D_EXPERT = 512
EPS = 1e-6
LOG2E = 1.4426950408889634

LANES = 128
SUBLANES = 8
VMEM_LIMIT = 56 * 1024 * 1024


def _params(sem):
    return pltpu.CompilerParams(dimension_semantics=sem, vmem_limit_bytes=VMEM_LIMIT)


def _dot(a, b, prec=None):
    return jnp.dot(a, b, precision=prec, preferred_element_type=F32)


def _dot_nt(a, b):
    return lax.dot_general(a, b, (((1,), (1,)), ((), ())), preferred_element_type=F32)


def _ada_kernel(c_ref, w_ref, b_ref, o_ref):
    cf = c_ref[...]
    s = cf * jax.nn.sigmoid(cf)
    o_ref[...] = _dot(s, w_ref[...], HI) + b_ref[...]


def _ada(cpad, w, b):
    d = cpad.shape[1]
    n = w.shape[1]
    return pl.pallas_call(
        _ada_kernel,
        grid=(n // d,),
        in_specs=[pl.BlockSpec((SUBLANES, d), lambda j: (0, 0)),
                  pl.BlockSpec((d, d), lambda j: (0, j)),
                  pl.BlockSpec((1, d), lambda j: (0, j))],
        out_specs=pl.BlockSpec((SUBLANES, d), lambda j: (0, j)),
        out_shape=jax.ShapeDtypeStruct((SUBLANES, n), F32),
        compiler_params=_params(("arbitrary",)),
        name="ada_mod",
    )(cpad, w, b)


A_COLS = 768


def _attn_layout(u, cos_t, sin_t, q_ref, kt_ref, v_ref):
    q = u[:, 0:256]
    k = u[:, 256:512]
    v = u[:, 512:768]
    if cos_t is not None:
        lane = lax.broadcasted_iota(jnp.int32, cos_t.shape, 1)
        first = (lane % (2 * 16)) < 16

        def rot(x):
            halves = []
            for j in range(2):
                xh = x[:, j * LANES:(j + 1) * LANES]
                swap = jnp.where(first, pltpu.roll(xh, LANES - 16, axis=1), pltpu.roll(xh, 16, axis=1))
                halves.append(xh * cos_t + swap * sin_t)
            return jnp.concatenate(halves, axis=1)

        q = rot(q)
        k = rot(k)
    q = q * (A_QK ** -0.5 * LOG2E)
    kt = k.T
    for hc in range(2 * A_HEADS):
        q_ref[hc] = q[:, hc * A_QK:(hc + 1) * A_QK].astype(BF16)
        kt_ref[hc] = kt[hc * A_QK:(hc + 1) * A_QK, :].astype(BF16)
    lane = lax.broadcasted_iota(jnp.int32, (v.shape[0], LANES - A_V), 1)
    ones_col = jnp.where(lane == 0, 1.0, 0.0)
    for h in range(A_HEADS):
        v_ref[h] = jnp.concatenate([v[:, h * A_V:(h + 1) * A_V], ones_col], axis=1).astype(BF16)


def _rope_tables(length):
    n_freq = A_QK // 4
    inv = ROPE_BASE ** (-jnp.arange(n_freq, dtype=F32) / n_freq)
    t = jnp.arange(length)
    row = (t // GRID_W).astype(F32)
    col = (t % GRID_W).astype(F32)
    ang = jnp.concatenate([row[:, None] * inv, col[:, None] * inv], axis=-1)
    cos, sin = jnp.cos(ang), jnp.sin(ang)
    cos_t = jnp.tile(jnp.concatenate([cos, cos], axis=-1), (1, LANES // 32))
    sin_t = jnp.tile(jnp.concatenate([-sin, sin], axis=-1), (1, LANES // 32))
    return cos_t, sin_t


def _norm_proj_kernel(*refs, rope):
    if rope:
        x_ref, g_ref, sc_ref, sh_ref, w_ref, cos_ref, sin_ref, u_ref, q_ref, kt_ref, v_ref = refs
    else:
        x_ref, g_ref, sc_ref, sh_ref, w_ref, u_ref, q_ref, kt_ref, v_ref = refs
    x = x_ref[...]
    ms = jnp.mean(x * x, axis=-1, keepdims=True)
    h = (x * lax.rsqrt(ms + EPS)) * g_ref[...] * (1.0 + sc_ref[...]) + sh_ref[...]
    u = _dot(h.astype(BF16), w_ref[...])
    u_ref[...] = u[:, A_COLS:]
    _attn_layout(u[:, :A_COLS], cos_ref[...] if rope else None, sin_ref[...] if rope else None, q_ref, kt_ref, v_ref)


def _norm_proj(x, g, mod3, row_of_batch, j_shift, j_scale, w, tm, rope):
    bsz, t, d = x.shape
    n = w.shape[1]
    nh = 2 * A_HEADS
    in_specs = [pl.BlockSpec((None, tm, d), lambda b, i: (b, i, 0)),
                pl.BlockSpec((1, d), lambda b, i: (0, 0)),
                pl.BlockSpec((None, 1, d), lambda b, i: (row_of_batch(b), 0, j_scale)),
                pl.BlockSpec((None, 1, d), lambda b, i: (row_of_batch(b), 0, j_shift)),
                pl.BlockSpec((d, n), lambda b, i: (0, 0))]
    args = [x, g, mod3, mod3, w]
    if rope:
        in_specs += [pl.BlockSpec((tm, LANES), lambda b, i: (i, 0))] * 2
        args += list(_rope_tables(t))
    return pl.pallas_call(
        functools.partial(_norm_proj_kernel, rope=rope),
        grid=(bsz, t // tm),
        in_specs=in_specs,
        out_specs=[pl.BlockSpec((None, tm, n - A_COLS), lambda b, i: (b, i, 0)),
                   pl.BlockSpec((None, nh, tm, A_QK), lambda b, i: (b, 0, i, 0)),
                   pl.BlockSpec((None, nh, A_QK, tm), lambda b, i: (b, 0, 0, i)),
                   pl.BlockSpec((None, A_HEADS, tm, LANES), lambda b, i: (b, 0, i, 0))],
        out_shape=[jax.ShapeDtypeStruct((bsz, t, n - A_COLS), F32),
                   jax.ShapeDtypeStruct((bsz, nh, t, A_QK), BF16),
                   jax.ShapeDtypeStruct((bsz, nh, A_QK, t), BF16),
                   jax.ShapeDtypeStruct((bsz, A_HEADS, t, LANES), BF16)],
        compiler_params=_params(("parallel", "arbitrary")),
        name="norm_in_proj_rope" if rope else "norm_in_proj_ctx",
    )(*args)


QK_LOOKAHEAD = 2


def _dattn_kernel(*refs, lam_init, has_lat):
    if has_lat:
        lam_ref, g_ref, q_ref, kc_ref, vc_ref, k_ref, v_ref, o_ref, m_scr, acc_scr = refs
    else:
        lam_ref, g_ref, q_ref, kc_ref, vc_ref, o_ref, m_scr, acc_scr = refs
    ki = pl.program_id(2)
    nk = pl.num_programs(2)
    nh = 2 * A_HEADS

    def update(kt_r, v_r):
        scores = [_dot(q_ref[j], kt_r[j]) for j in range(QK_LOOKAHEAD)]
        for hc in range(nh):
            s = scores[hc]
            if hc + QK_LOOKAHEAD < nh:
                scores.append(_dot(q_ref[hc + QK_LOOKAHEAD], kt_r[hc + QK_LOOKAHEAD]))
            m_prev = m_scr[hc]
            m_new = jnp.maximum(m_prev, jnp.max(s, axis=1, keepdims=True))
            alpha = jnp.exp2(m_prev - m_new)
            p = jnp.exp2((s - m_new[:, :1]).astype(BF16))
            acc_scr[hc] = alpha * acc_scr[hc] + _dot(p, v_r[hc // 2])
            m_scr[hc] = m_new

    @pl.when(ki == 0)
    def _():
        m_scr[...] = jnp.full(m_scr.shape, -jnp.inf, F32)
        acc_scr[...] = jnp.zeros(acc_scr.shape, F32)
        update(kc_ref, vc_ref)

    if has_lat:
        @pl.when(ki > 0)
        def _():
            update(k_ref, v_ref)

    @pl.when(ki == nk - 1)
    def _():
        lv = lam_ref[...]
        lam = (jnp.exp(jnp.sum(lv[0:1] * lv[1:2], axis=1, keepdims=True))
               - jnp.exp(jnp.sum(lv[2:3] * lv[3:4], axis=1, keepdims=True)) + lam_init)
        for h in range(A_HEADS):
            a0 = acc_scr[2 * h]
            a1 = acc_scr[2 * h + 1]
            o = a0[:, :A_V] / a0[:, A_V:A_V + 1] - lam * (a1[:, :A_V] / a1[:, A_V:A_V + 1])
            ms = jnp.mean(o * o, axis=-1, keepdims=True)
            o_ref[:, h * A_V:(h + 1) * A_V] = (o * lax.rsqrt(ms + EPS)) * g_ref[...] * (1.0 - lam_init)


def _diff_attn(q, kct, vc, kt, v, lam_vecs, subln_g, lam_init, tq, tk):
    bsz, nh, t, _ = q.shape
    c = kct.shape[-1]
    has_lat = kt is not None
    nk = 1 + (kt.shape[-1] // tk if has_lat else 0)
    in_specs = [pl.BlockSpec((4, A_QK), lambda b, i, k: (0, 0)),
                pl.BlockSpec((1, A_V), lambda b, i, k: (0, 0)),
                pl.BlockSpec((None, nh, tq, A_QK), lambda b, i, k: (b, 0, i, 0)),
                pl.BlockSpec((None, nh, A_QK, c), lambda b, i, k: (b, 0, 0, 0)),
                pl.BlockSpec((None, A_HEADS, c, LANES), lambda b, i, k: (b, 0, 0, 0))]
    args = [lam_vecs, subln_g, q, kct, vc]
    if has_lat:
        in_specs += [pl.BlockSpec((None, nh, A_QK, tk), lambda b, i, k: (b, 0, 0, jnp.maximum(k - 1, 0))),
                     pl.BlockSpec((None, A_HEADS, tk, LANES), lambda b, i, k: (b, 0, jnp.maximum(k - 1, 0), 0))]
        args += [kt, v]
    return pl.pallas_call(
        functools.partial(_dattn_kernel, lam_init=lam_init, has_lat=has_lat),
        grid=(bsz, t // tq, nk),
        in_specs=in_specs,
        out_specs=pl.BlockSpec((None, tq, A_HEADS * A_V), lambda b, i, k: (b, i, 0)),
        out_shape=jax.ShapeDtypeStruct((bsz, t, A_HEADS * A_V), F32),
        scratch_shapes=[pltpu.VMEM((nh, tq, LANES), F32),
                        pltpu.VMEM((nh, tq, LANES), F32)],
        compiler_params=_params(("parallel", "parallel", "arbitrary")),
        name="diff_attn" if has_lat else "diff_attn_ctx",
    )(*args)


NB_ROWS = 8


def _nbr_bias(rpb):
    cols = jnp.arange(GRID_W)
    c0 = jnp.clip(cols - WIN_C // 2, 0, GRID_W - WIN_C)
    in_win = (cols[None, :] >= c0[:, None]) & (cols[None, :] < c0[:, None] + WIN_C)
    dc = jnp.clip(cols[None, :] - cols[:, None], -(WIN_C - 1), WIN_C - 1) + (WIN_C - 1)
    onehot = (dc[None] == jnp.arange(2 * WIN_C - 1)[:, None, None]).astype(F32)
    g = jnp.einsum("hab,bqk->haqk", rpb.astype(F32), onehot, precision=HI)
    g = jnp.where(in_win[None, None], g, -jnp.inf)
    b = jnp.stack([g[:, a0:a0 + WIN_R] for a0 in range(WIN_R)], axis=0)
    b = jnp.transpose(b, (0, 1, 3, 2, 4))
    return b.reshape(WIN_R, B_HEADS, GRID_W, WIN_R * GRID_W)


def _nbr_kernel(q_ref, kp_ref, kc_ref, kn_ref, vp_ref, vcur_ref, vn_ref, kctx_ref, vctx_ref, bias_ref,
                o_ref, kwin, vwin, kcx, vcx, *, n_rows):
    rb = pl.program_id(1)
    blk = NB_ROWS * GRID_W
    scale = B_DIM ** -0.5
    for h in range(B_HEADS):
        sl = slice(h * B_DIM, (h + 1) * B_DIM)
        for j, (kr, vr) in enumerate(((kp_ref, vp_ref), (kc_ref, vcur_ref), (kn_ref, vn_ref))):
            kwin[h, j * blk:(j + 1) * blk, :] = kr[:, sl].astype(BF16)
            vwin[h, j * blk:(j + 1) * blk, :] = vr[:, sl].astype(BF16)
        kcx[h] = kctx_ref[:, sl].astype(BF16)
        vcx[h] = vctx_ref[:, sl].astype(BF16)

    def window(rr):
        r = rb * NB_ROWS + rr
        r0 = jnp.clip(r - WIN_R // 2, 0, n_rows - WIN_R)
        off = pl.multiple_of((r0 - (rb - 1) * NB_ROWS) * GRID_W, GRID_W)
        return off, r0 - r + (WIN_R - 1)

    def scores(rr):
        off, a0 = window(rr)
        qrow = q_ref[rr * GRID_W:(rr + 1) * GRID_W, :]
        out = []
        for h in range(B_HEADS):
            qh = qrow[:, h * B_DIM:(h + 1) * B_DIM].astype(BF16)
            s = _dot_nt(qh, kwin[h, pl.ds(off, WIN_R * GRID_W), :]) * scale + bias_ref[a0, h]
            out.append((s, _dot_nt(qh, kcx[h]) * scale))
        return out

    nxt = scores(0)
    for rr in range(NB_ROWS):
        cur = nxt
        if rr + 1 < NB_ROWS:
            nxt = scores(rr + 1)
        off, _ = window(rr)
        outs = []
        for h in range(B_HEADS):
            s, sc = cur[h]
            m = jnp.maximum(jnp.max(s, axis=1, keepdims=True), jnp.max(sc, axis=1, keepdims=True))
            p = jnp.exp(s - m)
            pc = jnp.exp(sc - m)
            l = jnp.sum(p, axis=1, keepdims=True) + jnp.sum(pc, axis=1, keepdims=True)
            o = _dot(p.astype(BF16), vwin[h, pl.ds(off, WIN_R * GRID_W), :]) + _dot(pc.astype(BF16), vcx[h])
            outs.append(o / l)
        o_ref[rr * GRID_W:(rr + 1) * GRID_W, :] = jnp.concatenate(outs, axis=1)


def _nbr_attn(u, uc, bias8):
    bsz, s, _ = u.shape
    c = uc.shape[1]
    n_rows = s // GRID_W
    nb = n_rows // NB_ROWS
    blk = NB_ROWS * GRID_W
    w = B_HEADS * B_DIM

    def spec(col, shift):
        return pl.BlockSpec((None, blk, w), lambda b, i: (b, jnp.clip(i + shift, 0, nb - 1), col))

    return pl.pallas_call(
        functools.partial(_nbr_kernel, n_rows=n_rows),
        grid=(bsz, nb),
        in_specs=[spec(0, 0), spec(1, -1), spec(1, 0), spec(1, 1), spec(2, -1), spec(2, 0), spec(2, 1),
                  pl.BlockSpec((None, c, w), lambda b, i: (b, 0, 1)),
                  pl.BlockSpec((None, c, w), lambda b, i: (b, 0, 2)),
                  pl.BlockSpec(bias8.shape, lambda b, i: (0, 0, 0, 0))],
        out_specs=pl.BlockSpec((None, blk, w), lambda b, i: (b, i, 0)),
        out_shape=jax.ShapeDtypeStruct((bsz, s, w), F32),
        scratch_shapes=[pltpu.VMEM((B_HEADS, 3 * blk, B_DIM), BF16),
                        pltpu.VMEM((B_HEADS, 3 * blk, B_DIM), BF16),
                        pltpu.VMEM((B_HEADS, c, B_DIM), BF16),
                        pltpu.VMEM((B_HEADS, c, B_DIM), BF16)],
        compiler_params=_params(("parallel", "arbitrary")),
        name="nbr_attn",
    )(u, u, u, u, u, u, u, uc, uc, bias8)


def _nbr_ctx_kernel(q_ref, k_ref, v_ref, o_ref):
    scale = B_DIM ** -0.5
    outs = []
    for h in range(B_HEADS):
        sl = slice(h * B_DIM, (h + 1) * B_DIM)
        s = _dot_nt(q_ref[:, sl].astype(BF16), k_ref[:, sl].astype(BF16)) * scale
        m = jnp.max(s, axis=1, keepdims=True)
        p = jnp.exp(s - m)
        l = jnp.sum(p, axis=1, keepdims=True)
        outs.append(_dot(p.astype(BF16), v_ref[:, sl].astype(BF16)) / l)
    o_ref[...] = jnp.concatenate(outs, axis=1)


def _nbr_ctx_attn(uc):
    bsz, c, _ = uc.shape
    w = B_HEADS * B_DIM
    return pl.pallas_call(
        _nbr_ctx_kernel,
        grid=(bsz,),
        in_specs=[pl.BlockSpec((None, c, w), lambda b: (b, 0, 0)),
                  pl.BlockSpec((None, c, w), lambda b: (b, 0, 1)),
                  pl.BlockSpec((None, c, w), lambda b: (b, 0, 2))],
        out_specs=pl.BlockSpec((None, c, w), lambda b: (b, 0, 0)),
        out_shape=jax.ShapeDtypeStruct((bsz, c, w), F32),
        compiler_params=_params(("arbitrary",)),
        name="nbr_attn_ctx",
    )(uc, uc, uc)


HALO = SUBLANES


def _halo_specs(tm, length, col, width):
    nt = length // tm
    per = tm // HALO
    last = length // HALO - 1
    return [pl.BlockSpec((None, HALO, width), lambda b, i, *_: (b, jnp.maximum(i * per - 1, 0), col)),
            pl.BlockSpec((None, tm, width), lambda b, i, *_: (b, i, col)),
            pl.BlockSpec((None, HALO, width), lambda b, i, *_: (b, jnp.minimum((i + 1) * per, last), col))], nt


def _fill_halo(buf, prev_ref, cur_ref, next_ref, i, nt, tm):
    zero = jnp.zeros(prev_ref.shape, F32)
    buf[0:HALO, :] = jnp.where(i > 0, prev_ref[...], zero)
    buf[HALO:HALO + tm, :] = cur_ref[...]
    buf[HALO + tm:, :] = jnp.where(i < nt - 1, next_ref[...], zero)


def _pool_kernel(prev_ref, cur_ref, next_ref, w_ref, ps_ref, o_ref, buf, *, tm, nt, length):
    i = pl.program_id(1)
    _fill_halo(buf, prev_ref, cur_ref, next_ref, i, nt, tm)

    def sh(j):
        return buf[HALO + j:HALO + j + tm, :]

    u = sh(0)
    sums = []
    acc = None
    lo, hi = 0, 0
    for w in POOL_SIZES:
        for j in list(range(-(w // 2), lo)) + list(range(hi, w // 2)):
            acc = sh(j) if acc is None else acc + sh(j)
        lo, hi = -(w // 2), w // 2
        sums.append(acc)
    lane = lax.broadcasted_iota(jnp.int32, (tm, D_GROUP), 1)
    t = (i * tm + lax.broadcasted_iota(jnp.int32, (tm, D_GROUP), 0))
    wsum = sums[-1]
    half = jnp.full((tm, D_GROUP), POOL_SIZES[-1] // 2, jnp.int32)
    for g in range(len(POOL_SIZES) - 2, -1, -1):
        sel = lane < (g + 1) * POOL_CH
        wsum = jnp.where(sel, sums[g], wsum)
        half = jnp.where(sel, POOL_SIZES[g] // 2, half)
    cnt = (jnp.minimum(t + half, length) - jnp.maximum(t - half, 0)).astype(F32)
    d = wsum / cnt - u
    o_ref[...] = _dot(d.astype(BF16), w_ref[...]) * ps_ref[...]


def _pool_mix(u, wbd, pool_scale, tm):
    bsz, length, _ = u.shape
    specs, nt = _halo_specs(tm, length, 3, D_GROUP)
    return pl.pallas_call(
        functools.partial(_pool_kernel, tm=tm, nt=nt, length=length),
        grid=(bsz, nt),
        in_specs=specs + [pl.BlockSpec((D_GROUP, D_GROUP), lambda b, i: (0, 0)),
                          pl.BlockSpec((1, D_GROUP), lambda b, i: (0, 0))],
        out_specs=pl.BlockSpec((None, tm, D_GROUP), lambda b, i: (b, i, 0)),
        out_shape=jax.ShapeDtypeStruct((bsz, length, D_GROUP), F32),
        scratch_shapes=[pltpu.VMEM((tm + 2 * HALO, D_GROUP), F32)],
        compiler_params=_params(("parallel", "arbitrary")),
        name="pool_mix",
    )(u, u, u, wbd, pool_scale)


def _hy_short_kernel(prev_ref, cur_ref, next_ref, w_ref, b_ref, o_ref, buf, *, tm, nt):
    i = pl.program_id(1)
    _fill_halo(buf, prev_ref, cur_ref, next_ref, i, nt, tm)
    w = w_ref[...]
    y = (buf[HALO - 1:HALO - 1 + tm, :] * w[0:1] + buf[HALO:HALO + tm, :] * w[1:2]
         + buf[HALO + 1:HALO + 1 + tm, :] * w[2:3] + b_ref[...])
    o_ref[...] = y.T


def _hy_short(u, w_short, b_short, tm):
    bsz, length, _ = u.shape
    nt = length // tm
    per = tm // HALO
    last = length // HALO - 1
    c0 = 4
    in_specs = [pl.BlockSpec((None, HALO, HY_CH), lambda b, i, j: (b, jnp.maximum(i * per - 1, 0), c0 + j)),
                pl.BlockSpec((None, tm, HY_CH), lambda b, i, j: (b, i, c0 + j)),
                pl.BlockSpec((None, HALO, HY_CH), lambda b, i, j: (b, jnp.minimum((i + 1) * per, last), c0 + j)),
                pl.BlockSpec((3, HY_CH), lambda b, i, j: (0, j)),
                pl.BlockSpec((1, HY_CH), lambda b, i, j: (0, j))]
    return pl.pallas_call(
        functools.partial(_hy_short_kernel, tm=tm, nt=nt),
        grid=(bsz, nt, 3),
        in_specs=in_specs,
        out_specs=pl.BlockSpec((None, None, HY_CH, tm), lambda b, i, j: (j, b, 0, i)),
        out_shape=jax.ShapeDtypeStruct((3, bsz, HY_CH, length), F32),
        scratch_shapes=[pltpu.VMEM((tm + 2 * HALO, HY_CH), F32)],
        compiler_params=_params(("parallel", "arbitrary", "arbitrary")),
        name="hyena_short_conv",
    )(u, u, u, w_short, b_short)


HY_FEAT = 40


def _hy_filter_kernel(band_ref, w1_ref, b1_ref, w2_ref, b2_ref, w3_ref, b3_ref, dl_ref,
                      k_ref, ssq_ref, nrm_ref, *, tp, length):
    i = pl.program_id(0)
    n_i = pl.num_programs(0)
    m = i * tp + lax.broadcasted_iota(jnp.int32, (1, tp), 1)
    t = jnp.where(m <= length, m, 2 * length - m).astype(F32)
    t_norm = t / max(length - 1, 1)
    ang = ((2.0 * math.pi / length) * t) * band_ref[...]
    row = lax.broadcasted_iota(jnp.int32, (HY_FEAT, tp), 0)
    z = jnp.where(row == 0, t_norm,
                  jnp.where(row <= HY_BANDS, jnp.cos(ang), jnp.where(row < HY_EMB, jnp.sin(ang), 0.0)))
    z = jnp.concatenate([z, jnp.zeros((LANES - HY_FEAT, tp), F32)], axis=0)
    h = jnp.sin(HY_SIN_FREQ * (_dot(w1_ref[...], z, HI) + b1_ref[...]))
    h = jnp.sin(HY_SIN_FREQ * (_dot(w2_ref[...], h, HI) + b2_ref[...]))
    hh, hl = _split_bf16(h)
    h = _dot(w3_ref[...], jnp.concatenate([hh, hl, hh], axis=0)) + b3_ref[...]
    h = h * jnp.exp(-t_norm * dl_ref[...])

    @pl.when(i == 0)
    def _():
        ssq_ref[...] = jnp.zeros(ssq_ref.shape, F32)

    for o in range(2):
        fwd = h[o * 2 * HY_CH:o * 2 * HY_CH + HY_CH]
        bwd = h[o * 2 * HY_CH + HY_CH:(o + 1) * 2 * HY_CH]
        k = jnp.where(m < length, fwd, jnp.where(m == length, 0.0, bwd))
        k_ref[o] = k
        extra = jnp.where(m == 0, bwd * bwd, 0.0)
        ssq_ref[o] += jnp.sum(k * k + extra, axis=1, keepdims=True)

    @pl.when(i == n_i - 1)
    def _():
        nrm_ref[...] = lax.rsqrt(ssq_ref[...] + EPS)


def _hy_filters(length, w1, b1, w2, b2, w3, b3, tp):
    bands = jnp.linspace(1e-4, HY_BANDS - 1, HY_BANDS, dtype=F32)
    band_col = jnp.concatenate([jnp.zeros((1,), F32), bands, bands,
                                jnp.zeros((HY_FEAT - HY_EMB,), F32)])[:, None]
    deltas = jnp.abs(jnp.linspace(HY_MIN_DECAY, HY_MAX_DECAY, HY_CH, dtype=F32))
    dl_col = jnp.tile(deltas, 4)[:, None]
    w1t = jnp.pad(w1.astype(F32).T, ((0, 0), (0, LANES - HY_EMB)))
    full = lambda shape: pl.BlockSpec(shape, lambda i: (0,) * len(shape))
    n = 2 * length
    return pl.pallas_call(
        functools.partial(_hy_filter_kernel, tp=tp, length=length),
        grid=(n // tp,),
        in_specs=[full((HY_FEAT, 1)), full((HY_HIDDEN, LANES)), full((HY_HIDDEN, 1)),
                  full((HY_HIDDEN, HY_HIDDEN)), full((HY_HIDDEN, 1)),
                  full((4 * HY_CH, 3 * HY_HIDDEN)), full((4 * HY_CH, 1)), full((4 * HY_CH, 1))],
        out_specs=[pl.BlockSpec((2, HY_CH, tp), lambda i: (0, 0, i)),
                   full((2, HY_CH, 1)), full((2, HY_CH, 1))],
        out_shape=[jax.ShapeDtypeStruct((2, HY_CH, n), F32),
                   jax.ShapeDtypeStruct((2, HY_CH, 1), F32),
                   jax.ShapeDtypeStruct((2, HY_CH, 1), F32)],
        compiler_params=_params(("arbitrary",)),
        name="hyena_filters",
    )(band_col, w1t, b1.astype(F32)[:, None], w2.astype(F32).T, b2.astype(F32)[:, None],
      _cat3(w3.astype(F32).T, 1), b3.astype(F32)[:, None], dl_col)


def _dft_consts(n1, n2):
    n = n1 * n2
    a1 = 2.0 * np.pi * ((np.arange(n1)[:, None] * np.arange(n1)[None, :]) % n1) / n1
    c1, s1 = np.cos(a1), np.sin(a1)
    a2 = 2.0 * np.pi * ((np.arange(n2)[:, None] * np.arange(n2)[None, :]) % n2) / n2
    c2, s2 = np.cos(a2), np.sin(a2)
    at = 2.0 * np.pi * ((np.arange(n1)[:, None] * np.arange(n2)[None, :]) % n) / n
    f1_full = np.concatenate([c1, -s1], axis=0)
    f1_half = f1_full[:, :n1 // 2]
    g1 = np.concatenate([c1[:n1 // 2], -s1[:n1 // 2]], axis=1)
    w2f = np.block([[c2, -s2], [s2, c2]])
    w2i = np.block([[c2, s2], [-s2, c2]])
    f = lambda x: jnp.asarray(x, F32)
    return dict(f1_full=f(f1_full), f1_half=f(f1_half), g1=f(g1), w2f=f(w2f), w2i=f(w2i),
                tr=f(np.cos(at)), ti=f(-np.sin(at)))


def _fft_fwd(slabs, f1, tr, ti, w2f, stack, n1, n2, prec):
    for c, x in enumerate(slabs):
        a = _dot(f1, x.astype(stack.dtype), prec)
        ar, ai = a[:n1], a[n1:]
        stack[c * n1:(c + 1) * n1, 0:n2] = (ar * tr - ai * ti).astype(stack.dtype)
        stack[c * n1:(c + 1) * n1, n2:2 * n2] = (ar * ti + ai * tr).astype(stack.dtype)
    return _dot(stack[...], w2f, prec)


def _split_bf16(x):
    hi = x.astype(BF16)
    return hi, (x - hi.astype(F32)).astype(BF16)


def _cat3(x, axis):
    hi, lo = _split_bf16(x)
    return jnp.concatenate([hi, hi, lo], axis=axis)


def _hy_spec_kernel(nrm_ref, k_ref, f1_ref, tr_ref, ti_ref, w2f_ref, o_ref, stack, *, cg, n1, n2):
    o = pl.program_id(0)
    g = pl.program_id(1)
    tr = tr_ref[...]
    ti = ti_ref[...]
    for c in range(cg):
        hi, lo = _split_bf16(k_ref[c])
        a = _dot(f1_ref[...], jnp.concatenate([hi, lo, hi], axis=0))
        ar, ai = a[:n1], a[n1:]
        sh, sl = _split_bf16(jnp.concatenate([ar * tr - ai * ti, ar * ti + ai * tr], axis=1))
        stack[c * n1:(c + 1) * n1, :] = jnp.concatenate([sh, sl, sh], axis=1)
    x = _dot(stack[...], w2f_ref[...])
    for c in range(cg):
        sc = nrm_ref[o * HY_CH + g * cg + c] * (1.0 / (n1 * n2))
        xc = x[c * n1:(c + 1) * n1] * sc
        o_ref[c, 0] = xc[:, :n2]
        o_ref[c, 1] = xc[:, n2:]


def _hy_spec(k4, nrm_flat, dc, cg, n1, n2):
    full = lambda shape: pl.BlockSpec(shape, lambda o, g: (0,) * len(shape))
    return pl.pallas_call(
        functools.partial(_hy_spec_kernel, cg=cg, n1=n1, n2=n2),
        grid=(2, HY_CH // cg),
        in_specs=[pl.BlockSpec(memory_space=pltpu.SMEM),
                  pl.BlockSpec((None, cg, n1, n2), lambda o, g: (o, g, 0, 0)),
                  full((2 * n1, 3 * n1)), full((n1, n2)), full((n1, n2)), full((6 * n2, 2 * n2))],
        out_specs=pl.BlockSpec((None, cg, 2, n1, n2), lambda o, g: (o, g, 0, 0, 0)),
        out_shape=jax.ShapeDtypeStruct((2, HY_CH, 2, n1, n2), F32),
        scratch_shapes=[pltpu.VMEM((cg * n1, 6 * n2), BF16)],
        compiler_params=_params(("parallel", "arbitrary")),
        name="hyena_filter_spectrum",
    )(nrm_flat, k4, _cat3(dc["f1_full"], 1), dc["tr"], dc["ti"], _cat3(dc["w2f"], 0))


def _hy_conv_kernel(skip_ref, x_ref, ks_ref, f1_ref, g1_ref, tr_ref, ti_ref, w2f_ref, w2i_ref,
                    o_ref, stack, *, cg, n1, n2):
    g = pl.program_id(1)
    tr = tr_ref[...]
    ti = ti_ref[...]

    def conv(slabs, order):
        x = _fft_fwd(slabs, f1_ref[...], tr, ti, w2f_ref[...], stack, n1, n2, None)
        for c in range(cg):
            xr, xi = x[c * n1:(c + 1) * n1, :n2], x[c * n1:(c + 1) * n1, n2:]
            kr, ki = ks_ref[order, c, 0], ks_ref[order, c, 1]
            stack[c * n1:(c + 1) * n1, 0:n2] = (xr * kr - xi * ki).astype(BF16)
            stack[c * n1:(c + 1) * n1, n2:2 * n2] = (xr * ki + xi * kr).astype(BF16)
        bm = _dot(stack[...], w2i_ref[...])
        outs = []
        for c in range(cg):
            br, bi = bm[c * n1:(c + 1) * n1, :n2], bm[c * n1:(c + 1) * n1, n2:]
            b2 = jnp.concatenate([br * tr + bi * ti, bi * tr - br * ti], axis=0)
            y = _dot(g1_ref[...], b2.astype(BF16))
            outs.append(y + slabs[c] * skip_ref[order * HY_CH + g * cg + c])
        return outs

    v = [x_ref[2, c] for c in range(cg)]
    y0 = conv(v, 0)
    z = [x_ref[0, c] * y0[c] for c in range(cg)]
    y1 = conv(z, 1)
    for c in range(cg):
        o_ref[c] = x_ref[1, c] * y1[c]


def _hy_conv(x4, kspec, skip_flat, dc, cg, n1, n2):
    bsz = x4.shape[1]
    full = lambda shape: pl.BlockSpec(shape, lambda b, g: (0,) * len(shape))
    return pl.pallas_call(
        functools.partial(_hy_conv_kernel, cg=cg, n1=n1, n2=n2),
        grid=(bsz, HY_CH // cg),
        in_specs=[pl.BlockSpec(memory_space=pltpu.SMEM),
                  pl.BlockSpec((3, None, cg, n1 // 2, n2), lambda b, g: (0, b, g, 0, 0)),
                  pl.BlockSpec((2, cg, 2, n1, n2), lambda b, g: (0, g, 0, 0, 0)),
                  full((2 * n1, n1 // 2)), full((n1 // 2, 2 * n1)), full((n1, n2)), full((n1, n2)),
                  full((2 * n2, 2 * n2)), full((2 * n2, 2 * n2))],
        out_specs=pl.BlockSpec((None, cg, n1 // 2, n2), lambda b, g: (b, g, 0, 0)),
        out_shape=jax.ShapeDtypeStruct((bsz, HY_CH, n1 // 2, n2), F32),
        scratch_shapes=[pltpu.VMEM((cg * n1, 2 * n2), BF16)],
        compiler_params=_params(("parallel", "arbitrary")),
        name="hyena_long_conv",
    )(skip_flat, x4, kspec, dc["f1_half"].astype(BF16), dc["g1"].astype(BF16), dc["tr"], dc["ti"],
      dc["w2f"].astype(BF16), dc["w2i"].astype(BF16))


def _hy_ctx_kernel(x_ref, k_ref, nrm_ref, skip_ref, fc_ref, fs_ref, o_ref, *, c):
    fc = fc_ref[...]
    fs = fs_ref[...]
    inv_n = 1.0 / (2 * c)

    def conv(x, order):
        kk = k_ref[order]
        kr, ki = _dot(kk, fc, HI), -_dot(kk, fs, HI)
        xr, xi = _dot(x, fc[:c], HI), -_dot(x, fs[:c], HI)
        yr, yi = xr * kr - xi * ki, xr * ki + xi * kr
        y = (_dot(yr, fc[:, :c], HI) - _dot(yi, fs[:, :c], HI)) * inv_n
        return y * nrm_ref[order] + x * skip_ref[order]

    z = x_ref[0] * conv(x_ref[2], 0)
    o_ref[...] = x_ref[1] * conv(z, 1)


def _hy_ctx_conv(xt, kt, nrm, skip_col):
    _, bsz, ch, c = xt.shape
    n = 2 * c
    ang = 2.0 * np.pi * ((np.arange(n)[:, None] * np.arange(n)[None, :]) % n) / n
    fc, fs = jnp.asarray(np.cos(ang), F32), jnp.asarray(np.sin(ang), F32)
    full = lambda shape: pl.BlockSpec(shape, lambda b: (0,) * len(shape))
    return pl.pallas_call(
        functools.partial(_hy_ctx_kernel, c=c),
        grid=(bsz,),
        in_specs=[pl.BlockSpec((3, None, ch, c), lambda b: (0, b, 0, 0)),
                  full((2, ch, n)), full((2, ch, 1)), full((2, ch, 1)), full((n, n)), full((n, n))],
        out_specs=pl.BlockSpec((None, ch, c), lambda b: (b, 0, 0)),
        out_shape=jax.ShapeDtypeStruct((bsz, ch, c), F32),
        compiler_params=_params(("arbitrary",)),
        name="hyena_ctx_conv",
    )(xt, kt, nrm, skip_col, fc, fs)


def _out_proj_kernel(a_ref, b_ref, p_ref, ht_ref, x_ref, g_ref, w_ref, o_ref):
    w = D_GROUP
    acc = _dot(a_ref[...].astype(BF16), w_ref[0:w])
    acc += _dot(b_ref[...].astype(BF16), w_ref[w:2 * w])
    acc += _dot(p_ref[...].astype(BF16), w_ref[2 * w:3 * w])
    acc += _dot(ht_ref[...].T.astype(BF16), w_ref[3 * w:4 * w])
    o_ref[...] = x_ref[...] + g_ref[...] * acc


def _out_proj(a, b, p, ht, x, mod3, row_of_batch, j_gate, w_out, tm):
    bsz, t, d = x.shape
    w = D_GROUP
    tok = pl.BlockSpec((None, tm, w), lambda bb, i: (bb, i, 0))
    return pl.pallas_call(
        _out_proj_kernel,
        grid=(bsz, t // tm),
        in_specs=[tok, tok, tok,
                  pl.BlockSpec((None, w, tm), lambda bb, i: (bb, 0, i)),
                  pl.BlockSpec((None, tm, d), lambda bb, i: (bb, i, 0)),
                  pl.BlockSpec((None, 1, d), lambda bb, i: (row_of_batch(bb), 0, j_gate)),
                  pl.BlockSpec((4 * w, d), lambda bb, i: (0, 0))],
        out_specs=pl.BlockSpec((None, tm, d), lambda bb, i: (bb, i, 0)),
        out_shape=jax.ShapeDtypeStruct((bsz, t, d), F32),
        compiler_params=_params(("parallel", "arbitrary")),
        name="out_proj_residual",
    )(a, b, p, ht, x, mod3, w_out)


MOE_ROWS = 256


def _route(x, g, sc, sh, rw3, rbc):
    per_group = N_EXPERTS // N_EXPERT_GROUPS
    tm = x.shape[0]
    ms = jnp.mean(x * x, axis=-1, keepdims=True)
    h = (x * lax.rsqrt(ms + EPS)) * g * (1.0 + sc) + sh
    hh, hl = _split_bf16(h)
    d = x.shape[1]
    logits = (_dot(hh, rw3[0:d]) + _dot(hl, rw3[d:2 * d]) + _dot(hh, rw3[2 * d:3 * d])).T[:N_EXPERTS]
    ex = jnp.exp(logits - jnp.max(logits, axis=0, keepdims=True))
    scores = ex / jnp.sum(ex, axis=0, keepdims=True)
    sel = scores + rbc
    srow = [sel[r:r + 1] for r in range(N_EXPERTS)]
    best = None
    for grp in range(N_EXPERT_GROUPS):
        rows = list(range(grp * per_group, (grp + 1) * per_group))
        v1 = functools.reduce(jnp.maximum, [srow[r] for r in rows])
        i1 = jnp.full((1, tm), rows[-1], jnp.int32)
        for r in reversed(rows[:-1]):
            i1 = jnp.where(srow[r] == v1, r, i1)
        rest = [jnp.where(i1 == r, -jnp.inf, srow[r]) for r in rows]
        v2 = functools.reduce(jnp.maximum, rest)
        i2 = jnp.full((1, tm), rows[-1], jnp.int32)
        for k in reversed(range(per_group - 1)):
            i2 = jnp.where(rest[k] == v2, rows[k], i2)
        gs = v1 + v2
        if best is None:
            best, e1, e2 = gs, i1, i2
        else:
            upd = gs > best
            best = jnp.where(upd, gs, best)
            e1 = jnp.where(upd, i1, e1)
            e2 = jnp.where(upd, i2, e2)
    row = lax.broadcasted_iota(jnp.int32, (N_EXPERTS, tm), 0)
    w1 = jnp.sum(jnp.where(row == e1, scores, 0.0), axis=0, keepdims=True)
    w2 = jnp.sum(jnp.where(row == e2, scores, 0.0), axis=0, keepdims=True)
    tot = w1 + w2
    gates_t = jnp.where(row == e1, w1 / tot, 0.0) + jnp.where(row == e2, w2 / tot, 0.0)
    gates_t = jnp.concatenate([gates_t, jnp.zeros((LANES - N_EXPERTS, tm), F32)], axis=0)
    return h, gates_t.T


def _moe_kernel(x_ref, g_ref, sc_ref, sh_ref, gate_ref, rw_ref, rb_ref, w1_ref, w3_ref, w2_ref, fg_ref,
                o_ref, h_scr, gates_scr, acc_scr, *, final):
    e = pl.program_id(2)
    tm = x_ref.shape[0]

    @pl.when(e == 0)
    def _():
        h, gates = _route(x_ref[...], g_ref[...], sc_ref[...], sh_ref[...], rw_ref[...], rb_ref[...])
        h_scr[...] = h.astype(BF16)
        gates_scr[...] = gates
        acc_scr[...] = jnp.zeros(acc_scr.shape, F32)

    lane = lax.broadcasted_iota(jnp.int32, (tm, LANES), 1)
    ge = jnp.sum(jnp.where(lane == e, gates_scr[...], 0.0), axis=1, keepdims=True)

    def up(j):
        hb = h_scr[j * MOE_ROWS:(j + 1) * MOE_ROWS, :]
        return _dot(hb, w1_ref[...]), _dot(hb, w3_ref[...])

    nxt = up(0)
    for j in range(tm // MOE_ROWS):
        a, b = nxt
        if (j + 1) * MOE_ROWS < tm:
            nxt = up(j + 1)
        rows = slice(j * MOE_ROWS, (j + 1) * MOE_ROWS)
        act = (a * jax.nn.sigmoid(a)) * b
        acc_scr[rows, :] += ge[rows] * _dot(act.astype(BF16), w2_ref[...])

    @pl.when(e == N_EXPERTS - 1)
    def _():
        y = x_ref[...] + gate_ref[...] * acc_scr[...]
        if final:
            ms = jnp.mean(y * y, axis=-1, keepdims=True)
            y = (y * lax.rsqrt(ms + EPS)) * fg_ref[...]
        o_ref[...] = y


def _moe(x, g, mod3, row_of_batch, j_shift, j_scale, j_gate, rw, rb, w1, w3, w2, final_g, final, tm):
    bsz, t, d = x.shape
    vec = lambda j: pl.BlockSpec((None, 1, d), lambda b, i, e: (row_of_batch(b), 0, j))
    full = lambda shape: pl.BlockSpec(shape, lambda b, i, e: (0,) * len(shape))
    return pl.pallas_call(
        functools.partial(_moe_kernel, final=final),
        grid=(bsz, t // tm, N_EXPERTS),
        in_specs=[pl.BlockSpec((None, tm, d), lambda b, i, e: (b, i, 0)),
                  full((1, d)), vec(j_scale), vec(j_shift), vec(j_gate),
                  full((3 * d, LANES)), full((N_EXPERTS, 1)),
                  pl.BlockSpec((None, d, D_EXPERT), lambda b, i, e: (e, 0, 0)),
                  pl.BlockSpec((None, d, D_EXPERT), lambda b, i, e: (e, 0, 0)),
                  pl.BlockSpec((None, D_EXPERT, d), lambda b, i, e: (e, 0, 0)),
                  full((1, d))],
        out_specs=pl.BlockSpec((None, tm, d), lambda b, i, e: (b, i, 0)),
        out_shape=jax.ShapeDtypeStruct((bsz, t, d), F32),
        scratch_shapes=[pltpu.VMEM((tm, d), BF16), pltpu.VMEM((tm, LANES), F32), pltpu.VMEM((tm, d), F32)],
        compiler_params=_params(("parallel", "parallel", "arbitrary")),
        name="moe_final" if final else "moe",
    )(x, g, mod3, mod3, mod3, rw, rb, w1, w3, w2, final_g)


MOE_SUB = 512
MOE_CAP = 128
MOE_BLK = 2048
MOE_PAIR = 2


def _moe_route_kernel(x_ref, g_ref, sc_ref, sh_ref, rw_ref, rb_ref, h_ref, gates_ref, cnt_ref):
    h, gates = _route(x_ref[...], g_ref[...], sc_ref[...], sh_ref[...], rw_ref[...], rb_ref[...])
    h_ref[...] = h.astype(BF16)
    gates_ref[...] = gates
    cnt = jnp.sum(jnp.where(gates > 0.0, 1.0, 0.0), axis=0, keepdims=True)
    cnt_ref[...] = jnp.broadcast_to(cnt, cnt_ref.shape)


def _moe_route(x, g, mod3, row_of_batch, j_shift, j_scale, rw, rb):
    bsz, t, d = x.shape
    tm = MOE_SUB
    vec = lambda j: pl.BlockSpec((None, 1, d), lambda b, i: (row_of_batch(b), 0, j))
    full = lambda shape: pl.BlockSpec(shape, lambda b, i: (0,) * len(shape))
    return pl.pallas_call(
        _moe_route_kernel,
        grid=(bsz, t // tm),
        in_specs=[pl.BlockSpec((None, tm, d), lambda b, i: (b, i, 0)),
                  full((1, d)), vec(j_scale), vec(j_shift), full((3 * d, LANES)), full((N_EXPERTS, 1))],
        out_specs=[pl.BlockSpec((None, tm, d), lambda b, i: (b, i, 0)),
                   pl.BlockSpec((None, tm, LANES), lambda b, i: (b, i, 0)),
                   pl.BlockSpec((None, None, SUBLANES, LANES), lambda b, i: (b, i, 0, 0))],
        out_shape=[jax.ShapeDtypeStruct((bsz, t, d), BF16),
                   jax.ShapeDtypeStruct((bsz, t, LANES), F32),
                   jax.ShapeDtypeStruct((bsz, t // tm, SUBLANES, LANES), F32)],
        compiler_params=_params(("parallel", "arbitrary")),
        name="moe_route",
    )(x, g, mod3, mod3, rw, rb)


def _moe_routed_kernel(np_ref, h_ref, gates_ref, x_ref, gate_ref, w1_ref, w3_ref, w2_ref, fg_ref,
                       o_ref, rank_scr, rankt_scr, gatet_scr, xc_scr, *, final, nblk):
    b = pl.program_id(0)
    i = pl.program_id(1)
    e = pl.program_id(2)
    nsub = h_ref.shape[0] // MOE_SUB
    sub = lambda j: slice(j * MOE_SUB, (j + 1) * MOE_SUB)

    @pl.when(e == 0)
    def _():
        o_ref[...] = jnp.zeros(o_ref.shape, F32)
        r = lax.broadcasted_iota(jnp.int32, (MOE_SUB, MOE_SUB), 0)
        c = lax.broadcasted_iota(jnp.int32, (MOE_SUB, MOE_SUB), 1)
        ltri = jnp.where(c <= r, 1.0, 0.0).astype(BF16)
        utri = jnp.where(r <= c, 1.0, 0.0).astype(BF16)
        for j in range(nsub):
            gt = gates_ref[sub(j), :]
            rank_scr[j] = _dot(ltri, jnp.where(gt > 0.0, 1.0, 0.0).astype(BF16))
            gtt = gt.T[:N_EXPERTS]
            gatet_scr[j] = gtt
            rankt_scr[j] = _dot(jnp.where(gtt > 0.0, 1.0, 0.0).astype(BF16), utri)

    lane = lax.broadcasted_iota(jnp.int32, (MOE_SUB, LANES), 1)
    slot_lane = lax.broadcasted_iota(jnp.int32, (MOE_SUB, MOE_PAIR * MOE_CAP), 1)
    second = slot_lane >= MOE_CAP

    def one_pass(p, carry):
        base = (p * MOE_CAP + 1).astype(F32)
        slot_r = lax.broadcasted_iota(jnp.int32, (MOE_CAP, MOE_SUB), 0).astype(F32) + base
        slot_c = jnp.where(second, slot_lane - MOE_CAP, slot_lane).astype(F32) + base
        ys = []
        gcs = [[], []]
        for j in range(nsub):
            picks = []
            for k in range(MOE_PAIR):
                ex = e * MOE_PAIR + k
                rr = rankt_scr[j, pl.ds(ex, 1), :]
                gr = gatet_scr[j, pl.ds(ex, 1), :]
                pick = (rr == slot_r) & (gr > 0.0)
                picks.append(jnp.where(pick, 1.0, 0.0).astype(BF16))
                gcs[k].append(jnp.sum(jnp.where(pick, gr, 0.0), axis=1, keepdims=True))
            xcj = _dot(jnp.concatenate(picks, axis=0), h_ref[sub(j), :]).astype(BF16)
            for k in range(MOE_PAIR):
                xc_scr[k, j * MOE_CAP:(j + 1) * MOE_CAP, :] = xcj[k * MOE_CAP:(k + 1) * MOE_CAP]
        for k in range(MOE_PAIR):
            xc = xc_scr[k]
            a = _dot(xc, w1_ref[k])
            bb = _dot(xc, w3_ref[k])
            y = _dot(((a * jax.nn.sigmoid(a)) * bb).astype(BF16), w2_ref[k])
            ys.append([(y[j * MOE_CAP:(j + 1) * MOE_CAP] * gcs[k][j]).astype(BF16) for j in range(nsub)])
        for j in range(nsub):
            cols = []
            for k in range(MOE_PAIR):
                ex = e * MOE_PAIR + k
                rc = jnp.sum(jnp.where(lane == ex, rank_scr[j], 0.0), axis=1, keepdims=True)
                gc = jnp.sum(jnp.where(lane == ex, gates_ref[sub(j), :], 0.0), axis=1, keepdims=True)
                cols.append(jnp.where(gc > 0.0, rc, 0.0))
            put = jnp.where(jnp.where(second, cols[1], cols[0]) == slot_c, 1.0, 0.0).astype(BF16)
            o_ref[sub(j), :] += _dot(put, jnp.concatenate([ys[0][j], ys[1][j]], axis=0))
        return carry

    lax.fori_loop(0, np_ref[(b * nblk + i) * (N_EXPERTS // MOE_PAIR) + e], one_pass, 0)

    @pl.when(e == N_EXPERTS // MOE_PAIR - 1)
    def _():
        y = x_ref[...] + gate_ref[...] * o_ref[...]
        if final:
            ms = jnp.mean(y * y, axis=-1, keepdims=True)
            y = (y * lax.rsqrt(ms + EPS)) * fg_ref[...]
        o_ref[...] = y


def _moe_routed(x, g, mod3, row_of_batch, j_shift, j_scale, j_gate, rw, rb, w1, w3, w2, final_g, final):
    bsz, t, d = x.shape
    h, gates, cnt = _moe_route(x, g, mod3, row_of_batch, j_shift, j_scale, rw, rb)
    tb = _tile(t, MOE_BLK)
    nblk = t // tb
    nsub = tb // MOE_SUB
    npair = N_EXPERTS // MOE_PAIR
    passes = jnp.ceil(cnt[:, :, 0, :N_EXPERTS] / MOE_CAP).astype(jnp.int32)
    passes = jnp.max(passes.reshape(bsz, nblk, nsub, npair, MOE_PAIR), axis=(2, 4)).reshape(-1)
    full = lambda shape: pl.BlockSpec(shape, lambda b, i, e, np_: (0,) * len(shape))
    blk = lambda w: pl.BlockSpec((None, tb, w), lambda b, i, e, np_: (b, i, 0), pipeline_mode=pl.Buffered(1))
    grid_spec = pltpu.PrefetchScalarGridSpec(
        num_scalar_prefetch=1,
        grid=(bsz, nblk, npair),
        in_specs=[blk(d), blk(LANES), blk(d),
                  pl.BlockSpec((None, 1, d), lambda b, i, e, np_: (row_of_batch(b), 0, j_gate)),
                  pl.BlockSpec((MOE_PAIR, d, D_EXPERT), lambda b, i, e, np_: (e, 0, 0)),
                  pl.BlockSpec((MOE_PAIR, d, D_EXPERT), lambda b, i, e, np_: (e, 0, 0)),
                  pl.BlockSpec((MOE_PAIR, D_EXPERT, d), lambda b, i, e, np_: (e, 0, 0)),
                  full((1, d))],
        out_specs=pl.BlockSpec((None, tb, d), lambda b, i, e, np_: (b, i, 0)),
        scratch_shapes=[pltpu.VMEM((nsub, MOE_SUB, LANES), F32),
                        pltpu.VMEM((nsub, N_EXPERTS, MOE_SUB), F32),
                        pltpu.VMEM((nsub, N_EXPERTS, MOE_SUB), F32),
                        pltpu.VMEM((MOE_PAIR, nsub * MOE_CAP, d), BF16)])
    return pl.pallas_call(
        functools.partial(_moe_routed_kernel, final=final, nblk=nblk),
        grid_spec=grid_spec,
        out_shape=jax.ShapeDtypeStruct((bsz, t, d), F32),
        compiler_params=_params(("parallel", "parallel", "arbitrary")),
        name="moe_routed_final" if final else "moe_routed",
    )(passes, h, gates, x, mod3, w1, w3, w2, final_g)


def _tile(n, pref):
    t = min(n, pref)
    assert n % t == 0
    return t


def _fft_split(n):
    n2 = LANES
    assert n % n2 == 0
    return n // n2, n2


def _hyena_latent(u, lp, fargs):
    bsz, s, _ = u.shape
    n1, n2 = _fft_split(2 * s)
    dc = _dft_consts(n1, n2)
    cg = 8
    kt_l, _, nrm_l = _hy_filters(s, *fargs, tp=_tile(2 * s, 1024))
    kspec = _hy_spec(kt_l.reshape(2, HY_CH, n1, n2), nrm_l.reshape(2 * HY_CH), dc, cg, n1, n2)
    xt = _hy_short(u, lp["hy_short_w"].astype(F32), lp["hy_short_b"].astype(F32)[None, :], _tile(s, 2048))
    h_l = _hy_conv(xt.reshape(3, bsz, HY_CH, n1 // 2, n2), kspec, lp["hy_skip"].astype(F32).reshape(2 * HY_CH),
                   dc, cg, n1, n2)
    return h_l.reshape(bsz, HY_CH, s)


def _mixers(lat_in, ctx_in, lp, li, need_ctx):
    u, ql, klt, vl = lat_in
    uc, qc, kct, vc = ctx_in
    bsz, s, _ = u.shape
    c = uc.shape[1]
    lam_init = 0.8 - 0.6 * math.exp(-0.3 * li)
    lam_vecs = lp["a_lambda"].astype(F32)
    subln = lp["a_subln_g"].astype(F32)[None, :]

    a_l = _diff_attn(ql, kct, vc, klt, vl, lam_vecs, subln, lam_init, _tile(s, 512), _tile(s, 4096))
    bias8 = _nbr_bias(lp["b_rpb"])
    b_l = _nbr_attn(u, uc, bias8)
    wbd = jax.scipy.linalg.block_diag(*[lp["pool_w"][g] for g in range(len(POOL_SIZES))]).astype(BF16)
    pscale = lp["pool_scale"].astype(F32)[None, :]
    p_l = _pool_mix(u, wbd, pscale, _tile(s, 2048))
    fargs = (lp["hy_f_w1"], lp["hy_f_b1"], lp["hy_f_w2"], lp["hy_f_b2"], lp["hy_f_w3"], lp["hy_f_b3"])
    skip = lp["hy_skip"].astype(F32)
    w_short = lp["hy_short_w"].astype(F32)
    b_short = lp["hy_short_b"].astype(F32)[None, :]
    h_l = _hyena_latent(u, lp, fargs)
    lat = (a_l, b_l, p_l, h_l)
    if not need_ctx:
        return lat, None
    a_c = _diff_attn(qc, kct, vc, None, None, lam_vecs, subln, lam_init, _tile(c, 256), None)
    b_c = _nbr_ctx_attn(uc)
    p_c = _pool_mix(uc, wbd, pscale, _tile(c, 256))
    kt_c, _, nrm_c = _hy_filters(c, *fargs, tp=_tile(2 * c, 512))
    xtc = _hy_short(uc, w_short, b_short, _tile(c, 256))
    h_c = _hy_ctx_conv(xtc, kt_c, nrm_c, skip[:, :, None])
    return lat, (a_c, b_c, p_c, h_c)


def kernel(x, c, ctx, c_ctx, norm1_g, norm2_g, ada_w, ada_b, w_in, w_out, a_lambda, a_subln_g, b_rpb, pool_w, pool_scale, hy_short_w, hy_short_b, hy_f_w1, hy_f_b1, hy_f_w2, hy_f_b2, hy_f_w3, hy_f_b3, hy_skip, router_w, router_b, moe_w1, moe_w3, moe_w2, final_g):
    depth = norm1_g.shape[0]
    bsz, s, d = x.shape
    cl = ctx.shape[1]
    assert bsz <= SUBLANES - 1
    xl, xc = x, ctx
    cpad = jnp.zeros((SUBLANES, d), F32).at[:bsz].set(c.astype(F32)).at[bsz].set(c_ctx.astype(F32))
    rw = _cat3(jnp.pad(router_w.astype(F32), ((0, 0), (0, LANES - N_EXPERTS))), 0)
    rb = router_b.astype(F32)[:, None]
    lat_row = lambda b: b
    ctx_row = lambda b: bsz
    fg = final_g.astype(F32)[None, :]
    tm = _tile(s, 512)
    tmc = _tile(cl, 256)
    for li in range(depth):
        need_ctx = li < depth - 1
        lp = dict(a_lambda=a_lambda[li], a_subln_g=a_subln_g[li], b_rpb=b_rpb[li], pool_w=pool_w[li],
                  pool_scale=pool_scale[li], hy_short_w=hy_short_w[li], hy_short_b=hy_short_b[li],
                  hy_f_w1=hy_f_w1[li], hy_f_b1=hy_f_b1[li], hy_f_w2=hy_f_w2[li], hy_f_b2=hy_f_b2[li],
                  hy_f_w3=hy_f_w3[li], hy_f_b3=hy_f_b3[li], hy_skip=hy_skip[li])
        mod3 = _ada(cpad, ada_w[li].astype(F32), ada_b[li].astype(F32)[None, :]).reshape(SUBLANES, 1, 6 * d)
        n1g = norm1_g[li].astype(F32)[None, :]
        n2g = norm2_g[li].astype(F32)[None, :]
        w_in_b = w_in[li].astype(BF16)
        w_out_b = w_out[li].astype(BF16)
        lat_in = _norm_proj(xl, n1g, mod3, lat_row, 0, 1, w_in_b, tm, True)
        ctx_in = _norm_proj(xc, n1g, mod3, ctx_row, 0, 1, w_in_b, tmc, False)
        lat, cx = _mixers(lat_in, ctx_in, lp, li, need_ctx)
        xl = _out_proj(*lat, xl, mod3, lat_row, 2, w_out_b, _tile(s, 1024))
        w1b, w3b, w2b = moe_w1[li].astype(BF16), moe_w3[li].astype(BF16), moe_w2[li].astype(BF16)
        if need_ctx:
            xc = _out_proj(*cx, xc, mod3, ctx_row, 2, w_out_b, tmc)
            xc = _moe(xc, n2g, mod3, ctx_row, 3, 4, 5, rw, rb, w1b, w3b, w2b, fg, False, tmc)
        xl = _moe_routed(xl, n2g, mod3, lat_row, 3, 4, 5, rw, rb, w1b, w3b, w2b, fg, li == depth - 1)
    return xl
```

```python
import functools
import math

import numpy as np
import jax
import jax.numpy as jnp
from jax import lax
from jax.experimental import pallas as pl
from jax.experimental.pallas import tpu as pltpu

F32 = jnp.float32
BF16 = jnp.bfloat16
HI = lax.Precision.HIGHEST

GRID_W = 64
A_HEADS = 4
A_QK = 32
A_V = 64
ROPE_BASE = 10000.0
B_HEADS = 4
B_DIM = 64
WIN_R = 8
WIN_C = 16
POOL_SIZES = (2, 4, 8, 16)
POOL_CH = 64
D_GROUP = 256
HY_CH = 256
HY_BANDS = 16
HY_EMB = 1 + 2 * HY_BANDS
HY_HIDDEN = 64
HY_SIN_FREQ = 1.0
HY_MIN_DECAY = math.log(1e-2) / 1.5
HY_MAX_DECAY = math.log(1e-2) / 0.3
N_EXPERTS = 16
N_EXPERT_GROUPS = 4
D_EXPERT = 512
EPS = 1e-6
LOG2E = 1.4426950408889634

LANES = 128
SUBLANES = 8
VMEM_LIMIT = 56 * 1024 * 1024


def _params(sem):
    return pltpu.CompilerParams(dimension_semantics=sem, vmem_limit_bytes=VMEM_LIMIT)


def _dot(a, b, prec=None):
    return jnp.dot(a, b, precision=prec, preferred_element_type=F32)


def _dot_nt(a, b):
    return lax.dot_general(a, b, (((1,), (1,)), ((), ())), preferred_element_type=F32)


def _ada_kernel(c_ref, w_ref, b_ref, o_ref):
    cf = c_ref[...]
    s = cf * jax.nn.sigmoid(cf)
    o_ref[...] = _dot(s, w_ref[...], HI) + b_ref[...]


def _ada(cpad, w, b):
    d = cpad.shape[1]
    n = w.shape[1]
    return pl.pallas_call(
        _ada_kernel,
        grid=(n // d,),
        in_specs=[pl.BlockSpec((SUBLANES, d), lambda j: (0, 0)),
                  pl.BlockSpec((d, d), lambda j: (0, j)),
                  pl.BlockSpec((1, d), lambda j: (0, j))],
        out_specs=pl.BlockSpec((SUBLANES, d), lambda j: (0, j)),
        out_shape=jax.ShapeDtypeStruct((SUBLANES, n), F32),
        compiler_params=_params(("arbitrary",)),
        name="ada_mod",
    )(cpad, w, b)


A_COLS = 768
B_COLS = 768


def _attn_layout(u, cos_t, sin_t, q_ref, kt_ref, v_ref):
    q = u[:, 0:256]
    k = u[:, 256:512]
    v = u[:, 512:768]
    if cos_t is not None:
        lane = lax.broadcasted_iota(jnp.int32, cos_t.shape, 1)
        first = (lane % (2 * 16)) < 16

        def rot(x):
            halves = []
            for j in range(2):
                xh = x[:, j * LANES:(j + 1) * LANES]
                swap = jnp.where(first, pltpu.roll(xh, LANES - 16, axis=1), pltpu.roll(xh, 16, axis=1))
                halves.append(xh * cos_t + swap * sin_t)
            return jnp.concatenate(halves, axis=1)

        q = rot(q)
        k = rot(k)
    q = q * (A_QK ** -0.5 * LOG2E)
    kt = k.T
    for hc in range(2 * A_HEADS):
        q_ref[hc] = q[:, hc * A_QK:(hc + 1) * A_QK].astype(BF16)
        kt_ref[hc] = kt[hc * A_QK:(hc + 1) * A_QK, :].astype(BF16)
    lane = lax.broadcasted_iota(jnp.int32, (v.shape[0], LANES - A_V), 1)
    ones_col = jnp.where(lane == 0, 1.0, 0.0)
    for h in range(A_HEADS):
        v_ref[h] = jnp.concatenate([v[:, h * A_V:(h + 1) * A_V], ones_col], axis=1).astype(BF16)


def _rope_tables(length):
    n_freq = A_QK // 4
    inv = ROPE_BASE ** (-jnp.arange(n_freq, dtype=F32) / n_freq)
    t = jnp.arange(length)
    row = (t // GRID_W).astype(F32)
    col = (t % GRID_W).astype(F32)
    ang = jnp.concatenate([row[:, None] * inv, col[:, None] * inv], axis=-1)
    cos, sin = jnp.cos(ang), jnp.sin(ang)
    cos_t = jnp.tile(jnp.concatenate([cos, cos], axis=-1), (1, LANES // 32))
    sin_t = jnp.tile(jnp.concatenate([-sin, sin], axis=-1), (1, LANES // 32))
    return cos_t, sin_t


def _norm_proj_kernel(*refs, rope):
    if rope:
        x_ref, g_ref, sc_ref, sh_ref, w_ref, cos_ref, sin_ref, ub_ref, uf_ref, q_ref, kt_ref, v_ref = refs
    else:
        x_ref, g_ref, sc_ref, sh_ref, w_ref, ub_ref, uf_ref, q_ref, kt_ref, v_ref = refs
    x = x_ref[...]
    ms = jnp.mean(x * x, axis=-1, keepdims=True)
    h = (x * lax.rsqrt(ms + EPS)) * g_ref[...] * (1.0 + sc_ref[...]) + sh_ref[...]
    u = _dot(h.astype(BF16), w_ref[...])
    ub_ref[...] = u[:, A_COLS:A_COLS + B_COLS].astype(BF16)
    uf_ref[...] = u[:, A_COLS + B_COLS:]
    _attn_layout(u[:, :A_COLS], cos_ref[...] if rope else None, sin_ref[...] if rope else None, q_ref, kt_ref, v_ref)


def _norm_proj(x, g, mod3, row_of_batch, j_shift, j_scale, w, tm, rope):
    bsz, t, d = x.shape
    n = w.shape[1]
    nh = 2 * A_HEADS
    in_specs = [pl.BlockSpec((None, tm, d), lambda b, i: (b, i, 0)),
                pl.BlockSpec((1, d), lambda b, i: (0, 0)),
                pl.BlockSpec((None, 1, d), lambda b, i: (row_of_batch(b), 0, j_scale)),
                pl.BlockSpec((None, 1, d), lambda b, i: (row_of_batch(b), 0, j_shift)),
                pl.BlockSpec((d, n), lambda b, i: (0, 0))]
    args = [x, g, mod3, mod3, w]
    if rope:
        in_specs += [pl.BlockSpec((tm, LANES), lambda b, i: (i, 0))] * 2
        args += list(_rope_tables(t))
    return pl.pallas_call(
        functools.partial(_norm_proj_kernel, rope=rope),
        grid=(bsz, t // tm),
        in_specs=in_specs,
        out_specs=[pl.BlockSpec((None, tm, B_COLS), lambda b, i: (b, i, 0)),
                   pl.BlockSpec((None, tm, n - A_COLS - B_COLS), lambda b, i: (b, i, 0)),
                   pl.BlockSpec((None, nh, tm, A_QK), lambda b, i: (b, 0, i, 0)),
                   pl.BlockSpec((None, nh, A_QK, tm), lambda b, i: (b, 0, 0, i)),
                   pl.BlockSpec((None, A_HEADS, tm, LANES), lambda b, i: (b, 0, i, 0))],
        out_shape=[jax.ShapeDtypeStruct((bsz, t, B_COLS), BF16),
                   jax.ShapeDtypeStruct((bsz, t, n - A_COLS - B_COLS), F32),
                   jax.ShapeDtypeStruct((bsz, nh, t, A_QK), BF16),
                   jax.ShapeDtypeStruct((bsz, nh, A_QK, t), BF16),
                   jax.ShapeDtypeStruct((bsz, A_HEADS, t, LANES), BF16)],
        compiler_params=_params(("parallel", "arbitrary")),
        name="norm_in_proj_rope" if rope else "norm_in_proj_ctx",
    )(*args)


QK_LOOKAHEAD = 2


def _dattn_kernel(*refs, lam_init, has_lat):
    if has_lat:
        lam_ref, g_ref, q_ref, kc_ref, vc_ref, k_ref, v_ref, o_ref, m_scr, acc_scr = refs
    else:
        lam_ref, g_ref, q_ref, kc_ref, vc_ref, o_ref, m_scr, acc_scr = refs
    ki = pl.program_id(2)
    nk = pl.num_programs(2)
    nh = 2 * A_HEADS

    def update(kt_r, v_r):
        scores = [_dot(q_ref[j], kt_r[j]) for j in range(QK_LOOKAHEAD)]
        for hc in range(nh):
            s = scores[hc]
            if hc + QK_LOOKAHEAD < nh:
                scores.append(_dot(q_ref[hc + QK_LOOKAHEAD], kt_r[hc + QK_LOOKAHEAD]))
            m_prev = m_scr[hc]
            m_new = jnp.maximum(m_prev, jnp.max(s, axis=1, keepdims=True))
            alpha = jnp.exp2(m_prev - m_new)
            p = jnp.exp2((s - m_new[:, :1]).astype(BF16))
            acc_scr[hc] = alpha * acc_scr[hc] + _dot(p, v_r[hc // 2])
            m_scr[hc] = m_new

    @pl.when(ki == 0)
    def _():
        m_scr[...] = jnp.full(m_scr.shape, -jnp.inf, F32)
        acc_scr[...] = jnp.zeros(acc_scr.shape, F32)
        update(kc_ref, vc_ref)

    if has_lat:
        @pl.when(ki > 0)
        def _():
            update(k_ref, v_ref)

    @pl.when(ki == nk - 1)
    def _():
        lv = lam_ref[...]
        lam = (jnp.exp(jnp.sum(lv[0:1] * lv[1:2], axis=1, keepdims=True))
               - jnp.exp(jnp.sum(lv[2:3] * lv[3:4], axis=1, keepdims=True)) + lam_init)
        for h in range(A_HEADS):
            a0 = acc_scr[2 * h]
            a1 = acc_scr[2 * h + 1]
            o = a0[:, :A_V] / a0[:, A_V:A_V + 1] - lam * (a1[:, :A_V] / a1[:, A_V:A_V + 1])
            ms = jnp.mean(o * o, axis=-1, keepdims=True)
            o_ref[:, h * A_V:(h + 1) * A_V] = (o * lax.rsqrt(ms + EPS)) * g_ref[...] * (1.0 - lam_init)


def _diff_attn(q, kct, vc, kt, v, lam_vecs, subln_g, lam_init, tq, tk):
    bsz, nh, t, _ = q.shape
    c = kct.shape[-1]
    has_lat = kt is not None
    nk = 1 + (kt.shape[-1] // tk if has_lat else 0)
    in_specs = [pl.BlockSpec((4, A_QK), lambda b, i, k: (0, 0)),
                pl.BlockSpec((1, A_V), lambda b, i, k: (0, 0)),
                pl.BlockSpec((None, nh, tq, A_QK), lambda b, i, k: (b, 0, i, 0)),
                pl.BlockSpec((None, nh, A_QK, c), lambda b, i, k: (b, 0, 0, 0)),
                pl.BlockSpec((None, A_HEADS, c, LANES), lambda b, i, k: (b, 0, 0, 0))]
    args = [lam_vecs, subln_g, q, kct, vc]
    if has_lat:
        in_specs += [pl.BlockSpec((None, nh, A_QK, tk), lambda b, i, k: (b, 0, 0, jnp.maximum(k - 1, 0))),
                     pl.BlockSpec((None, A_HEADS, tk, LANES), lambda b, i, k: (b, 0, jnp.maximum(k - 1, 0), 0))]
        args += [kt, v]
    return pl.pallas_call(
        functools.partial(_dattn_kernel, lam_init=lam_init, has_lat=has_lat),
        grid=(bsz, t // tq, nk),
        in_specs=in_specs,
        out_specs=pl.BlockSpec((None, tq, A_HEADS * A_V), lambda b, i, k: (b, i, 0)),
        out_shape=jax.ShapeDtypeStruct((bsz, t, A_HEADS * A_V), F32),
        scratch_shapes=[pltpu.VMEM((nh, tq, LANES), F32),
                        pltpu.VMEM((nh, tq, LANES), F32)],
        compiler_params=_params(("parallel", "parallel", "arbitrary")),
        name="diff_attn" if has_lat else "diff_attn_ctx",
    )(*args)


NB_ROWS = 8


def _nbr_bias(rpb):
    cols = jnp.arange(GRID_W)
    c0 = jnp.clip(cols - WIN_C // 2, 0, GRID_W - WIN_C)
    in_win = (cols[None, :] >= c0[:, None]) & (cols[None, :] < c0[:, None] + WIN_C)
    dc = jnp.clip(cols[None, :] - cols[:, None], -(WIN_C - 1), WIN_C - 1) + (WIN_C - 1)
    onehot = (dc[None] == jnp.arange(2 * WIN_C - 1)[:, None, None]).astype(F32)
    g = jnp.einsum("hab,bqk->haqk", rpb.astype(F32), onehot, precision=HI)
    g = jnp.where(in_win[None, None], g, -jnp.inf)
    b = jnp.stack([g[:, a0:a0 + WIN_R] for a0 in range(WIN_R)], axis=0)
    b = jnp.transpose(b, (0, 1, 3, 2, 4))
    return b.reshape(WIN_R, B_HEADS, GRID_W, WIN_R * GRID_W)


def _nbr_kernel(q_ref, kp_ref, kc_ref, kn_ref, vp_ref, vcur_ref, vn_ref, kctx_ref, vctx_ref, bias_ref,
                o_ref, kwin, vwin, kcx, vcx, *, n_rows):
    rb = pl.program_id(1)
    blk = NB_ROWS * GRID_W
    scale = B_DIM ** -0.5
    for h in range(B_HEADS):
        sl = slice(h * B_DIM, (h + 1) * B_DIM)
        for j, (kr, vr) in enumerate(((kp_ref, vp_ref), (kc_ref, vcur_ref), (kn_ref, vn_ref))):
            kwin[h, j * blk:(j + 1) * blk, :] = kr[:, sl].astype(BF16)
            vwin[h, j * blk:(j + 1) * blk, :] = vr[:, sl].astype(BF16)
        kcx[h] = kctx_ref[:, sl].astype(BF16)
        vcx[h] = vctx_ref[:, sl].astype(BF16)

    def window(rr):
        r = rb * NB_ROWS + rr
        r0 = jnp.clip(r - WIN_R // 2, 0, n_rows - WIN_R)
        off = pl.multiple_of((r0 - (rb - 1) * NB_ROWS) * GRID_W, GRID_W)
        return off, r0 - r + (WIN_R - 1)

    def scores(rr):
        off, a0 = window(rr)
        qrow = q_ref[rr * GRID_W:(rr + 1) * GRID_W, :]
        out = []
        for h in range(B_HEADS):
            qh = qrow[:, h * B_DIM:(h + 1) * B_DIM].astype(BF16)
            s = _dot_nt(qh, kwin[h, pl.ds(off, WIN_R * GRID_W), :]) * scale + bias_ref[a0, h]
            out.append((s, _dot_nt(qh, kcx[h]) * scale))
        return out

    nxt = scores(0)
    for rr in range(NB_ROWS):
        cur = nxt
        if rr + 1 < NB_ROWS:
            nxt = scores(rr + 1)
        off, _ = window(rr)
        outs = []
        for h in range(B_HEADS):
            s, sc = cur[h]
            m = jnp.maximum(jnp.max(s, axis=1, keepdims=True), jnp.max(sc, axis=1, keepdims=True))
            p = jnp.exp(s - m)
            pc = jnp.exp(sc - m)
            l = jnp.sum(p, axis=1, keepdims=True) + jnp.sum(pc, axis=1, keepdims=True)
            o = _dot(p.astype(BF16), vwin[h, pl.ds(off, WIN_R * GRID_W), :]) + _dot(pc.astype(BF16), vcx[h])
            outs.append(o / l)
        o_ref[rr * GRID_W:(rr + 1) * GRID_W, :] = jnp.concatenate(outs, axis=1)


def _nbr_attn(u, uc, bias8):
    bsz, s, _ = u.shape
    c = uc.shape[1]
    n_rows = s // GRID_W
    nb = n_rows // NB_ROWS
    blk = NB_ROWS * GRID_W
    w = B_HEADS * B_DIM

    def spec(col, shift):
        return pl.BlockSpec((None, blk, w), lambda b, i: (b, jnp.clip(i + shift, 0, nb - 1), col))

    return pl.pallas_call(
        functools.partial(_nbr_kernel, n_rows=n_rows),
        grid=(bsz, nb),
        in_specs=[spec(0, 0), spec(1, -1), spec(1, 0), spec(1, 1), spec(2, -1), spec(2, 0), spec(2, 1),
                  pl.BlockSpec((None, c, w), lambda b, i: (b, 0, 1)),
                  pl.BlockSpec((None, c, w), lambda b, i: (b, 0, 2)),
                  pl.BlockSpec(bias8.shape, lambda b, i: (0, 0, 0, 0))],
        out_specs=pl.BlockSpec((None, blk, w), lambda b, i: (b, i, 0)),
        out_shape=jax.ShapeDtypeStruct((bsz, s, w), F32),
        scratch_shapes=[pltpu.VMEM((B_HEADS, 3 * blk, B_DIM), BF16),
                        pltpu.VMEM((B_HEADS, 3 * blk, B_DIM), BF16),
                        pltpu.VMEM((B_HEADS, c, B_DIM), BF16),
                        pltpu.VMEM((B_HEADS, c, B_DIM), BF16)],
        compiler_params=_params(("parallel", "arbitrary")),
        name="nbr_attn",
    )(u, u, u, u, u, u, u, uc, uc, bias8)


def _nbr_ctx_kernel(q_ref, k_ref, v_ref, o_ref):
    scale = B_DIM ** -0.5
    outs = []
    for h in range(B_HEADS):
        sl = slice(h * B_DIM, (h + 1) * B_DIM)
        s = _dot_nt(q_ref[:, sl].astype(BF16), k_ref[:, sl].astype(BF16)) * scale
        m = jnp.max(s, axis=1, keepdims=True)
        p = jnp.exp(s - m)
        l = jnp.sum(p, axis=1, keepdims=True)
        outs.append(_dot(p.astype(BF16), v_ref[:, sl].astype(BF16)) / l)
    o_ref[...] = jnp.concatenate(outs, axis=1)


def _nbr_ctx_attn(uc):
    bsz, c, _ = uc.shape
    w = B_HEADS * B_DIM
    return pl.pallas_call(
        _nbr_ctx_kernel,
        grid=(bsz,),
        in_specs=[pl.BlockSpec((None, c, w), lambda b: (b, 0, 0)),
                  pl.BlockSpec((None, c, w), lambda b: (b, 0, 1)),
                  pl.BlockSpec((None, c, w), lambda b: (b, 0, 2))],
        out_specs=pl.BlockSpec((None, c, w), lambda b: (b, 0, 0)),
        out_shape=jax.ShapeDtypeStruct((bsz, c, w), F32),
        compiler_params=_params(("arbitrary",)),
        name="nbr_attn_ctx",
    )(uc, uc, uc)


HALO = SUBLANES


def _halo_specs(tm, length, col, width):
    nt = length // tm
    per = tm // HALO
    last = length // HALO - 1
    return [pl.BlockSpec((None, HALO, width), lambda b, i, *_: (b, jnp.maximum(i * per - 1, 0), col)),
            pl.BlockSpec((None, tm, width), lambda b, i, *_: (b, i, col)),
            pl.BlockSpec((None, HALO, width), lambda b, i, *_: (b, jnp.minimum((i + 1) * per, last), col))], nt


def _fill_halo(buf, prev_ref, cur_ref, next_ref, i, nt, tm):
    zero = jnp.zeros(prev_ref.shape, F32)
    buf[0:HALO, :] = jnp.where(i > 0, prev_ref[...], zero)
    buf[HALO:HALO + tm, :] = cur_ref[...]
    buf[HALO + tm:, :] = jnp.where(i < nt - 1, next_ref[...], zero)


def _pool_kernel(prev_ref, cur_ref, next_ref, w_ref, ps_ref, o_ref, buf, *, tm, nt, length):
    i = pl.program_id(1)
    _fill_halo(buf, prev_ref, cur_ref, next_ref, i, nt, tm)

    def sh(j):
        return buf[HALO + j:HALO + j + tm, :]

    u = sh(0)
    sums = []
    acc = None
    lo, hi = 0, 0
    for w in POOL_SIZES:
        for j in list(range(-(w // 2), lo)) + list(range(hi, w // 2)):
            acc = sh(j) if acc is None else acc + sh(j)
        lo, hi = -(w // 2), w // 2
        sums.append(acc)
    lane = lax.broadcasted_iota(jnp.int32, (tm, D_GROUP), 1)
    t = (i * tm + lax.broadcasted_iota(jnp.int32, (tm, D_GROUP), 0))
    wsum = sums[-1]
    half = jnp.full((tm, D_GROUP), POOL_SIZES[-1] // 2, jnp.int32)
    for g in range(len(POOL_SIZES) - 2, -1, -1):
        sel = lane < (g + 1) * POOL_CH
        wsum = jnp.where(sel, sums[g], wsum)
        half = jnp.where(sel, POOL_SIZES[g] // 2, half)
    cnt = (jnp.minimum(t + half, length) - jnp.maximum(t - half, 0)).astype(F32)
    d = wsum / cnt - u
    o_ref[...] = _dot(d.astype(BF16), w_ref[...]) * ps_ref[...]


def _pool_mix(u, wbd, pool_scale, tm):
    bsz, length, _ = u.shape
    specs, nt = _halo_specs(tm, length, 0, D_GROUP)
    return pl.pallas_call(
        functools.partial(_pool_kernel, tm=tm, nt=nt, length=length),
        grid=(bsz, nt),
        in_specs=specs + [pl.BlockSpec((D_GROUP, D_GROUP), lambda b, i: (0, 0)),
                          pl.BlockSpec((1, D_GROUP), lambda b, i: (0, 0))],
        out_specs=pl.BlockSpec((None, tm, D_GROUP), lambda b, i: (b, i, 0)),
        out_shape=jax.ShapeDtypeStruct((bsz, length, D_GROUP), F32),
        scratch_shapes=[pltpu.VMEM((tm + 2 * HALO, D_GROUP), F32)],
        compiler_params=_params(("parallel", "arbitrary")),
        name="pool_mix",
    )(u, u, u, wbd, pool_scale)


def _hy_short_kernel(prev_ref, cur_ref, next_ref, w_ref, b_ref, o_ref, buf, *, tm, nt):
    i = pl.program_id(1)
    _fill_halo(buf, prev_ref, cur_ref, next_ref, i, nt, tm)
    w = w_ref[...]
    y = (buf[HALO - 1:HALO - 1 + tm, :] * w[0:1] + buf[HALO:HALO + tm, :] * w[1:2]
         + buf[HALO + 1:HALO + 1 + tm, :] * w[2:3] + b_ref[...])
    o_ref[...] = y.T


def _hy_short(u, w_short, b_short, tm):
    bsz, length, _ = u.shape
    nt = length // tm
    per = tm // HALO
    last = length // HALO - 1
    c0 = 1
    in_specs = [pl.BlockSpec((None, HALO, HY_CH), lambda b, i, j: (b, jnp.maximum(i * per - 1, 0), c0 + j)),
                pl.BlockSpec((None, tm, HY_CH), lambda b, i, j: (b, i, c0 + j)),
                pl.BlockSpec((None, HALO, HY_CH), lambda b, i, j: (b, jnp.minimum((i + 1) * per, last), c0 + j)),
                pl.BlockSpec((3, HY_CH), lambda b, i, j: (0, j)),
                pl.BlockSpec((1, HY_CH), lambda b, i, j: (0, j))]
    return pl.pallas_call(
        functools.partial(_hy_short_kernel, tm=tm, nt=nt),
        grid=(bsz, nt, 3),
        in_specs=in_specs,
        out_specs=pl.BlockSpec((None, None, HY_CH, tm), lambda b, i, j: (j, b, 0, i)),
        out_shape=jax.ShapeDtypeStruct((3, bsz, HY_CH, length), F32),
        scratch_shapes=[pltpu.VMEM((tm + 2 * HALO, HY_CH), F32)],
        compiler_params=_params(("parallel", "arbitrary", "arbitrary")),
        name="hyena_short_conv",
    )(u, u, u, w_short, b_short)


HY_FEAT = 40


def _hy_filter_kernel(band_ref, w1_ref, b1_ref, w2_ref, b2_ref, w3_ref, b3_ref, dl_ref,
                      k_ref, ssq_ref, nrm_ref, *, tp, length):
    i = pl.program_id(0)
    n_i = pl.num_programs(0)
    m = i * tp + lax.broadcasted_iota(jnp.int32, (1, tp), 1)
    t = jnp.where(m <= length, m, 2 * length - m).astype(F32)
    t_norm = t / max(length - 1, 1)
    ang = ((2.0 * math.pi / length) * t) * band_ref[...]
    row = lax.broadcasted_iota(jnp.int32, (HY_FEAT, tp), 0)
    z = jnp.where(row == 0, t_norm,
                  jnp.where(row <= HY_BANDS, jnp.cos(ang), jnp.where(row < HY_EMB, jnp.sin(ang), 0.0)))
    z = jnp.concatenate([z, jnp.zeros((LANES - HY_FEAT, tp), F32)], axis=0)
    h = jnp.sin(HY_SIN_FREQ * (_dot(w1_ref[...], z, HI) + b1_ref[...]))
    h = jnp.sin(HY_SIN_FREQ * (_dot(w2_ref[...], h, HI) + b2_ref[...]))
    hh, hl = _split_bf16(h)
    h = _dot(w3_ref[...], jnp.concatenate([hh, hl, hh], axis=0)) + b3_ref[...]
    h = h * jnp.exp(-t_norm * dl_ref[...])

    @pl.when(i == 0)
    def _():
        ssq_ref[...] = jnp.zeros(ssq_ref.shape, F32)

    for o in range(2):
        fwd = h[o * 2 * HY_CH:o * 2 * HY_CH + HY_CH]
        bwd = h[o * 2 * HY_CH + HY_CH:(o + 1) * 2 * HY_CH]
        k = jnp.where(m < length, fwd, jnp.where(m == length, 0.0, bwd))
        k_ref[o] = k
        extra = jnp.where(m == 0, bwd * bwd, 0.0)
        ssq_ref[o] += jnp.sum(k * k + extra, axis=1, keepdims=True)

    @pl.when(i == n_i - 1)
    def _():
        nrm_ref[...] = lax.rsqrt(ssq_ref[...] + EPS)


def _hy_filters(length, w1, b1, w2, b2, w3, b3, tp):
    bands = jnp.linspace(1e-4, HY_BANDS - 1, HY_BANDS, dtype=F32)
    band_col = jnp.concatenate([jnp.zeros((1,), F32), bands, bands,
                                jnp.zeros((HY_FEAT - HY_EMB,), F32)])[:, None]
    deltas = jnp.abs(jnp.linspace(HY_MIN_DECAY, HY_MAX_DECAY, HY_CH, dtype=F32))
    dl_col = jnp.tile(deltas, 4)[:, None]
    w1t = jnp.pad(w1.astype(F32).T, ((0, 0), (0, LANES - HY_EMB)))
    full = lambda shape: pl.BlockSpec(shape, lambda i: (0,) * len(shape))
    n = 2 * length
    return pl.pallas_call(
        functools.partial(_hy_filter_kernel, tp=tp, length=length),
        grid=(n // tp,),
        in_specs=[full((HY_FEAT, 1)), full((HY_HIDDEN, LANES)), full((HY_HIDDEN, 1)),
                  full((HY_HIDDEN, HY_HIDDEN)), full((HY_HIDDEN, 1)),
                  full((4 * HY_CH, 3 * HY_HIDDEN)), full((4 * HY_CH, 1)), full((4 * HY_CH, 1))],
        out_specs=[pl.BlockSpec((2, HY_CH, tp), lambda i: (0, 0, i)),
                   full((2, HY_CH, 1)), full((2, HY_CH, 1))],
        out_shape=[jax.ShapeDtypeStruct((2, HY_CH, n), F32),
                   jax.ShapeDtypeStruct((2, HY_CH, 1), F32),
                   jax.ShapeDtypeStruct((2, HY_CH, 1), F32)],
        compiler_params=_params(("arbitrary",)),
        name="hyena_filters",
    )(band_col, w1t, b1.astype(F32)[:, None], w2.astype(F32).T, b2.astype(F32)[:, None],
      _cat3(w3.astype(F32).T, 1), b3.astype(F32)[:, None], dl_col)


def _dft_consts(n1, n2):
    n = n1 * n2
    a1 = 2.0 * np.pi * ((np.arange(n1)[:, None] * np.arange(n1)[None, :]) % n1) / n1
    c1, s1 = np.cos(a1), np.sin(a1)
    a2 = 2.0 * np.pi * ((np.arange(n2)[:, None] * np.arange(n2)[None, :]) % n2) / n2
    c2, s2 = np.cos(a2), np.sin(a2)
    at = 2.0 * np.pi * ((np.arange(n1)[:, None] * np.arange(n2)[None, :]) % n) / n
    f1_full = np.concatenate([c1, -s1], axis=0)
    f1_half = f1_full[:, :n1 // 2]
    g1 = np.concatenate([c1[:n1 // 2], -s1[:n1 // 2]], axis=1)
    w2f = np.block([[c2, -s2], [s2, c2]])
    w2i = np.block([[c2, s2], [-s2, c2]])
    f = lambda x: jnp.asarray(x, F32)
    return dict(f1_full=f(f1_full), f1_half=f(f1_half), g1=f(g1), w2f=f(w2f), w2i=f(w2i),
                tr=f(np.cos(at)), ti=f(-np.sin(at)))


def _fft_fwd(slabs, f1, tr, ti, w2f, stack, n1, n2, prec):
    for c, x in enumerate(slabs):
        a = _dot(f1, x.astype(stack.dtype), prec)
        ar, ai = a[:n1], a[n1:]
        stack[c * n1:(c + 1) * n1, 0:n2] = (ar * tr - ai * ti).astype(stack.dtype)
        stack[c * n1:(c + 1) * n1, n2:2 * n2] = (ar * ti + ai * tr).astype(stack.dtype)
    return _dot(stack[...], w2f, prec)


def _split_bf16(x):
    hi = x.astype(BF16)
    return hi, (x - hi.astype(F32)).astype(BF16)


def _cat3(x, axis):
    hi, lo = _split_bf16(x)
    return jnp.concatenate([hi, hi, lo], axis=axis)


def _hy_spec_kernel(nrm_ref, k_ref, f1_ref, tr_ref, ti_ref, w2f_ref, o_ref, stack, *, cg, n1, n2):
    o = pl.program_id(0)
    g = pl.program_id(1)
    tr = tr_ref[...]
    ti = ti_ref[...]
    for c in range(cg):
        hi, lo = _split_bf16(k_ref[c])
        a = _dot(f1_ref[...], jnp.concatenate([hi, lo, hi], axis=0))
        ar, ai = a[:n1], a[n1:]
        sh, sl = _split_bf16(jnp.concatenate([ar * tr - ai * ti, ar * ti + ai * tr], axis=1))
        stack[c * n1:(c + 1) * n1, :] = jnp.concatenate([sh, sl, sh], axis=1)
    x = _dot(stack[...], w2f_ref[...])
    for c in range(cg):
        sc = nrm_ref[o * HY_CH + g * cg + c] * (1.0 / (n1 * n2))
        xc = x[c * n1:(c + 1) * n1] * sc
        o_ref[c, 0] = xc[:, :n2]
        o_ref[c, 1] = xc[:, n2:]


def _hy_spec(k4, nrm_flat, dc, cg, n1, n2):
    full = lambda shape: pl.BlockSpec(shape, lambda o, g: (0,) * len(shape))
    return pl.pallas_call(
        functools.partial(_hy_spec_kernel, cg=cg, n1=n1, n2=n2),
        grid=(2, HY_CH // cg),
        in_specs=[pl.BlockSpec(memory_space=pltpu.SMEM),
                  pl.BlockSpec((None, cg, n1, n2), lambda o, g: (o, g, 0, 0)),
                  full((2 * n1, 3 * n1)), full((n1, n2)), full((n1, n2)), full((6 * n2, 2 * n2))],
        out_specs=pl.BlockSpec((None, cg, 2, n1, n2), lambda o, g: (o, g, 0, 0, 0)),
        out_shape=jax.ShapeDtypeStruct((2, HY_CH, 2, n1, n2), F32),
        scratch_shapes=[pltpu.VMEM((cg * n1, 6 * n2), BF16)],
        compiler_params=_params(("parallel", "arbitrary")),
        name="hyena_filter_spectrum",
    )(nrm_flat, k4, _cat3(dc["f1_full"], 1), dc["tr"], dc["ti"], _cat3(dc["w2f"], 0))


def _hy_conv_kernel(skip_ref, x_ref, ks_ref, f1_ref, g1_ref, tr_ref, ti_ref, w2f_ref, w2i_ref,
                    o_ref, stack, *, cg, n1, n2):
    g = pl.program_id(1)
    tr = tr_ref[...]
    ti = ti_ref[...]

    def conv(slabs, order):
        x = _fft_fwd(slabs, f1_ref[...], tr, ti, w2f_ref[...], stack, n1, n2, None)
        for c in range(cg):
            xr, xi = x[c * n1:(c + 1) * n1, :n2], x[c * n1:(c + 1) * n1, n2:]
            kr, ki = ks_ref[order, c, 0], ks_ref[order, c, 1]
            stack[c * n1:(c + 1) * n1, 0:n2] = (xr * kr - xi * ki).astype(BF16)
            stack[c * n1:(c + 1) * n1, n2:2 * n2] = (xr * ki + xi * kr).astype(BF16)
        bm = _dot(stack[...], w2i_ref[...])
        outs = []
        for c in range(cg):
            br, bi = bm[c * n1:(c + 1) * n1, :n2], bm[c * n1:(c + 1) * n1, n2:]
            b2 = jnp.concatenate([br * tr + bi * ti, bi * tr - br * ti], axis=0)
            y = _dot(g1_ref[...], b2.astype(BF16))
            outs.append(y + slabs[c] * skip_ref[order * HY_CH + g * cg + c])
        return outs

    v = [x_ref[2, c] for c in range(cg)]
    y0 = conv(v, 0)
    z = [x_ref[0, c] * y0[c] for c in range(cg)]
    y1 = conv(z, 1)
    for c in range(cg):
        o_ref[c] = x_ref[1, c] * y1[c]


def _hy_conv(x4, kspec, skip_flat, dc, cg, n1, n2):
    bsz = x4.shape[1]
    full = lambda shape: pl.BlockSpec(shape, lambda b, g: (0,) * len(shape))
    return pl.pallas_call(
        functools.partial(_hy_conv_kernel, cg=cg, n1=n1, n2=n2),
        grid=(bsz, HY_CH // cg),
        in_specs=[pl.BlockSpec(memory_space=pltpu.SMEM),
                  pl.BlockSpec((3, None, cg, n1 // 2, n2), lambda b, g: (0, b, g, 0, 0)),
                  pl.BlockSpec((2, cg, 2, n1, n2), lambda b, g: (0, g, 0, 0, 0)),
                  full((2 * n1, n1 // 2)), full((n1 // 2, 2 * n1)), full((n1, n2)), full((n1, n2)),
                  full((2 * n2, 2 * n2)), full((2 * n2, 2 * n2))],
        out_specs=pl.BlockSpec((None, cg, n1 // 2, n2), lambda b, g: (b, g, 0, 0)),
        out_shape=jax.ShapeDtypeStruct((bsz, HY_CH, n1 // 2, n2), F32),
        scratch_shapes=[pltpu.VMEM((cg * n1, 2 * n2), BF16)],
        compiler_params=_params(("parallel", "arbitrary")),
        name="hyena_long_conv",
    )(skip_flat, x4, kspec, dc["f1_half"].astype(BF16), dc["g1"].astype(BF16), dc["tr"], dc["ti"],
      dc["w2f"].astype(BF16), dc["w2i"].astype(BF16))


def _hy_ctx_kernel(x_ref, k_ref, nrm_ref, skip_ref, fc_ref, fs_ref, o_ref, *, c):
    fc = fc_ref[...]
    fs = fs_ref[...]
    inv_n = 1.0 / (2 * c)

    def conv(x, order):
        kk = k_ref[order]
        kr, ki = _dot(kk, fc, HI), -_dot(kk, fs, HI)
        xr, xi = _dot(x, fc[:c], HI), -_dot(x, fs[:c], HI)
        yr, yi = xr * kr - xi * ki, xr * ki + xi * kr
        y = (_dot(yr, fc[:, :c], HI) - _dot(yi, fs[:, :c], HI)) * inv_n
        return y * nrm_ref[order] + x * skip_ref[order]

    z = x_ref[0] * conv(x_ref[2], 0)
    o_ref[...] = x_ref[1] * conv(z, 1)


def _hy_ctx_conv(xt, kt, nrm, skip_col):
    _, bsz, ch, c = xt.shape
    n = 2 * c
    ang = 2.0 * np.pi * ((np.arange(n)[:, None] * np.arange(n)[None, :]) % n) / n
    fc, fs = jnp.asarray(np.cos(ang), F32), jnp.asarray(np.sin(ang), F32)
    full = lambda shape: pl.BlockSpec(shape, lambda b: (0,) * len(shape))
    return pl.pallas_call(
        functools.partial(_hy_ctx_kernel, c=c),
        grid=(bsz,),
        in_specs=[pl.BlockSpec((3, None, ch, c), lambda b: (0, b, 0, 0)),
                  full((2, ch, n)), full((2, ch, 1)), full((2, ch, 1)), full((n, n)), full((n, n))],
        out_specs=pl.BlockSpec((None, ch, c), lambda b: (b, 0, 0)),
        out_shape=jax.ShapeDtypeStruct((bsz, ch, c), F32),
        compiler_params=_params(("arbitrary",)),
        name="hyena_ctx_conv",
    )(xt, kt, nrm, skip_col, fc, fs)


def _out_proj_kernel(a_ref, b_ref, p_ref, ht_ref, x_ref, g_ref, w_ref, o_ref):
    w = D_GROUP
    acc = _dot(a_ref[...].astype(BF16), w_ref[0:w])
    acc += _dot(b_ref[...].astype(BF16), w_ref[w:2 * w])
    acc += _dot(p_ref[...].astype(BF16), w_ref[2 * w:3 * w])
    acc += _dot(ht_ref[...].T.astype(BF16), w_ref[3 * w:4 * w])
    o_ref[...] = x_ref[...] + g_ref[...] * acc


def _out_proj(a, b, p, ht, x, mod3, row_of_batch, j_gate, w_out, tm):
    bsz, t, d = x.shape
    w = D_GROUP
    tok = pl.BlockSpec((None, tm, w), lambda bb, i: (bb, i, 0))
    return pl.pallas_call(
        _out_proj_kernel,
        grid=(bsz, t // tm),
        in_specs=[tok, tok, tok,
                  pl.BlockSpec((None, w, tm), lambda bb, i: (bb, 0, i)),
                  pl.BlockSpec((None, tm, d), lambda bb, i: (bb, i, 0)),
                  pl.BlockSpec((None, 1, d), lambda bb, i: (row_of_batch(bb), 0, j_gate)),
                  pl.BlockSpec((4 * w, d), lambda bb, i: (0, 0))],
        out_specs=pl.BlockSpec((None, tm, d), lambda bb, i: (bb, i, 0)),
        out_shape=jax.ShapeDtypeStruct((bsz, t, d), F32),
        compiler_params=_params(("parallel", "arbitrary")),
        name="out_proj_residual",
    )(a, b, p, ht, x, mod3, w_out)


MOE_ROWS = 256


def _route(x, g, sc, sh, rw3, rbc):
    per_group = N_EXPERTS // N_EXPERT_GROUPS
    tm = x.shape[0]
    ms = jnp.mean(x * x, axis=-1, keepdims=True)
    h = (x * lax.rsqrt(ms + EPS)) * g * (1.0 + sc) + sh
    hh, hl = _split_bf16(h)
    d = x.shape[1]
    logits = (_dot(hh, rw3[0:d]) + _dot(hl, rw3[d:2 * d]) + _dot(hh, rw3[2 * d:3 * d])).T[:N_EXPERTS]
    ex = jnp.exp(logits - jnp.max(logits, axis=0, keepdims=True))
    scores = ex / jnp.sum(ex, axis=0, keepdims=True)
    sel = scores + rbc
    srow = [sel[r:r + 1] for r in range(N_EXPERTS)]
    best = None
    for grp in range(N_EXPERT_GROUPS):
        rows = list(range(grp * per_group, (grp + 1) * per_group))
        v1 = functools.reduce(jnp.maximum, [srow[r] for r in rows])
        i1 = jnp.full((1, tm), rows[-1], jnp.int32)
        for r in reversed(rows[:-1]):
            i1 = jnp.where(srow[r] == v1, r, i1)
        rest = [jnp.where(i1 == r, -jnp.inf, srow[r]) for r in rows]
        v2 = functools.reduce(jnp.maximum, rest)
        i2 = jnp.full((1, tm), rows[-1], jnp.int32)
        for k in reversed(range(per_group - 1)):
            i2 = jnp.where(rest[k] == v2, rows[k], i2)
        gs = v1 + v2
        if best is None:
            best, e1, e2 = gs, i1, i2
        else:
            upd = gs > best
            best = jnp.where(upd, gs, best)
            e1 = jnp.where(upd, i1, e1)
            e2 = jnp.where(upd, i2, e2)
    row = lax.broadcasted_iota(jnp.int32, (N_EXPERTS, tm), 0)
    w1 = jnp.sum(jnp.where(row == e1, scores, 0.0), axis=0, keepdims=True)
    w2 = jnp.sum(jnp.where(row == e2, scores, 0.0), axis=0, keepdims=True)
    tot = w1 + w2
    gates_t = jnp.where(row == e1, w1 / tot, 0.0) + jnp.where(row == e2, w2 / tot, 0.0)
    gates_t = jnp.concatenate([gates_t, jnp.zeros((LANES - N_EXPERTS, tm), F32)], axis=0)
    return h, gates_t.T


def _moe_kernel(x_ref, g_ref, sc_ref, sh_ref, gate_ref, rw_ref, rb_ref, w1_ref, w3_ref, w2_ref, fg_ref,
                o_ref, h_scr, gates_scr, acc_scr, *, final):
    e = pl.program_id(2)
    tm = x_ref.shape[0]

    @pl.when(e == 0)
    def _():
        h, gates = _route(x_ref[...], g_ref[...], sc_ref[...], sh_ref[...], rw_ref[...], rb_ref[...])
        h_scr[...] = h.astype(BF16)
        gates_scr[...] = gates
        acc_scr[...] = jnp.zeros(acc_scr.shape, F32)

    lane = lax.broadcasted_iota(jnp.int32, (tm, LANES), 1)
    ge = jnp.sum(jnp.where(lane == e, gates_scr[...], 0.0), axis=1, keepdims=True)

    def up(j):
        hb = h_scr[j * MOE_ROWS:(j + 1) * MOE_ROWS, :]
        return _dot(hb, w1_ref[...]), _dot(hb, w3_ref[...])

    nxt = up(0)
    for j in range(tm // MOE_ROWS):
        a, b = nxt
        if (j + 1) * MOE_ROWS < tm:
            nxt = up(j + 1)
        rows = slice(j * MOE_ROWS, (j + 1) * MOE_ROWS)
        act = (a * jax.nn.sigmoid(a)) * b
        acc_scr[rows, :] += ge[rows] * _dot(act.astype(BF16), w2_ref[...])

    @pl.when(e == N_EXPERTS - 1)
    def _():
        y = x_ref[...] + gate_ref[...] * acc_scr[...]
        if final:
            ms = jnp.mean(y * y, axis=-1, keepdims=True)
            y = (y * lax.rsqrt(ms + EPS)) * fg_ref[...]
        o_ref[...] = y


def _moe(x, g, mod3, row_of_batch, j_shift, j_scale, j_gate, rw, rb, w1, w3, w2, final_g, final, tm):
    bsz, t, d = x.shape
    vec = lambda j: pl.BlockSpec((None, 1, d), lambda b, i, e: (row_of_batch(b), 0, j))
    full = lambda shape: pl.BlockSpec(shape, lambda b, i, e: (0,) * len(shape))
    return pl.pallas_call(
        functools.partial(_moe_kernel, final=final),
        grid=(bsz, t // tm, N_EXPERTS),
        in_specs=[pl.BlockSpec((None, tm, d), lambda b, i, e: (b, i, 0)),
                  full((1, d)), vec(j_scale), vec(j_shift), vec(j_gate),
                  full((3 * d, LANES)), full((N_EXPERTS, 1)),
                  pl.BlockSpec((None, d, D_EXPERT), lambda b, i, e: (e, 0, 0)),
                  pl.BlockSpec((None, d, D_EXPERT), lambda b, i, e: (e, 0, 0)),
                  pl.BlockSpec((None, D_EXPERT, d), lambda b, i, e: (e, 0, 0)),
                  full((1, d))],
        out_specs=pl.BlockSpec((None, tm, d), lambda b, i, e: (b, i, 0)),
        out_shape=jax.ShapeDtypeStruct((bsz, t, d), F32),
        scratch_shapes=[pltpu.VMEM((tm, d), BF16), pltpu.VMEM((tm, LANES), F32), pltpu.VMEM((tm, d), F32)],
        compiler_params=_params(("parallel", "parallel", "arbitrary")),
        name="moe_final" if final else "moe",
    )(x, g, mod3, mod3, mod3, rw, rb, w1, w3, w2, final_g)


MOE_SUB = 512
MOE_CAP = 128
MOE_BLK = 2048
MOE_PAIR = 2


def _moe_route_kernel(x_ref, g_ref, sc_ref, sh_ref, rw_ref, rb_ref, h_ref, gates_ref, cnt_ref):
    h, gates = _route(x_ref[...], g_ref[...], sc_ref[...], sh_ref[...], rw_ref[...], rb_ref[...])
    h_ref[...] = h.astype(BF16)
    gates_ref[...] = gates
    cnt = jnp.sum(jnp.where(gates > 0.0, 1.0, 0.0), axis=0, keepdims=True)
    cnt_ref[...] = jnp.broadcast_to(cnt, cnt_ref.shape)


def _moe_route(x, g, mod3, row_of_batch, j_shift, j_scale, rw, rb):
    bsz, t, d = x.shape
    tm = MOE_SUB
    vec = lambda j: pl.BlockSpec((None, 1, d), lambda b, i: (row_of_batch(b), 0, j))
    full = lambda shape: pl.BlockSpec(shape, lambda b, i: (0,) * len(shape))
    return pl.pallas_call(
        _moe_route_kernel,
        grid=(bsz, t // tm),
        in_specs=[pl.BlockSpec((None, tm, d), lambda b, i: (b, i, 0)),
                  full((1, d)), vec(j_scale), vec(j_shift), full((3 * d, LANES)), full((N_EXPERTS, 1))],
        out_specs=[pl.BlockSpec((None, tm, d), lambda b, i: (b, i, 0)),
                   pl.BlockSpec((None, tm, LANES), lambda b, i: (b, i, 0)),
                   pl.BlockSpec((None, None, SUBLANES, LANES), lambda b, i: (b, i, 0, 0))],
        out_shape=[jax.ShapeDtypeStruct((bsz, t, d), BF16),
                   jax.ShapeDtypeStruct((bsz, t, LANES), F32),
                   jax.ShapeDtypeStruct((bsz, t // tm, SUBLANES, LANES), F32)],
        compiler_params=_params(("parallel", "arbitrary")),
        name="moe_route",
    )(x, g, mod3, mod3, rw, rb)


def _moe_routed_kernel(np_ref, h_ref, gates_ref, x_ref, gate_ref, w1_ref, w3_ref, w2_ref, fg_ref,
                       o_ref, rank_scr, rankt_scr, gatet_scr, xc_scr, *, final, nblk):
    b = pl.program_id(0)
    i = pl.program_id(1)
    e = pl.program_id(2)
    nsub = h_ref.shape[0] // MOE_SUB
    sub = lambda j: slice(j * MOE_SUB, (j + 1) * MOE_SUB)

    @pl.when(e == 0)
    def _():
        o_ref[...] = jnp.zeros(o_ref.shape, F32)
        r = lax.broadcasted_iota(jnp.int32, (MOE_SUB, MOE_SUB), 0)
        c = lax.broadcasted_iota(jnp.int32, (MOE_SUB, MOE_SUB), 1)
        ltri = jnp.where(c <= r, 1.0, 0.0).astype(BF16)
        utri = jnp.where(r <= c, 1.0, 0.0).astype(BF16)
        for j in range(nsub):
            gt = gates_ref[sub(j), :]
            rank_scr[j] = _dot(ltri, jnp.where(gt > 0.0, 1.0, 0.0).astype(BF16))
            gtt = gt.T[:N_EXPERTS]
            gatet_scr[j] = gtt
            rankt_scr[j] = _dot(jnp.where(gtt > 0.0, 1.0, 0.0).astype(BF16), utri)

    lane = lax.broadcasted_iota(jnp.int32, (MOE_SUB, LANES), 1)
    slot_lane = lax.broadcasted_iota(jnp.int32, (MOE_SUB, MOE_PAIR * MOE_CAP), 1)
    second = slot_lane >= MOE_CAP

    def one_pass(p, carry):
        base = (p * MOE_CAP + 1).astype(F32)
        slot_r = lax.broadcasted_iota(jnp.int32, (MOE_CAP, MOE_SUB), 0).astype(F32) + base
        slot_c = jnp.where(second, slot_lane - MOE_CAP, slot_lane).astype(F32) + base
        ys = []
        gcs = [[], []]
        for j in range(nsub):
            picks = []
            for k in range(MOE_PAIR):
                ex = e * MOE_PAIR + k
                rr = rankt_scr[j, pl.ds(ex, 1), :]
                gr = gatet_scr[j, pl.ds(ex, 1), :]
                pick = (rr == slot_r) & (gr > 0.0)
                picks.append(jnp.where(pick, 1.0, 0.0).astype(BF16))
                gcs[k].append(jnp.sum(jnp.where(pick, gr, 0.0), axis=1, keepdims=True))
            xcj = _dot(jnp.concatenate(picks, axis=0), h_ref[sub(j), :]).astype(BF16)
            for k in range(MOE_PAIR):
                xc_scr[k, j * MOE_CAP:(j + 1) * MOE_CAP, :] = xcj[k * MOE_CAP:(k + 1) * MOE_CAP]
        for k in range(MOE_PAIR):
            xc = xc_scr[k]
            a = _dot(xc, w1_ref[k])
            bb = _dot(xc, w3_ref[k])
            y = _dot(((a * jax.nn.sigmoid(a)) * bb).astype(BF16), w2_ref[k])
            ys.append([(y[j * MOE_CAP:(j + 1) * MOE_CAP] * gcs[k][j]).astype(BF16) for j in range(nsub)])
        for j in range(nsub):
            cols = []
            for k in range(MOE_PAIR):
                ex = e * MOE_PAIR + k
                rc = jnp.sum(jnp.where(lane == ex, rank_scr[j], 0.0), axis=1, keepdims=True)
                gc = jnp.sum(jnp.where(lane == ex, gates_ref[sub(j), :], 0.0), axis=1, keepdims=True)
                cols.append(jnp.where(gc > 0.0, rc, 0.0))
            put = jnp.where(jnp.where(second, cols[1], cols[0]) == slot_c, 1.0, 0.0).astype(BF16)
            o_ref[sub(j), :] += _dot(put, jnp.concatenate([ys[0][j], ys[1][j]], axis=0))
        return carry

    lax.fori_loop(0, np_ref[(b * nblk + i) * (N_EXPERTS // MOE_PAIR) + e], one_pass, 0)

    @pl.when(e == N_EXPERTS // MOE_PAIR - 1)
    def _():
        y = x_ref[...] + gate_ref[...] * o_ref[...]
        if final:
            ms = jnp.mean(y * y, axis=-1, keepdims=True)
            y = (y * lax.rsqrt(ms + EPS)) * fg_ref[...]
        o_ref[...] = y


def _moe_routed(x, g, mod3, row_of_batch, j_shift, j_scale, j_gate, rw, rb, w1, w3, w2, final_g, final):
    bsz, t, d = x.shape
    h, gates, cnt = _moe_route(x, g, mod3, row_of_batch, j_shift, j_scale, rw, rb)
    tb = _tile(t, MOE_BLK)
    nblk = t // tb
    nsub = tb // MOE_SUB
    npair = N_EXPERTS // MOE_PAIR
    passes = jnp.ceil(cnt[:, :, 0, :N_EXPERTS] / MOE_CAP).astype(jnp.int32)
    passes = jnp.max(passes.reshape(bsz, nblk, nsub, npair, MOE_PAIR), axis=(2, 4)).reshape(-1)
    full = lambda shape: pl.BlockSpec(shape, lambda b, i, e, np_: (0,) * len(shape))
    blk = lambda w: pl.BlockSpec((None, tb, w), lambda b, i, e, np_: (b, i, 0), pipeline_mode=pl.Buffered(1))
    grid_spec = pltpu.PrefetchScalarGridSpec(
        num_scalar_prefetch=1,
        grid=(bsz, nblk, npair),
        in_specs=[blk(d), blk(LANES), blk(d),
                  pl.BlockSpec((None, 1, d), lambda b, i, e, np_: (row_of_batch(b), 0, j_gate)),
                  pl.BlockSpec((MOE_PAIR, d, D_EXPERT), lambda b, i, e, np_: (e, 0, 0)),
                  pl.BlockSpec((MOE_PAIR, d, D_EXPERT), lambda b, i, e, np_: (e, 0, 0)),
                  pl.BlockSpec((MOE_PAIR, D_EXPERT, d), lambda b, i, e, np_: (e, 0, 0)),
                  full((1, d))],
        out_specs=pl.BlockSpec((None, tb, d), lambda b, i, e, np_: (b, i, 0)),
        scratch_shapes=[pltpu.VMEM((nsub, MOE_SUB, LANES), F32),
                        pltpu.VMEM((nsub, N_EXPERTS, MOE_SUB), F32),
                        pltpu.VMEM((nsub, N_EXPERTS, MOE_SUB), F32),
                        pltpu.VMEM((MOE_PAIR, nsub * MOE_CAP, d), BF16)])
    return pl.pallas_call(
        functools.partial(_moe_routed_kernel, final=final, nblk=nblk),
        grid_spec=grid_spec,
        out_shape=jax.ShapeDtypeStruct((bsz, t, d), F32),
        compiler_params=_params(("parallel", "parallel", "arbitrary")),
        name="moe_routed_final" if final else "moe_routed",
    )(passes, h, gates, x, mod3, w1, w3, w2, final_g)


def _tile(n, pref):
    t = min(n, pref)
    assert n % t == 0
    return t


def _fft_split(n):
    n2 = LANES
    assert n % n2 == 0
    return n // n2, n2


def _hyena_latent(u, lp, fargs):
    bsz, s, _ = u.shape
    n1, n2 = _fft_split(2 * s)
    dc = _dft_consts(n1, n2)
    cg = 8
    kt_l, _, nrm_l = _hy_filters(s, *fargs, tp=_tile(2 * s, 1024))
    kspec = _hy_spec(kt_l.reshape(2, HY_CH, n1, n2), nrm_l.reshape(2 * HY_CH), dc, cg, n1, n2)
    xt = _hy_short(u, lp["hy_short_w"].astype(F32), lp["hy_short_b"].astype(F32)[None, :], _tile(s, 2048))
    h_l = _hy_conv(xt.reshape(3, bsz, HY_CH, n1 // 2, n2), kspec, lp["hy_skip"].astype(F32).reshape(2 * HY_CH),
                   dc, cg, n1, n2)
    return h_l.reshape(bsz, HY_CH, s)


def _mixers(lat_in, ctx_in, lp, li, need_ctx):
    ub, u, ql, klt, vl = lat_in
    ucb, uc, qc, kct, vc = ctx_in
    bsz, s, _ = u.shape
    c = uc.shape[1]
    lam_init = 0.8 - 0.6 * math.exp(-0.3 * li)
    lam_vecs = lp["a_lambda"].astype(F32)
    subln = lp["a_subln_g"].astype(F32)[None, :]

    a_l = _diff_attn(ql, kct, vc, klt, vl, lam_vecs, subln, lam_init, _tile(s, 512), _tile(s, 4096))
    bias8 = _nbr_bias(lp["b_rpb"])
    b_l = _nbr_attn(ub, ucb, bias8)
    wbd = jax.scipy.linalg.block_diag(*[lp["pool_w"][g] for g in range(len(POOL_SIZES))]).astype(BF16)
    pscale = lp["pool_scale"].astype(F32)[None, :]
    p_l = _pool_mix(u, wbd, pscale, _tile(s, 2048))
    fargs = (lp["hy_f_w1"], lp["hy_f_b1"], lp["hy_f_w2"], lp["hy_f_b2"], lp["hy_f_w3"], lp["hy_f_b3"])
    skip = lp["hy_skip"].astype(F32)
    w_short = lp["hy_short_w"].astype(F32)
    b_short = lp["hy_short_b"].astype(F32)[None, :]
    h_l = _hyena_latent(u, lp, fargs)
    lat = (a_l, b_l, p_l, h_l)
    if not need_ctx:
        return lat, None
    a_c = _diff_attn(qc, kct, vc, None, None, lam_vecs, subln, lam_init, _tile(c, 256), None)
    b_c = _nbr_ctx_attn(ucb)
    p_c = _pool_mix(uc, wbd, pscale, _tile(c, 256))
    kt_c, _, nrm_c = _hy_filters(c, *fargs, tp=_tile(2 * c, 512))
    xtc = _hy_short(uc, w_short, b_short, _tile(c, 256))
    h_c = _hy_ctx_conv(xtc, kt_c, nrm_c, skip[:, :, None])
    return lat, (a_c, b_c, p_c, h_c)


def kernel(x, c, ctx, c_ctx, norm1_g, norm2_g, ada_w, ada_b, w_in, w_out, a_lambda, a_subln_g, b_rpb, pool_w, pool_scale, hy_short_w, hy_short_b, hy_f_w1, hy_f_b1, hy_f_w2, hy_f_b2, hy_f_w3, hy_f_b3, hy_skip, router_w, router_b, moe_w1, moe_w3, moe_w2, final_g):
    depth = norm1_g.shape[0]
    bsz, s, d = x.shape
    cl = ctx.shape[1]
    assert bsz <= SUBLANES - 1
    xl, xc = x, ctx
    cpad = jnp.zeros((SUBLANES, d), F32).at[:bsz].set(c.astype(F32)).at[bsz].set(c_ctx.astype(F32))
    rw = _cat3(jnp.pad(router_w.astype(F32), ((0, 0), (0, LANES - N_EXPERTS))), 0)
    rb = router_b.astype(F32)[:, None]
    lat_row = lambda b: b
    ctx_row = lambda b: bsz
    fg = final_g.astype(F32)[None, :]
    tm = _tile(s, 512)
    tmc = _tile(cl, 256)
    for li in range(depth):
        need_ctx = li < depth - 1
        lp = dict(a_lambda=a_lambda[li], a_subln_g=a_subln_g[li], b_rpb=b_rpb[li], pool_w=pool_w[li],
                  pool_scale=pool_scale[li], hy_short_w=hy_short_w[li], hy_short_b=hy_short_b[li],
                  hy_f_w1=hy_f_w1[li], hy_f_b1=hy_f_b1[li], hy_f_w2=hy_f_w2[li], hy_f_b2=hy_f_b2[li],
                  hy_f_w3=hy_f_w3[li], hy_f_b3=hy_f_b3[li], hy_skip=hy_skip[li])
        mod3 = _ada(cpad, ada_w[li].astype(F32), ada_b[li].astype(F32)[None, :]).reshape(SUBLANES, 1, 6 * d)
        n1g = norm1_g[li].astype(F32)[None, :]
        n2g = norm2_g[li].astype(F32)[None, :]
        w_in_b = w_in[li].astype(BF16)
        w_out_b = w_out[li].astype(BF16)
        lat_in = _norm_proj(xl, n1g, mod3, lat_row, 0, 1, w_in_b, tm, True)
        ctx_in = _norm_proj(xc, n1g, mod3, ctx_row, 0, 1, w_in_b, tmc, False)
        lat, cx = _mixers(lat_in, ctx_in, lp, li, need_ctx)
        xl = _out_proj(*lat, xl, mod3, lat_row, 2, w_out_b, _tile(s, 1024))
        w1b, w3b, w2b = moe_w1[li].astype(BF16), moe_w3[li].astype(BF16), moe_w2[li].astype(BF16)
        if need_ctx:
            xc = _out_proj(*cx, xc, mod3, ctx_row, 2, w_out_b, tmc)
            xc = _moe(xc, n2g, mod3, ctx_row, 3, 4, 5, rw, rb, w1b, w3b, w2b, fg, False, tmc)
        xl = _moe_routed(xl, n2g, mod3, lat_row, 3, 4, 5, rw, rb, w1b, w3b, w2b, fg, li == depth - 1)
    return xl
```

```python
import functools
import math

import numpy as np
import jax
import jax.numpy as jnp
from jax import lax
from jax.experimental import pallas as pl
from jax.experimental.pallas import tpu as pltpu

F32 = jnp.float32
BF16 = jnp.bfloat16
HI = lax.Precision.HIGHEST

GRID_W = 64
A_HEADS = 4
A_QK = 32
A_V = 64
ROPE_BASE = 10000.0
B_HEADS = 4
B_DIM = 64
WIN_R = 8
WIN_C = 16
POOL_SIZES = (2, 4, 8, 16)
POOL_CH = 64
D_GROUP = 256
HY_CH = 256
HY_BANDS = 16
HY_EMB = 1 + 2 * HY_BANDS
HY_HIDDEN = 64
HY_SIN_FREQ = 1.0
HY_MIN_DECAY = math.log(1e-2) / 1.5
HY_MAX_DECAY = math.log(1e-2) / 0.3
N_EXPERTS = 16
N_EXPERT_GROUPS = 4
D_EXPERT = 512
EPS = 1e-6
LOG2E = 1.4426950408889634

LANES = 128
SUBLANES = 8
VMEM_LIMIT = 56 * 1024 * 1024


def _params(sem):
    return pltpu.CompilerParams(dimension_semantics=sem, vmem_limit_bytes=VMEM_LIMIT)


def _dot(a, b, prec=None):
    return jnp.dot(a, b, precision=prec, preferred_element_type=F32)


def _dot_nt(a, b):
    return lax.dot_general(a, b, (((1,), (1,)), ((), ())), preferred_element_type=F32)


def _ada_kernel(c_ref, w_ref, b_ref, o_ref):
    cf = c_ref[...]
    s = cf * jax.nn.sigmoid(cf)
    o_ref[...] = _dot(s, w_ref[...], HI) + b_ref[...]


def _ada(cpad, w, b):
    d = cpad.shape[1]
    n = w.shape[1]
    return pl.pallas_call(
        _ada_kernel,
        grid=(n // d,),
        in_specs=[pl.BlockSpec((SUBLANES, d), lambda j: (0, 0)),
                  pl.BlockSpec((d, d), lambda j: (0, j)),
                  pl.BlockSpec((1, d), lambda j: (0, j))],
        out_specs=pl.BlockSpec((SUBLANES, d), lambda j: (0, j)),
        out_shape=jax.ShapeDtypeStruct((SUBLANES, n), F32),
        compiler_params=_params(("arbitrary",)),
        name="ada_mod",
    )(cpad, w, b)


A_COLS = 768


def _attn_layout(u, cos_t, sin_t, q_ref, kt_ref, v_ref):
    q = u[:, 0:256]
    k = u[:, 256:512]
    v = u[:, 512:768]
    if cos_t is not None:
        lane = lax.broadcasted_iota(jnp.int32, cos_t.shape, 1)
        first = (lane % (2 * 16)) < 16

        def rot(x):
            halves = []
            for j in range(2):
                xh = x[:, j * LANES:(j + 1) * LANES]
                swap = jnp.where(first, pltpu.roll(xh, LANES - 16, axis=1), pltpu.roll(xh, 16, axis=1))
                halves.append(xh * cos_t + swap * sin_t)
            return jnp.concatenate(halves, axis=1)

        q = rot(q)
        k = rot(k)
    q = q * (A_QK ** -0.5 * LOG2E)
    kt = k.T
    for hc in range(2 * A_HEADS):
        q_ref[hc] = q[:, hc * A_QK:(hc + 1) * A_QK].astype(BF16)
        kt_ref[hc] = kt[hc * A_QK:(hc + 1) * A_QK, :].astype(BF16)
    lane = lax.broadcasted_iota(jnp.int32, (v.shape[0], LANES - A_V), 1)
    ones_col = jnp.where(lane == 0, 1.0, 0.0)
    for h in range(A_HEADS):
        v_ref[h] = jnp.concatenate([v[:, h * A_V:(h + 1) * A_V], ones_col], axis=1).astype(BF16)


def _rope_tables(length):
    n_freq = A_QK // 4
    inv = ROPE_BASE ** (-jnp.arange(n_freq, dtype=F32) / n_freq)
    t = jnp.arange(length)
    row = (t // GRID_W).astype(F32)
    col = (t % GRID_W).astype(F32)
    ang = jnp.concatenate([row[:, None] * inv, col[:, None] * inv], axis=-1)
    cos, sin = jnp.cos(ang), jnp.sin(ang)
    cos_t = jnp.tile(jnp.concatenate([cos, cos], axis=-1), (1, LANES // 32))
    sin_t = jnp.tile(jnp.concatenate([-sin, sin], axis=-1), (1, LANES // 32))
    return cos_t, sin_t


def _norm_proj_kernel(*refs, rope):
    if rope:
        x_ref, g_ref, sc_ref, sh_ref, w_ref, cos_ref, sin_ref, u_ref, q_ref, kt_ref, v_ref = refs
    else:
        x_ref, g_ref, sc_ref, sh_ref, w_ref, u_ref, q_ref, kt_ref, v_ref = refs
    x = x_ref[...]
    ms = jnp.mean(x * x, axis=-1, keepdims=True)
    h = (x * lax.rsqrt(ms + EPS)) * g_ref[...] * (1.0 + sc_ref[...]) + sh_ref[...]
    u = _dot(h.astype(BF16), w_ref[...])
    u_ref[...] = u[:, A_COLS:]
    _attn_layout(u[:, :A_COLS], cos_ref[...] if rope else None, sin_ref[...] if rope else None, q_ref, kt_ref, v_ref)


def _norm_proj(x, g, mod3, row_of_batch, j_shift, j_scale, w, tm, rope):
    bsz, t, d = x.shape
    n = w.shape[1]
    nh = 2 * A_HEADS
    in_specs = [pl.BlockSpec((None, tm, d), lambda b, i: (b, i, 0)),
                pl.BlockSpec((1, d), lambda b, i: (0, 0)),
                pl.BlockSpec((None, 1, d), lambda b, i: (row_of_batch(b), 0, j_scale)),
                pl.BlockSpec((None, 1, d), lambda b, i: (row_of_batch(b), 0, j_shift)),
                pl.BlockSpec((d, n), lambda b, i: (0, 0))]
    args = [x, g, mod3, mod3, w]
    if rope:
        in_specs += [pl.BlockSpec((tm, LANES), lambda b, i: (i, 0))] * 2
        args += list(_rope_tables(t))
    return pl.pallas_call(
        functools.partial(_norm_proj_kernel, rope=rope),
        grid=(bsz, t // tm),
        in_specs=in_specs,
        out_specs=[pl.BlockSpec((None, tm, n - A_COLS), lambda b, i: (b, i, 0)),
                   pl.BlockSpec((None, nh, tm, A_QK), lambda b, i: (b, 0, i, 0)),
                   pl.BlockSpec((None, nh, A_QK, tm), lambda b, i: (b, 0, 0, i)),
                   pl.BlockSpec((None, A_HEADS, tm, LANES), lambda b, i: (b, 0, i, 0))],
        out_shape=[jax.ShapeDtypeStruct((bsz, t, n - A_COLS), F32),
                   jax.ShapeDtypeStruct((bsz, nh, t, A_QK), BF16),
                   jax.ShapeDtypeStruct((bsz, nh, A_QK, t), BF16),
                   jax.ShapeDtypeStruct((bsz, A_HEADS, t, LANES), BF16)],
        compiler_params=_params(("parallel", "arbitrary")),
        name="norm_in_proj_rope" if rope else "norm_in_proj_ctx",
    )(*args)


QK_LOOKAHEAD = 2


def _dattn_kernel(*refs, lam_init, has_lat):
    if has_lat:
        lam_ref, g_ref, q_ref, kc_ref, vc_ref, k_ref, v_ref, o_ref, m_scr, acc_scr = refs
    else:
        lam_ref, g_ref, q_ref, kc_ref, vc_ref, o_ref, m_scr, acc_scr = refs
    ki = pl.program_id(2)
    nk = pl.num_programs(2)
    nh = 2 * A_HEADS

    def update(kt_r, v_r):
        scores = [_dot(q_ref[j], kt_r[j]) for j in range(QK_LOOKAHEAD)]
        for hc in range(nh):
            s = scores[hc]
            if hc + QK_LOOKAHEAD < nh:
                scores.append(_dot(q_ref[hc + QK_LOOKAHEAD], kt_r[hc + QK_LOOKAHEAD]))
            m_prev = m_scr[hc]
            m_new = jnp.maximum(m_prev, jnp.max(s, axis=1, keepdims=True))
            alpha = jnp.exp2(m_prev - m_new)
            p = jnp.exp2((s - m_new[:, :1]).astype(BF16))
            acc_scr[hc] = alpha * acc_scr[hc] + _dot(p, v_r[hc // 2])
            m_scr[hc] = m_new

    @pl.when(ki == 0)
    def _():
        m_scr[...] = jnp.full(m_scr.shape, -jnp.inf, F32)
        acc_scr[...] = jnp.zeros(acc_scr.shape, F32)
        update(kc_ref, vc_ref)

    if has_lat:
        @pl.when(ki > 0)
        def _():
            update(k_ref, v_ref)

    @pl.when(ki == nk - 1)
    def _():
        lv = lam_ref[...]
        lam = (jnp.exp(jnp.sum(lv[0:1] * lv[1:2], axis=1, keepdims=True))
               - jnp.exp(jnp.sum(lv[2:3] * lv[3:4], axis=1, keepdims=True)) + lam_init)
        for h in range(A_HEADS):
            a0 = acc_scr[2 * h]
            a1 = acc_scr[2 * h + 1]
            o = a0[:, :A_V] / a0[:, A_V:A_V + 1] - lam * (a1[:, :A_V] / a1[:, A_V:A_V + 1])
            ms = jnp.mean(o * o, axis=-1, keepdims=True)
            o_ref[:, h * A_V:(h + 1) * A_V] = (o * lax.rsqrt(ms + EPS)) * g_ref[...] * (1.0 - lam_init)


def _diff_attn(q, kct, vc, kt, v, lam_vecs, subln_g, lam_init, tq, tk):
    bsz, nh, t, _ = q.shape
    c = kct.shape[-1]
    has_lat = kt is not None
    nk = 1 + (kt.shape[-1] // tk if has_lat else 0)
    in_specs = [pl.BlockSpec((4, A_QK), lambda b, i, k: (0, 0)),
                pl.BlockSpec((1, A_V), lambda b, i, k: (0, 0)),
                pl.BlockSpec((None, nh, tq, A_QK), lambda b, i, k: (b, 0, i, 0)),
                pl.BlockSpec((None, nh, A_QK, c), lambda b, i, k: (b, 0, 0, 0)),
                pl.BlockSpec((None, A_HEADS, c, LANES), lambda b, i, k: (b, 0, 0, 0))]
    args = [lam_vecs, subln_g, q, kct, vc]
    if has_lat:
        in_specs += [pl.BlockSpec((None, nh, A_QK, tk), lambda b, i, k: (b, 0, 0, jnp.maximum(k - 1, 0))),
                     pl.BlockSpec((None, A_HEADS, tk, LANES), lambda b, i, k: (b, 0, jnp.maximum(k - 1, 0), 0))]
        args += [kt, v]
    return pl.pallas_call(
        functools.partial(_dattn_kernel, lam_init=lam_init, has_lat=has_lat),
        grid=(bsz, t // tq, nk),
        in_specs=in_specs,
        out_specs=pl.BlockSpec((None, tq, A_HEADS * A_V), lambda b, i, k: (b, i, 0)),
        out_shape=jax.ShapeDtypeStruct((bsz, t, A_HEADS * A_V), F32),
        scratch_shapes=[pltpu.VMEM((nh, tq, LANES), F32),
                        pltpu.VMEM((nh, tq, LANES), F32)],
        compiler_params=_params(("parallel", "parallel", "arbitrary")),
        name="diff_attn" if has_lat else "diff_attn_ctx",
    )(*args)


NB_ROWS = 16


def _nbr_bias(rpb):
    cols = jnp.arange(GRID_W)
    c0 = jnp.clip(cols - WIN_C // 2, 0, GRID_W - WIN_C)
    in_win = (cols[None, :] >= c0[:, None]) & (cols[None, :] < c0[:, None] + WIN_C)
    dc = jnp.clip(cols[None, :] - cols[:, None], -(WIN_C - 1), WIN_C - 1) + (WIN_C - 1)
    onehot = (dc[None] == jnp.arange(2 * WIN_C - 1)[:, None, None]).astype(F32)
    g = jnp.einsum("hab,bqk->haqk", rpb.astype(F32), onehot, precision=HI)
    g = jnp.where(in_win[None, None], g, -jnp.inf)
    b = jnp.stack([g[:, a0:a0 + WIN_R] for a0 in range(WIN_R)], axis=0)
    b = jnp.transpose(b, (0, 1, 3, 2, 4))
    return b.reshape(WIN_R, B_HEADS, GRID_W, WIN_R * GRID_W)


def _nbr_kernel(q_ref, kp_ref, kc_ref, kn_ref, vp_ref, vcur_ref, vn_ref, kctx_ref, vctx_ref, bias_ref,
                o_ref, kwin, vwin, kcx, vcx, *, n_rows):
    rb = pl.program_id(1)
    blk = NB_ROWS * GRID_W
    scale = B_DIM ** -0.5
    for h in range(B_HEADS):
        sl = slice(h * B_DIM, (h + 1) * B_DIM)
        for j, (kr, vr) in enumerate(((kp_ref, vp_ref), (kc_ref, vcur_ref), (kn_ref, vn_ref))):
            kwin[h, j * blk:(j + 1) * blk, :] = kr[:, sl].astype(BF16)
            vwin[h, j * blk:(j + 1) * blk, :] = vr[:, sl].astype(BF16)
        kcx[h] = kctx_ref[:, sl].astype(BF16)
        vcx[h] = vctx_ref[:, sl].astype(BF16)

    def window(rr):
        r = rb * NB_ROWS + rr
        r0 = jnp.clip(r - WIN_R // 2, 0, n_rows - WIN_R)
        off = pl.multiple_of((r0 - (rb - 1) * NB_ROWS) * GRID_W, GRID_W)
        return off, r0 - r + (WIN_R - 1)

    def scores(rr):
        off, a0 = window(rr)
        qrow = q_ref[rr * GRID_W:(rr + 1) * GRID_W, :]
        out = []
        for h in range(B_HEADS):
            qh = qrow[:, h * B_DIM:(h + 1) * B_DIM].astype(BF16)
            s = _dot_nt(qh, kwin[h, pl.ds(off, WIN_R * GRID_W), :]) * scale + bias_ref[a0, h]
            out.append((s, _dot_nt(qh, kcx[h]) * scale))
        return out

    nxt = scores(0)
    for rr in range(NB_ROWS):
        cur = nxt
        if rr + 1 < NB_ROWS:
            nxt = scores(rr + 1)
        off, _ = window(rr)
        outs = []
        for h in range(B_HEADS):
            s, sc = cur[h]
            m = jnp.maximum(jnp.max(s, axis=1, keepdims=True), jnp.max(sc, axis=1, keepdims=True))
            p = jnp.exp(s - m)
            pc = jnp.exp(sc - m)
            l = jnp.sum(p, axis=1, keepdims=True) + jnp.sum(pc, axis=1, keepdims=True)
            o = _dot(p.astype(BF16), vwin[h, pl.ds(off, WIN_R * GRID_W), :]) + _dot(pc.astype(BF16), vcx[h])
            outs.append(o / l)
        o_ref[rr * GRID_W:(rr + 1) * GRID_W, :] = jnp.concatenate(outs, axis=1)


def _nbr_attn(u, uc, bias8):
    bsz, s, _ = u.shape
    c = uc.shape[1]
    n_rows = s // GRID_W
    nb = n_rows // NB_ROWS
    blk = NB_ROWS * GRID_W
    w = B_HEADS * B_DIM

    def spec(col, shift):
        return pl.BlockSpec((None, blk, w), lambda b, i: (b, jnp.clip(i + shift, 0, nb - 1), col))

    return pl.pallas_call(
        functools.partial(_nbr_kernel, n_rows=n_rows),
        grid=(bsz, nb),
        in_specs=[spec(0, 0), spec(1, -1), spec(1, 0), spec(1, 1), spec(2, -1), spec(2, 0), spec(2, 1),
                  pl.BlockSpec((None, c, w), lambda b, i: (b, 0, 1)),
                  pl.BlockSpec((None, c, w), lambda b, i: (b, 0, 2)),
                  pl.BlockSpec(bias8.shape, lambda b, i: (0, 0, 0, 0))],
        out_specs=pl.BlockSpec((None, blk, w), lambda b, i: (b, i, 0)),
        out_shape=jax.ShapeDtypeStruct((bsz, s, w), F32),
        scratch_shapes=[pltpu.VMEM((B_HEADS, 3 * blk, B_DIM), BF16),
                        pltpu.VMEM((B_HEADS, 3 * blk, B_DIM), BF16),
                        pltpu.VMEM((B_HEADS, c, B_DIM), BF16),
                        pltpu.VMEM((B_HEADS, c, B_DIM), BF16)],
        compiler_params=_params(("parallel", "arbitrary")),
        name="nbr_attn",
    )(u, u, u, u, u, u, u, uc, uc, bias8)


def _nbr_ctx_kernel(q_ref, k_ref, v_ref, o_ref):
    scale = B_DIM ** -0.5
    outs = []
    for h in range(B_HEADS):
        sl = slice(h * B_DIM, (h + 1) * B_DIM)
        s = _dot_nt(q_ref[:, sl].astype(BF16), k_ref[:, sl].astype(BF16)) * scale
        m = jnp.max(s, axis=1, keepdims=True)
        p = jnp.exp(s - m)
        l = jnp.sum(p, axis=1, keepdims=True)
        outs.append(_dot(p.astype(BF16), v_ref[:, sl].astype(BF16)) / l)
    o_ref[...] = jnp.concatenate(outs, axis=1)


def _nbr_ctx_attn(uc):
    bsz, c, _ = uc.shape
    w = B_HEADS * B_DIM
    return pl.pallas_call(
        _nbr_ctx_kernel,
        grid=(bsz,),
        in_specs=[pl.BlockSpec((None, c, w), lambda b: (b, 0, 0)),
                  pl.BlockSpec((None, c, w), lambda b: (b, 0, 1)),
                  pl.BlockSpec((None, c, w), lambda b: (b, 0, 2))],
        out_specs=pl.BlockSpec((None, c, w), lambda b: (b, 0, 0)),
        out_shape=jax.ShapeDtypeStruct((bsz, c, w), F32),
        compiler_params=_params(("arbitrary",)),
        name="nbr_attn_ctx",
    )(uc, uc, uc)


HALO = SUBLANES


def _halo_specs(tm, length, col, width):
    nt = length // tm
    per = tm // HALO
    last = length // HALO - 1
    return [pl.BlockSpec((None, HALO, width), lambda b, i, *_: (b, jnp.maximum(i * per - 1, 0), col)),
            pl.BlockSpec((None, tm, width), lambda b, i, *_: (b, i, col)),
            pl.BlockSpec((None, HALO, width), lambda b, i, *_: (b, jnp.minimum((i + 1) * per, last), col))], nt


def _fill_halo(buf, prev_ref, cur_ref, next_ref, i, nt, tm):
    zero = jnp.zeros(prev_ref.shape, F32)
    buf[0:HALO, :] = jnp.where(i > 0, prev_ref[...], zero)
    buf[HALO:HALO + tm, :] = cur_ref[...]
    buf[HALO + tm:, :] = jnp.where(i < nt - 1, next_ref[...], zero)


def _pool_kernel(prev_ref, cur_ref, next_ref, w_ref, ps_ref, o_ref, buf, *, tm, nt, length):
    i = pl.program_id(1)
    _fill_halo(buf, prev_ref, cur_ref, next_ref, i, nt, tm)

    def sh(j):
        return buf[HALO + j:HALO + j + tm, :]

    u = sh(0)
    sums = []
    acc = None
    lo, hi = 0, 0
    for w in POOL_SIZES:
        for j in list(range(-(w // 2), lo)) + list(range(hi, w // 2)):
            acc = sh(j) if acc is None else acc + sh(j)
        lo, hi = -(w // 2), w // 2
        sums.append(acc)
    lane = lax.broadcasted_iota(jnp.int32, (tm, D_GROUP), 1)
    t = (i * tm + lax.broadcasted_iota(jnp.int32, (tm, D_GROUP), 0))
    wsum = sums[-1]
    half = jnp.full((tm, D_GROUP), POOL_SIZES[-1] // 2, jnp.int32)
    for g in range(len(POOL_SIZES) - 2, -1, -1):
        sel = lane < (g + 1) * POOL_CH
        wsum = jnp.where(sel, sums[g], wsum)
        half = jnp.where(sel, POOL_SIZES[g] // 2, half)
    cnt = (jnp.minimum(t + half, length) - jnp.maximum(t - half, 0)).astype(F32)
    d = wsum / cnt - u
    o_ref[...] = _dot(d.astype(BF16), w_ref[...]) * ps_ref[...]


def _pool_mix(u, wbd, pool_scale, tm):
    bsz, length, _ = u.shape
    specs, nt = _halo_specs(tm, length, 3, D_GROUP)
    return pl.pallas_call(
        functools.partial(_pool_kernel, tm=tm, nt=nt, length=length),
        grid=(bsz, nt),
        in_specs=specs + [pl.BlockSpec((D_GROUP, D_GROUP), lambda b, i: (0, 0)),
                          pl.BlockSpec((1, D_GROUP), lambda b, i: (0, 0))],
        out_specs=pl.BlockSpec((None, tm, D_GROUP), lambda b, i: (b, i, 0)),
        out_shape=jax.ShapeDtypeStruct((bsz, length, D_GROUP), F32),
        scratch_shapes=[pltpu.VMEM((tm + 2 * HALO, D_GROUP), F32)],
        compiler_params=_params(("parallel", "arbitrary")),
        name="pool_mix",
    )(u, u, u, wbd, pool_scale)


def _hy_short_kernel(prev_ref, cur_ref, next_ref, w_ref, b_ref, o_ref, buf, *, tm, nt):
    i = pl.program_id(1)
    _fill_halo(buf, prev_ref, cur_ref, next_ref, i, nt, tm)
    w = w_ref[...]
    y = (buf[HALO - 1:HALO - 1 + tm, :] * w[0:1] + buf[HALO:HALO + tm, :] * w[1:2]
         + buf[HALO + 1:HALO + 1 + tm, :] * w[2:3] + b_ref[...])
    o_ref[...] = y.T


def _hy_short(u, w_short, b_short, tm):
    bsz, length, _ = u.shape
    nt = length // tm
    per = tm // HALO
    last = length // HALO - 1
    c0 = 4
    in_specs = [pl.BlockSpec((None, HALO, HY_CH), lambda b, i, j: (b, jnp.maximum(i * per - 1, 0), c0 + j)),
                pl.BlockSpec((None, tm, HY_CH), lambda b, i, j: (b, i, c0 + j)),
                pl.BlockSpec((None, HALO, HY_CH), lambda b, i, j: (b, jnp.minimum((i + 1) * per, last), c0 + j)),
                pl.BlockSpec((3, HY_CH), lambda b, i, j: (0, j)),
                pl.BlockSpec((1, HY_CH), lambda b, i, j: (0, j))]
    return pl.pallas_call(
        functools.partial(_hy_short_kernel, tm=tm, nt=nt),
        grid=(bsz, nt, 3),
        in_specs=in_specs,
        out_specs=pl.BlockSpec((None, None, HY_CH, tm), lambda b, i, j: (j, b, 0, i)),
        out_shape=jax.ShapeDtypeStruct((3, bsz, HY_CH, length), F32),
        scratch_shapes=[pltpu.VMEM((tm + 2 * HALO, HY_CH), F32)],
        compiler_params=_params(("parallel", "arbitrary", "arbitrary")),
        name="hyena_short_conv",
    )(u, u, u, w_short, b_short)


HY_FEAT = 40


def _hy_filter_kernel(band_ref, w1_ref, b1_ref, w2_ref, b2_ref, w3_ref, b3_ref, dl_ref,
                      k_ref, ssq_ref, nrm_ref, *, tp, length):
    i = pl.program_id(0)
    n_i = pl.num_programs(0)
    m = i * tp + lax.broadcasted_iota(jnp.int32, (1, tp), 1)
    t = jnp.where(m <= length, m, 2 * length - m).astype(F32)
    t_norm = t / max(length - 1, 1)
    ang = ((2.0 * math.pi / length) * t) * band_ref[...]
    row = lax.broadcasted_iota(jnp.int32, (HY_FEAT, tp), 0)
    z = jnp.where(row == 0, t_norm,
                  jnp.where(row <= HY_BANDS, jnp.cos(ang), jnp.where(row < HY_EMB, jnp.sin(ang), 0.0)))
    z = jnp.concatenate([z, jnp.zeros((LANES - HY_FEAT, tp), F32)], axis=0)
    h = jnp.sin(HY_SIN_FREQ * (_dot(w1_ref[...], z, HI) + b1_ref[...]))
    h = jnp.sin(HY_SIN_FREQ * (_dot(w2_ref[...], h, HI) + b2_ref[...]))
    hh, hl = _split_bf16(h)
    h = _dot(w3_ref[...], jnp.concatenate([hh, hl, hh], axis=0)) + b3_ref[...]
    h = h * jnp.exp(-t_norm * dl_ref[...])

    @pl.when(i == 0)
    def _():
        ssq_ref[...] = jnp.zeros(ssq_ref.shape, F32)

    for o in range(2):
        fwd = h[o * 2 * HY_CH:o * 2 * HY_CH + HY_CH]
        bwd = h[o * 2 * HY_CH + HY_CH:(o + 1) * 2 * HY_CH]
        k = jnp.where(m < length, fwd, jnp.where(m == length, 0.0, bwd))
        k_ref[o] = k
        extra = jnp.where(m == 0, bwd * bwd, 0.0)
        ssq_ref[o] += jnp.sum(k * k + extra, axis=1, keepdims=True)

    @pl.when(i == n_i - 1)
    def _():
        nrm_ref[...] = lax.rsqrt(ssq_ref[...] + EPS)


def _hy_filters(length, w1, b1, w2, b2, w3, b3, tp):
    bands = jnp.linspace(1e-4, HY_BANDS - 1, HY_BANDS, dtype=F32)
    band_col = jnp.concatenate([jnp.zeros((1,), F32), bands, bands,
                                jnp.zeros((HY_FEAT - HY_EMB,), F32)])[:, None]
    deltas = jnp.abs(jnp.linspace(HY_MIN_DECAY, HY_MAX_DECAY, HY_CH, dtype=F32))
    dl_col = jnp.tile(deltas, 4)[:, None]
    w1t = jnp.pad(w1.astype(F32).T, ((0, 0), (0, LANES - HY_EMB)))
    full = lambda shape: pl.BlockSpec(shape, lambda i: (0,) * len(shape))
    n = 2 * length
    return pl.pallas_call(
        functools.partial(_hy_filter_kernel, tp=tp, length=length),
        grid=(n // tp,),
        in_specs=[full((HY_FEAT, 1)), full((HY_HIDDEN, LANES)), full((HY_HIDDEN, 1)),
                  full((HY_HIDDEN, HY_HIDDEN)), full((HY_HIDDEN, 1)),
                  full((4 * HY_CH, 3 * HY_HIDDEN)), full((4 * HY_CH, 1)), full((4 * HY_CH, 1))],
        out_specs=[pl.BlockSpec((2, HY_CH, tp), lambda i: (0, 0, i)),
                   full((2, HY_CH, 1)), full((2, HY_CH, 1))],
        out_shape=[jax.ShapeDtypeStruct((2, HY_CH, n), F32),
                   jax.ShapeDtypeStruct((2, HY_CH, 1), F32),
                   jax.ShapeDtypeStruct((2, HY_CH, 1), F32)],
        compiler_params=_params(("arbitrary",)),
        name="hyena_filters",
    )(band_col, w1t, b1.astype(F32)[:, None], w2.astype(F32).T, b2.astype(F32)[:, None],
      _cat3(w3.astype(F32).T, 1), b3.astype(F32)[:, None], dl_col)


def _dft_consts(n1, n2):
    n = n1 * n2
    a1 = 2.0 * np.pi * ((np.arange(n1)[:, None] * np.arange(n1)[None, :]) % n1) / n1
    c1, s1 = np.cos(a1), np.sin(a1)
    a2 = 2.0 * np.pi * ((np.arange(n2)[:, None] * np.arange(n2)[None, :]) % n2) / n2
    c2, s2 = np.cos(a2), np.sin(a2)
    at = 2.0 * np.pi * ((np.arange(n1)[:, None] * np.arange(n2)[None, :]) % n) / n
    f1_full = np.concatenate([c1, -s1], axis=0)
    f1_half = f1_full[:, :n1 // 2]
    g1 = np.concatenate([c1[:n1 // 2], -s1[:n1 // 2]], axis=1)
    w2f = np.block([[c2, -s2], [s2, c2]])
    w2i = np.block([[c2, s2], [-s2, c2]])
    f = lambda x: jnp.asarray(x, F32)
    return dict(f1_full=f(f1_full), f1_half=f(f1_half), g1=f(g1), w2f=f(w2f), w2i=f(w2i),
                tr=f(np.cos(at)), ti=f(-np.sin(at)))


def _fft_fwd(slabs, f1, tr, ti, w2f, stack, n1, n2, prec):
    for c, x in enumerate(slabs):
        a = _dot(f1, x.astype(stack.dtype), prec)
        ar, ai = a[:n1], a[n1:]
        stack[c * n1:(c + 1) * n1, 0:n2] = (ar * tr - ai * ti).astype(stack.dtype)
        stack[c * n1:(c + 1) * n1, n2:2 * n2] = (ar * ti + ai * tr).astype(stack.dtype)
    return _dot(stack[...], w2f, prec)


def _split_bf16(x):
    hi = x.astype(BF16)
    return hi, (x - hi.astype(F32)).astype(BF16)


def _cat3(x, axis):
    hi, lo = _split_bf16(x)
    return jnp.concatenate([hi, hi, lo], axis=axis)


def _hy_spec_kernel(nrm_ref, k_ref, f1_ref, tr_ref, ti_ref, w2f_ref, o_ref, stack, *, cg, n1, n2):
    o = pl.program_id(0)
    g = pl.program_id(1)
    tr = tr_ref[...]
    ti = ti_ref[...]
    for c in range(cg):
        hi, lo = _split_bf16(k_ref[c])
        a = _dot(f1_ref[...], jnp.concatenate([hi, lo, hi], axis=0))
        ar, ai = a[:n1], a[n1:]
        sh, sl = _split_bf16(jnp.concatenate([ar * tr - ai * ti, ar * ti + ai * tr], axis=1))
        stack[c * n1:(c + 1) * n1, :] = jnp.concatenate([sh, sl, sh], axis=1)
    x = _dot(stack[...], w2f_ref[...])
    for c in range(cg):
        sc = nrm_ref[o * HY_CH + g * cg + c] * (1.0 / (n1 * n2))
        xc = x[c * n1:(c + 1) * n1] * sc
        o_ref[c, 0] = xc[:, :n2]
        o_ref[c, 1] = xc[:, n2:]


def _hy_spec(k4, nrm_flat, dc, cg, n1, n2):
    full = lambda shape: pl.BlockSpec(shape, lambda o, g: (0,) * len(shape))
    return pl.pallas_call(
        functools.partial(_hy_spec_kernel, cg=cg, n1=n1, n2=n2),
        grid=(2, HY_CH // cg),
        in_specs=[pl.BlockSpec(memory_space=pltpu.SMEM),
                  pl.BlockSpec((None, cg, n1, n2), lambda o, g: (o, g, 0, 0)),
                  full((2 * n1, 3 * n1)), full((n1, n2)), full((n1, n2)), full((6 * n2, 2 * n2))],
        out_specs=pl.BlockSpec((None, cg, 2, n1, n2), lambda o, g: (o, g, 0, 0, 0)),
        out_shape=jax.ShapeDtypeStruct((2, HY_CH, 2, n1, n2), F32),
        scratch_shapes=[pltpu.VMEM((cg * n1, 6 * n2), BF16)],
        compiler_params=_params(("parallel", "arbitrary")),
        name="hyena_filter_spectrum",
    )(nrm_flat, k4, _cat3(dc["f1_full"], 1), dc["tr"], dc["ti"], _cat3(dc["w2f"], 0))


def _hy_conv_kernel(skip_ref, x_ref, ks_ref, f1_ref, g1_ref, tr_ref, ti_ref, w2f_ref, w2i_ref,
                    o_ref, stack, *, cg, n1, n2):
    g = pl.program_id(1)
    tr = tr_ref[...]
    ti = ti_ref[...]

    def conv(slabs, order):
        x = _fft_fwd(slabs, f1_ref[...], tr, ti, w2f_ref[...], stack, n1, n2, None)
        for c in range(cg):
            xr, xi = x[c * n1:(c + 1) * n1, :n2], x[c * n1:(c + 1) * n1, n2:]
            kr, ki = ks_ref[order, c, 0], ks_ref[order, c, 1]
            stack[c * n1:(c + 1) * n1, 0:n2] = (xr * kr - xi * ki).astype(BF16)
            stack[c * n1:(c + 1) * n1, n2:2 * n2] = (xr * ki + xi * kr).astype(BF16)
        bm = _dot(stack[...], w2i_ref[...])
        outs = []
        for c in range(cg):
            br, bi = bm[c * n1:(c + 1) * n1, :n2], bm[c * n1:(c + 1) * n1, n2:]
            b2 = jnp.concatenate([br * tr + bi * ti, bi * tr - br * ti], axis=0)
            y = _dot(g1_ref[...], b2.astype(BF16))
            outs.append(y + slabs[c] * skip_ref[order * HY_CH + g * cg + c])
        return outs

    v = [x_ref[2, c] for c in range(cg)]
    y0 = conv(v, 0)
    z = [x_ref[0, c] * y0[c] for c in range(cg)]
    y1 = conv(z, 1)
    for c in range(cg):
        o_ref[c] = x_ref[1, c] * y1[c]


def _hy_conv(x4, kspec, skip_flat, dc, cg, n1, n2):
    bsz = x4.shape[1]
    full = lambda shape: pl.BlockSpec(shape, lambda b, g: (0,) * len(shape))
    return pl.pallas_call(
        functools.partial(_hy_conv_kernel, cg=cg, n1=n1, n2=n2),
        grid=(bsz, HY_CH // cg),
        in_specs=[pl.BlockSpec(memory_space=pltpu.SMEM),
                  pl.BlockSpec((3, None, cg, n1 // 2, n2), lambda b, g: (0, b, g, 0, 0)),
                  pl.BlockSpec((2, cg, 2, n1, n2), lambda b, g: (0, g, 0, 0, 0)),
                  full((2 * n1, n1 // 2)), full((n1 // 2, 2 * n1)), full((n1, n2)), full((n1, n2)),
                  full((2 * n2, 2 * n2)), full((2 * n2, 2 * n2))],
        out_specs=pl.BlockSpec((None, cg, n1 // 2, n2), lambda b, g: (b, g, 0, 0)),
        out_shape=jax.ShapeDtypeStruct((bsz, HY_CH, n1 // 2, n2), F32),
        scratch_shapes=[pltpu.VMEM((cg * n1, 2 * n2), BF16)],
        compiler_params=_params(("parallel", "arbitrary")),
        name="hyena_long_conv",
    )(skip_flat, x4, kspec, dc["f1_half"].astype(BF16), dc["g1"].astype(BF16), dc["tr"], dc["ti"],
      dc["w2f"].astype(BF16), dc["w2i"].astype(BF16))


def _hy_ctx_kernel(x_ref, k_ref, nrm_ref, skip_ref, fc_ref, fs_ref, o_ref, *, c):
    fc = fc_ref[...]
    fs = fs_ref[...]
    inv_n = 1.0 / (2 * c)

    def conv(x, order):
        kk = k_ref[order]
        kr, ki = _dot(kk, fc, HI), -_dot(kk, fs, HI)
        xr, xi = _dot(x, fc[:c], HI), -_dot(x, fs[:c], HI)
        yr, yi = xr * kr - xi * ki, xr * ki + xi * kr
        y = (_dot(yr, fc[:, :c], HI) - _dot(yi, fs[:, :c], HI)) * inv_n
        return y * nrm_ref[order] + x * skip_ref[order]

    z = x_ref[0] * conv(x_ref[2], 0)
    o_ref[...] = x_ref[1] * conv(z, 1)


def _hy_ctx_conv(xt, kt, nrm, skip_col):
    _, bsz, ch, c = xt.shape
    n = 2 * c
    ang = 2.0 * np.pi * ((np.arange(n)[:, None] * np.arange(n)[None, :]) % n) / n
    fc, fs = jnp.asarray(np.cos(ang), F32), jnp.asarray(np.sin(ang), F32)
    full = lambda shape: pl.BlockSpec(shape, lambda b: (0,) * len(shape))
    return pl.pallas_call(
        functools.partial(_hy_ctx_kernel, c=c),
        grid=(bsz,),
        in_specs=[pl.BlockSpec((3, None, ch, c), lambda b: (0, b, 0, 0)),
                  full((2, ch, n)), full((2, ch, 1)), full((2, ch, 1)), full((n, n)), full((n, n))],
        out_specs=pl.BlockSpec((None, ch, c), lambda b: (b, 0, 0)),
        out_shape=jax.ShapeDtypeStruct((bsz, ch, c), F32),
        compiler_params=_params(("arbitrary",)),
        name="hyena_ctx_conv",
    )(xt, kt, nrm, skip_col, fc, fs)


def _out_proj_kernel(a_ref, b_ref, p_ref, ht_ref, x_ref, g_ref, w_ref, o_ref):
    w = D_GROUP
    acc = _dot(a_ref[...].astype(BF16), w_ref[0:w])
    acc += _dot(b_ref[...].astype(BF16), w_ref[w:2 * w])
    acc += _dot(p_ref[...].astype(BF16), w_ref[2 * w:3 * w])
    acc += _dot(ht_ref[...].T.astype(BF16), w_ref[3 * w:4 * w])
    o_ref[...] = x_ref[...] + g_ref[...] * acc


def _out_proj(a, b, p, ht, x, mod3, row_of_batch, j_gate, w_out, tm):
    bsz, t, d = x.shape
    w = D_GROUP
    tok = pl.BlockSpec((None, tm, w), lambda bb, i: (bb, i, 0))
    return pl.pallas_call(
        _out_proj_kernel,
        grid=(bsz, t // tm),
        in_specs=[tok, tok, tok,
                  pl.BlockSpec((None, w, tm), lambda bb, i: (bb, 0, i)),
                  pl.BlockSpec((None, tm, d), lambda bb, i: (bb, i, 0)),
                  pl.BlockSpec((None, 1, d), lambda bb, i: (row_of_batch(bb), 0, j_gate)),
                  pl.BlockSpec((4 * w, d), lambda bb, i: (0, 0))],
        out_specs=pl.BlockSpec((None, tm, d), lambda bb, i: (bb, i, 0)),
        out_shape=jax.ShapeDtypeStruct((bsz, t, d), F32),
        compiler_params=_params(("parallel", "arbitrary")),
        name="out_proj_residual",
    )(a, b, p, ht, x, mod3, w_out)


MOE_ROWS = 256


def _route(x, g, sc, sh, rw3, rbc):
    per_group = N_EXPERTS // N_EXPERT_GROUPS
    tm = x.shape[0]
    ms = jnp.mean(x * x, axis=-1, keepdims=True)
    h = (x * lax.rsqrt(ms + EPS)) * g * (1.0 + sc) + sh
    hh, hl = _split_bf16(h)
    d = x.shape[1]
    logits = (_dot(hh, rw3[0:d]) + _dot(hl, rw3[d:2 * d]) + _dot(hh, rw3[2 * d:3 * d])).T[:N_EXPERTS]
    ex = jnp.exp(logits - jnp.max(logits, axis=0, keepdims=True))
    scores = ex / jnp.sum(ex, axis=0, keepdims=True)
    sel = scores + rbc
    srow = [sel[r:r + 1] for r in range(N_EXPERTS)]
    best = None
    for grp in range(N_EXPERT_GROUPS):
        rows = list(range(grp * per_group, (grp + 1) * per_group))
        v1 = functools.reduce(jnp.maximum, [srow[r] for r in rows])
        i1 = jnp.full((1, tm), rows[-1], jnp.int32)
        for r in reversed(rows[:-1]):
            i1 = jnp.where(srow[r] == v1, r, i1)
        rest = [jnp.where(i1 == r, -jnp.inf, srow[r]) for r in rows]
        v2 = functools.reduce(jnp.maximum, rest)
        i2 = jnp.full((1, tm), rows[-1], jnp.int32)
        for k in reversed(range(per_group - 1)):
            i2 = jnp.where(rest[k] == v2, rows[k], i2)
        gs = v1 + v2
        if best is None:
            best, e1, e2 = gs, i1, i2
        else:
            upd = gs > best
            best = jnp.where(upd, gs, best)
            e1 = jnp.where(upd, i1, e1)
            e2 = jnp.where(upd, i2, e2)
    row = lax.broadcasted_iota(jnp.int32, (N_EXPERTS, tm), 0)
    w1 = jnp.sum(jnp.where(row == e1, scores, 0.0), axis=0, keepdims=True)
    w2 = jnp.sum(jnp.where(row == e2, scores, 0.0), axis=0, keepdims=True)
    tot = w1 + w2
    gates_t = jnp.where(row == e1, w1 / tot, 0.0) + jnp.where(row == e2, w2 / tot, 0.0)
    gates_t = jnp.concatenate([gates_t, jnp.zeros((LANES - N_EXPERTS, tm), F32)], axis=0)
    return h, gates_t.T


def _moe_kernel(x_ref, g_ref, sc_ref, sh_ref, gate_ref, rw_ref, rb_ref, w1_ref, w3_ref, w2_ref, fg_ref,
                o_ref, h_scr, gates_scr, acc_scr, *, final):
    e = pl.program_id(2)
    tm = x_ref.shape[0]

    @pl.when(e == 0)
    def _():
        h, gates = _route(x_ref[...], g_ref[...], sc_ref[...], sh_ref[...], rw_ref[...], rb_ref[...])
        h_scr[...] = h.astype(BF16)
        gates_scr[...] = gates
        acc_scr[...] = jnp.zeros(acc_scr.shape, F32)

    lane = lax.broadcasted_iota(jnp.int32, (tm, LANES), 1)
    ge = jnp.sum(jnp.where(lane == e, gates_scr[...], 0.0), axis=1, keepdims=True)

    def up(j):
        hb = h_scr[j * MOE_ROWS:(j + 1) * MOE_ROWS, :]
        return _dot(hb, w1_ref[...]), _dot(hb, w3_ref[...])

    nxt = up(0)
    for j in range(tm // MOE_ROWS):
        a, b = nxt
        if (j + 1) * MOE_ROWS < tm:
            nxt = up(j + 1)
        rows = slice(j * MOE_ROWS, (j + 1) * MOE_ROWS)
        act = (a * jax.nn.sigmoid(a)) * b
        acc_scr[rows, :] += ge[rows] * _dot(act.astype(BF16), w2_ref[...])

    @pl.when(e == N_EXPERTS - 1)
    def _():
        y = x_ref[...] + gate_ref[...] * acc_scr[...]
        if final:
            ms = jnp.mean(y * y, axis=-1, keepdims=True)
            y = (y * lax.rsqrt(ms + EPS)) * fg_ref[...]
        o_ref[...] = y


def _moe(x, g, mod3, row_of_batch, j_shift, j_scale, j_gate, rw, rb, w1, w3, w2, final_g, final, tm):
    bsz, t, d = x.shape
    vec = lambda j: pl.BlockSpec((None, 1, d), lambda b, i, e: (row_of_batch(b), 0, j))
    full = lambda shape: pl.BlockSpec(shape, lambda b, i, e: (0,) * len(shape))
    return pl.pallas_call(
        functools.partial(_moe_kernel, final=final),
        grid=(bsz, t // tm, N_EXPERTS),
        in_specs=[pl.BlockSpec((None, tm, d), lambda b, i, e: (b, i, 0)),
                  full((1, d)), vec(j_scale), vec(j_shift), vec(j_gate),
                  full((3 * d, LANES)), full((N_EXPERTS, 1)),
                  pl.BlockSpec((None, d, D_EXPERT), lambda b, i, e: (e, 0, 0)),
                  pl.BlockSpec((None, d, D_EXPERT), lambda b, i, e: (e, 0, 0)),
                  pl.BlockSpec((None, D_EXPERT, d), lambda b, i, e: (e, 0, 0)),
                  full((1, d))],
        out_specs=pl.BlockSpec((None, tm, d), lambda b, i, e: (b, i, 0)),
        out_shape=jax.ShapeDtypeStruct((bsz, t, d), F32),
        scratch_shapes=[pltpu.VMEM((tm, d), BF16), pltpu.VMEM((tm, LANES), F32), pltpu.VMEM((tm, d), F32)],
        compiler_params=_params(("parallel", "parallel", "arbitrary")),
        name="moe_final" if final else "moe",
    )(x, g, mod3, mod3, mod3, rw, rb, w1, w3, w2, final_g)


MOE_SUB = 512
MOE_CAP = 128
MOE_BLK = 2048
MOE_PAIR = 2


def _moe_route_kernel(x_ref, g_ref, sc_ref, sh_ref, rw_ref, rb_ref, h_ref, gates_ref, cnt_ref):
    h, gates = _route(x_ref[...], g_ref[...], sc_ref[...], sh_ref[...], rw_ref[...], rb_ref[...])
    h_ref[...] = h.astype(BF16)
    gates_ref[...] = gates
    cnt = jnp.sum(jnp.where(gates > 0.0, 1.0, 0.0), axis=0, keepdims=True)
    cnt_ref[...] = jnp.broadcast_to(cnt, cnt_ref.shape)


def _moe_route(x, g, mod3, row_of_batch, j_shift, j_scale, rw, rb):
    bsz, t, d = x.shape
    tm = MOE_SUB
    vec = lambda j: pl.BlockSpec((None, 1, d), lambda b, i: (row_of_batch(b), 0, j))
    full = lambda shape: pl.BlockSpec(shape, lambda b, i: (0,) * len(shape))
    return pl.pallas_call(
        _moe_route_kernel,
        grid=(bsz, t // tm),
        in_specs=[pl.BlockSpec((None, tm, d), lambda b, i: (b, i, 0)),
                  full((1, d)), vec(j_scale), vec(j_shift), full((3 * d, LANES)), full((N_EXPERTS, 1))],
        out_specs=[pl.BlockSpec((None, tm, d), lambda b, i: (b, i, 0)),
                   pl.BlockSpec((None, tm, LANES), lambda b, i: (b, i, 0)),
                   pl.BlockSpec((None, None, SUBLANES, LANES), lambda b, i: (b, i, 0, 0))],
        out_shape=[jax.ShapeDtypeStruct((bsz, t, d), BF16),
                   jax.ShapeDtypeStruct((bsz, t, LANES), F32),
                   jax.ShapeDtypeStruct((bsz, t // tm, SUBLANES, LANES), F32)],
        compiler_params=_params(("parallel", "arbitrary")),
        name="moe_route",
    )(x, g, mod3, mod3, rw, rb)


def _moe_routed_kernel(np_ref, h_ref, gates_ref, x_ref, gate_ref, w1_ref, w3_ref, w2_ref, fg_ref,
                       o_ref, rank_scr, rankt_scr, gatet_scr, xc_scr, *, final, nblk):
    b = pl.program_id(0)
    i = pl.program_id(1)
    e = pl.program_id(2)
    nsub = h_ref.shape[0] // MOE_SUB
    sub = lambda j: slice(j * MOE_SUB, (j + 1) * MOE_SUB)

    @pl.when(e == 0)
    def _():
        o_ref[...] = jnp.zeros(o_ref.shape, F32)
        r = lax.broadcasted_iota(jnp.int32, (MOE_SUB, MOE_SUB), 0)
        c = lax.broadcasted_iota(jnp.int32, (MOE_SUB, MOE_SUB), 1)
        ltri = jnp.where(c <= r, 1.0, 0.0).astype(BF16)
        utri = jnp.where(r <= c, 1.0, 0.0).astype(BF16)
        for j in range(nsub):
            gt = gates_ref[sub(j), :]
            rank_scr[j] = _dot(ltri, jnp.where(gt > 0.0, 1.0, 0.0).astype(BF16))
            gtt = gt.T[:N_EXPERTS]
            gatet_scr[j] = gtt
            rankt_scr[j] = _dot(jnp.where(gtt > 0.0, 1.0, 0.0).astype(BF16), utri)

    lane = lax.broadcasted_iota(jnp.int32, (MOE_SUB, LANES), 1)
    slot_lane = lax.broadcasted_iota(jnp.int32, (MOE_SUB, MOE_PAIR * MOE_CAP), 1)
    second = slot_lane >= MOE_CAP

    def one_pass(p, carry):
        base = (p * MOE_CAP + 1).astype(F32)
        slot_r = lax.broadcasted_iota(jnp.int32, (MOE_CAP, MOE_SUB), 0).astype(F32) + base
        slot_c = jnp.where(second, slot_lane - MOE_CAP, slot_lane).astype(F32) + base
        ys = []
        gcs = [[], []]
        for j in range(nsub):
            picks = []
            for k in range(MOE_PAIR):
                ex = e * MOE_PAIR + k
                rr = rankt_scr[j, pl.ds(ex, 1), :]
                gr = gatet_scr[j, pl.ds(ex, 1), :]
                pick = (rr == slot_r) & (gr > 0.0)
                picks.append(jnp.where(pick, 1.0, 0.0).astype(BF16))
                gcs[k].append(jnp.sum(jnp.where(pick, gr, 0.0), axis=1, keepdims=True))
            xcj = _dot(jnp.concatenate(picks, axis=0), h_ref[sub(j), :]).astype(BF16)
            for k in range(MOE_PAIR):
                xc_scr[k, j * MOE_CAP:(j + 1) * MOE_CAP, :] = xcj[k * MOE_CAP:(k + 1) * MOE_CAP]
        for k in range(MOE_PAIR):
            xc = xc_scr[k]
            a = _dot(xc, w1_ref[k])
            bb = _dot(xc, w3_ref[k])
            y = _dot(((a * jax.nn.sigmoid(a)) * bb).astype(BF16), w2_ref[k])
            ys.append([(y[j * MOE_CAP:(j + 1) * MOE_CAP] * gcs[k][j]).astype(BF16) for j in range(nsub)])
        for j in range(nsub):
            cols = []
            for k in range(MOE_PAIR):
                ex = e * MOE_PAIR + k
                rc = jnp.sum(jnp.where(lane == ex, rank_scr[j], 0.0), axis=1, keepdims=True)
                gc = jnp.sum(jnp.where(lane == ex, gates_ref[sub(j), :], 0.0), axis=1, keepdims=True)
                cols.append(jnp.where(gc > 0.0, rc, 0.0))
            put = jnp.where(jnp.where(second, cols[1], cols[0]) == slot_c, 1.0, 0.0).astype(BF16)
            o_ref[sub(j), :] += _dot(put, jnp.concatenate([ys[0][j], ys[1][j]], axis=0))
        return carry

    lax.fori_loop(0, np_ref[(b * nblk + i) * (N_EXPERTS // MOE_PAIR) + e], one_pass, 0)

    @pl.when(e == N_EXPERTS // MOE_PAIR - 1)
    def _():
        y = x_ref[...] + gate_ref[...] * o_ref[...]
        if final:
            ms = jnp.mean(y * y, axis=-1, keepdims=True)
            y = (y * lax.rsqrt(ms + EPS)) * fg_ref[...]
        o_ref[...] = y


def _moe_routed(x, g, mod3, row_of_batch, j_shift, j_scale, j_gate, rw, rb, w1, w3, w2, final_g, final):
    bsz, t, d = x.shape
    h, gates, cnt = _moe_route(x, g, mod3, row_of_batch, j_shift, j_scale, rw, rb)
    tb = _tile(t, MOE_BLK)
    nblk = t // tb
    nsub = tb // MOE_SUB
    npair = N_EXPERTS // MOE_PAIR
    passes = jnp.ceil(cnt[:, :, 0, :N_EXPERTS] / MOE_CAP).astype(jnp.int32)
    passes = jnp.max(passes.reshape(bsz, nblk, nsub, npair, MOE_PAIR), axis=(2, 4)).reshape(-1)
    full = lambda shape: pl.BlockSpec(shape, lambda b, i, e, np_: (0,) * len(shape))
    blk = lambda w: pl.BlockSpec((None, tb, w), lambda b, i, e, np_: (b, i, 0), pipeline_mode=pl.Buffered(1))
    grid_spec = pltpu.PrefetchScalarGridSpec(
        num_scalar_prefetch=1,
        grid=(bsz, nblk, npair),
        in_specs=[blk(d), blk(LANES), blk(d),
                  pl.BlockSpec((None, 1, d), lambda b, i, e, np_: (row_of_batch(b), 0, j_gate)),
                  pl.BlockSpec((MOE_PAIR, d, D_EXPERT), lambda b, i, e, np_: (e, 0, 0)),
                  pl.BlockSpec((MOE_PAIR, d, D_EXPERT), lambda b, i, e, np_: (e, 0, 0)),
                  pl.BlockSpec((MOE_PAIR, D_EXPERT, d), lambda b, i, e, np_: (e, 0, 0)),
                  full((1, d))],
        out_specs=pl.BlockSpec((None, tb, d), lambda b, i, e, np_: (b, i, 0)),
        scratch_shapes=[pltpu.VMEM((nsub, MOE_SUB, LANES), F32),
                        pltpu.VMEM((nsub, N_EXPERTS, MOE_SUB), F32),
                        pltpu.VMEM((nsub, N_EXPERTS, MOE_SUB), F32),
                        pltpu.VMEM((MOE_PAIR, nsub * MOE_CAP, d), BF16)])
    return pl.pallas_call(
        functools.partial(_moe_routed_kernel, final=final, nblk=nblk),
        grid_spec=grid_spec,
        out_shape=jax.ShapeDtypeStruct((bsz, t, d), F32),
        compiler_params=_params(("parallel", "parallel", "arbitrary")),
        name="moe_routed_final" if final else "moe_routed",
    )(passes, h, gates, x, mod3, w1, w3, w2, final_g)


def _tile(n, pref):
    t = min(n, pref)
    assert n % t == 0
    return t


def _fft_split(n):
    n2 = LANES
    assert n % n2 == 0
    return n // n2, n2


def _hyena_latent(u, lp, fargs):
    bsz, s, _ = u.shape
    n1, n2 = _fft_split(2 * s)
    dc = _dft_consts(n1, n2)
    cg = 16
    kt_l, _, nrm_l = _hy_filters(s, *fargs, tp=_tile(2 * s, 2048))
    kspec = _hy_spec(kt_l.reshape(2, HY_CH, n1, n2), nrm_l.reshape(2 * HY_CH), dc, cg, n1, n2)
    xt = _hy_short(u, lp["hy_short_w"].astype(F32), lp["hy_short_b"].astype(F32)[None, :], _tile(s, 2048))
    h_l = _hy_conv(xt.reshape(3, bsz, HY_CH, n1 // 2, n2), kspec, lp["hy_skip"].astype(F32).reshape(2 * HY_CH),
                   dc, cg, n1, n2)
    return h_l.reshape(bsz, HY_CH, s)


def _mixers(lat_in, ctx_in, lp, li, need_ctx):
    u, ql, klt, vl = lat_in
    uc, qc, kct, vc = ctx_in
    bsz, s, _ = u.shape
    c = uc.shape[1]
    lam_init = 0.8 - 0.6 * math.exp(-0.3 * li)
    lam_vecs = lp["a_lambda"].astype(F32)
    subln = lp["a_subln_g"].astype(F32)[None, :]

    a_l = _diff_attn(ql, kct, vc, klt, vl, lam_vecs, subln, lam_init, _tile(s, 512), _tile(s, 4096))
    bias8 = _nbr_bias(lp["b_rpb"])
    b_l = _nbr_attn(u, uc, bias8)
    wbd = jax.scipy.linalg.block_diag(*[lp["pool_w"][g] for g in range(len(POOL_SIZES))]).astype(BF16)
    pscale = lp["pool_scale"].astype(F32)[None, :]
    p_l = _pool_mix(u, wbd, pscale, _tile(s, 2048))
    fargs = (lp["hy_f_w1"], lp["hy_f_b1"], lp["hy_f_w2"], lp["hy_f_b2"], lp["hy_f_w3"], lp["hy_f_b3"])
    skip = lp["hy_skip"].astype(F32)
    w_short = lp["hy_short_w"].astype(F32)
    b_short = lp["hy_short_b"].astype(F32)[None, :]
    h_l = _hyena_latent(u, lp, fargs)
    lat = (a_l, b_l, p_l, h_l)
    if not need_ctx:
        return lat, None
    a_c = _diff_attn(qc, kct, vc, None, None, lam_vecs, subln, lam_init, _tile(c, 256), None)
    b_c = _nbr_ctx_attn(uc)
    p_c = _pool_mix(uc, wbd, pscale, _tile(c, 256))
    kt_c, _, nrm_c = _hy_filters(c, *fargs, tp=_tile(2 * c, 512))
    xtc = _hy_short(uc, w_short, b_short, _tile(c, 256))
    h_c = _hy_ctx_conv(xtc, kt_c, nrm_c, skip[:, :, None])
    return lat, (a_c, b_c, p_c, h_c)


def kernel(x, c, ctx, c_ctx, norm1_g, norm2_g, ada_w, ada_b, w_in, w_out, a_lambda, a_subln_g, b_rpb, pool_w, pool_scale, hy_short_w, hy_short_b, hy_f_w1, hy_f_b1, hy_f_w2, hy_f_b2, hy_f_w3, hy_f_b3, hy_skip, router_w, router_b, moe_w1, moe_w3, moe_w2, final_g):
    depth = norm1_g.shape[0]
    bsz, s, d = x.shape
    cl = ctx.shape[1]
    assert bsz <= SUBLANES - 1
    xl, xc = x, ctx
    cpad = jnp.zeros((SUBLANES, d), F32).at[:bsz].set(c.astype(F32)).at[bsz].set(c_ctx.astype(F32))
    rw = _cat3(jnp.pad(router_w.astype(F32), ((0, 0), (0, LANES - N_EXPERTS))), 0)
    rb = router_b.astype(F32)[:, None]
    lat_row = lambda b: b
    ctx_row = lambda b: bsz
    fg = final_g.astype(F32)[None, :]
    tm = _tile(s, 512)
    tmc = _tile(cl, 256)
    for li in range(depth):
        need_ctx = li < depth - 1
        lp = dict(a_lambda=a_lambda[li], a_subln_g=a_subln_g[li], b_rpb=b_rpb[li], pool_w=pool_w[li],
                  pool_scale=pool_scale[li], hy_short_w=hy_short_w[li], hy_short_b=hy_short_b[li],
                  hy_f_w1=hy_f_w1[li], hy_f_b1=hy_f_b1[li], hy_f_w2=hy_f_w2[li], hy_f_b2=hy_f_b2[li],
                  hy_f_w3=hy_f_w3[li], hy_f_b3=hy_f_b3[li], hy_skip=hy_skip[li])
        mod3 = _ada(cpad, ada_w[li].astype(F32), ada_b[li].astype(F32)[None, :]).reshape(SUBLANES, 1, 6 * d)
        n1g = norm1_g[li].astype(F32)[None, :]
        n2g = norm2_g[li].astype(F32)[None, :]
        w_in_b = w_in[li].astype(BF16)
        w_out_b = w_out[li].astype(BF16)
        lat_in = _norm_proj(xl, n1g, mod3, lat_row, 0, 1, w_in_b, tm, True)
        ctx_in = _norm_proj(xc, n1g, mod3, ctx_row, 0, 1, w_in_b, tmc, False)
        lat, cx = _mixers(lat_in, ctx_in, lp, li, need_ctx)
        xl = _out_proj(*lat, xl, mod3, lat_row, 2, w_out_b, _tile(s, 1024))
        w1b, w3b, w2b = moe_w1[li].astype(BF16), moe_w3[li].astype(BF16), moe_w2[li].astype(BF16)
        if need_ctx:
            xc = _out_proj(*cx, xc, mod3, ctx_row, 2, w_out_b, tmc)
            xc = _moe(xc, n2g, mod3, ctx_row, 3, 4, 5, rw, rb, w1b, w3b, w2b, fg, False, tmc)
        xl = _moe_routed(xl, n2g, mod3, lat_row, 3, 4, 5, rw, rb, w1b, w3b, w2b, fg, li == depth - 1)
    return xl
```
